```python
import jax, jax.numpy as jnp
from jax import lax
import numpy as np

D_MODEL = 1024
BATCH = 4
SEQ = 4096
DEPTH = 4
DEC_BATCH = 128
DEC_SEQ = 4
PAST_LEN = 2048
PAGE_SIZE = 128

N_EVEN = (DEPTH + 1) // 2
N_ODD = DEPTH // 2

RET_HEADS = 4
RET_DK = 128
RET_DV = 128
RET_CHUNK = 128
RET_THETA = 10000.0
SG_GROUPS = 4
SG_CH = 128
SG_CHUNK = 128
RET_W = RET_HEADS * RET_DK
SG_W = SG_GROUPS * SG_CH
EVEN_IN = 4 * RET_W + 2 * SG_W
EVEN_OUT = RET_HEADS * RET_DV + SG_W

NSA_HEADS = 16
NSA_KV = 4
HEAD_DIM = D_MODEL // NSA_HEADS
NSA_REP = NSA_HEADS // NSA_KV
NSA_Q_W = NSA_HEADS * HEAD_DIM
NSA_KV_W = 6 * NSA_KV * HEAD_DIM
ODD_IN = NSA_Q_W + NSA_KV_W + 3 * NSA_HEADS
CMP_LEN = 32
CMP_STRIDE = 16
SEL_BLK = 64
SEL_TOPK = 16
SEL_QBLK = 64
WINDOW = 512
WIN_BLK = 128
ROPE_DIM = HEAD_DIM // 4
ROPE_THETA = 500000.0
ATTN_SCALE = HEAD_DIM ** -0.5

D_FF = ((8 * D_MODEL // 3 + 127) // 128) * 128
CONV_W = 3

EPS = 1e-6
NEG = -1e30
FORCE = 1e9

kernel_name = 'retnet_gmlp_nsa_convffn_hybrid_step'


def rms_norm(x, g):
    xf = x.astype(jnp.float32)
    y = xf * lax.rsqrt(jnp.mean(xf * xf, -1, keepdims=True) + EPS)
    return y.astype(x.dtype) * g


def rms_unit(x):
    xf = x.astype(jnp.float32)
    return (xf * lax.rsqrt(jnp.mean(xf * xf, -1, keepdims=True) + EPS)).astype(x.dtype)


def layer_norm(x, g, b):
    xf = x.astype(jnp.float32)
    xc = xf - jnp.mean(xf, -1, keepdims=True)
    y = xc * lax.rsqrt(jnp.mean(xc * xc, -1, keepdims=True) + EPS)
    return y.astype(x.dtype) * g + b


def rope(x, pos, n_rot, inv_freq):
    ang = pos.astype(jnp.float32)[:, None] * inv_freq[None, :]
    cos = jnp.cos(ang)[:, None, :].astype(x.dtype)
    sin = jnp.sin(ang)[:, None, :].astype(x.dtype)
    h = n_rot // 2
    x1, x2, rest = x[..., :h], x[..., h:n_rot], x[..., n_rot:]
    return jnp.concatenate([x1 * cos - x2 * sin, x2 * cos + x1 * sin, rest], axis=-1)


def ret_freq():
    return 1.0 / (RET_THETA ** jnp.linspace(0.0, 1.0, RET_DK // 2))


def nsa_freq():
    return 1.0 / (ROPE_THETA ** (jnp.arange(0, ROPE_DIM, 2, dtype=jnp.float32) / ROPE_DIM))


def ret_log_decay():
    return jnp.log(1.0 - 2.0 ** (-5.0 - jnp.arange(RET_HEADS, dtype=jnp.float32)))


def retention(q, k, v, s0):
    B, n, H, _ = q.shape
    C = min(n, RET_CHUNK)
    nc = n // C
    log_g = ret_log_decay()
    idx = jnp.arange(C, dtype=jnp.float32)
    diff = idx[:, None] - idx[None, :]
    dmask = jnp.where(diff >= 0, jnp.exp(log_g[:, None, None] * jnp.maximum(diff, 0.0)), 0.0).astype(q.dtype)
    qdec = jnp.exp(log_g[:, None] * (idx + 1.0))[..., None].astype(q.dtype)
    kdec = jnp.exp(log_g[:, None] * (C - 1.0 - idx))[..., None]
    cdec = jnp.exp(log_g * C)[:, None, None]

    def chunks(t):
        return t.reshape(B, nc, C, H, t.shape[-1]).transpose(1, 0, 3, 2, 4)

    def step(s, qkv):
        qc, kc, vc = qkv
        inner = jnp.einsum('bhid,bhjd->bhij', qc, kc) * dmask
        o = (jnp.einsum('bhij,bhjv->bhiv', inner, vc)
             + jnp.einsum('bhid,bhdv->bhiv', qc * qdec, s.astype(qc.dtype)))
        s = s * cdec + jnp.einsum('bhjd,bhjv->bhdv', kc.astype(jnp.float32) * kdec, vc.astype(jnp.float32))
        return s, o

    s, o = lax.scan(step, s0, (chunks(q), chunks(k), chunks(v)))
    return o.transpose(1, 0, 3, 2, 4).reshape(B, n, H, RET_DV), s


def even_mixer(h, pos, s0, w_in, w_out, ln_g, ln_b, sg_w, sg_b):
    B, n, _ = h.shape
    p = h @ w_in
    q, k, v, g, u, sv = jnp.split(p, 6, axis=-1)
    q = rope(q.reshape(B, n, RET_HEADS, RET_DK), pos, RET_DK, ret_freq())
    k = rope(k.reshape(B, n, RET_HEADS, RET_DK), pos, RET_DK, ret_freq()) * (RET_DK ** -0.5)
    o, s1 = retention(q, k, v.reshape(B, n, RET_HEADS, RET_DV), s0)
    ret_out = jax.nn.silu(g) * rms_unit(o).reshape(B, n, RET_HEADS * RET_DV)
    svn = layer_norm(sv.reshape(B, n, SG_GROUPS, SG_CH), ln_g.reshape(SG_GROUPS, SG_CH), ln_b.reshape(SG_GROUPS, SG_CH))
    c = min(n, SG_CHUNK)
    wm = jnp.tril(sg_w[:, :c, :c])
    mixed = (jnp.einsum('gij,bnjgc->bnigc', wm, svn.reshape(B, n // c, c, SG_GROUPS, SG_CH))
             + sg_b[:, :c].T[None, None, :, :, None])
    sg_out = u * mixed.reshape(B, n, SG_W)
    y = jnp.concatenate([ret_out, sg_out], axis=-1) @ w_out
    return y, s1, svn.reshape(B, n, SG_W)


def nsa_project(h, pos, w_in):
    B, n, _ = h.shape
    p = h @ w_in
    q = rope(p[..., :NSA_Q_W].reshape(B, n, NSA_HEADS, HEAD_DIM), pos, ROPE_DIM, nsa_freq())
    q = q.reshape(B, n, NSA_KV, NSA_REP, HEAD_DIM)
    kv = p[..., NSA_Q_W:NSA_Q_W + NSA_KV_W].reshape(B, n, 3, 2, NSA_KV, HEAD_DIM)
    k = rope(kv[:, :, :, 0].reshape(B, n, 3 * NSA_KV, HEAD_DIM), pos, ROPE_DIM, nsa_freq())
    kv = jnp.stack([k.reshape(B, n, 3, NSA_KV, HEAD_DIM), kv[:, :, :, 1]], axis=3)
    gates = jax.nn.sigmoid(p[..., NSA_Q_W + NSA_KV_W:]).reshape(B, n, 3, NSA_KV, NSA_REP)
    return q, kv[:, :, 0], kv[:, :, 1], kv[:, :, 2], gates


def compress(rows, pe_k, pe_v, phi_k, phi_v):
    B, L = rows.shape[:2]
    nc = (L - CMP_LEN) // CMP_STRIDE + 1
    nper = CMP_LEN // CMP_STRIDE
    nsub = nc + nper - 1
    sub = rows[:, :nsub * CMP_STRIDE].reshape((B, nsub, CMP_STRIDE) + rows.shape[2:])
    blocks = jnp.concatenate([sub[:, j:j + nc] for j in range(nper)], axis=2)
    kb = blocks[:, :, :, 0] + pe_k[:, None, :]
    vb = blocks[:, :, :, 1] + pe_v[:, None, :]
    kc = jnp.einsum('bnlgd,lde->bnge', kb, phi_k)
    vc = jnp.einsum('bnlgd,lde->bnge', vb, phi_v)
    return kc, vc


def cmp_attend(q, qpos, kc, vc):
    nc = kc.shape[1]
    blk_end = jnp.arange(nc) * CMP_STRIDE + CMP_LEN - 1
    valid = (blk_end[None, :] <= qpos[:, None])[None, :, None, None, :]
    s = jnp.einsum('bngrd,bmgd->bngrm', q, kc).astype(jnp.float32) * ATTN_SCALE
    p = jax.nn.softmax(jnp.where(valid, s, NEG), axis=-1) * valid
    o = jnp.einsum('bngrm,bmgd->bngrd', p.astype(q.dtype), vc)
    return o, p


def select_blocks(p_cmp, qpos, n_sel):
    nc = p_cmp.shape[-1]
    c_start = jnp.arange(nc) * CMP_STRIDE
    s_start = jnp.arange(n_sel) * SEL_BLK
    cover = ((c_start[:, None] < s_start[None, :] + SEL_BLK)
             & (c_start[:, None] + CMP_LEN > s_start[None, :])).astype(jnp.float32)
    imp = jnp.einsum('bngrm,mj->bngj', p_cmp, cover)
    cur = qpos // SEL_BLK
    j = jnp.arange(n_sel)
    forced = (j[None, :] == 0) | (j[None, :] == cur[:, None]) | (j[None, :] == cur[:, None] - 1)
    causal = s_start[None, :] <= qpos[:, None]
    imp = jnp.where(forced[None, :, None, :], FORCE, jnp.where(causal[None, :, None, :], imp, -1.0))
    _, idx = lax.top_k(imp, min(SEL_TOPK, n_sel))
    return idx


def sel_attend(q, qpos, idx, kblk, vblk):
    idx_t = idx.transpose(0, 2, 1, 3)
    take = jax.vmap(jax.vmap(lambda blk, ii: blk[ii]))
    kg = take(kblk, idx_t)
    vg = take(vblk, idx_t)
    kpos = idx_t[..., None] * SEL_BLK + jnp.arange(SEL_BLK)
    mask = (kpos <= qpos[None, None, :, None, None])[:, :, :, None]
    qg = q.transpose(0, 2, 1, 3, 4)
    s = jnp.einsum('bgqrd,bgqkld->bgqrkl', qg, kg).astype(jnp.float32) * ATTN_SCALE
    s = jnp.where(mask, s, NEG)
    shp = s.shape
    p = jax.nn.softmax(s.reshape(shp[:4] + (-1,)), axis=-1).reshape(shp).astype(q.dtype)
    o = jnp.einsum('bgqrkl,bgqkld->bgqrd', p, vg)
    return o.transpose(0, 2, 1, 3, 4)


def sel_attend_blocked(q, qpos, idx, rows, qblk):
    B, n = q.shape[:2]
    nsel = rows.shape[1] // SEL_BLK
    kblk = rows[:, :, 0].reshape(B, nsel, SEL_BLK, NSA_KV, HEAD_DIM).transpose(0, 3, 1, 2, 4)
    vblk = rows[:, :, 1].reshape(B, nsel, SEL_BLK, NSA_KV, HEAD_DIM).transpose(0, 3, 1, 2, 4)
    nb = n // qblk

    def split(t):
        return t.reshape((B, nb, qblk) + t.shape[2:]).swapaxes(0, 1)

    o = lax.map(lambda a: sel_attend(a[0], a[1], a[2], kblk, vblk), (split(q), qpos.reshape(nb, qblk), split(idx)))
    return o.swapaxes(0, 1).reshape(q.shape)


def window_attend(q, qpos, kv, kpos):
    mask = (kpos[None, :] <= qpos[:, None]) & (kpos[None, :] > qpos[:, None] - WINDOW) & (kpos[None, :] >= 0)
    s = jnp.einsum('bqgrd,bkgd->bqgrk', q, kv[:, :, 0]).astype(jnp.float32) * ATTN_SCALE
    s = jnp.where(mask[None, :, None, None, :], s, NEG)
    p = jax.nn.softmax(s, axis=-1).astype(q.dtype)
    return jnp.einsum('bqgrk,bkgd->bqgrd', p, kv[:, :, 1])


def window_attend_prompt(q, rows):
    B, S = q.shape[:2]
    nb = S // WIN_BLK
    lk = WIN_BLK + WINDOW
    rp = jnp.pad(rows, ((0, 0), (WINDOW, 0), (0, 0), (0, 0), (0, 0)))
    qb = q.reshape((B, nb, WIN_BLK) + q.shape[2:]).swapaxes(0, 1)

    def blk(a):
        b, qq = a
        kv = lax.dynamic_slice_in_dim(rp, b * WIN_BLK, lk, axis=1)
        qpos = b * WIN_BLK + jnp.arange(WIN_BLK)
        kpos = b * WIN_BLK - WINDOW + jnp.arange(lk)
        return window_attend(qq, qpos, kv, kpos)

    o = lax.map(blk, (jnp.arange(nb), qb))
    return o.swapaxes(0, 1).reshape(q.shape)


def nsa_combine(gates, o_c, o_s, o_w, w_out):
    o = gates[:, :, 0, ..., None] * o_c + gates[:, :, 1, ..., None] * o_s + gates[:, :, 2, ..., None] * o_w
    B, n = o.shape[:2]
    return o.reshape(B, n, NSA_Q_W) @ w_out


def nsa_prompt(h, w_in, w_out, pe_k, pe_v, phi_k, phi_v):
    B, S, _ = h.shape
    pos = jnp.arange(S, dtype=jnp.int32)
    q, rows_c, rows_s, rows_w, gates = nsa_project(h, pos, w_in)
    kc, vc = compress(rows_c, pe_k, pe_v, phi_k, phi_v)
    o_c, p_c = cmp_attend(q, pos, kc, vc)
    idx = select_blocks(p_c, pos, S // SEL_BLK)
    o_s = sel_attend_blocked(q, pos, idx, rows_s, SEL_QBLK)
    o_w = window_attend_prompt(q, rows_w)
    y = nsa_combine(gates, o_c, o_s, o_w, w_out)
    return y, rows_c, rows_s, rows_w[:, -min(WINDOW, S):]


def gather_pages(pool, page_table):
    g = pool[page_table]
    return g.reshape((g.shape[0], g.shape[1] * g.shape[2]) + g.shape[3:])


def nsa_sample(h, pos, pool_c, pool_s, buf_w, page_table, w_in, w_out, pe_k, pe_v, phi_k, phi_v):
    B, n, _ = h.shape
    past_len = page_table.shape[1] * PAGE_SIZE
    q, rows_c, rows_s, rows_w, gates = nsa_project(h, pos, w_in)
    full_c = jnp.concatenate([gather_pages(pool_c, page_table).astype(rows_c.dtype), rows_c], axis=1)
    kc, vc = compress(full_c, pe_k, pe_v, phi_k, phi_v)
    o_c, p_c = cmp_attend(q, pos, kc, vc)
    L = past_len + n
    Lp = -(-L // SEL_BLK) * SEL_BLK
    full_s = jnp.concatenate([gather_pages(pool_s, page_table).astype(rows_s.dtype), rows_s,
                              jnp.zeros((B, Lp - L) + rows_s.shape[2:], rows_s.dtype)], axis=1)
    idx = select_blocks(p_c, pos, Lp // SEL_BLK)
    o_s = sel_attend_blocked(q, pos, idx, full_s, 1)
    w_buf = buf_w.shape[1]
    kv_w = jnp.concatenate([buf_w.astype(rows_w.dtype), rows_w], axis=1)
    kpos = jnp.concatenate([past_len - w_buf + jnp.arange(w_buf, dtype=jnp.int32), pos])
    o_w = window_attend(q, pos, kv_w, kpos)
    y = nsa_combine(gates, o_c, o_s, o_w, w_out)
    keep = min(WINDOW, w_buf + n)
    return y, rows_c, rows_s, kv_w[:, -keep:]


def conv_ffn(h, prev, w_in, conv_w, conv_b, w_out):
    n = h.shape[1]
    up = h @ w_in
    full = jnp.concatenate([prev.astype(up.dtype), up], axis=1)
    c = conv_b + full[:, 0:n] * conv_w[0]
    for j in range(1, CONV_W):
        c = c + full[:, j:j + n] * conv_w[j]
    a, b = jnp.split(c, 2, axis=-1)
    y = (jax.nn.gelu(a) * b) @ w_out
    return y, full[:, n:]


def setup_inputs(seed: int = 0) -> dict:
    key = jax.random.key(seed)
    keys = iter(jax.random.split(key, 32))

    def nrm(shape, scale):
        return jax.random.normal(next(keys), shape, jnp.float32) * scale

    n_pages = PAST_LEN // PAGE_SIZE
    n_pool = (DEC_BATCH * n_pages * 5 + 3) // 4
    w_buf = min(WINDOW, PAST_LEN)
    kv_row = (2, NSA_KV, HEAD_DIM)
    x_prompt = nrm((BATCH, SEQ, D_MODEL), 1.0)
    x_sample = nrm((DEC_BATCH, DEC_SEQ, D_MODEL), 1.0)
    state_ret = nrm((N_EVEN, DEC_BATCH, RET_HEADS, RET_DK, RET_DV), 0.3)
    cache_cmp = nrm((N_ODD, n_pool, PAGE_SIZE) + kv_row, 1.0)
    cache_sel = nrm((N_ODD, n_pool, PAGE_SIZE) + kv_row, 1.0)
    state_win = nrm((N_ODD, DEC_BATCH, w_buf) + kv_row, 1.0)
    state_ffn = nrm((DEPTH, DEC_BATCH, CONV_W - 1, 2 * D_FF), 1.0)
    page_table = jax.random.permutation(next(keys), n_pool)[:DEC_BATCH * n_pages].reshape(DEC_BATCH, n_pages).astype(jnp.int32)
    return {
        'x_prompt': x_prompt,
        'x_sample': x_sample,
        'state_ret': state_ret,
        'cache_cmp': cache_cmp,
        'cache_sel': cache_sel,
        'state_win': state_win,
        'state_ffn': state_ffn,
        'page_table': page_table,
        'norm_mix_pre': 1.0 + nrm((DEPTH, D_MODEL), 0.1),
        'norm_mix_post': 1.0 + nrm((DEPTH, D_MODEL), 0.1),
        'norm_ffn_pre': 1.0 + nrm((DEPTH, D_MODEL), 0.1),
        'norm_ffn_post': 1.0 + nrm((DEPTH, D_MODEL), 0.1),
        'e_w_in': nrm((N_EVEN, D_MODEL, EVEN_IN), D_MODEL ** -0.5),
        'e_w_out': nrm((N_EVEN, EVEN_OUT, D_MODEL), EVEN_OUT ** -0.5),
        'e_sg_ln_g': 1.0 + nrm((N_EVEN, SG_W), 0.1),
        'e_sg_ln_b': nrm((N_EVEN, SG_W), 0.1),
        'e_sg_w': nrm((N_EVEN, SG_GROUPS, SG_CHUNK, SG_CHUNK), SG_CHUNK ** -0.5),
        'e_sg_b': 1.0 + nrm((N_EVEN, SG_GROUPS, SG_CHUNK), 0.1),
        'o_w_in': nrm((N_ODD, D_MODEL, ODD_IN), D_MODEL ** -0.5),
        'o_w_out': nrm((N_ODD, NSA_Q_W, D_MODEL), NSA_Q_W ** -0.5),
        'o_pe_k': nrm((N_ODD, CMP_LEN, HEAD_DIM), 0.5),
        'o_pe_v': nrm((N_ODD, CMP_LEN, HEAD_DIM), 0.5),
        'o_phi_k': nrm((N_ODD, CMP_LEN, HEAD_DIM, HEAD_DIM), (CMP_LEN * HEAD_DIM) ** -0.5),
        'o_phi_v': nrm((N_ODD, CMP_LEN, HEAD_DIM, HEAD_DIM), (CMP_LEN * HEAD_DIM) ** -0.5),
        'f_w_in': nrm((DEPTH, D_MODEL, 2 * D_FF), D_MODEL ** -0.5),
        'f_conv_w': nrm((DEPTH, CONV_W, 2 * D_FF), CONV_W ** -0.5),
        'f_conv_b': nrm((DEPTH, 2 * D_FF), 0.1),
        'f_w_out': nrm((DEPTH, D_FF, D_MODEL), D_FF ** -0.5),
    }


def reference(x_prompt, x_sample, state_ret, cache_cmp, cache_sel, state_win, state_ffn, page_table,
              norm_mix_pre, norm_mix_post, norm_ffn_pre, norm_ffn_post,
              e_w_in, e_w_out, e_sg_ln_g, e_sg_ln_b, e_sg_w, e_sg_b,
              o_w_in, o_w_out, o_pe_k, o_pe_v, o_phi_k, o_phi_v,
              f_w_in, f_conv_w, f_conv_b, f_w_out):
    xp, xs = x_prompt, x_sample
    B, S = xp.shape[:2]
    DB, n = xs.shape[:2]
    past_len = page_table.shape[1] * PAGE_SIZE
    pos_p = jnp.arange(S, dtype=jnp.int32)
    pos_s = past_len + jnp.arange(n, dtype=jnp.int32)
    ret_p, ret_s, sgv_s = [], [], []
    cmp_p, cmp_s, sel_p, sel_s, win_p, win_s = [], [], [], [], [], []
    ffn_p, ffn_s = [], []
    for l in range(DEPTH):
        i = l // 2
        hp = rms_norm(xp, norm_mix_pre[l])
        hs = rms_norm(xs, norm_mix_pre[l])
        if l % 2 == 0:
            mp, sp, _ = even_mixer(hp, pos_p, jnp.zeros((B, RET_HEADS, RET_DK, RET_DV), jnp.float32),
                                   e_w_in[i], e_w_out[i], e_sg_ln_g[i], e_sg_ln_b[i], e_sg_w[i], e_sg_b[i])
            ms, ss, vs = even_mixer(hs, pos_s, state_ret[i].astype(jnp.float32),
                                    e_w_in[i], e_w_out[i], e_sg_ln_g[i], e_sg_ln_b[i], e_sg_w[i], e_sg_b[i])
            ret_p.append(sp.astype(xp.dtype))
            ret_s.append(ss.astype(xs.dtype))
            sgv_s.append(vs)
        else:
            mp, rcp, rsp, bwp = nsa_prompt(hp, o_w_in[i], o_w_out[i], o_pe_k[i], o_pe_v[i], o_phi_k[i], o_phi_v[i])
            ms, rcs, rss, bws = nsa_sample(hs, pos_s, cache_cmp[i], cache_sel[i], state_win[i], page_table,
                                           o_w_in[i], o_w_out[i], o_pe_k[i], o_pe_v[i], o_phi_k[i], o_phi_v[i])
            cmp_p.append(rcp)
            cmp_s.append(rcs)
            sel_p.append(rsp)
            sel_s.append(rss)
            win_p.append(bwp)
            win_s.append(bws)
        xp = xp + rms_norm(mp, norm_mix_post[l])
        xs = xs + rms_norm(ms, norm_mix_post[l])
        fp, cp = conv_ffn(rms_norm(xp, norm_ffn_pre[l]), jnp.zeros((B, CONV_W - 1, 2 * D_FF), xp.dtype),
                          f_w_in[l], f_conv_w[l], f_conv_b[l], f_w_out[l])
        fs, cs = conv_ffn(rms_norm(xs, norm_ffn_pre[l]), state_ffn[l],
                          f_w_in[l], f_conv_w[l], f_conv_b[l], f_w_out[l])
        ffn_p.append(cp)
        ffn_s.append(cs)
        xp = xp + rms_norm(fp, norm_ffn_post[l])
        xs = xs + rms_norm(fs, norm_ffn_post[l])
    return (xp, xs, jnp.stack(ret_p), jnp.stack(ret_s), jnp.stack(sgv_s),
            jnp.stack(cmp_p), jnp.stack(cmp_s), jnp.stack(sel_p), jnp.stack(sel_s),
            jnp.stack(win_p), jnp.stack(win_s), jnp.stack(ffn_p), jnp.stack(ffn_s))
```

```python
import functools

import jax
import jax.numpy as jnp
from jax import lax
from jax.experimental import pallas as pl
from jax.experimental.pallas import tpu as pltpu

F32 = jnp.float32
BF16 = jnp.bfloat16

EPS = 1e-6
NEG = -1e30
FORCE = 1e9

LANE = 128
VMEM_LIMIT = 56 * 1024 * 1024

RET_CHUNK = 128
RET_THETA = 10000.0
SG_CHUNK = 128
HEAD_DIM = 64
NSA_KV = 4
NSA_REP = 4
CMP_LEN = 32
CMP_STRIDE = 16
SEL_BLK = 64
SEL_TOPK = 16
WINDOW = 512
ROPE_DIM = HEAD_DIM // 4
ROPE_THETA = 500000.0
ATTN_SCALE = HEAD_DIM ** -0.5
CONV_W = 3

SAMPLE_PAD = 8

ROPE_NONE, ROPE_RET_Q, ROPE_RET_K, ROPE_NSA = 0, 1, 2, 3


def _params(*sem):
    return pltpu.CompilerParams(dimension_semantics=sem, vmem_limit_bytes=VMEM_LIMIT)


def _rms(x, g):
    return x * lax.rsqrt(jnp.mean(x * x, -1, keepdims=True) + EPS) * g


def _col_chunk(n):
    for c in (512, 384, 256, 128):
        if n % c == 0:
            return c
    raise ValueError(n)


def _proj_in_body(x_ref, g_ref, w_ref, tc_ref, ta_ref, tb_ref, *o_refs, modes, kscale, outs):
    h = _rms(x_ref[...], g_ref[...]).astype(BF16)
    n = w_ref.shape[1]
    cw = _col_chunk(n)
    for c0 in range(0, n, cw):
        y = jnp.dot(h, w_ref[:, c0:c0 + cw], preferred_element_type=F32)
        for j in range(cw // LANE):
            col = c0 + j * LANE
            blk = y[:, j * LANE:(j + 1) * LANE]
            mode = modes[col // LANE]
            if mode in (ROPE_RET_Q, ROPE_RET_K):
                blk = blk * tc_ref[...] + pltpu.roll(blk, LANE // 2, 1) * ta_ref[...]
                if mode == ROPE_RET_K:
                    blk = blk * kscale
            elif mode == ROPE_NSA:
                blk = (blk * tc_ref[...] + pltpu.roll(blk, LANE - ROPE_DIM // 2, 1) * ta_ref[...]
                       + pltpu.roll(blk, ROPE_DIM // 2, 1) * tb_ref[...])
            for o_ref, (oc, ow, osc) in zip(o_refs, outs):
                if oc <= col < oc + ow:
                    v = blk if osc == 1.0 else blk * osc
                    o_ref[:, col - oc:col - oc + LANE] = v.astype(o_ref.dtype)


def proj_in(x, g, w, tabs, modes, outs, kscale=1.0, tm=512):
    m, d = x.shape
    n = w.shape[1]
    tm = min(tm, m)
    nt = tabs[0].shape[0] // tm
    tab_spec = pl.BlockSpec((tm, LANE), lambda i: (i % nt, 0))
    return pl.pallas_call(
        functools.partial(_proj_in_body, modes=tuple(modes), kscale=kscale,
                          outs=tuple(o[:3] for o in outs)),
        grid=(m // tm,),
        in_specs=[pl.BlockSpec((tm, d), lambda i: (i, 0)),
                  pl.BlockSpec((1, d), lambda i: (0, 0)),
                  pl.BlockSpec((d, n), lambda i: (0, 0)),
                  tab_spec, tab_spec, tab_spec],
        out_specs=[pl.BlockSpec((tm, o[1]), lambda i: (i, 0)) for o in outs],
        out_shape=[jax.ShapeDtypeStruct((m, o[1]), o[3]) for o in outs],
        compiler_params=_params("parallel"),
        name="proj_in",
    )(x, g.reshape(1, d), w, *tabs)


def _ret_rope_tabs(pos):
    inv = 1.0 / (RET_THETA ** jnp.linspace(0.0, 1.0, LANE // 2))
    ang = pos.astype(F32)[:, None] * inv[None, :]
    cos, sin = jnp.cos(ang), jnp.sin(ang)
    return jnp.concatenate([cos, cos], -1), jnp.concatenate([-sin, sin], -1)


def _nsa_rope_tabs(pos):
    hr = ROPE_DIM // 2
    inv = 1.0 / (ROPE_THETA ** (jnp.arange(0, ROPE_DIM, 2, dtype=F32) / ROPE_DIM))
    ang = pos.astype(F32)[:, None] * inv[None, :]
    cos, sin = jnp.cos(ang), jnp.sin(ang)
    n = pos.shape[0]
    rest = HEAD_DIM - ROPE_DIM
    c = jnp.concatenate([cos, cos, jnp.ones((n, rest), F32)], -1)
    a = jnp.concatenate([-sin, jnp.zeros((n, hr + rest), F32)], -1)
    b = jnp.concatenate([jnp.zeros((n, hr), F32), sin, jnp.zeros((n, rest), F32)], -1)
    rep = LANE // HEAD_DIM
    return jnp.tile(c, (1, rep)), jnp.tile(a, (1, rep)), jnp.tile(b, (1, rep))


def _ret_tabs(c, n_valid, heads):
    log_g = jnp.log(1.0 - 2.0 ** (-5.0 - jnp.arange(heads, dtype=F32)))
    idx = jnp.arange(c, dtype=F32)
    diff = idx[:, None] - idx[None, :]
    ok = (diff >= 0) & (idx[None, :] < n_valid)
    dmask = jnp.where(ok, jnp.exp(log_g[:, None, None] * jnp.maximum(diff, 0.0)), 0.0)
    qdec = jnp.exp(log_g[:, None] * (idx + 1.0))
    kdec = jnp.where(idx < n_valid, jnp.exp(log_g[:, None] * (n_valid - 1.0 - idx)), 0.0)
    cdec = jnp.exp(log_g * n_valid)
    bc = lambda t: jnp.broadcast_to(t[..., None], t.shape + (LANE,))
    dm = dmask if c == LANE else jnp.pad(dmask, ((0, 0), (0, 0), (0, LANE - c)))
    return dm, bc(qdec), bc(kdec), jnp.broadcast_to(cdec[:, None, None], (heads, 8, LANE))


def _layer_norm_rows(x, g, b):
    xc = x - jnp.mean(x, -1, keepdims=True)
    return xc * lax.rsqrt(jnp.mean(xc * xc, -1, keepdims=True) + EPS) * g + b


def _rms_unit(x):
    return x * lax.rsqrt(jnp.mean(x * x, -1, keepdims=True) + EPS)


def _even_seq_body(q_ref, k_ref, v_ref, g_ref, u_ref, sv_ref, dm_ref, qd_ref, kd_ref, cd_ref,
                   lng_ref, lnb_ref, wm_ref, sgb_ref, o_ref, st_ref, s_scr, *, tq, heads, groups):
    t = pl.program_id(1)
    c = RET_CHUNK

    @pl.when(t == 0)
    def _():
        s_scr[...] = jnp.zeros(s_scr.shape, F32)

    for ci in range(tq // c):
        rows = slice(ci * c, (ci + 1) * c)
        for h in range(heads):
            cols = slice(h * LANE, (h + 1) * LANE)
            qc, kc, vc = q_ref[rows, cols], k_ref[rows, cols], v_ref[rows, cols]
            vb = vc.astype(BF16)
            s = s_scr[h]
            inner = lax.dot_general(qc.astype(BF16), kc.astype(BF16), (((1,), (1,)), ((), ())),
                                    preferred_element_type=F32) * dm_ref[h]
            o = (jnp.dot(inner.astype(BF16), vb, preferred_element_type=F32)
                 + jnp.dot((qc * qd_ref[h]).astype(BF16), s.astype(BF16), preferred_element_type=F32))
            s_scr[h] = s * cd_ref[h][0:1, :] + lax.dot_general(
                (kc * kd_ref[h]).astype(BF16), vb, (((0,), (0,)), ((), ())), preferred_element_type=F32)
            gg = g_ref[rows, cols]
            o_ref[rows, cols] = gg * jax.nn.sigmoid(gg) * _rms_unit(o)
        for gi in range(groups):
            cols = slice(gi * LANE, (gi + 1) * LANE)
            svn = _layer_norm_rows(sv_ref[rows, cols], lng_ref[:, cols], lnb_ref[:, cols])
            mixed = jnp.dot(wm_ref[gi], svn.astype(BF16), preferred_element_type=F32) + sgb_ref[gi]
            o_ref[rows, heads * LANE + gi * LANE:heads * LANE + (gi + 1) * LANE] = u_ref[rows, cols] * mixed

    @pl.when(t == pl.num_programs(1) - 1)
    def _():
        st_ref[...] = s_scr[...]


def even_seq(p, nseq, seq_len, ln_g, ln_b, sg_w, sg_b, heads, groups, tq=512):
    m = p.shape[0]
    w = heads * LANE
    c = RET_CHUNK
    nt = seq_len // tq
    dm, qd, kd, cd = _ret_tabs(c, c, heads)
    wm = jnp.tril(sg_w[:, :c, :c]).astype(BF16)
    sgb = jnp.broadcast_to(sg_b[:, :c, None], (groups, c, LANE))
    part = lambda j: pl.BlockSpec((tq, w), lambda b, t: (b * nt + t, j))
    full = lambda a: pl.BlockSpec(a.shape, lambda b, t: (0,) * a.ndim)
    return pl.pallas_call(
        functools.partial(_even_seq_body, tq=tq, heads=heads, groups=groups),
        grid=(nseq, nt),
        in_specs=[part(j) for j in range(6)] + [full(dm), full(qd), full(kd), full(cd),
                                                pl.BlockSpec((1, w), lambda b, t: (0, 0)),
                                                pl.BlockSpec((1, w), lambda b, t: (0, 0)),
                                                full(wm), full(sgb)],
        out_specs=[pl.BlockSpec((tq, 2 * w), lambda b, t: (b * nt + t, 0)),
                   pl.BlockSpec((None, heads, LANE, LANE), lambda b, t: (b, 0, 0, 0))],
        out_shape=[jax.ShapeDtypeStruct((m, 2 * w), F32),
                   jax.ShapeDtypeStruct((nseq, heads, LANE, LANE), F32)],
        scratch_shapes=[pltpu.VMEM((heads, LANE, LANE), F32)],
        compiler_params=_params("arbitrary", "arbitrary"),
        name="even_seq",
    )(p, p, p, p, p, p, dm, qd, kd, cd, ln_g.reshape(1, w), ln_b.reshape(1, w), wm, sgb)


def _even_seq_sample_body(q_ref, k_ref, v_ref, g_ref, u_ref, sv_ref, s0_ref, dm_ref, qd_ref, kd_ref, cd_ref,
                          lng_ref, lnb_ref, wm_ref, sgb_ref, o_ref, st_ref, svn_ref, *, nb, n, heads, groups):
    np_ = SAMPLE_PAD

    def one(b, carry):
        rows = pl.ds(pl.multiple_of(b * np_, np_), np_)
        for h in range(heads):
            cols = slice(h * LANE, (h + 1) * LANE)
            q, k, v = q_ref[rows, cols], k_ref[rows, cols], v_ref[rows, cols]
            s = s0_ref[b, h]
            o = jnp.dot((q * qd_ref[h]).astype(BF16), s.astype(BF16), preferred_element_type=F32)
            dm = dm_ref[h]
            for j in range(n):
                inner = jnp.sum(q * k[j:j + 1, :], axis=-1, keepdims=True)
                o = o + (inner * dm[:, j:j + 1]) * v[j:j + 1, :]
            st_ref[b, h] = s * cd_ref[h][0:1, :] + lax.dot_general(
                (k * kd_ref[h]).astype(BF16), v.astype(BF16), (((0,), (0,)), ((), ())),
                preferred_element_type=F32)
            gg = g_ref[rows, cols]
            o_ref[rows, cols] = gg * jax.nn.sigmoid(gg) * _rms_unit(o)
        for gi in range(groups):
            cols = slice(gi * LANE, (gi + 1) * LANE)
            svn = _layer_norm_rows(sv_ref[rows, cols], lng_ref[:, cols], lnb_ref[:, cols])
            svn_ref[rows, cols] = svn
            wm = wm_ref[gi]
            mixed = sgb_ref[gi]
            for j in range(n):
                mixed = mixed + wm[:, j:j + 1] * svn[j:j + 1, :]
            o_ref[rows, heads * LANE + gi * LANE:heads * LANE + (gi + 1) * LANE] = u_ref[rows, cols] * mixed
        return carry

    lax.fori_loop(0, nb, one, 0)


def even_seq_sample(p, s0, n, ln_g, ln_b, sg_w, sg_b, heads, groups, nb=8):
    m = p.shape[0]
    db = m // SAMPLE_PAD
    w = heads * LANE
    dm, qd, kd, cd = _ret_tabs(SAMPLE_PAD, n, heads)
    wm = jnp.pad(jnp.tril(sg_w[:, :n, :n]), ((0, 0), (0, SAMPLE_PAD - n), (0, LANE - n)))
    sgb = jnp.broadcast_to(jnp.pad(sg_b[:, :n], ((0, 0), (0, SAMPLE_PAD - n)))[:, :, None], (groups, SAMPLE_PAD, LANE))
    rows = nb * SAMPLE_PAD
    part = lambda j: pl.BlockSpec((rows, w), lambda i: (i, j))
    full = lambda a: pl.BlockSpec(a.shape, lambda i: (0,) * a.ndim)
    st_spec = pl.BlockSpec((nb, heads, LANE, LANE), lambda i: (i, 0, 0, 0))
    return pl.pallas_call(
        functools.partial(_even_seq_sample_body, nb=nb, n=n, heads=heads, groups=groups),
        grid=(db // nb,),
        in_specs=[part(j) for j in range(6)] + [st_spec, full(dm), full(qd), full(kd), full(cd),
                                                pl.BlockSpec((1, w), lambda i: (0, 0)),
                                                pl.BlockSpec((1, w), lambda i: (0, 0)),
                                                full(wm), full(sgb)],
        out_specs=[pl.BlockSpec((rows, 2 * w), lambda i: (i, 0)), st_spec,
                   pl.BlockSpec((rows, w), lambda i: (i, 0))],
        out_shape=[jax.ShapeDtypeStruct((m, 2 * w), F32),
                   jax.ShapeDtypeStruct(s0.shape, F32),
                   jax.ShapeDtypeStruct((m, w), F32)],
        compiler_params=_params("parallel"),
        name="even_seq_sample",
    )(p, p, p, p, p, p, s0, dm, qd, kd, cd, ln_g.reshape(1, w), ln_b.reshape(1, w), wm, sgb)


def _proj_out_body(*refs, n_a):
    w_ref, g_ref, x_ref, o_ref = refs[n_a:]
    a = refs[0][...]
    for r in refs[1:n_a]:
        a = a + r[...]
    y = jnp.dot(a.astype(BF16), w_ref[...], preferred_element_type=F32)
    o_ref[...] = x_ref[...] + _rms(y, g_ref[...])


def proj_out(a_list, w, g, x, tm=512):
    m, d = x.shape
    k = w.shape[0]
    tm = min(tm, m)
    row = lambda i: (i, 0)
    return pl.pallas_call(
        functools.partial(_proj_out_body, n_a=len(a_list)),
        grid=(m // tm,),
        in_specs=[pl.BlockSpec((tm, k), row) for _ in a_list]
        + [pl.BlockSpec((k, d), lambda i: (0, 0)),
           pl.BlockSpec((1, d), lambda i: (0, 0)),
           pl.BlockSpec((tm, d), row)],
        out_specs=pl.BlockSpec((tm, d), row),
        out_shape=jax.ShapeDtypeStruct((m, d), F32),
        compiler_params=_params("parallel"),
        name="proj_out",
    )(*a_list, w, g.reshape(1, d), x)


FFN_HALO = 16


def _ffn_body(*refs, tm, tiles_per_seq, sample, nf_static):
    if sample:
        (x_ref, gpre_ref, wa_ref, wb_ref, cwa_ref, cwb_ref, cba_ref, cbb_ref, wo_ref, gpost_ref,
         e1a_ref, e1b_ref, e2a_ref, e2b_ref, o_ref, st_ref, h_scr, upa_scr, upb_scr, acc_scr) = refs
    else:
        (x_ref, halo_ref, gpre_ref, wa_ref, wb_ref, cwa_ref, cwb_ref, cba_ref, cbb_ref, wo_ref, gpost_ref,
         o_ref, st_ref, h_scr, upa_scr, upb_scr, acc_scr) = refs
    i = pl.program_id(0)
    f = pl.program_id(1)
    nf = pl.num_programs(1)
    hl = FFN_HALO

    @pl.when(f == 0)
    def _():
        if sample:
            h_scr[0:hl, :] = jnp.zeros((hl, h_scr.shape[1]), BF16)
        else:
            hh = _rms(halo_ref[...], gpre_ref[...])
            h_scr[0:hl, :] = jnp.where(i % tiles_per_seq == 0, 0.0, hh).astype(BF16)
        h_scr[hl:, :] = _rms(x_ref[...], gpre_ref[...]).astype(BF16)

    h = h_scr[...]
    upa_scr[...] = jnp.dot(h, wa_ref[...], preferred_element_type=F32)
    upb_scr[...] = jnp.dot(h, wb_ref[...], preferred_element_type=F32)

    if sample:
        t = lax.broadcasted_iota(jnp.int32, (tm, 1), 0) % SAMPLE_PAD
        m1 = t >= 1
        m2 = t >= 2

    def conv(up_scr, cw_ref, cb_ref, e1_ref, e2_ref):
        s2 = up_scr[pl.ds(hl - 2, tm), :]
        s1 = up_scr[pl.ds(hl - 1, tm), :]
        s0 = up_scr[pl.ds(hl, tm), :]
        if sample:
            s2 = jnp.where(m2, s2, 0.0) + e2_ref[...]
            s1 = jnp.where(m1, s1, 0.0) + e1_ref[...]
        return cb_ref[...] + s2 * cw_ref[0:1, :] + s1 * cw_ref[1:2, :] + s0 * cw_ref[2:3, :]

    a = conv(upa_scr, cwa_ref, cba_ref, e1a_ref if sample else None, e2a_ref if sample else None)
    b = conv(upb_scr, cwb_ref, cbb_ref, e1b_ref if sample else None, e2b_ref if sample else None)
    act = (jax.nn.gelu(a) * b).astype(BF16)
    contrib = jnp.dot(act, wo_ref[...], preferred_element_type=F32)

    @pl.when(f == 0)
    def _():
        acc_scr[...] = contrib

    @pl.when(f > 0)
    def _():
        acc_scr[...] += contrib

    @pl.when(f == nf - 1)
    def _():
        o_ref[...] = x_ref[...] + _rms(acc_scr[...], gpost_ref[...])

    fw = upa_scr.shape[1]
    rows = st_ref.shape[0]
    last = True if sample else (i % tiles_per_seq == tiles_per_seq - 1)
    for j in range(nf_static):
        @pl.when(jnp.logical_and(f == j, last))
        def _(j=j):
            st_ref[:, j * fw:(j + 1) * fw] = upa_scr[pl.ds(hl + tm - rows, rows), :]
            st_ref[:, (nf_static + j) * fw:(nf_static + j + 1) * fw] = upb_scr[pl.ds(hl + tm - rows, rows), :]


def conv_ffn(x, seq_len, gpre, w_in, conv_w, conv_b, w_out, gpost, prev=None, tm=512):
    m, d = x.shape
    ff = w_out.shape[0]
    fw = 1408 if ff % 1408 == 0 else ff
    nf = ff // fw
    sample = prev is not None
    tm = min(tm, m)
    tps = max(seq_len // tm, 1)
    row = lambda i, f: (i, 0)
    const = lambda i, f: (0, 0)
    cola = lambda i, f: (0, f)
    colb = lambda i, f: (0, nf + f)
    in_specs = [pl.BlockSpec((tm, d), row)]
    args = [x]
    if not sample:
        hb = tm // FFN_HALO
        in_specs.append(pl.BlockSpec((FFN_HALO, d), lambda i, f: (jnp.maximum(i * hb - 1, 0), 0)))
        args.append(x)
    in_specs += [pl.BlockSpec((1, d), const),
                 pl.BlockSpec((d, fw), cola), pl.BlockSpec((d, fw), colb),
                 pl.BlockSpec((CONV_W, fw), cola), pl.BlockSpec((CONV_W, fw), colb),
                 pl.BlockSpec((1, fw), cola), pl.BlockSpec((1, fw), colb),
                 pl.BlockSpec((fw, d), lambda i, f: (f, 0)),
                 pl.BlockSpec((1, d), const)]
    args += [gpre.reshape(1, d), w_in, w_in, conv_w, conv_w, conv_b.reshape(1, -1), conv_b.reshape(1, -1),
             w_out, gpost.reshape(1, d)]
    if sample:
        e1, e2 = prev
        in_specs += [pl.BlockSpec((tm, fw), lambda i, f: (i, f)), pl.BlockSpec((tm, fw), lambda i, f: (i, nf + f)),
                     pl.BlockSpec((tm, fw), lambda i, f: (i, f)), pl.BlockSpec((tm, fw), lambda i, f: (i, nf + f))]
        args += [e1, e1, e2, e2]
        st_shape = jax.ShapeDtypeStruct((m, 2 * ff), F32)
        st_spec = pl.BlockSpec((tm, 2 * ff), lambda i, f: (i, 0))
    else:
        nseq = m // seq_len
        st_shape = jax.ShapeDtypeStruct((nseq * 8, 2 * ff), F32)
        st_spec = pl.BlockSpec((8, 2 * ff), lambda i, f: (i // tps, 0))
    return pl.pallas_call(
        functools.partial(_ffn_body, tm=tm, tiles_per_seq=tps, sample=sample, nf_static=nf),
        grid=(m // tm, nf),
        in_specs=in_specs,
        out_specs=[pl.BlockSpec((tm, d), row), st_spec],
        out_shape=[jax.ShapeDtypeStruct((m, d), F32), st_shape],
        scratch_shapes=[pltpu.VMEM((tm + FFN_HALO, d), BF16),
                        pltpu.VMEM((tm + FFN_HALO, fw), F32),
                        pltpu.VMEM((tm + FFN_HALO, fw), F32),
                        pltpu.VMEM((tm, d), F32)],
        compiler_params=_params("arbitrary", "arbitrary"),
        name="conv_ffn_sample" if sample else "conv_ffn",
    )(*args)


def _ffn_prev(prev):
    db, _, w = prev.shape
    z = jnp.zeros((db, SAMPLE_PAD, w), prev.dtype)
    e1 = z.at[:, 0].set(prev[:, 1])
    e2 = z.at[:, 0:2].set(prev)
    return e1.reshape(db * SAMPLE_PAD, w), e2.reshape(db * SAMPLE_PAD, w)


def _odd_modes():
    return [ROPE_NSA] * 8 + [ROPE_NSA, ROPE_NSA, ROPE_NONE, ROPE_NONE] * 3 + [ROPE_NONE]


def _odd_outs(lowp):
    kv = BF16 if lowp else F32
    return [(0, 1024, ATTN_SCALE, kv), (1024, 512, 1.0, F32), (1536, 512, 1.0, F32), (2048, 512, 1.0, F32),
            (2560, 128, 1.0, F32)] + ([(1536, 512, 1.0, BF16), (2048, 512, 1.0, BF16)] if lowp else [])


SUBS = CMP_LEN // CMP_STRIDE
ROW_W = 2 * NSA_KV * HEAD_DIM
SUB_W = CMP_STRIDE * ROW_W


def _compress_weights(phi_k, phi_v, pe_k, pe_v):
    def one(phi, pe):
        p4 = phi.reshape(SUBS, CMP_STRIDE, HEAD_DIM, HEAD_DIM)
        w = jnp.einsum('hlde,gG->lgdhGe', p4, jnp.eye(2, dtype=phi.dtype))
        w = w.reshape(CMP_STRIDE * 2 * HEAD_DIM, SUBS * 2 * HEAD_DIM)
        pr = jnp.broadcast_to(pe.reshape(SUBS, CMP_STRIDE, 1, HEAD_DIM), (SUBS, CMP_STRIDE, 2, HEAD_DIM))
        pr = jnp.pad(pr.reshape(SUBS, -1), ((0, 16 - SUBS), (0, 0)))
        return w, pr
    wk, pk = one(phi_k, pe_k)
    wv, pv = one(phi_v, pe_v)
    return jnp.stack([wk, wv]).astype(BF16), jnp.stack([pk, pv]).astype(BF16)


def _compress_body(*refs, n_x, n_prefetch=0):
    refs = refs[n_prefetch:]
    x_refs = refs[:n_x]
    w_ref, pe_ref, o_ref, ab_scr = refs[n_x:]
    nsub = o_ref.shape[0]
    half = LANE
    for j in range(ROW_W // LANE):
        pieces = []
        for l in range(CMP_STRIDE):
            lanes = slice(l * ROW_W + j * LANE, l * ROW_W + (j + 1) * LANE)
            if n_x == 1:
                pieces.append(x_refs[0][:, lanes])
            else:
                pieces.append(jnp.concatenate([r[:, lanes] for r in x_refs], axis=0))
        xj = jnp.concatenate(pieces, axis=1).astype(BF16)
        xe = jnp.concatenate([xj, pe_ref[j // 2]], axis=0)
        ab_scr[...] = jnp.dot(xe, w_ref[j // 2], preferred_element_type=F32)
        bias = ab_scr[nsub:nsub + 1, 0:half] + ab_scr[nsub + 1:nsub + 2, half:2 * half]
        o_ref[:, j * LANE:(j + 1) * LANE] = (
            ab_scr[0:nsub, 0:half] + ab_scr[pl.ds(1, nsub), half:2 * half] + bias).astype(o_ref.dtype)


def compress_prompt(rows, nseq, wts, pes):
    nsub = rows.shape[0] // nseq // CMP_STRIDE
    x = rows.reshape(nseq * nsub, SUB_W)
    return pl.pallas_call(
        functools.partial(_compress_body, n_x=1),
        grid=(nseq,),
        in_specs=[pl.BlockSpec((nsub, SUB_W), lambda b: (b, 0)),
                  pl.BlockSpec(wts.shape, lambda b: (0, 0, 0)),
                  pl.BlockSpec(pes.shape, lambda b: (0, 0, 0))],
        out_specs=pl.BlockSpec((nsub, ROW_W), lambda b: (b, 0)),
        out_shape=jax.ShapeDtypeStruct((nseq * nsub, ROW_W), BF16),
        scratch_shapes=[pltpu.VMEM((nsub + 16, 2 * LANE), F32)],
        compiler_params=_params("parallel"),
        name="compress_prompt",
    )(x, wts, pes)


def compress_pages(pool, layer, page_table, wts, pes, nb=2):
    db, n_pages = page_table.shape
    page = pool.shape[2]
    sub_pp = page // CMP_STRIDE
    x = pool.reshape(pool.shape[0], pool.shape[1], sub_pp, SUB_W)
    nsub = nb * n_pages * sub_pp
    specs = [pl.BlockSpec((None, None, sub_pp, SUB_W),
                          functools.partial(lambda i, pt, s, k: (layer, pt[(i * nb + s) * n_pages + k], 0, 0), s=s, k=k))
             for s in range(nb) for k in range(n_pages)]
    grid_spec = pltpu.PrefetchScalarGridSpec(
        num_scalar_prefetch=1,
        grid=(db // nb,),
        in_specs=specs + [pl.BlockSpec(wts.shape, lambda i, pt: (0, 0, 0)),
                          pl.BlockSpec(pes.shape, lambda i, pt: (0, 0, 0))],
        out_specs=pl.BlockSpec((nsub, ROW_W), lambda i, pt: (i, 0)),
        scratch_shapes=[pltpu.VMEM((nsub + 16, 2 * LANE), F32)],
    )
    return pl.pallas_call(
        functools.partial(_compress_body, n_x=nb * n_pages, n_prefetch=1),
        grid_spec=grid_spec,
        out_shape=jax.ShapeDtypeStruct((db * n_pages * sub_pp, ROW_W), BF16),
        compiler_params=_params("parallel"),
        name="compress_pages",
    )(page_table.reshape(-1), *([x] * (nb * n_pages)), wts, pes)


def _cover(nsub, nc, n_sel):
    c_start = jnp.arange(nsub) * CMP_STRIDE
    s_start = jnp.arange(SEL_BLK) * SEL_BLK
    ok = ((c_start[:, None] < s_start[None, :] + SEL_BLK) & (c_start[:, None] + CMP_LEN > s_start[None, :])
          & (jnp.arange(nsub)[:, None] < nc) & (jnp.arange(SEL_BLK)[None, :] < n_sel))
    return ok.astype(BF16)


def _expand(n_keys, tk):
    key = jnp.arange(n_keys).reshape(n_keys // tk, 1, tk)
    return (key // SEL_BLK == jnp.arange(SEL_BLK)[None, :, None]).astype(BF16)


def _masked_softmax(s, ok):
    sm = jnp.where(ok, s, NEG)
    ex = jnp.exp(sm - jnp.max(sm, -1, keepdims=True))
    return ex / jnp.sum(ex, -1, keepdims=True)


def _importance(psum, cov_ref, tpos, n_sel):
    hi = psum.astype(BF16)
    lo = (psum - hi.astype(F32)).astype(BF16)
    imp = (jnp.dot(hi, cov_ref[...], preferred_element_type=F32)
           + jnp.dot(lo, cov_ref[...], preferred_element_type=F32))
    jl = lax.broadcasted_iota(jnp.int32, imp.shape, 1)
    cur = tpos // SEL_BLK
    forced = (jl == 0) | (jl == cur) | (jl == cur - 1)
    causal = jl * SEL_BLK <= tpos
    imp = jnp.where(forced, FORCE, jnp.where(causal, imp, -1.0))
    return jnp.where(jl < n_sel, imp, -2.0)


def _topk_masks(imps):
    out = []
    for pair in range(len(imps) // 2):
        xt = jnp.concatenate([imps[2 * pair], imps[2 * pair + 1]], axis=1).T
        halves = []
        for hh in range(2):
            xa = xt[SEL_BLK * hh:SEL_BLK * (hh + 1)]
            jrow = lax.broadcasted_iota(jnp.int32, xa.shape, 0)
            cnt = jnp.zeros(xa.shape, F32)
            for i in range(SEL_BLK):
                xi = xa[i:i + 1, :]
                tie = jnp.where(jrow > i, 1.0, 0.0)
                cnt = cnt + jnp.where(xi > xa, 1.0, jnp.where(xi == xa, tie, 0.0))
            halves.append(jnp.where(cnt < SEL_TOPK, 1.0, 0.0))
        sel2 = jnp.concatenate(halves, axis=0).T
        out += [sel2[:, 0:SEL_BLK], sel2[:, SEL_BLK:2 * SEL_BLK]]
    return out


_NT = (((1,), (1,)), ((), ()))


def _nsa_prompt_body(q_ref, kc_ref, ks_ref, kw_ref, gt_ref, cov_ref, e_ref, o_ref, *, tq, tk, nc, n_sel):
    t0 = pl.program_id(1) * tq
    tpos = t0 + lax.broadcasted_iota(jnp.int32, (tq, 1), 0)
    sig = jax.nn.sigmoid(gt_ref[...])
    nsub = kc_ref.shape[0]
    rep, hd, kvw = NSA_REP, HEAD_DIM, NSA_KV * HEAD_DIM
    mcol = lax.broadcasted_iota(jnp.int32, (tq, nsub), 1)
    valid = (mcol * CMP_STRIDE + CMP_LEN - 1 <= tpos) & (mcol < nc)
    validf = jnp.where(valid, 1.0, 0.0)

    q_st, o_cmp, imps = [], [], []
    for g in range(NSA_KV):
        qg = jnp.concatenate([q_ref[:, (g * rep + r) * hd:(g * rep + r + 1) * hd] for r in range(rep)], axis=0)
        q_st.append(qg)
        s = lax.dot_general(qg, kc_ref[:, g * hd:(g + 1) * hd], _NT, preferred_element_type=F32)
        p = _masked_softmax(s.reshape(rep, tq, nsub), valid[None]) * validf[None]
        o_cmp.append(jnp.dot(p.reshape(rep * tq, nsub).astype(BF16), kc_ref[:, kvw + g * hd:kvw + (g + 1) * hd],
                             preferred_element_type=F32))
        imps.append(_importance(jnp.sum(p, axis=0), cov_ref, tpos, n_sel))
    sels = _topk_masks(imps)

    n_kt = (t0 + tq + tk - 1) // tk
    kstart = jnp.maximum(t0 - WINDOW, 0)
    wk = WINDOW + tq
    kpos_w = kstart + lax.broadcasted_iota(jnp.int32, (tq, wk), 1)
    ok_w = (kpos_w <= tpos) & (kpos_w > tpos - WINDOW)
    for g in range(NSA_KV):
        qg = q_st[g]
        selg = sels[g].astype(BF16)

        def step(kt, carry, g=g, qg=qg, selg=selg):
            m_i, l_i, acc = carry
            k0 = pl.multiple_of(kt * tk, tk)
            kb = ks_ref[pl.ds(k0, tk), g * hd:(g + 1) * hd]
            vb = ks_ref[pl.ds(k0, tk), kvw + g * hd:kvw + (g + 1) * hd]
            s = lax.dot_general(qg, kb, _NT, preferred_element_type=F32).reshape(rep, tq, tk)
            mf = jnp.dot(selg, e_ref[kt], preferred_element_type=F32)
            kpos = k0 + lax.broadcasted_iota(jnp.int32, (tq, tk), 1)
            ok = (mf > 0.5) & (kpos <= tpos)
            s = jnp.where(ok[None], s, NEG)
            m_new = jnp.maximum(m_i, jnp.max(s, -1, keepdims=True))
            alpha = jnp.exp(m_i - m_new)
            p = jnp.exp(s - m_new)
            l_new = alpha * l_i + jnp.sum(p, -1, keepdims=True)
            pv = jnp.dot(p.reshape(rep * tq, tk).astype(BF16), vb, preferred_element_type=F32)
            return m_new, l_new, alpha * acc + pv.reshape(rep, tq, hd)

        init = (jnp.full((rep, tq, 1), NEG, F32), jnp.zeros((rep, tq, 1), F32), jnp.zeros((rep, tq, hd), F32))
        _, l_s, acc_s = lax.fori_loop(0, n_kt, step, init)
        o_sel = acc_s / l_s

        kb = kw_ref[pl.ds(pl.multiple_of(kstart, tq), wk), g * hd:(g + 1) * hd]
        vb = kw_ref[pl.ds(pl.multiple_of(kstart, tq), wk), kvw + g * hd:kvw + (g + 1) * hd]
        s = lax.dot_general(qg, kb, _NT, preferred_element_type=F32).reshape(rep, tq, wk)
        p = _masked_softmax(s, ok_w[None])
        o_win = jnp.dot(p.reshape(rep * tq, wk).astype(BF16), vb, preferred_element_type=F32).reshape(rep, tq, hd)

        o_c = o_cmp[g].reshape(rep, tq, hd)
        for r in range(rep):
            h = g * rep + r
            nh = NSA_KV * rep
            o_ref[:, h * hd:(h + 1) * hd] = (sig[:, h:h + 1] * o_c[r] + sig[:, nh + h:nh + h + 1] * o_sel[r]
                                            + sig[:, 2 * nh + h:2 * nh + h + 1] * o_win[r])


def nsa_prompt(q, kcvc, ks, kw, gates, nseq, seq_len, tq=128, tk=512):
    m = q.shape[0]
    nsub = kcvc.shape[0] // nseq
    nc = (seq_len - CMP_LEN) // CMP_STRIDE + 1
    n_sel = seq_len // SEL_BLK
    assert n_sel <= SEL_BLK and seq_len >= WINDOW + tq and seq_len % tk == 0
    nq = seq_len // tq
    cov = _cover(nsub, nc, n_sel)
    exp = _expand(seq_len, tk)
    tile = lambda w: pl.BlockSpec((tq, w), lambda b, t: (b * nq + t, 0))
    seq = lambda r, w: pl.BlockSpec((r, w), lambda b, t: (b, 0))
    return pl.pallas_call(
        functools.partial(_nsa_prompt_body, tq=tq, tk=tk, nc=nc, n_sel=n_sel),
        grid=(nseq, nq),
        in_specs=[tile(q.shape[1]), seq(nsub, ROW_W), seq(seq_len, ROW_W), seq(seq_len, ROW_W), tile(LANE),
                  pl.BlockSpec(cov.shape, lambda b, t: (0, 0)),
                  pl.BlockSpec(exp.shape, lambda b, t: (0, 0, 0))],
        out_specs=tile(q.shape[1]),
        out_shape=jax.ShapeDtypeStruct((m, q.shape[1]), F32),
        compiler_params=_params("parallel", "parallel"),
        name="nsa_prompt",
    )(q, kcvc, ks, kw, gates, cov, exp)


def _joint_softmax(s_a, ok_a, s_b, ok_b):
    s_a = jnp.where(ok_a, s_a, NEG)
    s_b = jnp.where(ok_b, s_b, NEG)
    mx = jnp.maximum(jnp.max(s_a, -1, keepdims=True), jnp.max(s_b, -1, keepdims=True))
    e_a = jnp.exp(s_a - mx)
    e_b = jnp.exp(s_b - mx)
    return e_a, e_b, jnp.sum(e_a, -1, keepdims=True) + jnp.sum(e_b, -1, keepdims=True)


def _nsa_sample_body(pt_ref, q_ref, kc_ref, *rest, n, n_pages, past_len, nc, n_sel):
    page_refs = rest[:n_pages]
    ns_ref, nw_ref, win_ref, gt_ref, cov_ref, e_ref, o_ref, wo_ref = rest[n_pages:]
    np_, rep, hd, kvw, nkv = SAMPLE_PAD, NSA_REP, HEAD_DIM, NSA_KV * HEAD_DIM, NSA_KV
    tpos = past_len + lax.broadcasted_iota(jnp.int32, (np_, 1), 0)
    sig = jax.nn.sigmoid(gt_ref[...])
    q = q_ref[...]

    blocks = []
    for g in range(nkv):
        for r in range(rep):
            h = g * rep + r
            parts = [q[:, h * hd:(h + 1) * hd]]
            if g > 0:
                parts.insert(0, jnp.zeros((np_, g * hd), F32))
            if g < nkv - 1:
                parts.append(jnp.zeros((np_, (nkv - 1 - g) * hd), F32))
            blocks.append(jnp.concatenate(parts, axis=1))
    qbd = jnp.concatenate(blocks, axis=0).astype(BF16)
    nrow = nkv * rep * np_

    def grp(x):
        return x.reshape(nkv, rep, np_, x.shape[-1])

    def scores(keys):
        return grp(lax.dot_general(qbd, keys, _NT, preferred_element_type=F32))

    new_lane = lax.broadcasted_iota(jnp.int32, (np_, np_), 1)
    new_pos = past_len + new_lane

    nsub = kc_ref.shape[0]
    mcol = lax.broadcasted_iota(jnp.int32, (np_, nsub), 1)
    valid = (mcol * CMP_STRIDE + CMP_LEN - 1 <= tpos) & (mcol < nc)
    p = _masked_softmax(scores(kc_ref[:, 0:kvw]), valid[None, None]) * jnp.where(valid, 1.0, 0.0)[None, None]
    o_cmp = jnp.dot(p.reshape(nrow, nsub).astype(BF16), kc_ref[:, kvw:2 * kvw], preferred_element_type=F32)
    sels = _topk_masks([_importance(jnp.sum(p[g], axis=0), cov_ref, tpos, n_sel) for g in range(nkv)])

    k_all = jnp.concatenate([pr[:, 0:kvw] for pr in page_refs], axis=0).astype(BF16)
    v_all = jnp.concatenate([pr[:, kvw:2 * kvw] for pr in page_refs], axis=0).astype(BF16)
    kpos = lax.broadcasted_iota(jnp.int32, (np_, past_len), 1)
    ok_c = jnp.stack([jnp.where(kpos <= tpos, jnp.dot(sels[g].astype(BF16), e_ref[0], preferred_element_type=F32), 0.0)
                      for g in range(nkv)]) > 0.5
    new_ok = (new_pos <= tpos) & (new_lane < n)
    ok_n = jnp.stack([jnp.where(new_ok, jnp.concatenate(
        [sels[g][:, (past_len + i) // SEL_BLK:(past_len + i) // SEL_BLK + 1] for i in range(np_)], axis=1), 0.0)
        for g in range(nkv)]) > 0.5
    ns = ns_ref[...]
    e_c, e_n, l_s = _joint_softmax(scores(k_all), ok_c[:, None], scores(ns[:, 0:kvw].astype(BF16)), ok_n[:, None])
    o_sel = jnp.dot(e_c.reshape(nrow, past_len).astype(BF16), v_all, preferred_element_type=F32)
    e_n = e_n.reshape(nrow, np_)
    for i in range(n):
        o_sel = o_sel + e_n[:, i:i + 1] * ns[i:i + 1, kvw:2 * kvw]
    o_sel = o_sel / l_s.reshape(nrow, 1)

    w_buf = win_ref.shape[0]
    kpos_w = past_len - w_buf + lax.broadcasted_iota(jnp.int32, (np_, w_buf), 1)
    ok_w = (kpos_w <= tpos) & (kpos_w > tpos - WINDOW)
    ok_wn = (new_pos <= tpos) & (new_pos > tpos - WINDOW) & (new_lane < n)
    nw = nw_ref[...]
    e_w, e_wn, l_w = _joint_softmax(scores(win_ref[:, 0:kvw].astype(BF16)), ok_w[None, None],
                                    scores(nw[:, 0:kvw].astype(BF16)), ok_wn[None, None])
    o_win = jnp.dot(e_w.reshape(nrow, w_buf).astype(BF16), win_ref[:, kvw:2 * kvw].astype(BF16),
                    preferred_element_type=F32)
    e_wn = e_wn.reshape(nrow, np_)
    for i in range(n):
        o_win = o_win + e_wn[:, i:i + 1] * nw[i:i + 1, kvw:2 * kvw]
    o_win = o_win / l_w.reshape(nrow, 1)

    nh = nkv * rep
    for g in range(nkv):
        for r in range(rep):
            h = g * rep + r
            rows = slice(h * np_, (h + 1) * np_)
            cols = slice(g * hd, (g + 1) * hd)
            o_ref[:, h * hd:(h + 1) * hd] = (sig[:, h:h + 1] * o_cmp[rows, cols]
                                            + sig[:, nh + h:nh + h + 1] * o_sel[rows, cols]
                                            + sig[:, 2 * nh + h:2 * nh + h + 1] * o_win[rows, cols])

    body = w_buf - np_
    wo_ref[0:body, :] = win_ref[pl.ds(n, body), :]
    row = lax.broadcasted_iota(jnp.int32, (np_, 1), 0)
    wo_ref[body:w_buf, :] = jnp.where(row < np_ - n, pltpu.roll(win_ref[body:w_buf, :], np_ - n, 0),
                                      pltpu.roll(nw, np_ - n, 0))


def nsa_sample(q, kcvc, pool, layer, page_table, rows_s, rows_w, win, gates, n):
    db, n_pages = page_table.shape
    page = pool.shape[2]
    past_len = n_pages * page
    nsub = kcvc.shape[0] // db
    w_buf = win.shape[2]
    assert w_buf == WINDOW and n <= SAMPLE_PAD and past_len % SEL_BLK == 0
    nc = (past_len + n - CMP_LEN) // CMP_STRIDE + 1
    n_sel = -(-(past_len + n) // SEL_BLK)
    assert n_sel <= SEL_BLK and nc <= nsub
    cov = _cover(nsub, nc, n_sel)
    exp = _expand(past_len, past_len)
    rows8 = lambda w: pl.BlockSpec((SAMPLE_PAD, w), lambda i, pt: (i, 0))
    pages = [pl.BlockSpec((None, None, page, ROW_W),
                          functools.partial(lambda i, pt, k: (layer, pt[i * n_pages + k], 0, 0), k=k))
             for k in range(n_pages)]
    grid_spec = pltpu.PrefetchScalarGridSpec(
        num_scalar_prefetch=1,
        grid=(db,),
        in_specs=[rows8(q.shape[1]), pl.BlockSpec((nsub, ROW_W), lambda i, pt: (i, 0))] + pages
        + [rows8(ROW_W), rows8(ROW_W),
           pl.BlockSpec((None, None, w_buf, ROW_W), lambda i, pt: (layer, i, 0, 0)),
           rows8(LANE),
           pl.BlockSpec(cov.shape, lambda i, pt: (0, 0)),
           pl.BlockSpec(exp.shape, lambda i, pt: (0, 0, 0))],
        out_specs=[rows8(q.shape[1]), pl.BlockSpec((None, w_buf, ROW_W), lambda i, pt: (i, 0, 0))],
    )
    return pl.pallas_call(
        functools.partial(_nsa_sample_body, n=n, n_pages=n_pages, past_len=past_len, nc=nc, n_sel=n_sel),
        grid_spec=grid_spec,
        out_shape=[jax.ShapeDtypeStruct(q.shape, F32), jax.ShapeDtypeStruct((db, w_buf, ROW_W), F32)],
        compiler_params=_params("parallel"),
        name="nsa_sample",
    )(page_table.reshape(-1), q, kcvc, *([pool] * n_pages), rows_s, rows_w, win, gates, cov, exp)


def kernel(x_prompt, x_sample, state_ret, cache_cmp, cache_sel, state_win, state_ffn, page_table, norm_mix_pre, norm_mix_post, norm_ffn_pre, norm_ffn_post, e_w_in, e_w_out, e_sg_ln_g, e_sg_ln_b, e_sg_w, e_sg_b, o_w_in, o_w_out, o_pe_k, o_pe_v, o_phi_k, o_phi_v, f_w_in, f_conv_w, f_conv_b, f_w_out):
    b, s, d = x_prompt.shape
    db, n, _ = x_sample.shape
    depth = norm_mix_pre.shape[0]
    heads = state_ret.shape[2]
    groups = e_sg_w.shape[1]
    n_pages = page_table.shape[1]
    page = cache_cmp.shape[2]
    past_len = n_pages * page
    pad = SAMPLE_PAD
    assert n <= pad and n >= CONV_W - 1 and n < CMP_STRIDE and state_ret.shape[3] == LANE

    xp = x_prompt.reshape(b * s, d)
    xs = jnp.pad(x_sample, ((0, 0), (0, pad - n), (0, 0))).reshape(db * pad, d)
    pos_p = jnp.arange(s, dtype=jnp.int32)
    pos_s = jnp.tile(past_len + jnp.arange(pad, dtype=jnp.int32), db)
    ret_p, ret_s = _ret_rope_tabs(pos_p), _ret_rope_tabs(pos_s)
    nsa_p, nsa_s = _nsa_rope_tabs(pos_p), _nsa_rope_tabs(pos_s)
    even_modes = [ROPE_RET_Q] * heads + [ROPE_RET_K] * heads + [ROPE_NONE] * (4 * heads)
    even_outs = [(0, 6 * heads * LANE, 1.0, F32)]
    kscale = LANE ** -0.5
    odd_cols = o_w_in.shape[2]
    odd_pad = -(-odd_cols // LANE) * LANE
    pool_c = cache_cmp.reshape(cache_cmp.shape[0], cache_cmp.shape[1], page, ROW_W)
    pool_s = cache_sel.reshape(cache_sel.shape[0], cache_sel.shape[1], page, ROW_W)
    win = state_win.reshape(state_win.shape[0], db, state_win.shape[2], ROW_W)
    kv_shape = cache_cmp.shape[3:]
    keep = min(WINDOW, s)

    out = {k: [] for k in ("ret_p", "ret_s", "sgv", "cmp_p", "cmp_s", "sel_p", "sel_s", "win_p", "win_s", "ffn_p", "ffn_s")}
    for l in range(depth):
        i = l // 2
        if l % 2 == 0:
            w_in = e_w_in[i].astype(BF16)
            w_out = e_w_out[i].astype(BF16)
            (pp,) = proj_in(xp, norm_mix_pre[l], w_in, (ret_p[0], ret_p[1], ret_p[1]), even_modes, even_outs, kscale)
            (ps,) = proj_in(xs, norm_mix_pre[l], w_in, (ret_s[0], ret_s[1], ret_s[1]), even_modes, even_outs, kscale)
            cat_p, st_p = even_seq(pp, b, s, e_sg_ln_g[i], e_sg_ln_b[i], e_sg_w[i], e_sg_b[i], heads, groups)
            cat_s, st_s, svn_s = even_seq_sample(ps, state_ret[i], n, e_sg_ln_g[i], e_sg_ln_b[i], e_sg_w[i],
                                                 e_sg_b[i], heads, groups)
            xp = proj_out([cat_p], w_out, norm_mix_post[l], xp)
            xs = proj_out([cat_s], w_out, norm_mix_post[l], xs)
            out["ret_p"].append(st_p)
            out["ret_s"].append(st_s)
            out["sgv"].append(svn_s.reshape(db, pad, -1)[:, :n])
        else:
            w_in = jnp.pad(o_w_in[i], ((0, 0), (0, odd_pad - odd_cols))).astype(BF16)
            w_out = o_w_out[i].astype(BF16)
            wts, pes = _compress_weights(o_phi_k[i], o_phi_v[i], o_pe_k[i], o_pe_v[i])
            q_p, rc_p, rs_p, rw_p, g_p, ks_p, kw_p = proj_in(xp, norm_mix_pre[l], w_in, nsa_p, _odd_modes(), _odd_outs(True))
            q_s, rc_s, rs_s, rw_s, g_s = proj_in(xs, norm_mix_pre[l], w_in, nsa_s, _odd_modes(), _odd_outs(False))
            kc_p = compress_prompt(rc_p, b, wts, pes)
            o_p = nsa_prompt(q_p, kc_p, ks_p, kw_p, g_p, b, s)
            kc_s = compress_pages(pool_c, i, page_table, wts, pes)
            o_s, win_s = nsa_sample(q_s, kc_s, pool_s, i, page_table, rs_s, rw_s, win, g_s, n)
            xp = proj_out([o_p], w_out, norm_mix_post[l], xp)
            xs = proj_out([o_s], w_out, norm_mix_post[l], xs)
            out["cmp_p"].append(rc_p.reshape((b, s) + kv_shape))
            out["sel_p"].append(rs_p.reshape((b, s) + kv_shape))
            out["win_p"].append(rw_p.reshape((b, s) + kv_shape)[:, s - keep:])
            out["cmp_s"].append(rc_s.reshape((db, pad) + kv_shape)[:, :n])
            out["sel_s"].append(rs_s.reshape((db, pad) + kv_shape)[:, :n])
            out["win_s"].append(win_s.reshape((db, win_s.shape[1]) + kv_shape))
        w_in = f_w_in[l].astype(BF16)
        w_out = f_w_out[l].astype(BF16)
        xp, st_p = conv_ffn(xp, s, norm_ffn_pre[l], w_in, f_conv_w[l], f_conv_b[l], w_out, norm_ffn_post[l])
        xs, st_s = conv_ffn(xs, pad, norm_ffn_pre[l], w_in, f_conv_w[l], f_conv_b[l], w_out, norm_ffn_post[l],
                            prev=_ffn_prev(state_ffn[l]), tm=256)
        out["ffn_p"].append(st_p.reshape(b, 8, -1)[:, 8 - (CONV_W - 1):])
        out["ffn_s"].append(st_s.reshape(db, pad, -1)[:, n - (CONV_W - 1):n])

    stack = lambda k: jnp.stack(out[k])
    return (xp.reshape(b, s, d), xs.reshape(db, pad, d)[:, :n], stack("ret_p"), stack("ret_s"), stack("sgv"),
            stack("cmp_p"), stack("cmp_s"), stack("sel_p"), stack("sel_s"), stack("win_p"), stack("win_s"),
            stack("ffn_p"), stack("ffn_s"))
```

```python
import functools

import jax
import jax.numpy as jnp
from jax import lax
from jax.experimental import pallas as pl
from jax.experimental.pallas import tpu as pltpu

F32 = jnp.float32
BF16 = jnp.bfloat16

EPS = 1e-6
NEG = -1e30
FORCE = 1e9

LANE = 128
VMEM_LIMIT = 56 * 1024 * 1024

RET_CHUNK = 128
RET_THETA = 10000.0
SG_CHUNK = 128
HEAD_DIM = 64
NSA_KV = 4
NSA_REP = 4
CMP_LEN = 32
CMP_STRIDE = 16
SEL_BLK = 64
SEL_TOPK = 16
WINDOW = 512
ROPE_DIM = HEAD_DIM // 4
ROPE_THETA = 500000.0
ATTN_SCALE = HEAD_DIM ** -0.5
CONV_W = 3

SAMPLE_PAD = 8

ROPE_NONE, ROPE_RET_Q, ROPE_RET_K, ROPE_NSA = 0, 1, 2, 3


def _params(*sem):
    return pltpu.CompilerParams(dimension_semantics=sem, vmem_limit_bytes=VMEM_LIMIT)


def _rms(x, g):
    return x * lax.rsqrt(jnp.mean(x * x, -1, keepdims=True) + EPS) * g


def _col_chunk(n):
    for c in (512, 384, 256, 128):
        if n % c == 0:
            return c
    raise ValueError(n)


def _proj_in_body(x_ref, g_ref, w_ref, tc_ref, ta_ref, tb_ref, *o_refs, modes, kscale, outs):
    h = _rms(x_ref[...], g_ref[...]).astype(BF16)
    n = w_ref.shape[1]
    cw = _col_chunk(n)
    for c0 in range(0, n, cw):
        y = jnp.dot(h, w_ref[:, c0:c0 + cw], preferred_element_type=F32)
        for j in range(cw // LANE):
            col = c0 + j * LANE
            blk = y[:, j * LANE:(j + 1) * LANE]
            mode = modes[col // LANE]
            if mode in (ROPE_RET_Q, ROPE_RET_K):
                blk = blk * tc_ref[...] + pltpu.roll(blk, LANE // 2, 1) * ta_ref[...]
                if mode == ROPE_RET_K:
                    blk = blk * kscale
            elif mode == ROPE_NSA:
                blk = (blk * tc_ref[...] + pltpu.roll(blk, LANE - ROPE_DIM // 2, 1) * ta_ref[...]
                       + pltpu.roll(blk, ROPE_DIM // 2, 1) * tb_ref[...])
            for o_ref, (oc, ow, osc, tw) in zip(o_refs, outs):
                if oc <= col < oc + ow:
                    v = blk if osc == 1.0 else blk * osc
                    cs = slice(col - oc, col - oc + LANE)
                    if tw is None:
                        o_ref[:, cs] = v.astype(o_ref.dtype)
                    elif tw == 0:
                        o_ref[cs, :] = v.T.astype(o_ref.dtype)
                    else:
                        for s in range(v.shape[0] // tw):
                            o_ref[s, cs, :] = v[s * tw:(s + 1) * tw, :].T.astype(o_ref.dtype)


def proj_in(x, g, w, tabs, modes, outs, kscale=1.0, tm=512, seq_len=None):
    m, d = x.shape
    n = w.shape[1]
    tm = min(tm, m)
    nt = tabs[0].shape[0] // tm
    tab_spec = pl.BlockSpec((tm, LANE), lambda i: (i % nt, 0))
    tps = (seq_len // tm) if seq_len else 1
    specs, shapes = [], []
    for (_, ow, _, dt, tw) in outs:
        if tw is None:
            specs.append(pl.BlockSpec((tm, ow), lambda i: (i, 0)))
            shapes.append(jax.ShapeDtypeStruct((m, ow), dt))
        elif tw == 0:
            specs.append(pl.BlockSpec((None, ow, tm), lambda i: (i // tps, 0, i % tps)))
            shapes.append(jax.ShapeDtypeStruct((m // seq_len, ow, seq_len), dt))
        else:
            specs.append(pl.BlockSpec((None, tm // tw, ow, tw), lambda i: (i // tps, i % tps, 0, 0)))
            shapes.append(jax.ShapeDtypeStruct((m // seq_len, seq_len // tw, ow, tw), dt))
    return pl.pallas_call(
        functools.partial(_proj_in_body, modes=tuple(modes), kscale=kscale,
                          outs=tuple((o[0], o[1], o[2], o[4]) for o in outs)),
        grid=(m // tm,),
        in_specs=[pl.BlockSpec((tm, d), lambda i: (i, 0)),
                  pl.BlockSpec((1, d), lambda i: (0, 0)),
                  pl.BlockSpec((d, n), lambda i: (0, 0)),
                  tab_spec, tab_spec, tab_spec],
        out_specs=specs,
        out_shape=shapes,
        compiler_params=_params("parallel"),
        name="proj_in",
    )(x, g.reshape(1, d), w, *tabs)


def _ret_rope_tabs(pos):
    inv = 1.0 / (RET_THETA ** jnp.linspace(0.0, 1.0, LANE // 2))
    ang = pos.astype(F32)[:, None] * inv[None, :]
    cos, sin = jnp.cos(ang), jnp.sin(ang)
    return jnp.concatenate([cos, cos], -1), jnp.concatenate([-sin, sin], -1)


def _nsa_rope_tabs(pos):
    hr = ROPE_DIM // 2
    inv = 1.0 / (ROPE_THETA ** (jnp.arange(0, ROPE_DIM, 2, dtype=F32) / ROPE_DIM))
    ang = pos.astype(F32)[:, None] * inv[None, :]
    cos, sin = jnp.cos(ang), jnp.sin(ang)
    n = pos.shape[0]
    rest = HEAD_DIM - ROPE_DIM
    c = jnp.concatenate([cos, cos, jnp.ones((n, rest), F32)], -1)
    a = jnp.concatenate([-sin, jnp.zeros((n, hr + rest), F32)], -1)
    b = jnp.concatenate([jnp.zeros((n, hr), F32), sin, jnp.zeros((n, rest), F32)], -1)
    rep = LANE // HEAD_DIM
    return jnp.tile(c, (1, rep)), jnp.tile(a, (1, rep)), jnp.tile(b, (1, rep))


def _ret_tabs(c, n_valid, heads):
    log_g = jnp.log(1.0 - 2.0 ** (-5.0 - jnp.arange(heads, dtype=F32)))
    idx = jnp.arange(c, dtype=F32)
    diff = idx[:, None] - idx[None, :]
    ok = (diff >= 0) & (idx[None, :] < n_valid)
    dmask = jnp.where(ok, jnp.exp(log_g[:, None, None] * jnp.maximum(diff, 0.0)), 0.0)
    qdec = jnp.exp(log_g[:, None] * (idx + 1.0))
    kdec = jnp.where(idx < n_valid, jnp.exp(log_g[:, None] * (n_valid - 1.0 - idx)), 0.0)
    cdec = jnp.exp(log_g * n_valid)
    bc = lambda t: jnp.broadcast_to(t[..., None], t.shape + (LANE,))
    dm = dmask if c == LANE else jnp.pad(dmask, ((0, 0), (0, 0), (0, LANE - c)))
    return dm, bc(qdec), bc(kdec), jnp.broadcast_to(cdec[:, None, None], (heads, 8, LANE))


def _layer_norm_rows(x, g, b):
    xc = x - jnp.mean(x, -1, keepdims=True)
    return xc * lax.rsqrt(jnp.mean(xc * xc, -1, keepdims=True) + EPS) * g + b


def _rms_unit(x):
    return x * lax.rsqrt(jnp.mean(x * x, -1, keepdims=True) + EPS)


def _even_seq_body(q_ref, k_ref, v_ref, g_ref, u_ref, sv_ref, dm_ref, qd_ref, kd_ref, cd_ref,
                   lng_ref, lnb_ref, wm_ref, sgb_ref, o_ref, st_ref, s_scr, *, tq, heads, groups):
    t = pl.program_id(1)
    c = RET_CHUNK

    @pl.when(t == 0)
    def _():
        s_scr[...] = jnp.zeros(s_scr.shape, F32)

    for ci in range(tq // c):
        rows = slice(ci * c, (ci + 1) * c)
        for h in range(heads):
            cols = slice(h * LANE, (h + 1) * LANE)
            qc, kc, vc = q_ref[rows, cols], k_ref[rows, cols], v_ref[rows, cols]
            vb = vc.astype(BF16)
            s = s_scr[h]
            inner = lax.dot_general(qc.astype(BF16), kc.astype(BF16), (((1,), (1,)), ((), ())),
                                    preferred_element_type=F32) * dm_ref[h]
            o = (jnp.dot(inner.astype(BF16), vb, preferred_element_type=F32)
                 + jnp.dot((qc * qd_ref[h]).astype(BF16), s.astype(BF16), preferred_element_type=F32))
            s_scr[h] = s * cd_ref[h][0:1, :] + lax.dot_general(
                (kc * kd_ref[h]).astype(BF16), vb, (((0,), (0,)), ((), ())), preferred_element_type=F32)
            gg = g_ref[rows, cols]
            o_ref[rows, cols] = gg * jax.nn.sigmoid(gg) * _rms_unit(o)
        for gi in range(groups):
            cols = slice(gi * LANE, (gi + 1) * LANE)
            svn = _layer_norm_rows(sv_ref[rows, cols], lng_ref[:, cols], lnb_ref[:, cols])
            mixed = jnp.dot(wm_ref[gi], svn.astype(BF16), preferred_element_type=F32) + sgb_ref[gi]
            o_ref[rows, heads * LANE + gi * LANE:heads * LANE + (gi + 1) * LANE] = u_ref[rows, cols] * mixed

    @pl.when(t == pl.num_programs(1) - 1)
    def _():
        st_ref[...] = s_scr[...]


def even_seq(p, nseq, seq_len, ln_g, ln_b, sg_w, sg_b, heads, groups, tq=512):
    m = p.shape[0]
    w = heads * LANE
    c = RET_CHUNK
    nt = seq_len // tq
    dm, qd, kd, cd = _ret_tabs(c, c, heads)
    wm = jnp.tril(sg_w[:, :c, :c]).astype(BF16)
    sgb = jnp.broadcast_to(sg_b[:, :c, None], (groups, c, LANE))
    part = lambda j: pl.BlockSpec((tq, w), lambda b, t: (b * nt + t, j))
    full = lambda a: pl.BlockSpec(a.shape, lambda b, t: (0,) * a.ndim)
    return pl.pallas_call(
        functools.partial(_even_seq_body, tq=tq, heads=heads, groups=groups),
        grid=(nseq, nt),
        in_specs=[part(j) for j in range(6)] + [full(dm), full(qd), full(kd), full(cd),
                                                pl.BlockSpec((1, w), lambda b, t: (0, 0)),
                                                pl.BlockSpec((1, w), lambda b, t: (0, 0)),
                                                full(wm), full(sgb)],
        out_specs=[pl.BlockSpec((tq, 2 * w), lambda b, t: (b * nt + t, 0)),
                   pl.BlockSpec((None, heads, LANE, LANE), lambda b, t: (b, 0, 0, 0))],
        out_shape=[jax.ShapeDtypeStruct((m, 2 * w), F32),
                   jax.ShapeDtypeStruct((nseq, heads, LANE, LANE), F32)],
        scratch_shapes=[pltpu.VMEM((heads, LANE, LANE), F32)],
        compiler_params=_params("arbitrary", "arbitrary"),
        name="even_seq",
    )(p, p, p, p, p, p, dm, qd, kd, cd, ln_g.reshape(1, w), ln_b.reshape(1, w), wm, sgb)


def _even_seq_sample_body(q_ref, k_ref, v_ref, g_ref, u_ref, sv_ref, s0_ref, dm_ref, qd_ref, kd_ref, cd_ref,
                          lng_ref, lnb_ref, wm_ref, sgb_ref, o_ref, st_ref, svn_ref, *, nb, n, heads, groups):
    np_ = SAMPLE_PAD

    def one(b, carry):
        rows = pl.ds(pl.multiple_of(b * np_, np_), np_)
        for h in range(heads):
            cols = slice(h * LANE, (h + 1) * LANE)
            q, k, v = q_ref[rows, cols], k_ref[rows, cols], v_ref[rows, cols]
            s = s0_ref[b, h]
            o = jnp.dot((q * qd_ref[h]).astype(BF16), s.astype(BF16), preferred_element_type=F32)
            dm = dm_ref[h]
            for j in range(n):
                inner = jnp.sum(q * k[j:j + 1, :], axis=-1, keepdims=True)
                o = o + (inner * dm[:, j:j + 1]) * v[j:j + 1, :]
            st_ref[b, h] = s * cd_ref[h][0:1, :] + lax.dot_general(
                (k * kd_ref[h]).astype(BF16), v.astype(BF16), (((0,), (0,)), ((), ())),
                preferred_element_type=F32)
            gg = g_ref[rows, cols]
            o_ref[rows, cols] = gg * jax.nn.sigmoid(gg) * _rms_unit(o)
        for gi in range(groups):
            cols = slice(gi * LANE, (gi + 1) * LANE)
            svn = _layer_norm_rows(sv_ref[rows, cols], lng_ref[:, cols], lnb_ref[:, cols])
            svn_ref[rows, cols] = svn
            wm = wm_ref[gi]
            mixed = sgb_ref[gi]
            for j in range(n):
                mixed = mixed + wm[:, j:j + 1] * svn[j:j + 1, :]
            o_ref[rows, heads * LANE + gi * LANE:heads * LANE + (gi + 1) * LANE] = u_ref[rows, cols] * mixed
        return carry

    lax.fori_loop(0, nb, one, 0)


def even_seq_sample(p, s0, layer, n, ln_g, ln_b, sg_w, sg_b, heads, groups, nb=8):
    m = p.shape[0]
    db = m // SAMPLE_PAD
    w = heads * LANE
    dm, qd, kd, cd = _ret_tabs(SAMPLE_PAD, n, heads)
    wm = jnp.pad(jnp.tril(sg_w[:, :n, :n]), ((0, 0), (0, SAMPLE_PAD - n), (0, LANE - n)))
    sgb = jnp.broadcast_to(jnp.pad(sg_b[:, :n], ((0, 0), (0, SAMPLE_PAD - n)))[:, :, None], (groups, SAMPLE_PAD, LANE))
    rows = nb * SAMPLE_PAD
    part = lambda j: pl.BlockSpec((rows, w), lambda i: (i, j))
    full = lambda a: pl.BlockSpec(a.shape, lambda i: (0,) * a.ndim)
    st_spec = pl.BlockSpec((nb, heads, LANE, LANE), lambda i: (i, 0, 0, 0))
    return pl.pallas_call(
        functools.partial(_even_seq_sample_body, nb=nb, n=n, heads=heads, groups=groups),
        grid=(db // nb,),
        in_specs=[part(j) for j in range(6)] + [pl.BlockSpec((None, nb, heads, LANE, LANE),
                                                             lambda i: (layer, i, 0, 0, 0)),
                                                full(dm), full(qd), full(kd), full(cd),
                                                pl.BlockSpec((1, w), lambda i: (0, 0)),
                                                pl.BlockSpec((1, w), lambda i: (0, 0)),
                                                full(wm), full(sgb)],
        out_specs=[pl.BlockSpec((rows, 2 * w), lambda i: (i, 0)), st_spec,
                   pl.BlockSpec((rows, w), lambda i: (i, 0))],
        out_shape=[jax.ShapeDtypeStruct((m, 2 * w), F32),
                   jax.ShapeDtypeStruct(s0.shape[1:], F32),
                   jax.ShapeDtypeStruct((m, w), F32)],
        compiler_params=_params("parallel"),
        name="even_seq_sample",
    )(p, p, p, p, p, p, s0, dm, qd, kd, cd, ln_g.reshape(1, w), ln_b.reshape(1, w), wm, sgb)


def _proj_out_body(*refs, n_a):
    w_ref, g_ref, x_ref, o_ref = refs[n_a:]
    a = refs[0][...]
    for r in refs[1:n_a]:
        a = a + r[...]
    y = jnp.dot(a.astype(BF16), w_ref[...], preferred_element_type=F32)
    o_ref[...] = x_ref[...] + _rms(y, g_ref[...])


def proj_out(a_list, w, g, x, tm=512):
    m, d = x.shape
    k = w.shape[0]
    tm = min(tm, m)
    row = lambda i: (i, 0)
    return pl.pallas_call(
        functools.partial(_proj_out_body, n_a=len(a_list)),
        grid=(m // tm,),
        in_specs=[pl.BlockSpec((tm, k), row) for _ in a_list]
        + [pl.BlockSpec((k, d), lambda i: (0, 0)),
           pl.BlockSpec((1, d), lambda i: (0, 0)),
           pl.BlockSpec((tm, d), row)],
        out_specs=pl.BlockSpec((tm, d), row),
        out_shape=jax.ShapeDtypeStruct((m, d), F32),
        compiler_params=_params("parallel"),
        name="proj_out",
    )(*a_list, w, g.reshape(1, d), x)


FFN_HALO = 16


def _ffn_body(*refs, tm, tiles_per_seq, sample, nf_static):
    if sample:
        (x_ref, gpre_ref, wa_ref, wb_ref, cwa_ref, cwb_ref, cba_ref, cbb_ref, wo_ref, gpost_ref,
         e1a_ref, e1b_ref, e2a_ref, e2b_ref, o_ref, st_ref, h_scr, upa_scr, upb_scr, acc_scr) = refs
    else:
        (x_ref, halo_ref, gpre_ref, wa_ref, wb_ref, cwa_ref, cwb_ref, cba_ref, cbb_ref, wo_ref, gpost_ref,
         o_ref, st_ref, h_scr, upa_scr, upb_scr, acc_scr) = refs
    i = pl.program_id(0)
    f = pl.program_id(1)
    nf = pl.num_programs(1)
    hl = FFN_HALO

    @pl.when(f == 0)
    def _():
        if sample:
            h_scr[0:hl, :] = jnp.zeros((hl, h_scr.shape[1]), BF16)
        else:
            hh = _rms(halo_ref[...], gpre_ref[...])
            h_scr[0:hl, :] = jnp.where(i % tiles_per_seq == 0, 0.0, hh).astype(BF16)
        h_scr[hl:, :] = _rms(x_ref[...], gpre_ref[...]).astype(BF16)

    h = h_scr[...]
    upa_scr[...] = jnp.dot(h, wa_ref[...], preferred_element_type=F32)
    upb_scr[...] = jnp.dot(h, wb_ref[...], preferred_element_type=F32)

    if sample:
        t = lax.broadcasted_iota(jnp.int32, (tm, 1), 0) % SAMPLE_PAD
        m1 = t >= 1
        m2 = t >= 2

    def conv(up_scr, cw_ref, cb_ref, e1_ref, e2_ref):
        s2 = up_scr[pl.ds(hl - 2, tm), :]
        s1 = up_scr[pl.ds(hl - 1, tm), :]
        s0 = up_scr[pl.ds(hl, tm), :]
        if sample:
            s2 = jnp.where(m2, s2, 0.0) + e2_ref[...]
            s1 = jnp.where(m1, s1, 0.0) + e1_ref[...]
        return cb_ref[...] + s2 * cw_ref[0:1, :] + s1 * cw_ref[1:2, :] + s0 * cw_ref[2:3, :]

    a = conv(upa_scr, cwa_ref, cba_ref, e1a_ref if sample else None, e2a_ref if sample else None)
    b = conv(upb_scr, cwb_ref, cbb_ref, e1b_ref if sample else None, e2b_ref if sample else None)
    act = (jax.nn.gelu(a) * b).astype(BF16)
    contrib = jnp.dot(act, wo_ref[...], preferred_element_type=F32)

    @pl.when(f == 0)
    def _():
        acc_scr[...] = contrib

    @pl.when(f > 0)
    def _():
        acc_scr[...] += contrib

    @pl.when(f == nf - 1)
    def _():
        o_ref[...] = x_ref[...] + _rms(acc_scr[...], gpost_ref[...])

    fw = upa_scr.shape[1]
    rows = st_ref.shape[0]
    last = True if sample else (i % tiles_per_seq == tiles_per_seq - 1)
    for j in range(nf_static):
        @pl.when(jnp.logical_and(f == j, last))
        def _(j=j):
            st_ref[:, j * fw:(j + 1) * fw] = upa_scr[pl.ds(hl + tm - rows, rows), :]
            st_ref[:, (nf_static + j) * fw:(nf_static + j + 1) * fw] = upb_scr[pl.ds(hl + tm - rows, rows), :]


def conv_ffn(x, seq_len, gpre, w_in, conv_w, conv_b, w_out, gpost, prev=None, tm=512):
    m, d = x.shape
    ff = w_out.shape[0]
    fw = 1408 if ff % 1408 == 0 else ff
    nf = ff // fw
    sample = prev is not None
    tm = min(tm, m)
    tps = max(seq_len // tm, 1)
    row = lambda i, f: (i, 0)
    const = lambda i, f: (0, 0)
    cola = lambda i, f: (0, f)
    colb = lambda i, f: (0, nf + f)
    in_specs = [pl.BlockSpec((tm, d), row)]
    args = [x]
    if not sample:
        hb = tm // FFN_HALO
        in_specs.append(pl.BlockSpec((FFN_HALO, d), lambda i, f: (jnp.maximum(i * hb - 1, 0), 0)))
        args.append(x)
    in_specs += [pl.BlockSpec((1, d), const),
                 pl.BlockSpec((d, fw), cola), pl.BlockSpec((d, fw), colb),
                 pl.BlockSpec((CONV_W, fw), cola), pl.BlockSpec((CONV_W, fw), colb),
                 pl.BlockSpec((1, fw), cola), pl.BlockSpec((1, fw), colb),
                 pl.BlockSpec((fw, d), lambda i, f: (f, 0)),
                 pl.BlockSpec((1, d), const)]
    args += [gpre.reshape(1, d), w_in, w_in, conv_w, conv_w, conv_b.reshape(1, -1), conv_b.reshape(1, -1),
             w_out, gpost.reshape(1, d)]
    if sample:
        e1, e2 = prev
        in_specs += [pl.BlockSpec((tm, fw), lambda i, f: (i, f)), pl.BlockSpec((tm, fw), lambda i, f: (i, nf + f)),
                     pl.BlockSpec((tm, fw), lambda i, f: (i, f)), pl.BlockSpec((tm, fw), lambda i, f: (i, nf + f))]
        args += [e1, e1, e2, e2]
        st_shape = jax.ShapeDtypeStruct((m, 2 * ff), F32)
        st_spec = pl.BlockSpec((tm, 2 * ff), lambda i, f: (i, 0))
    else:
        nseq = m // seq_len
        st_shape = jax.ShapeDtypeStruct((nseq * 8, 2 * ff), F32)
        st_spec = pl.BlockSpec((8, 2 * ff), lambda i, f: (i // tps, 0))
    return pl.pallas_call(
        functools.partial(_ffn_body, tm=tm, tiles_per_seq=tps, sample=sample, nf_static=nf),
        grid=(m // tm, nf),
        in_specs=in_specs,
        out_specs=[pl.BlockSpec((tm, d), row), st_spec],
        out_shape=[jax.ShapeDtypeStruct((m, d), F32), st_shape],
        scratch_shapes=[pltpu.VMEM((tm + FFN_HALO, d), BF16),
                        pltpu.VMEM((tm + FFN_HALO, fw), F32),
                        pltpu.VMEM((tm + FFN_HALO, fw), F32),
                        pltpu.VMEM((tm, d), F32)],
        compiler_params=_params("arbitrary", "arbitrary"),
        name="conv_ffn_sample" if sample else "conv_ffn",
    )(*args)


def _ffn_prev(prev):
    db, _, w = prev.shape
    z = jnp.zeros((db, SAMPLE_PAD, w), prev.dtype)
    e1 = z.at[:, 0].set(prev[:, 1])
    e2 = z.at[:, 0:2].set(prev)
    return e1.reshape(db * SAMPLE_PAD, w), e2.reshape(db * SAMPLE_PAD, w)


def _odd_modes():
    return [ROPE_NSA] * 8 + [ROPE_NSA, ROPE_NSA, ROPE_NONE, ROPE_NONE] * 3 + [ROPE_NONE]


SEL_TK = 512
WIN_TK = 128


def _odd_outs(prompt):
    if not prompt:
        return [(0, 1024, ATTN_SCALE, F32, None), (1024, 512, 1.0, F32, None), (1536, 512, 1.0, F32, None),
                (2048, 512, 1.0, F32, None), (2560, 128, 1.0, F32, None)]
    return [(0, 1024, ATTN_SCALE, BF16, None), (1024, 512, 1.0, F32, None), (1024, 512, 1.0, F32, 0),
            (1536, 512, 1.0, F32, 0), (2048, 512, 1.0, F32, 0), (2560, 128, 1.0, F32, None),
            (1536, 512, 1.0, BF16, SEL_TK), (2048, 512, 1.0, BF16, WIN_TK)]


SUBS = CMP_LEN // CMP_STRIDE
ROW_W = 2 * NSA_KV * HEAD_DIM
SUB_W = CMP_STRIDE * ROW_W


def _compress_weights(phi_k, phi_v, pe_k, pe_v):
    def one(phi, pe):
        p4 = phi.reshape(SUBS, CMP_STRIDE, HEAD_DIM, HEAD_DIM)
        w = jnp.einsum('hlde,gG->lgdhGe', p4, jnp.eye(2, dtype=phi.dtype))
        w = w.reshape(CMP_STRIDE * 2 * HEAD_DIM, SUBS * 2 * HEAD_DIM)
        pr = jnp.broadcast_to(pe.reshape(SUBS, CMP_STRIDE, 1, HEAD_DIM), (SUBS, CMP_STRIDE, 2, HEAD_DIM))
        pr = jnp.pad(pr.reshape(SUBS, -1), ((0, 16 - SUBS), (0, 0)))
        return w, pr
    wk, pk = one(phi_k, pe_k)
    wv, pv = one(phi_v, pe_v)
    return jnp.stack([wk, wv]).astype(BF16), jnp.stack([pk, pv]).astype(BF16)


def _compress_column(xj, j, w_ref, pe_ref, o_ref, ab_scr):
    nsub = o_ref.shape[0]
    half = LANE
    xe = jnp.concatenate([xj, pe_ref[j // 2]], axis=0)
    ab_scr[...] = jnp.dot(xe, w_ref[j // 2], preferred_element_type=F32)
    bias = ab_scr[nsub:nsub + 1, 0:half] + ab_scr[nsub + 1:nsub + 2, half:2 * half]
    o_ref[:, j * LANE:(j + 1) * LANE] = (
        ab_scr[0:nsub, 0:half] + ab_scr[pl.ds(1, nsub), half:2 * half] + bias).astype(o_ref.dtype)


def _compress_body(x_ref, w_ref, pe_ref, o_ref, ab_scr):
    for j in range(ROW_W // LANE):
        xj = jnp.concatenate([x_ref[:, l * ROW_W + j * LANE:l * ROW_W + (j + 1) * LANE]
                              for l in range(CMP_STRIDE)], axis=1).astype(BF16)
        _compress_column(xj, j, w_ref, pe_ref, o_ref, ab_scr)


def _compress_pages_body(pt_ref, *refs, n_x):
    x_refs = refs[:n_x]
    w_ref, pe_ref, o_ref, ab_scr, kx_scr = refs[n_x:]
    nsub = o_ref.shape[0]
    page = x_refs[0].shape[-1]
    hd = HEAD_DIM
    for j in range(ROW_W // LANE):
        c, gp = divmod(j, NSA_KV // 2)
        for k, xr in enumerate(x_refs):
            kx_scr[k * page:(k + 1) * page, 0:hd] = xr[c, 2 * gp].T
            kx_scr[k * page:(k + 1) * page, hd:2 * hd] = xr[c, 2 * gp + 1].T
        xj = jnp.concatenate([kx_scr[pl.ds(l, nsub, stride=CMP_STRIDE), :] for l in range(CMP_STRIDE)],
                             axis=1).astype(BF16)
        _compress_column(xj, j, w_ref, pe_ref, o_ref, ab_scr)


def compress_prompt(rows, nseq, wts, pes):
    nsub = rows.shape[0] // nseq // CMP_STRIDE
    x = rows.reshape(nseq * nsub, SUB_W)
    return pl.pallas_call(
        _compress_body,
        grid=(nseq,),
        in_specs=[pl.BlockSpec((nsub, SUB_W), lambda b: (b, 0)),
                  pl.BlockSpec(wts.shape, lambda b: (0, 0, 0)),
                  pl.BlockSpec(pes.shape, lambda b: (0, 0, 0))],
        out_specs=pl.BlockSpec((nsub, ROW_W), lambda b: (b, 0)),
        out_shape=jax.ShapeDtypeStruct((nseq * nsub, ROW_W), BF16),
        scratch_shapes=[pltpu.VMEM((nsub + 16, 2 * LANE), F32)],
        compiler_params=_params("parallel"),
        name="compress_prompt",
    )(x, wts, pes)


def compress_pages(pool, layer, page_table, wts, pes, nb=2):
    db, n_pages = page_table.shape
    page = pool.shape[-1]
    sub_pp = page // CMP_STRIDE
    nsub = nb * n_pages * sub_pp
    specs = [pl.BlockSpec((None, None) + pool.shape[2:],
                          functools.partial(lambda i, pt, s, k: (layer, pt[(i * nb + s) * n_pages + k], 0, 0, 0, 0),
                                            s=s, k=k))
             for s in range(nb) for k in range(n_pages)]
    grid_spec = pltpu.PrefetchScalarGridSpec(
        num_scalar_prefetch=1,
        grid=(db // nb,),
        in_specs=specs + [pl.BlockSpec(wts.shape, lambda i, pt: (0, 0, 0)),
                          pl.BlockSpec(pes.shape, lambda i, pt: (0, 0, 0))],
        out_specs=pl.BlockSpec((nsub, ROW_W), lambda i, pt: (i, 0)),
        scratch_shapes=[pltpu.VMEM((nsub + 16, 2 * LANE), F32),
                        pltpu.VMEM((nb * n_pages * page, LANE), F32)],
    )
    return pl.pallas_call(
        functools.partial(_compress_pages_body, n_x=nb * n_pages),
        grid_spec=grid_spec,
        out_shape=jax.ShapeDtypeStruct((db * n_pages * sub_pp, ROW_W), BF16),
        compiler_params=_params("parallel"),
        name="compress_pages",
    )(page_table.reshape(-1), *([pool] * (nb * n_pages)), wts, pes)


def _cover(nsub, nc, n_sel):
    c_start = jnp.arange(nsub) * CMP_STRIDE
    s_start = jnp.arange(SEL_BLK) * SEL_BLK
    ok = ((c_start[:, None] < s_start[None, :] + SEL_BLK) & (c_start[:, None] + CMP_LEN > s_start[None, :])
          & (jnp.arange(nsub)[:, None] < nc) & (jnp.arange(SEL_BLK)[None, :] < n_sel))
    return ok.astype(BF16)


def _expand(n_keys, tk):
    key = jnp.arange(n_keys).reshape(n_keys // tk, 1, tk)
    return (key // SEL_BLK == jnp.arange(SEL_BLK)[None, :, None]).astype(BF16)


def _masked_softmax(s, ok):
    sm = jnp.where(ok, s, NEG)
    ex = jnp.exp(sm - jnp.max(sm, -1, keepdims=True))
    return ex / jnp.sum(ex, -1, keepdims=True)


def _importance(psum, cov_ref, tpos, n_sel):
    hi = psum.astype(BF16)
    lo = (psum - hi.astype(F32)).astype(BF16)
    imp = (jnp.dot(hi, cov_ref[...], preferred_element_type=F32)
           + jnp.dot(lo, cov_ref[...], preferred_element_type=F32))
    jl = lax.broadcasted_iota(jnp.int32, imp.shape, 1)
    cur = tpos // SEL_BLK
    forced = (jl == 0) | (jl == cur) | (jl == cur - 1)
    causal = jl * SEL_BLK <= tpos
    imp = jnp.where(forced, FORCE, jnp.where(causal, imp, -1.0))
    return jnp.where(jl < n_sel, imp, -2.0)


def _topk_masks(imps):
    out = []
    for pair in range(len(imps) // 2):
        xt = jnp.concatenate([imps[2 * pair], imps[2 * pair + 1]], axis=1).T
        halves = []
        for hh in range(2):
            xa = xt[SEL_BLK * hh:SEL_BLK * (hh + 1)]
            jrow = lax.broadcasted_iota(jnp.int32, xa.shape, 0)
            cnt = jnp.zeros(xa.shape, F32)
            for i in range(SEL_BLK):
                xi = xa[i:i + 1, :]
                tie = jnp.where(jrow > i, 1.0, 0.0)
                cnt = cnt + jnp.where(xi > xa, 1.0, jnp.where(xi == xa, tie, 0.0))
            halves.append(jnp.where(cnt < SEL_TOPK, 1.0, 0.0))
        sel2 = jnp.concatenate(halves, axis=0).T
        out += [sel2[:, 0:SEL_BLK], sel2[:, SEL_BLK:2 * SEL_BLK]]
    return out


_NT = (((1,), (1,)), ((), ()))


def _nsa_prompt_body(q_ref, kc_ref, ks_ref, kw_ref, gt_ref, cov_ref, e_ref, o_ref, *, tq, nc, n_sel):
    t0 = pl.program_id(1) * tq
    tpos = t0 + lax.broadcasted_iota(jnp.int32, (tq, 1), 0)
    sig = jax.nn.sigmoid(gt_ref[...])
    nsub = kc_ref.shape[0]
    rep, hd, kvw = NSA_REP, HEAD_DIM, NSA_KV * HEAD_DIM
    mcol = lax.broadcasted_iota(jnp.int32, (tq, nsub), 1)
    valid = (mcol * CMP_STRIDE + CMP_LEN - 1 <= tpos) & (mcol < nc)
    validf = jnp.where(valid, 1.0, 0.0)

    q_st, o_cmp, imps = [], [], []
    for g in range(NSA_KV):
        qg = jnp.concatenate([q_ref[:, (g * rep + r) * hd:(g * rep + r + 1) * hd] for r in range(rep)], axis=0)
        q_st.append(qg)
        s = lax.dot_general(qg, kc_ref[:, g * hd:(g + 1) * hd], _NT, preferred_element_type=F32)
        p = _masked_softmax(s.reshape(rep, tq, nsub), valid[None]) * validf[None]
        o_cmp.append(jnp.dot(p.reshape(rep * tq, nsub).astype(BF16), kc_ref[:, kvw + g * hd:kvw + (g + 1) * hd],
                             preferred_element_type=F32))
        imps.append(_importance(jnp.sum(p, axis=0), cov_ref, tpos, n_sel))
    sels = _topk_masks(imps)

    tk = ks_ref.shape[-1]
    wt = kw_ref.shape[-1]
    kt_d = t0 // tk
    kpos_d = kt_d * tk + lax.broadcasted_iota(jnp.int32, (tq, tk), 1)
    bias_d = jnp.where(kpos_d <= tpos, 0.0, NEG)
    n_wt = (WINDOW + tq) // wt
    wt0 = jnp.maximum(t0 - WINDOW, 0) // wt
    kpos_w = wt0 * wt + lax.broadcasted_iota(jnp.int32, (tq, n_wt * wt), 1)
    bias_w = jnp.where((kpos_w <= tpos) & (kpos_w > tpos - WINDOW), 0.0, NEG)

    def biased(s, bias):
        return (s.reshape(rep, tq, s.shape[-1]) + bias[None]).reshape(s.shape)

    def lanes(x):
        return jnp.broadcast_to(x, (x.shape[0], LANE))

    krows = [slice(g * hd, (g + 1) * hd) for g in range(NSA_KV)]
    vrows = [slice(kvw + g * hd, kvw + (g + 1) * hd) for g in range(NSA_KV)]

    o_win = []
    for g in range(NSA_KV):
        k_w = jnp.concatenate([kw_ref[wt0 + i, krows[g], :] for i in range(n_wt)], axis=1)
        v_w = jnp.concatenate([kw_ref[wt0 + i, vrows[g], :] for i in range(n_wt)], axis=1)
        s = biased(jnp.dot(q_st[g], k_w, preferred_element_type=F32), bias_w)
        p = jnp.exp(s - jnp.max(s, -1, keepdims=True))
        o_win.append((lax.dot_general(p.astype(BF16), v_w, _NT, preferred_element_type=F32)
                      / jnp.sum(p, -1, keepdims=True)).reshape(rep, tq, hd))

    q_aug = []
    for g in range(NSA_KV):
        selneg = jnp.where(sels[g] > 0.5, 0.0, NEG)
        q_aug.append(jnp.concatenate(
            [jnp.concatenate([q_ref[:, (g * rep + r) * hd:(g * rep + r + 1) * hd].astype(F32), selneg], axis=1)
             for r in range(rep)], axis=0).astype(BF16))

    def scores(kt, g):
        k_aug = jnp.concatenate([ks_ref[kt, krows[g], :], e_ref[kt]], axis=0)
        return jnp.dot(q_aug[g], k_aug, preferred_element_type=F32)

    init = []
    for g in range(NSA_KV):
        s = biased(scores(kt_d, g), bias_d)
        m0 = jnp.max(s, -1, keepdims=True)
        p = jnp.exp(s - m0)
        l0 = jnp.sum(p, -1, keepdims=True)
        acc0 = lax.dot_general(p.astype(BF16), ks_ref[kt_d, vrows[g], :], _NT, preferred_element_type=F32)
        init.append((lanes(m0), lanes(l0), acc0))

    def step(kt, carry):
        out = []
        for g in range(NSA_KV):
            m_i, l_i, acc = carry[g]
            s = scores(kt, g)
            m_new = jnp.maximum(m_i, lanes(jnp.max(s, -1, keepdims=True)))
            alpha = jnp.exp(m_i - m_new)
            p = jnp.exp(s - jnp.tile(m_new, (1, tk // LANE)))
            l_new = alpha * l_i + lanes(jnp.sum(p, -1, keepdims=True))
            pv = lax.dot_general(p.astype(BF16), ks_ref[kt, vrows[g], :], _NT, preferred_element_type=F32)
            out.append((m_new, l_new, alpha[:, 0:hd] * acc + pv))
        return tuple(out)

    final = lax.fori_loop(0, kt_d, step, tuple(init))

    nh = NSA_KV * rep
    for g in range(NSA_KV):
        _, l_s, acc_s = final[g]
        o_sel = (acc_s / l_s[:, 0:hd]).reshape(rep, tq, hd)
        o_c = o_cmp[g].reshape(rep, tq, hd)
        for r in range(rep):
            h = g * rep + r
            o_ref[:, h * hd:(h + 1) * hd] = (sig[:, h:h + 1] * o_c[r] + sig[:, nh + h:nh + h + 1] * o_sel[r]
                                            + sig[:, 2 * nh + h:2 * nh + h + 1] * o_win[g][r])


def nsa_prompt(q, kcvc, ks, kw, gates, nseq, seq_len, tq=128):
    m = q.shape[0]
    nsub = kcvc.shape[0] // nseq
    tk, wt = ks.shape[-1], kw.shape[-1]
    nc = (seq_len - CMP_LEN) // CMP_STRIDE + 1
    n_sel = seq_len // SEL_BLK
    assert n_sel <= SEL_BLK and seq_len >= WINDOW + tq and tk % tq == 0 and tq % wt == 0 and WINDOW % wt == 0
    nq = seq_len // tq
    cov = _cover(nsub, nc, n_sel)
    exp = _expand(seq_len, tk)
    tile = lambda w: pl.BlockSpec((tq, w), lambda b, t: (b * nq + t, 0))
    whole = lambda a: pl.BlockSpec((None,) + a.shape[1:], lambda b, t: (b, 0, 0, 0))
    return pl.pallas_call(
        functools.partial(_nsa_prompt_body, tq=tq, nc=nc, n_sel=n_sel),
        grid=(nseq, nq),
        in_specs=[tile(q.shape[1]), pl.BlockSpec((nsub, ROW_W), lambda b, t: (b, 0)), whole(ks), whole(kw), tile(LANE),
                  pl.BlockSpec(cov.shape, lambda b, t: (0, 0)),
                  pl.BlockSpec(exp.shape, lambda b, t: (0, 0, 0))],
        out_specs=tile(q.shape[1]),
        out_shape=jax.ShapeDtypeStruct((m, q.shape[1]), F32),
        compiler_params=_params("parallel", "parallel"),
        name="nsa_prompt",
    )(q, kcvc, ks, kw, gates, cov, exp)


def _joint_softmax(s_a, ok_a, s_b, ok_b):
    s_a = jnp.where(ok_a, s_a, NEG)
    s_b = jnp.where(ok_b, s_b, NEG)
    mx = jnp.maximum(jnp.max(s_a, -1, keepdims=True), jnp.max(s_b, -1, keepdims=True))
    e_a = jnp.exp(s_a - mx)
    e_b = jnp.exp(s_b - mx)
    return e_a, e_b, jnp.sum(e_a, -1, keepdims=True) + jnp.sum(e_b, -1, keepdims=True)


def _nsa_sample_body(pt_ref, q_ref, kc_ref, *rest, n, n_pages, past_len, nc, n_sel):
    page_refs = rest[:n_pages]
    ns_ref, nw_ref, win_ref, gt_ref, cov_ref, e_ref, o_ref, wo_ref = rest[n_pages:]
    np_, rep, hd, kvw, nkv = SAMPLE_PAD, NSA_REP, HEAD_DIM, NSA_KV * HEAD_DIM, NSA_KV
    tpos = past_len + lax.broadcasted_iota(jnp.int32, (np_, 1), 0)
    sig = jax.nn.sigmoid(gt_ref[...])
    q = q_ref[...]

    blocks = []
    for g in range(nkv):
        for r in range(rep):
            h = g * rep + r
            parts = [q[:, h * hd:(h + 1) * hd]]
            if g > 0:
                parts.insert(0, jnp.zeros((np_, g * hd), F32))
            if g < nkv - 1:
                parts.append(jnp.zeros((np_, (nkv - 1 - g) * hd), F32))
            blocks.append(jnp.concatenate(parts, axis=1))
    qbd = jnp.concatenate(blocks, axis=0).astype(BF16)
    nrow = nkv * rep * np_

    def grp(x):
        return x.reshape(nkv, rep, np_, x.shape[-1])

    def scores(keys):
        return grp(lax.dot_general(qbd, keys, _NT, preferred_element_type=F32))

    def scores_t(keys_t):
        return grp(jnp.dot(qbd, keys_t, preferred_element_type=F32))

    new_lane = lax.broadcasted_iota(jnp.int32, (np_, np_), 1)
    new_pos = past_len + new_lane

    nsub = kc_ref.shape[0]
    mcol = lax.broadcasted_iota(jnp.int32, (np_, nsub), 1)
    valid = (mcol * CMP_STRIDE + CMP_LEN - 1 <= tpos) & (mcol < nc)
    p = _masked_softmax(scores(kc_ref[:, 0:kvw]), valid[None, None]) * jnp.where(valid, 1.0, 0.0)[None, None]
    o_cmp = jnp.dot(p.reshape(nrow, nsub).astype(BF16), kc_ref[:, kvw:2 * kvw], preferred_element_type=F32)
    sels = _topk_masks([_importance(jnp.sum(p[g], axis=0), cov_ref, tpos, n_sel) for g in range(nkv)])

    page = page_refs[0].shape[-1]
    k_all = jnp.concatenate([pr[0].reshape(kvw, page) for pr in page_refs], axis=1).astype(BF16)
    v_all = jnp.concatenate([pr[1].reshape(kvw, page) for pr in page_refs], axis=1).astype(BF16)
    kpos = lax.broadcasted_iota(jnp.int32, (np_, past_len), 1)
    ok_c = jnp.stack([jnp.where(kpos <= tpos, jnp.dot(sels[g].astype(BF16), e_ref[0], preferred_element_type=F32), 0.0)
                      for g in range(nkv)]) > 0.5
    new_ok = (new_pos <= tpos) & (new_lane < n)
    ok_n = jnp.stack([jnp.where(new_ok, jnp.concatenate(
        [sels[g][:, (past_len + i) // SEL_BLK:(past_len + i) // SEL_BLK + 1] for i in range(np_)], axis=1), 0.0)
        for g in range(nkv)]) > 0.5
    ns = ns_ref[...]
    e_c, e_n, l_s = _joint_softmax(scores_t(k_all), ok_c[:, None], scores(ns[:, 0:kvw].astype(BF16)), ok_n[:, None])
    o_sel = lax.dot_general(e_c.reshape(nrow, past_len).astype(BF16), v_all, _NT, preferred_element_type=F32)
    e_n = e_n.reshape(nrow, np_)
    for i in range(n):
        o_sel = o_sel + e_n[:, i:i + 1] * ns[i:i + 1, kvw:2 * kvw]
    o_sel = o_sel / l_s.reshape(nrow, 1)

    w_buf = win_ref.shape[-1]
    k_w = win_ref[0].reshape(kvw, w_buf)
    v_w = win_ref[1].reshape(kvw, w_buf)
    kpos_w = past_len - w_buf + lax.broadcasted_iota(jnp.int32, (np_, w_buf), 1)
    ok_w = (kpos_w <= tpos) & (kpos_w > tpos - WINDOW)
    ok_wn = (new_pos <= tpos) & (new_pos > tpos - WINDOW) & (new_lane < n)
    nw = nw_ref[...]
    e_w, e_wn, l_w = _joint_softmax(scores_t(k_w.astype(BF16)), ok_w[None, None],
                                    scores(nw[:, 0:kvw].astype(BF16)), ok_wn[None, None])
    o_win = lax.dot_general(e_w.reshape(nrow, w_buf).astype(BF16), v_w.astype(BF16), _NT,
                            preferred_element_type=F32)
    e_wn = e_wn.reshape(nrow, np_)
    for i in range(n):
        o_win = o_win + e_wn[:, i:i + 1] * nw[i:i + 1, kvw:2 * kvw]
    o_win = o_win / l_w.reshape(nrow, 1)

    nh = nkv * rep
    for g in range(nkv):
        for r in range(rep):
            h = g * rep + r
            rows = slice(h * np_, (h + 1) * np_)
            cols = slice(g * hd, (g + 1) * hd)
            o_ref[:, h * hd:(h + 1) * hd] = (sig[:, h:h + 1] * o_cmp[rows, cols]
                                            + sig[:, nh + h:nh + h + 1] * o_sel[rows, cols]
                                            + sig[:, 2 * nh + h:2 * nh + h + 1] * o_win[rows, cols])

    body = w_buf - LANE
    for c, old in enumerate((k_w, v_w)):
        shifted = pltpu.roll(old, w_buf - n, 1)
        new_t = jnp.concatenate([nw[:, c * kvw:(c + 1) * kvw].T, jnp.zeros((kvw, LANE - np_), F32)], axis=1)
        lane = lax.broadcasted_iota(jnp.int32, (kvw, LANE), 1)
        wo_ref[c * kvw:(c + 1) * kvw, 0:body] = shifted[:, 0:body]
        wo_ref[c * kvw:(c + 1) * kvw, body:w_buf] = jnp.where(lane < LANE - n, shifted[:, body:w_buf],
                                                              pltpu.roll(new_t, LANE - n, 1))


def nsa_sample(q, kcvc, pool, layer, page_table, rows_s, rows_w, win, gates, n):
    db, n_pages = page_table.shape
    page = pool.shape[-1]
    past_len = n_pages * page
    nsub = kcvc.shape[0] // db
    w_buf = win.shape[-1]
    assert w_buf == WINDOW and n <= SAMPLE_PAD and past_len % SEL_BLK == 0
    nc = (past_len + n - CMP_LEN) // CMP_STRIDE + 1
    n_sel = -(-(past_len + n) // SEL_BLK)
    assert n_sel <= SEL_BLK and nc <= nsub
    cov = _cover(nsub, nc, n_sel)
    exp = _expand(past_len, past_len)
    rows8 = lambda w: pl.BlockSpec((SAMPLE_PAD, w), lambda i, pt: (i, 0))
    pages = [pl.BlockSpec((None, None) + pool.shape[2:],
                          functools.partial(lambda i, pt, k: (layer, pt[i * n_pages + k], 0, 0, 0, 0), k=k))
             for k in range(n_pages)]
    grid_spec = pltpu.PrefetchScalarGridSpec(
        num_scalar_prefetch=1,
        grid=(db,),
        in_specs=[rows8(q.shape[1]), pl.BlockSpec((nsub, ROW_W), lambda i, pt: (i, 0))] + pages
        + [rows8(ROW_W), rows8(ROW_W),
           pl.BlockSpec((None, None) + win.shape[2:], lambda i, pt: (layer, i, 0, 0, 0, 0)),
           rows8(LANE),
           pl.BlockSpec(cov.shape, lambda i, pt: (0, 0)),
           pl.BlockSpec(exp.shape, lambda i, pt: (0, 0, 0))],
        out_specs=[rows8(q.shape[1]), pl.BlockSpec((None, ROW_W, w_buf), lambda i, pt: (i, 0, 0))],
    )
    return pl.pallas_call(
        functools.partial(_nsa_sample_body, n=n, n_pages=n_pages, past_len=past_len, nc=nc, n_sel=n_sel),
        grid_spec=grid_spec,
        out_shape=[jax.ShapeDtypeStruct(q.shape, F32), jax.ShapeDtypeStruct((db, ROW_W, w_buf), F32)],
        compiler_params=_params("parallel"),
        name="nsa_sample",
    )(page_table.reshape(-1), q, kcvc, *([pool] * n_pages), rows_s, rows_w, win, gates, cov, exp)


def kernel(x_prompt, x_sample, state_ret, cache_cmp, cache_sel, state_win, state_ffn, page_table, norm_mix_pre, norm_mix_post, norm_ffn_pre, norm_ffn_post, e_w_in, e_w_out, e_sg_ln_g, e_sg_ln_b, e_sg_w, e_sg_b, o_w_in, o_w_out, o_pe_k, o_pe_v, o_phi_k, o_phi_v, f_w_in, f_conv_w, f_conv_b, f_w_out):
    b, s, d = x_prompt.shape
    db, n, _ = x_sample.shape
    depth = norm_mix_pre.shape[0]
    heads = state_ret.shape[2]
    groups = e_sg_w.shape[1]
    n_pages = page_table.shape[1]
    page = cache_cmp.shape[2]
    past_len = n_pages * page
    pad = SAMPLE_PAD
    assert n <= pad and n >= CONV_W - 1 and n < CMP_STRIDE and state_ret.shape[3] == LANE

    xp = x_prompt.reshape(b * s, d)
    xs = jnp.pad(x_sample, ((0, 0), (0, pad - n), (0, 0))).reshape(db * pad, d)
    pos_p = jnp.arange(s, dtype=jnp.int32)
    pos_s = jnp.tile(past_len + jnp.arange(pad, dtype=jnp.int32), db)
    ret_p, ret_s = _ret_rope_tabs(pos_p), _ret_rope_tabs(pos_s)
    nsa_p, nsa_s = _nsa_rope_tabs(pos_p), _nsa_rope_tabs(pos_s)
    even_modes = [ROPE_RET_Q] * heads + [ROPE_RET_K] * heads + [ROPE_NONE] * (4 * heads)
    even_outs = [(0, 6 * heads * LANE, 1.0, F32, None)]
    kscale = LANE ** -0.5
    odd_cols = o_w_in.shape[2]
    odd_pad = -(-odd_cols // LANE) * LANE
    rows_last = lambda a: jnp.transpose(a, (0, 1, 3, 4, 5, 2))
    rows_first = lambda a: jnp.transpose(a.reshape((a.shape[0],) + kv_shape + (a.shape[-1],)), (0, 4, 1, 2, 3))
    pool_c, pool_s, win = rows_last(cache_cmp), rows_last(cache_sel), rows_last(state_win)
    kv_shape = cache_cmp.shape[3:]
    keep = min(WINDOW, s)

    out = {k: [] for k in ("ret_p", "ret_s", "sgv", "cmp_p", "cmp_s", "sel_p", "sel_s", "win_p", "win_s", "ffn_p", "ffn_s")}
    for l in range(depth):
        i = l // 2
        if l % 2 == 0:
            w_in = e_w_in[i].astype(BF16)
            w_out = e_w_out[i].astype(BF16)
            (pp,) = proj_in(xp, norm_mix_pre[l], w_in, (ret_p[0], ret_p[1], ret_p[1]), even_modes, even_outs, kscale)
            (ps,) = proj_in(xs, norm_mix_pre[l], w_in, (ret_s[0], ret_s[1], ret_s[1]), even_modes, even_outs, kscale)
            cat_p, st_p = even_seq(pp, b, s, e_sg_ln_g[i], e_sg_ln_b[i], e_sg_w[i], e_sg_b[i], heads, groups)
            cat_s, st_s, svn_s = even_seq_sample(ps, state_ret, i, n, e_sg_ln_g[i], e_sg_ln_b[i], e_sg_w[i],
                                                 e_sg_b[i], heads, groups)
            xp = proj_out([cat_p], w_out, norm_mix_post[l], xp)
            xs = proj_out([cat_s], w_out, norm_mix_post[l], xs)
            out["ret_p"].append(st_p)
            out["ret_s"].append(st_s)
            out["sgv"].append(svn_s.reshape(db, pad, -1)[:, :n])
        else:
            w_in = jnp.pad(o_w_in[i], ((0, 0), (0, odd_pad - odd_cols))).astype(BF16)
            w_out = o_w_out[i].astype(BF16)
            wts, pes = _compress_weights(o_phi_k[i], o_phi_v[i], o_pe_k[i], o_pe_v[i])
            q_p, rc_p, cmp_t, sel_t, win_t, g_p, ks_p, kw_p = proj_in(xp, norm_mix_pre[l], w_in, nsa_p, _odd_modes(),
                                                                   _odd_outs(True), seq_len=s)
            q_s, rc_s, rs_s, rw_s, g_s = proj_in(xs, norm_mix_pre[l], w_in, nsa_s, _odd_modes(), _odd_outs(False))
            kc_p = compress_prompt(rc_p, b, wts, pes)
            o_p = nsa_prompt(q_p, kc_p, ks_p, kw_p, g_p, b, s)
            kc_s = compress_pages(pool_c, i, page_table, wts, pes)
            o_s, win_s = nsa_sample(q_s, kc_s, pool_s, i, page_table, rs_s, rw_s, win, g_s, n)
            xp = proj_out([o_p], w_out, norm_mix_post[l], xp)
            xs = proj_out([o_s], w_out, norm_mix_post[l], xs)
            out["cmp_p"].append(rows_first(cmp_t))
            out["sel_p"].append(rows_first(sel_t))
            out["win_p"].append(rows_first(win_t[:, :, s - keep:]))
            out["cmp_s"].append(rc_s.reshape((db, pad) + kv_shape)[:, :n])
            out["sel_s"].append(rs_s.reshape((db, pad) + kv_shape)[:, :n])
            out["win_s"].append(rows_first(win_s))
        w_in = f_w_in[l].astype(BF16)
        w_out = f_w_out[l].astype(BF16)
        xp, st_p = conv_ffn(xp, s, norm_ffn_pre[l], w_in, f_conv_w[l], f_conv_b[l], w_out, norm_ffn_post[l])
        xs, st_s = conv_ffn(xs, pad, norm_ffn_pre[l], w_in, f_conv_w[l], f_conv_b[l], w_out, norm_ffn_post[l],
                            prev=_ffn_prev(state_ffn[l]), tm=256)
        out["ffn_p"].append(st_p.reshape(b, 8, -1)[:, 8 - (CONV_W - 1):])
        out["ffn_s"].append(st_s.reshape(db, pad, -1)[:, n - (CONV_W - 1):n])

    stack = lambda k: jnp.stack(out[k])
    return (xp.reshape(b, s, d), xs.reshape(db, pad, d)[:, :n], stack("ret_p"), stack("ret_s"), stack("sgv"),
            stack("cmp_p"), stack("cmp_s"), stack("sel_p"), stack("sel_s"), stack("win_p"), stack("win_s"),
            stack("ffn_p"), stack("ffn_s"))
```

```python
import functools

import jax
import jax.numpy as jnp
from jax import lax
from jax.experimental import pallas as pl
from jax.experimental.pallas import tpu as pltpu

F32 = jnp.float32
BF16 = jnp.bfloat16

EPS = 1e-6
NEG = -1e30
FORCE = 1e9

LANE = 128
VMEM_LIMIT = 56 * 1024 * 1024

RET_CHUNK = 128
RET_THETA = 10000.0
SG_CHUNK = 128
HEAD_DIM = 64
NSA_KV = 4
NSA_REP = 4
CMP_LEN = 32
CMP_STRIDE = 16
SEL_BLK = 64
SEL_TOPK = 16
WINDOW = 512
ROPE_DIM = HEAD_DIM // 4
ROPE_THETA = 500000.0
ATTN_SCALE = HEAD_DIM ** -0.5
CONV_W = 3

SAMPLE_PAD = 8

ROPE_NONE, ROPE_RET_Q, ROPE_RET_K, ROPE_NSA = 0, 1, 2, 3
PAD_HEADS = -1


def _params(*sem):
    return pltpu.CompilerParams(dimension_semantics=sem, vmem_limit_bytes=VMEM_LIMIT)


def _rms(x, g):
    return x * lax.rsqrt(jnp.mean(x * x, -1, keepdims=True) + EPS) * g


def _col_chunk(n):
    for c in (512, 384, 256, 128):
        if n % c == 0:
            return c
    raise ValueError(n)


def _proj_in_body(x_ref, g_ref, w_ref, tc_ref, ta_ref, tb_ref, *o_refs, modes, kscale, outs):
    h = _rms(x_ref[...], g_ref[...]).astype(BF16)
    n = w_ref.shape[1]
    cw = _col_chunk(n)
    for c0 in range(0, n, cw):
        y = jnp.dot(h, w_ref[:, c0:c0 + cw], preferred_element_type=F32)
        for j in range(cw // LANE):
            col = c0 + j * LANE
            blk = y[:, j * LANE:(j + 1) * LANE]
            mode = modes[col // LANE]
            if mode in (ROPE_RET_Q, ROPE_RET_K):
                blk = blk * tc_ref[...] + pltpu.roll(blk, LANE // 2, 1) * ta_ref[...]
                if mode == ROPE_RET_K:
                    blk = blk * kscale
            elif mode == ROPE_NSA:
                blk = (blk * tc_ref[...] + pltpu.roll(blk, LANE - ROPE_DIM // 2, 1) * ta_ref[...]
                       + pltpu.roll(blk, ROPE_DIM // 2, 1) * tb_ref[...])
            for o_ref, (oc, ow, osc, tw) in zip(o_refs, outs):
                if oc <= col < oc + ow:
                    v = blk if osc == 1.0 else blk * osc
                    cs = slice(col - oc, col - oc + LANE)
                    if tw is None:
                        o_ref[:, cs] = v.astype(o_ref.dtype)
                    elif tw == PAD_HEADS:
                        low = lax.broadcasted_iota(jnp.int32, v.shape, 1) < HEAD_DIM
                        c2 = 2 * (col - oc)
                        o_ref[:, c2:c2 + LANE] = jnp.where(low, v, 0.0).astype(o_ref.dtype)
                        o_ref[:, c2 + LANE:c2 + 2 * LANE] = jnp.where(low, pltpu.roll(v, HEAD_DIM, 1), 0.0).astype(o_ref.dtype)
                    elif tw == 0:
                        o_ref[cs, :] = v.T.astype(o_ref.dtype)
                    else:
                        for s in range(v.shape[0] // tw):
                            o_ref[s, cs, :] = v[s * tw:(s + 1) * tw, :].T.astype(o_ref.dtype)


def proj_in(x, g, w, tabs, modes, outs, kscale=1.0, tm=512, seq_len=None):
    m, d = x.shape
    n = w.shape[1]
    tm = min(tm, m)
    nt = tabs[0].shape[0] // tm
    tab_spec = pl.BlockSpec((tm, LANE), lambda i: (i % nt, 0))
    tps = (seq_len // tm) if seq_len else 1
    specs, shapes = [], []
    for (_, ow, _, dt, tw) in outs:
        if tw is None:
            specs.append(pl.BlockSpec((tm, ow), lambda i: (i, 0)))
            shapes.append(jax.ShapeDtypeStruct((m, ow), dt))
        elif tw == PAD_HEADS:
            specs.append(pl.BlockSpec((tm, 2 * ow), lambda i: (i, 0)))
            shapes.append(jax.ShapeDtypeStruct((m, 2 * ow), dt))
        elif tw == 0:
            specs.append(pl.BlockSpec((None, ow, tm), lambda i: (i // tps, 0, i % tps)))
            shapes.append(jax.ShapeDtypeStruct((m // seq_len, ow, seq_len), dt))
        else:
            specs.append(pl.BlockSpec((None, tm // tw, ow, tw), lambda i: (i // tps, i % tps, 0, 0)))
            shapes.append(jax.ShapeDtypeStruct((m // seq_len, seq_len // tw, ow, tw), dt))
    return pl.pallas_call(
        functools.partial(_proj_in_body, modes=tuple(modes), kscale=kscale,
                          outs=tuple((o[0], o[1], o[2], o[4]) for o in outs)),
        grid=(m // tm,),
        in_specs=[pl.BlockSpec((tm, d), lambda i: (i, 0)),
                  pl.BlockSpec((1, d), lambda i: (0, 0)),
                  pl.BlockSpec((d, n), lambda i: (0, 0)),
                  tab_spec, tab_spec, tab_spec],
        out_specs=specs,
        out_shape=shapes,
        compiler_params=_params("parallel"),
        name="proj_in",
    )(x, g.reshape(1, d), w, *tabs)


def _ret_rope_tabs(pos):
    inv = 1.0 / (RET_THETA ** jnp.linspace(0.0, 1.0, LANE // 2))
    ang = pos.astype(F32)[:, None] * inv[None, :]
    cos, sin = jnp.cos(ang), jnp.sin(ang)
    return jnp.concatenate([cos, cos], -1), jnp.concatenate([-sin, sin], -1)


def _nsa_rope_tabs(pos):
    hr = ROPE_DIM // 2
    inv = 1.0 / (ROPE_THETA ** (jnp.arange(0, ROPE_DIM, 2, dtype=F32) / ROPE_DIM))
    ang = pos.astype(F32)[:, None] * inv[None, :]
    cos, sin = jnp.cos(ang), jnp.sin(ang)
    n = pos.shape[0]
    rest = HEAD_DIM - ROPE_DIM
    c = jnp.concatenate([cos, cos, jnp.ones((n, rest), F32)], -1)
    a = jnp.concatenate([-sin, jnp.zeros((n, hr + rest), F32)], -1)
    b = jnp.concatenate([jnp.zeros((n, hr), F32), sin, jnp.zeros((n, rest), F32)], -1)
    rep = LANE // HEAD_DIM
    return jnp.tile(c, (1, rep)), jnp.tile(a, (1, rep)), jnp.tile(b, (1, rep))


def _ret_tabs(c, n_valid, heads):
    log_g = jnp.log(1.0 - 2.0 ** (-5.0 - jnp.arange(heads, dtype=F32)))
    idx = jnp.arange(c, dtype=F32)
    diff = idx[:, None] - idx[None, :]
    ok = (diff >= 0) & (idx[None, :] < n_valid)
    dmask = jnp.where(ok, jnp.exp(log_g[:, None, None] * jnp.maximum(diff, 0.0)), 0.0)
    qdec = jnp.exp(log_g[:, None] * (idx + 1.0))
    kdec = jnp.where(idx < n_valid, jnp.exp(log_g[:, None] * (n_valid - 1.0 - idx)), 0.0)
    cdec = jnp.exp(log_g * n_valid)
    bc = lambda t: jnp.broadcast_to(t[..., None], t.shape + (LANE,))
    dm = dmask if c == LANE else jnp.pad(dmask, ((0, 0), (0, 0), (0, LANE - c)))
    return dm, bc(qdec), bc(kdec), jnp.broadcast_to(cdec[:, None, None], (heads, 8, LANE))


def _layer_norm_rows(x, g, b):
    xc = x - jnp.mean(x, -1, keepdims=True)
    return xc * lax.rsqrt(jnp.mean(xc * xc, -1, keepdims=True) + EPS) * g + b


def _rms_unit(x):
    return x * lax.rsqrt(jnp.mean(x * x, -1, keepdims=True) + EPS)


def _even_seq_body(q_ref, k_ref, v_ref, g_ref, u_ref, sv_ref, dm_ref, qd_ref, kd_ref, cd_ref,
                   lng_ref, lnb_ref, wm_ref, sgb_ref, o_ref, st_ref, s_scr, *, tq, heads, groups):
    t = pl.program_id(1)
    c = RET_CHUNK

    @pl.when(t == 0)
    def _():
        s_scr[...] = jnp.zeros(s_scr.shape, F32)

    for ci in range(tq // c):
        rows = slice(ci * c, (ci + 1) * c)
        for h in range(heads):
            cols = slice(h * LANE, (h + 1) * LANE)
            qc, kc, vc = q_ref[rows, cols], k_ref[rows, cols], v_ref[rows, cols]
            vb = vc.astype(BF16)
            s = s_scr[h]
            inner = lax.dot_general(qc.astype(BF16), kc.astype(BF16), (((1,), (1,)), ((), ())),
                                    preferred_element_type=F32) * dm_ref[h]
            o = (jnp.dot(inner.astype(BF16), vb, preferred_element_type=F32)
                 + jnp.dot((qc * qd_ref[h]).astype(BF16), s.astype(BF16), preferred_element_type=F32))
            s_scr[h] = s * cd_ref[h][0:1, :] + lax.dot_general(
                (kc * kd_ref[h]).astype(BF16), vb, (((0,), (0,)), ((), ())), preferred_element_type=F32)
            gg = g_ref[rows, cols]
            o_ref[rows, cols] = gg * jax.nn.sigmoid(gg) * _rms_unit(o)
        for gi in range(groups):
            cols = slice(gi * LANE, (gi + 1) * LANE)
            svn = _layer_norm_rows(sv_ref[rows, cols], lng_ref[:, cols], lnb_ref[:, cols])
            mixed = jnp.dot(wm_ref[gi], svn.astype(BF16), preferred_element_type=F32) + sgb_ref[gi]
            o_ref[rows, heads * LANE + gi * LANE:heads * LANE + (gi + 1) * LANE] = u_ref[rows, cols] * mixed

    @pl.when(t == pl.num_programs(1) - 1)
    def _():
        st_ref[...] = s_scr[...]


def even_seq(p, nseq, seq_len, ln_g, ln_b, sg_w, sg_b, heads, groups, tq=512):
    m = p.shape[0]
    w = heads * LANE
    c = RET_CHUNK
    nt = seq_len // tq
    dm, qd, kd, cd = _ret_tabs(c, c, heads)
    wm = jnp.tril(sg_w[:, :c, :c]).astype(BF16)
    sgb = jnp.broadcast_to(sg_b[:, :c, None], (groups, c, LANE))
    part = lambda j: pl.BlockSpec((tq, w), lambda b, t: (b * nt + t, j))
    full = lambda a: pl.BlockSpec(a.shape, lambda b, t: (0,) * a.ndim)
    return pl.pallas_call(
        functools.partial(_even_seq_body, tq=tq, heads=heads, groups=groups),
        grid=(nseq, nt),
        in_specs=[part(j) for j in range(6)] + [full(dm), full(qd), full(kd), full(cd),
                                                pl.BlockSpec((1, w), lambda b, t: (0, 0)),
                                                pl.BlockSpec((1, w), lambda b, t: (0, 0)),
                                                full(wm), full(sgb)],
        out_specs=[pl.BlockSpec((tq, 2 * w), lambda b, t: (b * nt + t, 0)),
                   pl.BlockSpec((None, heads, LANE, LANE), lambda b, t: (b, 0, 0, 0))],
        out_shape=[jax.ShapeDtypeStruct((m, 2 * w), F32),
                   jax.ShapeDtypeStruct((nseq, heads, LANE, LANE), F32)],
        scratch_shapes=[pltpu.VMEM((heads, LANE, LANE), F32)],
        compiler_params=_params("arbitrary", "arbitrary"),
        name="even_seq",
    )(p, p, p, p, p, p, dm, qd, kd, cd, ln_g.reshape(1, w), ln_b.reshape(1, w), wm, sgb)


def _even_seq_sample_body(q_ref, k_ref, v_ref, g_ref, u_ref, sv_ref, s0_ref, dm_ref, qd_ref, kd_ref, cd_ref,
                          lng_ref, lnb_ref, wm_ref, sgb_ref, o_ref, st_ref, svn_ref, *, nb, n, heads, groups):
    np_ = SAMPLE_PAD

    def one(b, carry):
        rows = pl.ds(pl.multiple_of(b * np_, np_), np_)
        for h in range(heads):
            cols = slice(h * LANE, (h + 1) * LANE)
            q, k, v = q_ref[rows, cols], k_ref[rows, cols], v_ref[rows, cols]
            s = s0_ref[b, h]
            o = jnp.dot((q * qd_ref[h]).astype(BF16), s.astype(BF16), preferred_element_type=F32)
            dm = dm_ref[h]
            for j in range(n):
                inner = jnp.sum(q * k[j:j + 1, :], axis=-1, keepdims=True)
                o = o + (inner * dm[:, j:j + 1]) * v[j:j + 1, :]
            st_ref[b, h] = s * cd_ref[h][0:1, :] + lax.dot_general(
                (k * kd_ref[h]).astype(BF16), v.astype(BF16), (((0,), (0,)), ((), ())),
                preferred_element_type=F32)
            gg = g_ref[rows, cols]
            o_ref[rows, cols] = gg * jax.nn.sigmoid(gg) * _rms_unit(o)
        for gi in range(groups):
            cols = slice(gi * LANE, (gi + 1) * LANE)
            svn = _layer_norm_rows(sv_ref[rows, cols], lng_ref[:, cols], lnb_ref[:, cols])
            svn_ref[rows, cols] = svn
            wm = wm_ref[gi]
            mixed = sgb_ref[gi]
            for j in range(n):
                mixed = mixed + wm[:, j:j + 1] * svn[j:j + 1, :]
            o_ref[rows, heads * LANE + gi * LANE:heads * LANE + (gi + 1) * LANE] = u_ref[rows, cols] * mixed
        return carry

    lax.fori_loop(0, nb, one, 0)


def even_seq_sample(p, s0, layer, n, ln_g, ln_b, sg_w, sg_b, heads, groups, nb=8):
    m = p.shape[0]
    db = m // SAMPLE_PAD
    w = heads * LANE
    dm, qd, kd, cd = _ret_tabs(SAMPLE_PAD, n, heads)
    wm = jnp.pad(jnp.tril(sg_w[:, :n, :n]), ((0, 0), (0, SAMPLE_PAD - n), (0, LANE - n)))
    sgb = jnp.broadcast_to(jnp.pad(sg_b[:, :n], ((0, 0), (0, SAMPLE_PAD - n)))[:, :, None], (groups, SAMPLE_PAD, LANE))
    rows = nb * SAMPLE_PAD
    part = lambda j: pl.BlockSpec((rows, w), lambda i: (i, j))
    full = lambda a: pl.BlockSpec(a.shape, lambda i: (0,) * a.ndim)
    st_spec = pl.BlockSpec((nb, heads, LANE, LANE), lambda i: (i, 0, 0, 0))
    return pl.pallas_call(
        functools.partial(_even_seq_sample_body, nb=nb, n=n, heads=heads, groups=groups),
        grid=(db // nb,),
        in_specs=[part(j) for j in range(6)] + [pl.BlockSpec((None, nb, heads, LANE, LANE),
                                                             lambda i: (layer, i, 0, 0, 0)),
                                                full(dm), full(qd), full(kd), full(cd),
                                                pl.BlockSpec((1, w), lambda i: (0, 0)),
                                                pl.BlockSpec((1, w), lambda i: (0, 0)),
                                                full(wm), full(sgb)],
        out_specs=[pl.BlockSpec((rows, 2 * w), lambda i: (i, 0)), st_spec,
                   pl.BlockSpec((rows, w), lambda i: (i, 0))],
        out_shape=[jax.ShapeDtypeStruct((m, 2 * w), F32),
                   jax.ShapeDtypeStruct(s0.shape[1:], F32),
                   jax.ShapeDtypeStruct((m, w), F32)],
        compiler_params=_params("parallel"),
        name="even_seq_sample",
    )(p, p, p, p, p, p, s0, dm, qd, kd, cd, ln_g.reshape(1, w), ln_b.reshape(1, w), wm, sgb)


def _proj_out_body(*refs, n_a):
    w_ref, g_ref, x_ref, o_ref = refs[n_a:]
    a = refs[0][...]
    for r in refs[1:n_a]:
        a = a + r[...]
    y = jnp.dot(a.astype(BF16), w_ref[...], preferred_element_type=F32)
    o_ref[...] = x_ref[...] + _rms(y, g_ref[...])


def proj_out(a_list, w, g, x, tm=512):
    m, d = x.shape
    k = w.shape[0]
    tm = min(tm, m)
    row = lambda i: (i, 0)
    return pl.pallas_call(
        functools.partial(_proj_out_body, n_a=len(a_list)),
        grid=(m // tm,),
        in_specs=[pl.BlockSpec((tm, k), row) for _ in a_list]
        + [pl.BlockSpec((k, d), lambda i: (0, 0)),
           pl.BlockSpec((1, d), lambda i: (0, 0)),
           pl.BlockSpec((tm, d), row)],
        out_specs=pl.BlockSpec((tm, d), row),
        out_shape=jax.ShapeDtypeStruct((m, d), F32),
        compiler_params=_params("parallel"),
        name="proj_out",
    )(*a_list, w, g.reshape(1, d), x)


FFN_HALO = 16
FFN_CHUNK = 2048


def _ffn_body(*refs, tm, tiles_per_seq, sample, nf_static):
    if sample:
        (x_ref, gpre_ref, wa_ref, wb_ref, cwa_ref, cwb_ref, cba_ref, cbb_ref, wo_ref, gpost_ref,
         e1a_ref, e1b_ref, e2a_ref, e2b_ref, o_ref, st_ref, h_scr, upa_scr, upb_scr, acc_scr) = refs
    else:
        (x_ref, halo_ref, gpre_ref, wa_ref, wb_ref, cwa_ref, cwb_ref, cba_ref, cbb_ref, wo_ref, gpost_ref,
         o_ref, st_ref, h_scr, upa_scr, upb_scr, acc_scr) = refs
    i = pl.program_id(0)
    f = pl.program_id(1)
    nf = pl.num_programs(1)
    hl = FFN_HALO

    @pl.when(f == 0)
    def _():
        if sample:
            h_scr[0:hl, :] = jnp.zeros((hl, h_scr.shape[1]), BF16)
        else:
            hh = _rms(halo_ref[...], gpre_ref[...])
            h_scr[0:hl, :] = jnp.where(i % tiles_per_seq == 0, 0.0, hh).astype(BF16)
        h_scr[hl:, :] = _rms(x_ref[...], gpre_ref[...]).astype(BF16)

    h = h_scr[...]
    if sample:
        t = lax.broadcasted_iota(jnp.int32, (tm, 1), 0) % SAMPLE_PAD
        m1 = t >= 1
        m2 = t >= 2

    def conv(up_scr, cw_ref, cb_ref, e1_ref, e2_ref, cols):
        s2 = up_scr[pl.ds(hl - 2, tm), cols]
        s1 = up_scr[pl.ds(hl - 1, tm), cols]
        s0 = up_scr[pl.ds(hl, tm), cols]
        if sample:
            s2 = jnp.where(m2, s2, 0.0) + e2_ref[:, cols]
            s1 = jnp.where(m1, s1, 0.0) + e1_ref[:, cols]
        return cb_ref[:, cols] + s2 * cw_ref[0:1, cols] + s1 * cw_ref[1:2, cols] + s0 * cw_ref[2:3, cols]

    contrib = None
    for c0 in range(0, upa_scr.shape[1], FFN_CHUNK):
        cols = slice(c0, min(c0 + FFN_CHUNK, upa_scr.shape[1]))
        upa_scr[:, cols] = jnp.dot(h, wa_ref[:, cols], preferred_element_type=F32)
        upb_scr[:, cols] = jnp.dot(h, wb_ref[:, cols], preferred_element_type=F32)
        a = conv(upa_scr, cwa_ref, cba_ref, e1a_ref if sample else None, e2a_ref if sample else None, cols)
        b = conv(upb_scr, cwb_ref, cbb_ref, e1b_ref if sample else None, e2b_ref if sample else None, cols)
        act = (jax.nn.gelu(a) * b).astype(BF16)
        part = jnp.dot(act, wo_ref[cols, :], preferred_element_type=F32)
        contrib = part if contrib is None else contrib + part

    @pl.when(f == 0)
    def _():
        acc_scr[...] = contrib

    @pl.when(f > 0)
    def _():
        acc_scr[...] += contrib

    @pl.when(f == nf - 1)
    def _():
        o_ref[...] = x_ref[...] + _rms(acc_scr[...], gpost_ref[...])

    fw = upa_scr.shape[1]
    rows = st_ref.shape[0]
    last = True if sample else (i % tiles_per_seq == tiles_per_seq - 1)
    for j in range(nf_static):
        @pl.when(jnp.logical_and(f == j, last))
        def _(j=j):
            st_ref[:, j * fw:(j + 1) * fw] = upa_scr[pl.ds(hl + tm - rows, rows), :]
            st_ref[:, (nf_static + j) * fw:(nf_static + j + 1) * fw] = upb_scr[pl.ds(hl + tm - rows, rows), :]


def conv_ffn(x, seq_len, gpre, w_in, conv_w, conv_b, w_out, gpost, prev=None, tm=512):
    m, d = x.shape
    ff = w_out.shape[0]
    fw = 1408 if ff % 1408 == 0 else ff
    nf = ff // fw
    sample = prev is not None
    tm = min(tm, m)
    tps = max(seq_len // tm, 1)
    row = lambda i, f: (i, 0)
    const = lambda i, f: (0, 0)
    cola = lambda i, f: (0, f)
    colb = lambda i, f: (0, nf + f)
    in_specs = [pl.BlockSpec((tm, d), row)]
    args = [x]
    if not sample:
        hb = tm // FFN_HALO
        in_specs.append(pl.BlockSpec((FFN_HALO, d), lambda i, f: (jnp.maximum(i * hb - 1, 0), 0)))
        args.append(x)
    in_specs += [pl.BlockSpec((1, d), const),
                 pl.BlockSpec((d, fw), cola), pl.BlockSpec((d, fw), colb),
                 pl.BlockSpec((CONV_W, fw), cola), pl.BlockSpec((CONV_W, fw), colb),
                 pl.BlockSpec((1, fw), cola), pl.BlockSpec((1, fw), colb),
                 pl.BlockSpec((fw, d), lambda i, f: (f, 0)),
                 pl.BlockSpec((1, d), const)]
    args += [gpre.reshape(1, d), w_in, w_in, conv_w, conv_w, conv_b.reshape(1, -1), conv_b.reshape(1, -1),
             w_out, gpost.reshape(1, d)]
    if sample:
        e1, e2 = prev
        in_specs += [pl.BlockSpec((tm, fw), lambda i, f: (i, f)), pl.BlockSpec((tm, fw), lambda i, f: (i, nf + f)),
                     pl.BlockSpec((tm, fw), lambda i, f: (i, f)), pl.BlockSpec((tm, fw), lambda i, f: (i, nf + f))]
        args += [e1, e1, e2, e2]
        st_shape = jax.ShapeDtypeStruct((m, 2 * ff), F32)
        st_spec = pl.BlockSpec((tm, 2 * ff), lambda i, f: (i, 0))
    else:
        nseq = m // seq_len
        st_shape = jax.ShapeDtypeStruct((nseq * 8, 2 * ff), F32)
        st_spec = pl.BlockSpec((8, 2 * ff), lambda i, f: (i // tps, 0))
    return pl.pallas_call(
        functools.partial(_ffn_body, tm=tm, tiles_per_seq=tps, sample=sample, nf_static=nf),
        grid=(m // tm, nf),
        in_specs=in_specs,
        out_specs=[pl.BlockSpec((tm, d), row), st_spec],
        out_shape=[jax.ShapeDtypeStruct((m, d), F32), st_shape],
        scratch_shapes=[pltpu.VMEM((tm + FFN_HALO, d), BF16),
                        pltpu.VMEM((tm + FFN_HALO, fw), F32),
                        pltpu.VMEM((tm + FFN_HALO, fw), F32),
                        pltpu.VMEM((tm, d), F32)],
        compiler_params=_params("arbitrary", "arbitrary"),
        name="conv_ffn_sample" if sample else "conv_ffn",
    )(*args)


def _ffn_prev(prev):
    db, _, w = prev.shape
    z = jnp.zeros((db, SAMPLE_PAD, w), prev.dtype)
    e1 = z.at[:, 0].set(prev[:, 1])
    e2 = z.at[:, 0:2].set(prev)
    return e1.reshape(db * SAMPLE_PAD, w), e2.reshape(db * SAMPLE_PAD, w)


def _odd_modes():
    return [ROPE_NSA] * 8 + [ROPE_NSA, ROPE_NSA, ROPE_NONE, ROPE_NONE] * 3 + [ROPE_NONE]


SEL_TK = 512
WIN_TK = 128


def _odd_outs(prompt):
    if not prompt:
        return [(0, 1024, ATTN_SCALE, F32, None), (1024, 512, 1.0, F32, None), (1536, 512, 1.0, F32, None),
                (2048, 512, 1.0, F32, None), (2560, 128, 1.0, F32, None)]
    return [(0, 1024, ATTN_SCALE, BF16, 0), (1024, 512, 1.0, F32, None), (1024, 512, 1.0, F32, 0),
            (1536, 512, 1.0, F32, 0), (2048, 512, 1.0, F32, 0), (2560, 128, 1.0, F32, 0),
            (1536, 256, 1.0, BF16, PAD_HEADS), (1792, 256, 1.0, BF16, SEL_TK),
            (2048, 256, 1.0, BF16, None), (2304, 256, 1.0, BF16, WIN_TK)]


SUBS = CMP_LEN // CMP_STRIDE
ROW_W = 2 * NSA_KV * HEAD_DIM
SUB_W = CMP_STRIDE * ROW_W


def _compress_weights(phi_k, phi_v, pe_k, pe_v):
    def one(phi, pe):
        p4 = phi.reshape(SUBS, CMP_STRIDE, HEAD_DIM, HEAD_DIM)
        w = jnp.einsum('hlde,gG->lgdhGe', p4, jnp.eye(2, dtype=phi.dtype))
        w = w.reshape(CMP_STRIDE * 2 * HEAD_DIM, SUBS * 2 * HEAD_DIM)
        pr = jnp.broadcast_to(pe.reshape(SUBS, CMP_STRIDE, 1, HEAD_DIM), (SUBS, CMP_STRIDE, 2, HEAD_DIM))
        pr = jnp.pad(pr.reshape(SUBS, -1), ((0, 16 - SUBS), (0, 0)))
        return w, pr
    wk, pk = one(phi_k, pe_k)
    wv, pv = one(phi_v, pe_v)
    return jnp.stack([wk, wv]).astype(BF16), jnp.stack([pk, pv]).astype(BF16)


def _compress_column(xj, j, w_ref, pe_ref, o_ref, ab_scr):
    nsub = o_ref.shape[0]
    half = LANE
    xe = jnp.concatenate([xj, pe_ref[j // 2]], axis=0)
    ab_scr[...] = jnp.dot(xe, w_ref[j // 2], preferred_element_type=F32)
    bias = ab_scr[nsub:nsub + 1, 0:half] + ab_scr[nsub + 1:nsub + 2, half:2 * half]
    o_ref[:, j * LANE:(j + 1) * LANE] = (
        ab_scr[0:nsub, 0:half] + ab_scr[pl.ds(1, nsub), half:2 * half] + bias).astype(o_ref.dtype)


def _compress_body(x_ref, w_ref, pe_ref, o_ref, ab_scr):
    for j in range(ROW_W // LANE):
        xj = jnp.concatenate([x_ref[:, l * ROW_W + j * LANE:l * ROW_W + (j + 1) * LANE]
                              for l in range(CMP_STRIDE)], axis=1).astype(BF16)
        _compress_column(xj, j, w_ref, pe_ref, o_ref, ab_scr)


def _compress_pages_body(pt_ref, *refs, n_x):
    x_refs = refs[:n_x]
    perm_ref, w_ref, pe_ref, o_ref, ab_scr, xs_scr = refs[n_x:]
    page = x_refs[0].shape[-1]
    sub_pp = page // CMP_STRIDE
    kvw = NSA_KV * HEAD_DIM
    for k, xr in enumerate(x_refs):
        for c in range(2):
            t = xr[c].reshape(kvw, page).astype(BF16)
            out = lax.dot_general(perm_ref[...], t, (((1,), (1,)), ((), ())), preferred_element_type=F32)
            for gp in range(kvw // LANE):
                for l in range(CMP_STRIDE):
                    xs_scr[c * (kvw // LANE) + gp, k * sub_pp:(k + 1) * sub_pp, l * LANE:(l + 1) * LANE] = (
                        out[l * sub_pp:(l + 1) * sub_pp, gp * LANE:(gp + 1) * LANE])
    for j in range(ROW_W // LANE):
        _compress_column(xs_scr[j].astype(BF16), j, w_ref, pe_ref, o_ref, ab_scr)


def compress_prompt(rows, nseq, wts, pes):
    nsub = rows.shape[0] // nseq // CMP_STRIDE
    x = rows.reshape(nseq * nsub, SUB_W)
    return pl.pallas_call(
        _compress_body,
        grid=(nseq,),
        in_specs=[pl.BlockSpec((nsub, SUB_W), lambda b: (b, 0)),
                  pl.BlockSpec(wts.shape, lambda b: (0, 0, 0)),
                  pl.BlockSpec(pes.shape, lambda b: (0, 0, 0))],
        out_specs=pl.BlockSpec((nsub, ROW_W), lambda b: (b, 0)),
        out_shape=jax.ShapeDtypeStruct((nseq * nsub, ROW_W), BF16),
        scratch_shapes=[pltpu.VMEM((nsub + 16, 2 * LANE), F32)],
        compiler_params=_params("parallel"),
        name="compress_prompt",
    )(x, wts, pes)


def compress_pages(pool, layer, page_table, wts, pes, nb=2):
    db, n_pages = page_table.shape
    page = pool.shape[-1]
    sub_pp = page // CMP_STRIDE
    nsub = nb * n_pages * sub_pp
    out_row = jnp.arange(page)
    perm = (jnp.arange(page)[None, :] == ((out_row % sub_pp) * CMP_STRIDE + out_row // sub_pp)[:, None]).astype(BF16)
    specs = [pl.BlockSpec((None, None) + pool.shape[2:],
                          functools.partial(lambda i, pt, s, k: (layer, pt[(i * nb + s) * n_pages + k], 0, 0, 0, 0),
                                            s=s, k=k))
             for s in range(nb) for k in range(n_pages)]
    grid_spec = pltpu.PrefetchScalarGridSpec(
        num_scalar_prefetch=1,
        grid=(db // nb,),
        in_specs=specs + [pl.BlockSpec(perm.shape, lambda i, pt: (0, 0)),
                          pl.BlockSpec(wts.shape, lambda i, pt: (0, 0, 0)),
                          pl.BlockSpec(pes.shape, lambda i, pt: (0, 0, 0))],
        out_specs=pl.BlockSpec((nsub, ROW_W), lambda i, pt: (i, 0)),
        scratch_shapes=[pltpu.VMEM((nsub + 16, 2 * LANE), F32),
                        pltpu.VMEM((ROW_W // LANE, nsub, CMP_STRIDE * LANE), F32)],
    )
    return pl.pallas_call(
        functools.partial(_compress_pages_body, n_x=nb * n_pages),
        grid_spec=grid_spec,
        out_shape=jax.ShapeDtypeStruct((db * n_pages * sub_pp, ROW_W), BF16),
        compiler_params=_params("parallel"),
        name="compress_pages",
    )(page_table.reshape(-1), *([pool] * (nb * n_pages)), perm, wts, pes)


def _cover(nsub, nc, n_sel):
    c_start = jnp.arange(nsub) * CMP_STRIDE
    s_start = jnp.arange(SEL_BLK) * SEL_BLK
    ok = ((c_start[:, None] < s_start[None, :] + SEL_BLK) & (c_start[:, None] + CMP_LEN > s_start[None, :])
          & (jnp.arange(nsub)[:, None] < nc) & (jnp.arange(SEL_BLK)[None, :] < n_sel))
    return ok.astype(BF16)


def _expand(n_keys, tk):
    key = jnp.arange(n_keys).reshape(n_keys // tk, 1, tk)
    return (key // SEL_BLK == jnp.arange(SEL_BLK)[None, :, None]).astype(BF16)


def _masked_softmax(s, ok):
    sm = jnp.where(ok, s, NEG)
    ex = jnp.exp(sm - jnp.max(sm, -1, keepdims=True))
    return ex / jnp.sum(ex, -1, keepdims=True)


def _importance(psum, cov_ref, tpos, n_sel):
    hi = psum.astype(BF16)
    lo = (psum - hi.astype(F32)).astype(BF16)
    imp = (jnp.dot(hi, cov_ref[...], preferred_element_type=F32)
           + jnp.dot(lo, cov_ref[...], preferred_element_type=F32))
    jl = lax.broadcasted_iota(jnp.int32, imp.shape, 1)
    cur = tpos // SEL_BLK
    forced = (jl == 0) | (jl == cur) | (jl == cur - 1)
    causal = jl * SEL_BLK <= tpos
    imp = jnp.where(forced, FORCE, jnp.where(causal, imp, -1.0))
    return jnp.where(jl < n_sel, imp, -2.0)


def _topk_masks(imps):
    out = []
    for pair in range(len(imps) // 2):
        xt = jnp.concatenate([imps[2 * pair], imps[2 * pair + 1]], axis=1).T
        halves = []
        for hh in range(2):
            xa = xt[SEL_BLK * hh:SEL_BLK * (hh + 1)]
            jrow = lax.broadcasted_iota(jnp.int32, xa.shape, 0)
            cnt = jnp.zeros(xa.shape, F32)
            for i in range(SEL_BLK):
                xi = xa[i:i + 1, :]
                tie = jnp.where(jrow > i, 1.0, 0.0)
                cnt = cnt + jnp.where(xi > xa, 1.0, jnp.where(xi == xa, tie, 0.0))
            halves.append(jnp.where(cnt < SEL_TOPK, 1.0, 0.0))
        sel2 = jnp.concatenate(halves, axis=0).T
        out += [sel2[:, 0:SEL_BLK], sel2[:, SEL_BLK:2 * SEL_BLK]]
    return out


def _topk_masks_rows(imps):
    out = []
    for x in imps:
        jl = lax.broadcasted_iota(jnp.int32, x.shape, 1)
        cnt = jnp.zeros(x.shape, F32)
        for i in range(SEL_BLK):
            xi = x[:, i:i + 1]
            tie = jnp.where(jl > i, 1.0, 0.0)
            cnt = cnt + jnp.where(xi > x, 1.0, jnp.where(xi == x, tie, 0.0))
        out.append(jnp.where(cnt < SEL_TOPK, 1.0, 0.0))
    return out


_NT = (((1,), (1,)), ((), ()))


def _nsa_prompt_body(q_ref, kc_ref, ks_ref, kw_ref, gt_ref, cov_ref, e_ref, o_ref, *, tq, nc, n_sel):
    t0 = pl.program_id(1) * tq
    tpos = t0 + lax.broadcasted_iota(jnp.int32, (tq, 1), 0)
    sig = jax.nn.sigmoid(gt_ref[...])
    nsub = kc_ref.shape[0]
    rep, hd, kvw = NSA_REP, HEAD_DIM, NSA_KV * HEAD_DIM
    mcol = lax.broadcasted_iota(jnp.int32, (tq, nsub), 1)
    valid = (mcol * CMP_STRIDE + CMP_LEN - 1 <= tpos) & (mcol < nc)
    validf = jnp.where(valid, 1.0, 0.0)

    q_st, o_cmp, imps = [], [], []
    for g in range(NSA_KV):
        qg = jnp.concatenate([q_ref[:, (g * rep + r) * hd:(g * rep + r + 1) * hd] for r in range(rep)], axis=0)
        q_st.append(qg)
        s = lax.dot_general(qg, kc_ref[:, g * hd:(g + 1) * hd], _NT, preferred_element_type=F32)
        p = _masked_softmax(s.reshape(rep, tq, nsub), valid[None]) * validf[None]
        o_cmp.append(jnp.dot(p.reshape(rep * tq, nsub).astype(BF16), kc_ref[:, kvw + g * hd:kvw + (g + 1) * hd],
                             preferred_element_type=F32))
        imps.append(_importance(jnp.sum(p, axis=0), cov_ref, tpos, n_sel))
    sels = _topk_masks(imps)

    tk = ks_ref.shape[-1]
    wt = kw_ref.shape[-1]
    kt_d = t0 // tk
    kpos_d = kt_d * tk + lax.broadcasted_iota(jnp.int32, (tq, tk), 1)
    bias_d = jnp.where(kpos_d <= tpos, 0.0, NEG)
    n_wt = (WINDOW + tq) // wt
    wt0 = jnp.maximum(t0 - WINDOW, 0) // wt
    kpos_w = wt0 * wt + lax.broadcasted_iota(jnp.int32, (tq, n_wt * wt), 1)
    bias_w = jnp.where((kpos_w <= tpos) & (kpos_w > tpos - WINDOW), 0.0, NEG)

    def biased(s, bias):
        return (s.reshape(rep, tq, s.shape[-1]) + bias[None]).reshape(s.shape)

    def lanes(x):
        return jnp.broadcast_to(x, (x.shape[0], LANE))

    krows = [slice(g * hd, (g + 1) * hd) for g in range(NSA_KV)]
    vrows = [slice(kvw + g * hd, kvw + (g + 1) * hd) for g in range(NSA_KV)]

    o_win = []
    for g in range(NSA_KV):
        k_w = jnp.concatenate([kw_ref[wt0 + i, krows[g], :] for i in range(n_wt)], axis=1)
        v_w = jnp.concatenate([kw_ref[wt0 + i, vrows[g], :] for i in range(n_wt)], axis=1)
        s = biased(jnp.dot(q_st[g], k_w, preferred_element_type=F32), bias_w)
        p = jnp.exp(s - jnp.max(s, -1, keepdims=True))
        o_win.append((lax.dot_general(p.astype(BF16), v_w, _NT, preferred_element_type=F32)
                      / jnp.sum(p, -1, keepdims=True)).reshape(rep, tq, hd))

    q_aug = []
    for g in range(NSA_KV):
        selneg = jnp.where(sels[g] > 0.5, 0.0, NEG)
        q_aug.append(jnp.concatenate(
            [jnp.concatenate([q_ref[:, (g * rep + r) * hd:(g * rep + r + 1) * hd].astype(F32), selneg], axis=1)
             for r in range(rep)], axis=0).astype(BF16))

    def scores(kt, g):
        k_aug = jnp.concatenate([ks_ref[kt, krows[g], :], e_ref[kt]], axis=0)
        return jnp.dot(q_aug[g], k_aug, preferred_element_type=F32)

    init = []
    for g in range(NSA_KV):
        s = biased(scores(kt_d, g), bias_d)
        m0 = jnp.max(s, -1, keepdims=True)
        p = jnp.exp(s - m0)
        l0 = jnp.sum(p, -1, keepdims=True)
        acc0 = lax.dot_general(p.astype(BF16), ks_ref[kt_d, vrows[g], :], _NT, preferred_element_type=F32)
        init.append((lanes(m0), lanes(l0), acc0))

    def step(kt, carry):
        out = []
        for g in range(NSA_KV):
            m_i, l_i, acc = carry[g]
            s = scores(kt, g)
            m_new = jnp.maximum(m_i, lanes(jnp.max(s, -1, keepdims=True)))
            alpha = jnp.exp(m_i - m_new)
            p = jnp.exp(s - jnp.tile(m_new, (1, tk // LANE)))
            l_new = alpha * l_i + lanes(jnp.sum(p, -1, keepdims=True))
            pv = lax.dot_general(p.astype(BF16), ks_ref[kt, vrows[g], :], _NT, preferred_element_type=F32)
            out.append((m_new, l_new, alpha[:, 0:hd] * acc + pv))
        return tuple(out)

    final = lax.fori_loop(0, kt_d, step, tuple(init))

    nh = NSA_KV * rep
    for g in range(NSA_KV):
        _, l_s, acc_s = final[g]
        o_sel = (acc_s / l_s[:, 0:hd]).reshape(rep, tq, hd)
        o_c = o_cmp[g].reshape(rep, tq, hd)
        for r in range(rep):
            h = g * rep + r
            o_ref[:, h * hd:(h + 1) * hd] = (sig[:, h:h + 1] * o_c[r] + sig[:, nh + h:nh + h + 1] * o_sel[r]
                                            + sig[:, 2 * nh + h:2 * nh + h + 1] * o_win[g][r])


def nsa_prompt(q, kcvc, ks, kw, gates, nseq, seq_len, tq=128):
    m = q.shape[0]
    nsub = kcvc.shape[0] // nseq
    tk, wt = ks.shape[-1], kw.shape[-1]
    nc = (seq_len - CMP_LEN) // CMP_STRIDE + 1
    n_sel = seq_len // SEL_BLK
    assert n_sel <= SEL_BLK and seq_len >= WINDOW + tq and tk % tq == 0 and tq % wt == 0 and WINDOW % wt == 0
    nq = seq_len // tq
    cov = _cover(nsub, nc, n_sel)
    exp = _expand(seq_len, tk)
    tile = lambda w: pl.BlockSpec((tq, w), lambda b, t: (b * nq + t, 0))
    whole = lambda a: pl.BlockSpec((None,) + a.shape[1:], lambda b, t: (b, 0, 0, 0))
    return pl.pallas_call(
        functools.partial(_nsa_prompt_body, tq=tq, nc=nc, n_sel=n_sel),
        grid=(nseq, nq),
        in_specs=[tile(q.shape[1]), pl.BlockSpec((nsub, ROW_W), lambda b, t: (b, 0)), whole(ks), whole(kw), tile(LANE),
                  pl.BlockSpec(cov.shape, lambda b, t: (0, 0)),
                  pl.BlockSpec(exp.shape, lambda b, t: (0, 0, 0))],
        out_specs=tile(q.shape[1]),
        out_shape=jax.ShapeDtypeStruct((m, q.shape[1]), F32),
        compiler_params=_params("parallel", "parallel"),
        name="nsa_prompt",
    )(q, kcvc, ks, kw, gates, cov, exp)


def _softmax_cols(s):
    p = jnp.exp(s - jnp.max(s, axis=0, keepdims=True))
    return p, jnp.sum(p, axis=0, keepdims=True)


def _nsa_prompt_t_body(q_ref, kc_ref, ka_ref, vs_ref, kw_ref, vw_ref, gt_ref, cov_ref, et_ref, o_ref, *, tq, nc, n_sel):
    t0 = pl.program_id(1) * tq
    tl = t0 + lax.broadcasted_iota(jnp.int32, (1, tq), 1)
    sig = jax.nn.sigmoid(gt_ref[...])
    nsub = kc_ref.shape[0]
    rep, hd = NSA_REP, HEAD_DIM
    nh = NSA_KV * rep
    tk = vs_ref.shape[-1]
    wt = vw_ref.shape[-1]
    wide = lambda x: jnp.concatenate([x] * rep, axis=1)
    q4 = [jnp.concatenate([q_ref[(g * rep + r) * hd:(g * rep + r + 1) * hd, :] for r in range(rep)], axis=1)
          for g in range(NSA_KV)]

    mrow = lax.broadcasted_iota(jnp.int32, (nsub, 1), 0)
    valid = (mrow * CMP_STRIDE + CMP_LEN - 1 <= tl) & (mrow < nc)
    validf = wide(jnp.where(valid, 1.0, 0.0))
    jrow = lax.broadcasted_iota(jnp.int32, (SEL_BLK, tq), 0)
    cur = tl // SEL_BLK
    forced = (jrow == 0) | (jrow == cur) | (jrow == cur - 1)
    causal = jrow * SEL_BLK <= tl
    o_cmp, sel_neg = [], []
    for g in range(NSA_KV):
        kc = kc_ref[:, g * hd:(g + 1) * hd]
        vc_t = kc_ref[:, NSA_KV * hd + g * hd:NSA_KV * hd + (g + 1) * hd].astype(F32).T.astype(BF16)
        s = jnp.dot(kc, q4[g], preferred_element_type=F32)
        s = jnp.where(validf > 0.5, s, NEG)
        ex, l = _softmax_cols(s)
        p = ex / l * validf
        o_cmp.append(jnp.dot(vc_t, p.astype(BF16), preferred_element_type=F32))
        psum = p[:, 0:tq]
        for r in range(1, rep):
            psum = psum + p[:, r * tq:(r + 1) * tq]
        hi = psum.astype(BF16)
        lo = (psum - hi.astype(F32)).astype(BF16)
        imp = (jnp.dot(cov_ref[...], hi, preferred_element_type=F32)
               + jnp.dot(cov_ref[...], lo, preferred_element_type=F32))
        imp = jnp.where(forced, FORCE, jnp.where(causal, imp, -1.0))
        imp = jnp.where(jrow < n_sel, imp, -2.0)
        cnt = jnp.zeros(imp.shape, F32)
        for i in range(SEL_BLK):
            xi = imp[i:i + 1, :]
            tie = jnp.where(jrow > i, 1.0, 0.0)
            cnt = cnt + jnp.where(xi > imp, 1.0, jnp.where(xi == imp, tie, 0.0))
        sel_neg.append(jnp.where(cnt < SEL_TOPK, 0.0, NEG).astype(BF16))

    n_wt = (WINDOW + tq) // wt
    w0 = jnp.maximum(t0 - WINDOW, 0)
    wt0 = w0 // wt
    wrows = n_wt * wt
    kpos_w = w0 + lax.broadcasted_iota(jnp.int32, (wrows, 1), 0)
    bias_w = wide(jnp.where((kpos_w <= tl) & (kpos_w > tl - WINDOW), 0.0, NEG))
    o_win = []
    for g in range(NSA_KV):
        k_w = kw_ref[pl.ds(pl.multiple_of(w0, wt), wrows), g * hd:(g + 1) * hd]
        v_w = jnp.concatenate([vw_ref[wt0 + i, g * hd:(g + 1) * hd, :] for i in range(n_wt)], axis=1)
        p, l = _softmax_cols(jnp.dot(k_w, q4[g], preferred_element_type=F32) + bias_w)
        o_win.append(jnp.dot(v_w, p.astype(BF16), preferred_element_type=F32) / l)

    q_aug = [jnp.concatenate(
        [jnp.concatenate([q_ref[(g * rep + r) * hd:(g * rep + r + 1) * hd, :], sel_neg[g]], axis=0)
         for r in range(rep)], axis=1) for g in range(NSA_KV)]

    def keys(kt, g):
        rows = pl.ds(pl.multiple_of(kt * tk, tk), tk)
        return ka_ref[rows, g * LANE:(g + 1) * LANE] + et_ref[rows, :]

    kt_d = t0 // tk
    kpos_d = kt_d * tk + lax.broadcasted_iota(jnp.int32, (tk, 1), 0)
    bias_d = wide(jnp.where(kpos_d <= tl, 0.0, NEG))
    init = []
    for g in range(NSA_KV):
        s = jnp.dot(keys(kt_d, g), q_aug[g], preferred_element_type=F32) + bias_d
        m0 = jnp.max(s, axis=0, keepdims=True)
        p = jnp.exp(s - m0)
        init.append((m0, jnp.sum(p, axis=0, keepdims=True),
                     jnp.dot(vs_ref[kt_d, g * hd:(g + 1) * hd, :], p.astype(BF16), preferred_element_type=F32)))

    def step(kt, carry):
        ng = NSA_KV
        s, soft, out = [None] * ng, [None] * ng, [None] * ng

        def scores(g):
            s[g] = jnp.dot(keys(kt, g), q_aug[g], preferred_element_type=F32)

        def softmax(g):
            m_i, l_i, _ = carry[g]
            m_new = jnp.maximum(m_i, jnp.max(s[g], axis=0, keepdims=True))
            alpha = jnp.exp(m_i - m_new)
            p = jnp.exp(s[g] - m_new)
            soft[g] = (m_new, alpha, alpha * l_i + jnp.sum(p, axis=0, keepdims=True), p.astype(BF16))

        def values(g):
            m_new, alpha, l_new, p = soft[g]
            pv = jnp.dot(vs_ref[kt, g * hd:(g + 1) * hd, :], p, preferred_element_type=F32)
            out[g] = (m_new, l_new, alpha * carry[g][2] + pv)

        scores(0)
        for g in range(ng):
            if g + 1 < ng:
                scores(g + 1)
            softmax(g)
            if g >= 1:
                values(g - 1)
        values(ng - 1)
        return tuple(out)

    final = lax.fori_loop(0, kt_d, step, tuple(init))

    for g in range(NSA_KV):
        _, l_s, acc_s = final[g]
        o_sel = acc_s / l_s
        heads = []
        for r in range(rep):
            h = g * rep + r
            cols = slice(r * tq, (r + 1) * tq)
            heads.append(sig[h:h + 1, :] * o_cmp[g][:, cols] + sig[nh + h:nh + h + 1, :] * o_sel[:, cols]
                         + sig[2 * nh + h:2 * nh + h + 1, :] * o_win[g][:, cols])
        for pair in range(rep // 2):
            o_ref[:, (g * rep + 2 * pair) * hd:(g * rep + 2 * pair + 2) * hd] = (
                jnp.concatenate([heads[2 * pair], heads[2 * pair + 1]], axis=0).T)


def _run_skewed(tasks):
    n = len(tasks)
    tasks[0][0]()
    for i in range(n):
        if i + 1 < n:
            tasks[i + 1][0]()
        tasks[i][1]()
        if i >= 1:
            tasks[i - 1][2]()
    tasks[n - 1][2]()


def _nsa_prompt_lanes_body(q_ref, kc_ref, ka_ref, vs_ref, kw_ref, vw_ref, gt_ref, cov_ref, et_ref, o_ref, *,
                           tq, nc, n_sel):
    t0 = pl.program_id(1) * tq
    tl = t0 + lax.broadcasted_iota(jnp.int32, (1, tq), 1)
    sig = jax.nn.sigmoid(gt_ref[...])
    nsub = kc_ref.shape[0]
    rep, hd, ng = NSA_REP, HEAD_DIM, NSA_KV
    nh = ng * rep
    tk = vs_ref.shape[-1]
    wt = vw_ref.shape[-1]
    wide = lambda x: jnp.concatenate([x] * rep, axis=1)
    head_rows = lambda g, r: q_ref[(g * rep + r) * hd:(g * rep + r + 1) * hd, :]
    q4 = [jnp.concatenate([head_rows(g, r) for r in range(rep)], axis=1) for g in range(ng)]

    mrow = lax.broadcasted_iota(jnp.int32, (nsub, 1), 0)
    validf = wide(jnp.where((mrow * CMP_STRIDE + CMP_LEN - 1 <= tl) & (mrow < nc), 1.0, 0.0))
    jrow = lax.broadcasted_iota(jnp.int32, (SEL_BLK, tq), 0)
    cur = tl // SEL_BLK
    forced = (jrow == 0) | (jrow == cur) | (jrow == cur - 1)
    causal = jrow * SEL_BLK <= tl
    s_cmp, p_cmp, o_cmp, sel_neg = [None] * ng, [None] * ng, [None] * ng, [None] * ng

    def cmp_scores(g):
        s_cmp[g] = jnp.dot(kc_ref[:, g * hd:(g + 1) * hd], q4[g], preferred_element_type=F32)

    def cmp_softmax(g):
        ex, l = _softmax_cols(jnp.where(validf > 0.5, s_cmp[g], NEG))
        p = ex / l * validf
        p_cmp[g] = p.astype(BF16)
        psum = p[:, 0:tq]
        for r in range(1, rep):
            psum = psum + p[:, r * tq:(r + 1) * tq]
        hi = psum.astype(BF16)
        lo = (psum - hi.astype(F32)).astype(BF16)
        imp = (jnp.dot(cov_ref[...], hi, preferred_element_type=F32)
               + jnp.dot(cov_ref[...], lo, preferred_element_type=F32))
        imp = jnp.where(forced, FORCE, jnp.where(causal, imp, -1.0))
        imp = jnp.where(jrow < n_sel, imp, -2.0)
        sub = 8
        slabs = [imp[v * sub:(v + 1) * sub, :] for v in range(SEL_BLK // sub)]
        cnts = [jnp.zeros((sub, tq), F32) for _ in slabs]
        srow = lax.broadcasted_iota(jnp.int32, (sub, tq), 0)
        for i in range(SEL_BLK):
            xi = jnp.broadcast_to(imp[i:i + 1, :], (sub, tq))
            for v, x in enumerate(slabs):
                if v > i // sub:
                    hit = jnp.where(xi >= x, 1.0, 0.0)
                elif v < i // sub:
                    hit = jnp.where(xi > x, 1.0, 0.0)
                else:
                    tie = jnp.where(srow > i % sub, 1.0, 0.0)
                    hit = jnp.where(xi > x, 1.0, jnp.where(xi == x, tie, 0.0))
                cnts[v] = cnts[v] + hit
        cnt = jnp.concatenate(cnts, axis=0)
        sel_neg[g] = jnp.where(cnt < SEL_TOPK, 0.0, NEG).astype(BF16)

    def cmp_values(g):
        vc_t = kc_ref[:, ng * hd + g * hd:ng * hd + (g + 1) * hd].astype(F32).T.astype(BF16)
        o_cmp[g] = jnp.dot(vc_t, p_cmp[g], preferred_element_type=F32)

    n_wt = (WINDOW + tq) // wt
    w0 = jnp.maximum(t0 - WINDOW, 0)
    wt0 = w0 // wt
    wrows = n_wt * wt
    kpos_w = w0 + lax.broadcasted_iota(jnp.int32, (wrows, 1), 0)
    bias_w = wide(jnp.where((kpos_w <= tl) & (kpos_w > tl - WINDOW), 0.0, NEG))
    s_win, p_win, o_win = [None] * ng, [None] * ng, [None] * ng

    def win_scores(g):
        k_w = kw_ref[pl.ds(pl.multiple_of(w0, wt), wrows), g * hd:(g + 1) * hd]
        s_win[g] = jnp.dot(k_w, q4[g], preferred_element_type=F32) + bias_w

    def win_softmax(g):
        p, l = _softmax_cols(s_win[g])
        p_win[g] = (p.astype(BF16), l)

    def win_values(g):
        p, l = p_win[g]
        v_w = jnp.concatenate([vw_ref[wt0 + i, g * hd:(g + 1) * hd, :] for i in range(n_wt)], axis=1)
        o_win[g] = jnp.dot(v_w, p, preferred_element_type=F32) / l

    q_aug = [None] * ng

    def keys(kt, g):
        rows = pl.ds(pl.multiple_of(kt * tk, tk), tk)
        return ka_ref[rows, g * LANE:(g + 1) * LANE] + et_ref[rows, :]

    def sweep_tasks(kt, carry, out, bias):
        s, soft = [None] * ng, [None] * ng

        def scores(g):
            if q_aug[g] is None:
                q_aug[g] = jnp.concatenate([jnp.concatenate([head_rows(g, r), sel_neg[g]], axis=0)
                                            for r in range(rep)], axis=1)
            s[g] = jnp.dot(keys(kt, g), q_aug[g], preferred_element_type=F32)
            if bias is not None:
                s[g] = s[g] + bias

        def softmax(g):
            m_i, l_i, _ = carry[g]
            m_new = jnp.maximum(m_i, jnp.max(s[g], axis=0, keepdims=True))
            alpha = jnp.exp(m_i - m_new)
            p = jnp.exp(s[g] - m_new)
            soft[g] = (m_new, alpha, alpha * l_i + jnp.sum(p, axis=0, keepdims=True), p.astype(BF16))

        def values(g):
            m_new, alpha, l_new, p = soft[g]
            pv = jnp.dot(vs_ref[kt, g * hd:(g + 1) * hd, :], p, preferred_element_type=F32)
            out[g] = (m_new, l_new, alpha * carry[g][2] + pv)

        return [(functools.partial(scores, g), functools.partial(softmax, g), functools.partial(values, g))
                for g in range(ng)]

    kt_d = t0 // tk
    kpos_d = kt_d * tk + lax.broadcasted_iota(jnp.int32, (tk, 1), 0)
    bias_d = wide(jnp.where(kpos_d <= tl, 0.0, NEG))
    empty = (jnp.full((1, rep * tq), NEG, F32), jnp.zeros((1, rep * tq), F32), jnp.zeros((hd, rep * tq), F32))
    init = [None] * ng
    tasks = []
    for g in range(ng):
        tasks.append((functools.partial(cmp_scores, g), functools.partial(cmp_softmax, g),
                      functools.partial(cmp_values, g)))
        tasks.append((functools.partial(win_scores, g), functools.partial(win_softmax, g),
                      functools.partial(win_values, g)))
    _run_skewed(tasks + sweep_tasks(kt_d, [empty] * ng, init, bias_d))

    def step(kt, carry):
        out = [None] * ng
        _run_skewed(sweep_tasks(kt, carry, out, None))
        return tuple(out)

    final = lax.fori_loop(0, kt_d, step, tuple(init))

    for g in range(ng):
        _, l_s, acc_s = final[g]
        o_sel = acc_s / l_s
        heads = []
        for r in range(rep):
            h = g * rep + r
            cols = slice(r * tq, (r + 1) * tq)
            heads.append(sig[h:h + 1, :] * o_cmp[g][:, cols] + sig[nh + h:nh + h + 1, :] * o_sel[:, cols]
                         + sig[2 * nh + h:2 * nh + h + 1, :] * o_win[g][:, cols])
        for pair in range(rep // 2):
            o_ref[:, (g * rep + 2 * pair) * hd:(g * rep + 2 * pair + 2) * hd] = (
                jnp.concatenate([heads[2 * pair], heads[2 * pair + 1]], axis=0).T)


def nsa_prompt_t(q_t, kcvc, ka, vs_t, kw, vw_t, gates_t, nseq, seq_len, tq=128):
    m = ka.shape[0]
    nsub = kcvc.shape[0] // nseq
    tk, wt = vs_t.shape[-1], vw_t.shape[-1]
    nc = (seq_len - CMP_LEN) // CMP_STRIDE + 1
    n_sel = seq_len // SEL_BLK
    assert n_sel <= SEL_BLK and seq_len >= WINDOW + tq and tk % tq == 0 and tq % wt == 0 and WINDOW % wt == 0
    nq = seq_len // tq
    width = q_t.shape[1]
    cov_t = _cover(nsub, nc, n_sel).T
    key = jnp.arange(seq_len)
    et = jnp.concatenate([jnp.zeros((seq_len, HEAD_DIM), BF16),
                          (key[:, None] // SEL_BLK == jnp.arange(SEL_BLK)[None, :]).astype(BF16)], axis=1)
    cols = lambda a: pl.BlockSpec((None, a.shape[1], tq), lambda b, t: (b, 0, t))
    seq_rows = lambda a: pl.BlockSpec((seq_len, a.shape[1]), lambda b, t: (b, 0))
    whole = lambda a: pl.BlockSpec((None,) + a.shape[1:], lambda b, t: (b, 0, 0, 0))
    return pl.pallas_call(
        functools.partial(_nsa_prompt_lanes_body, tq=tq, nc=nc, n_sel=n_sel),
        grid=(nseq, nq),
        in_specs=[cols(q_t), pl.BlockSpec((nsub, ROW_W), lambda b, t: (b, 0)), seq_rows(ka), whole(vs_t),
                  seq_rows(kw), whole(vw_t), cols(gates_t),
                  pl.BlockSpec(cov_t.shape, lambda b, t: (0, 0)),
                  pl.BlockSpec(et.shape, lambda b, t: (0, 0))],
        out_specs=pl.BlockSpec((tq, width), lambda b, t: (b * nq + t, 0)),
        out_shape=jax.ShapeDtypeStruct((m, width), F32),
        compiler_params=_params("parallel", "parallel"),
        name="nsa_prompt",
    )(q_t, kcvc, ka, vs_t, kw, vw_t, gates_t, cov_t, et)


def _joint_softmax(s_a, ok_a, s_b, ok_b):
    s_a = jnp.where(ok_a, s_a, NEG)
    s_b = jnp.where(ok_b, s_b, NEG)
    mx = jnp.maximum(jnp.max(s_a, -1, keepdims=True), jnp.max(s_b, -1, keepdims=True))
    e_a = jnp.exp(s_a - mx)
    e_b = jnp.exp(s_b - mx)
    return e_a, e_b, jnp.sum(e_a, -1, keepdims=True) + jnp.sum(e_b, -1, keepdims=True)


def _nsa_sample_body(pt_ref, q_ref, kc_ref, *rest, n, n_pages, past_len, nc, n_sel):
    page_refs = rest[:n_pages]
    ns_ref, nw_ref, win_ref, gt_ref, cov_ref, e_ref, o_ref, wo_ref = rest[n_pages:]
    np_, rep, hd, kvw, nkv = SAMPLE_PAD, NSA_REP, HEAD_DIM, NSA_KV * HEAD_DIM, NSA_KV
    tpos = past_len + lax.broadcasted_iota(jnp.int32, (np_, 1), 0)
    sig = jax.nn.sigmoid(gt_ref[...])
    q = q_ref[...]

    blocks = []
    for g in range(nkv):
        for r in range(rep):
            h = g * rep + r
            parts = [q[:, h * hd:(h + 1) * hd]]
            if g > 0:
                parts.insert(0, jnp.zeros((np_, g * hd), F32))
            if g < nkv - 1:
                parts.append(jnp.zeros((np_, (nkv - 1 - g) * hd), F32))
            blocks.append(jnp.concatenate(parts, axis=1))
    qbd = jnp.concatenate(blocks, axis=0).astype(BF16)
    nrow = nkv * rep * np_

    def grp(x):
        return x.reshape(nkv, rep, np_, x.shape[-1])

    def scores(keys):
        return grp(lax.dot_general(qbd, keys, _NT, preferred_element_type=F32))

    def scores_t(keys_t):
        return grp(jnp.dot(qbd, keys_t, preferred_element_type=F32))

    new_lane = lax.broadcasted_iota(jnp.int32, (np_, np_), 1)
    new_pos = past_len + new_lane

    nsub = kc_ref.shape[0]
    mcol = lax.broadcasted_iota(jnp.int32, (np_, nsub), 1)
    valid = (mcol * CMP_STRIDE + CMP_LEN - 1 <= tpos) & (mcol < nc)
    p = _masked_softmax(scores(kc_ref[:, 0:kvw]), valid[None, None]) * jnp.where(valid, 1.0, 0.0)[None, None]
    o_cmp = jnp.dot(p.reshape(nrow, nsub).astype(BF16), kc_ref[:, kvw:2 * kvw], preferred_element_type=F32)
    sels = _topk_masks_rows([_importance(jnp.sum(p[g], axis=0), cov_ref, tpos, n_sel) for g in range(nkv)])

    page = page_refs[0].shape[-1]
    k_all = jnp.concatenate([pr[0].reshape(kvw, page) for pr in page_refs], axis=1).astype(BF16)
    v_all = jnp.concatenate([pr[1].reshape(kvw, page) for pr in page_refs], axis=1).astype(BF16)
    kpos = lax.broadcasted_iota(jnp.int32, (np_, past_len), 1)
    ok_c = jnp.stack([jnp.where(kpos <= tpos, jnp.dot(sels[g].astype(BF16), e_ref[0], preferred_element_type=F32), 0.0)
                      for g in range(nkv)]) > 0.5
    new_ok = (new_pos <= tpos) & (new_lane < n)
    ok_n = jnp.stack([jnp.where(new_ok, jnp.concatenate(
        [sels[g][:, (past_len + i) // SEL_BLK:(past_len + i) // SEL_BLK + 1] for i in range(np_)], axis=1), 0.0)
        for g in range(nkv)]) > 0.5
    ns = ns_ref[...]
    e_c, e_n, l_s = _joint_softmax(scores_t(k_all), ok_c[:, None], scores(ns[:, 0:kvw].astype(BF16)), ok_n[:, None])
    o_sel = lax.dot_general(e_c.reshape(nrow, past_len).astype(BF16), v_all, _NT, preferred_element_type=F32)
    e_n = e_n.reshape(nrow, np_)
    for i in range(n):
        o_sel = o_sel + e_n[:, i:i + 1] * ns[i:i + 1, kvw:2 * kvw]
    o_sel = o_sel / l_s.reshape(nrow, 1)

    w_buf = win_ref.shape[-1]
    k_w = win_ref[0].reshape(kvw, w_buf)
    v_w = win_ref[1].reshape(kvw, w_buf)
    kpos_w = past_len - w_buf + lax.broadcasted_iota(jnp.int32, (np_, w_buf), 1)
    ok_w = (kpos_w <= tpos) & (kpos_w > tpos - WINDOW)
    ok_wn = (new_pos <= tpos) & (new_pos > tpos - WINDOW) & (new_lane < n)
    nw = nw_ref[...]
    e_w, e_wn, l_w = _joint_softmax(scores_t(k_w.astype(BF16)), ok_w[None, None],
                                    scores(nw[:, 0:kvw].astype(BF16)), ok_wn[None, None])
    o_win = lax.dot_general(e_w.reshape(nrow, w_buf).astype(BF16), v_w.astype(BF16), _NT,
                            preferred_element_type=F32)
    e_wn = e_wn.reshape(nrow, np_)
    for i in range(n):
        o_win = o_win + e_wn[:, i:i + 1] * nw[i:i + 1, kvw:2 * kvw]
    o_win = o_win / l_w.reshape(nrow, 1)

    nh = nkv * rep
    for g in range(nkv):
        for r in range(rep):
            h = g * rep + r
            rows = slice(h * np_, (h + 1) * np_)
            cols = slice(g * hd, (g + 1) * hd)
            o_ref[:, h * hd:(h + 1) * hd] = (sig[:, h:h + 1] * o_cmp[rows, cols]
                                            + sig[:, nh + h:nh + h + 1] * o_sel[rows, cols]
                                            + sig[:, 2 * nh + h:2 * nh + h + 1] * o_win[rows, cols])

    body = w_buf - LANE
    for c, old in enumerate((k_w, v_w)):
        shifted = pltpu.roll(old, w_buf - n, 1)
        new_t = jnp.concatenate([nw[:, c * kvw:(c + 1) * kvw].T, jnp.zeros((kvw, LANE - np_), F32)], axis=1)
        lane = lax.broadcasted_iota(jnp.int32, (kvw, LANE), 1)
        wo_ref[c * kvw:(c + 1) * kvw, 0:body] = shifted[:, 0:body]
        wo_ref[c * kvw:(c + 1) * kvw, body:w_buf] = jnp.where(lane < LANE - n, shifted[:, body:w_buf],
                                                              pltpu.roll(new_t, LANE - n, 1))


def nsa_sample(q, kcvc, pool, layer, page_table, rows_s, rows_w, win, gates, n):
    db, n_pages = page_table.shape
    page = pool.shape[-1]
    past_len = n_pages * page
    nsub = kcvc.shape[0] // db
    w_buf = win.shape[-1]
    assert w_buf == WINDOW and n <= SAMPLE_PAD and past_len % SEL_BLK == 0
    nc = (past_len + n - CMP_LEN) // CMP_STRIDE + 1
    n_sel = -(-(past_len + n) // SEL_BLK)
    assert n_sel <= SEL_BLK and nc <= nsub
    cov = _cover(nsub, nc, n_sel)
    exp = _expand(past_len, past_len)
    rows8 = lambda w: pl.BlockSpec((SAMPLE_PAD, w), lambda i, pt: (i, 0))
    pages = [pl.BlockSpec((None, None) + pool.shape[2:],
                          functools.partial(lambda i, pt, k: (layer, pt[i * n_pages + k], 0, 0, 0, 0), k=k))
             for k in range(n_pages)]
    grid_spec = pltpu.PrefetchScalarGridSpec(
        num_scalar_prefetch=1,
        grid=(db,),
        in_specs=[rows8(q.shape[1]), pl.BlockSpec((nsub, ROW_W), lambda i, pt: (i, 0))] + pages
        + [rows8(ROW_W), rows8(ROW_W),
           pl.BlockSpec((None, None) + win.shape[2:], lambda i, pt: (layer, i, 0, 0, 0, 0)),
           rows8(LANE),
           pl.BlockSpec(cov.shape, lambda i, pt: (0, 0)),
           pl.BlockSpec(exp.shape, lambda i, pt: (0, 0, 0))],
        out_specs=[rows8(q.shape[1]), pl.BlockSpec((None, ROW_W, w_buf), lambda i, pt: (i, 0, 0))],
    )
    return pl.pallas_call(
        functools.partial(_nsa_sample_body, n=n, n_pages=n_pages, past_len=past_len, nc=nc, n_sel=n_sel),
        grid_spec=grid_spec,
        out_shape=[jax.ShapeDtypeStruct(q.shape, F32), jax.ShapeDtypeStruct((db, ROW_W, w_buf), F32)],
        compiler_params=_params("parallel"),
        name="nsa_sample",
    )(page_table.reshape(-1), q, kcvc, *([pool] * n_pages), rows_s, rows_w, win, gates, cov, exp)


def kernel(x_prompt, x_sample, state_ret, cache_cmp, cache_sel, state_win, state_ffn, page_table, norm_mix_pre, norm_mix_post, norm_ffn_pre, norm_ffn_post, e_w_in, e_w_out, e_sg_ln_g, e_sg_ln_b, e_sg_w, e_sg_b, o_w_in, o_w_out, o_pe_k, o_pe_v, o_phi_k, o_phi_v, f_w_in, f_conv_w, f_conv_b, f_w_out):
    b, s, d = x_prompt.shape
    db, n, _ = x_sample.shape
    depth = norm_mix_pre.shape[0]
    heads = state_ret.shape[2]
    groups = e_sg_w.shape[1]
    n_pages = page_table.shape[1]
    page = cache_cmp.shape[2]
    past_len = n_pages * page
    pad = SAMPLE_PAD
    assert n <= pad and n >= CONV_W - 1 and n < CMP_STRIDE and state_ret.shape[3] == LANE

    xp = x_prompt.reshape(b * s, d)
    xs = jnp.pad(x_sample, ((0, 0), (0, pad - n), (0, 0))).reshape(db * pad, d)
    pos_p = jnp.arange(s, dtype=jnp.int32)
    pos_s = jnp.tile(past_len + jnp.arange(pad, dtype=jnp.int32), db)
    ret_p, ret_s = _ret_rope_tabs(pos_p), _ret_rope_tabs(pos_s)
    nsa_p, nsa_s = _nsa_rope_tabs(pos_p), _nsa_rope_tabs(pos_s)
    even_modes = [ROPE_RET_Q] * heads + [ROPE_RET_K] * heads + [ROPE_NONE] * (4 * heads)
    even_outs = [(0, 6 * heads * LANE, 1.0, F32, None)]
    kscale = LANE ** -0.5
    odd_cols = o_w_in.shape[2]
    odd_pad = -(-odd_cols // LANE) * LANE
    rows_last = lambda a: jnp.transpose(a, (0, 1, 3, 4, 5, 2))
    rows_first = lambda a: jnp.transpose(a.reshape((a.shape[0],) + kv_shape + (a.shape[-1],)), (0, 4, 1, 2, 3))
    pool_c, pool_s, win = rows_last(cache_cmp), rows_last(cache_sel), rows_last(state_win)
    kv_shape = cache_cmp.shape[3:]
    keep = min(WINDOW, s)

    out = {k: [] for k in ("ret_p", "ret_s", "sgv", "cmp_p", "cmp_s", "sel_p", "sel_s", "win_p", "win_s", "ffn_p", "ffn_s")}
    for l in range(depth):
        i = l // 2
        if l % 2 == 0:
            w_in = e_w_in[i].astype(BF16)
            w_out = e_w_out[i].astype(BF16)
            (pp,) = proj_in(xp, norm_mix_pre[l], w_in, (ret_p[0], ret_p[1], ret_p[1]), even_modes, even_outs, kscale)
            (ps,) = proj_in(xs, norm_mix_pre[l], w_in, (ret_s[0], ret_s[1], ret_s[1]), even_modes, even_outs, kscale)
            cat_p, st_p = even_seq(pp, b, s, e_sg_ln_g[i], e_sg_ln_b[i], e_sg_w[i], e_sg_b[i], heads, groups)
            cat_s, st_s, svn_s = even_seq_sample(ps, state_ret, i, n, e_sg_ln_g[i], e_sg_ln_b[i], e_sg_w[i],
                                                 e_sg_b[i], heads, groups)
            xp = proj_out([cat_p], w_out, norm_mix_post[l], xp)
            xs = proj_out([cat_s], w_out, norm_mix_post[l], xs)
            out["ret_p"].append(st_p)
            out["ret_s"].append(st_s)
            out["sgv"].append(svn_s.reshape(db, pad, -1)[:, :n])
        else:
            w_in = jnp.pad(o_w_in[i], ((0, 0), (0, odd_pad - odd_cols))).astype(BF16)
            w_out = o_w_out[i].astype(BF16)
            wts, pes = _compress_weights(o_phi_k[i], o_phi_v[i], o_pe_k[i], o_pe_v[i])
            q_p, rc_p, cmp_t, sel_t, win_t, g_p, ka_p, vs_p, kw_p, vw_p = proj_in(
                xp, norm_mix_pre[l], w_in, nsa_p, _odd_modes(), _odd_outs(True), seq_len=s)
            q_s, rc_s, rs_s, rw_s, g_s = proj_in(xs, norm_mix_pre[l], w_in, nsa_s, _odd_modes(), _odd_outs(False))
            kc_p = compress_prompt(rc_p, b, wts, pes)
            o_p = nsa_prompt_t(q_p, kc_p, ka_p, vs_p, kw_p, vw_p, g_p, b, s)
            kc_s = compress_pages(pool_c, i, page_table, wts, pes)
            o_s, win_s = nsa_sample(q_s, kc_s, pool_s, i, page_table, rs_s, rw_s, win, g_s, n)
            xp = proj_out([o_p], w_out, norm_mix_post[l], xp)
            xs = proj_out([o_s], w_out, norm_mix_post[l], xs)
            out["cmp_p"].append(rows_first(cmp_t))
            out["sel_p"].append(rows_first(sel_t))
            out["win_p"].append(rows_first(win_t[:, :, s - keep:]))
            out["cmp_s"].append(rc_s.reshape((db, pad) + kv_shape)[:, :n])
            out["sel_s"].append(rs_s.reshape((db, pad) + kv_shape)[:, :n])
            out["win_s"].append(rows_first(win_s))
        w_in = f_w_in[l].astype(BF16)
        w_out = f_w_out[l].astype(BF16)
        xp, st_p = conv_ffn(xp, s, norm_ffn_pre[l], w_in, f_conv_w[l], f_conv_b[l], w_out, norm_ffn_post[l])
        xs, st_s = conv_ffn(xs, pad, norm_ffn_pre[l], w_in, f_conv_w[l], f_conv_b[l], w_out, norm_ffn_post[l],
                            prev=_ffn_prev(state_ffn[l]), tm=256)
        out["ffn_p"].append(st_p.reshape(b, 8, -1)[:, 8 - (CONV_W - 1):])
        out["ffn_s"].append(st_s.reshape(db, pad, -1)[:, n - (CONV_W - 1):n])

    stack = lambda k: jnp.stack(out[k])
    return (xp.reshape(b, s, d), xs.reshape(db, pad, d)[:, :n], stack("ret_p"), stack("ret_s"), stack("sgv"),
            stack("cmp_p"), stack("cmp_s"), stack("sel_p"), stack("sel_s"), stack("win_p"), stack("win_s"),
            stack("ffn_p"), stack("ffn_s"))
```

```python
import functools

import jax
import jax.numpy as jnp
from jax import lax
from jax.experimental import pallas as pl
from jax.experimental.pallas import tpu as pltpu

F32 = jnp.float32
BF16 = jnp.bfloat16

EPS = 1e-6
NEG = -1e30
FORCE = 1e9

LANE = 128
VMEM_LIMIT = 56 * 1024 * 1024

RET_CHUNK = 128
RET_THETA = 10000.0
SG_CHUNK = 128
HEAD_DIM = 64
NSA_KV = 4
NSA_REP = 4
CMP_LEN = 32
CMP_STRIDE = 16
SEL_BLK = 64
SEL_TOPK = 16
WINDOW = 512
ROPE_DIM = HEAD_DIM // 4
ROPE_THETA = 500000.0
ATTN_SCALE = HEAD_DIM ** -0.5
LOG2E = 1.4426950408889634
CONV_W = 3

SAMPLE_PAD = 8

ROPE_NONE, ROPE_RET_Q, ROPE_RET_K, ROPE_NSA = 0, 1, 2, 3
PAD_HEADS = -1


def _params(*sem):
    return pltpu.CompilerParams(dimension_semantics=sem, vmem_limit_bytes=VMEM_LIMIT)


def _rms(x, g):
    return x * lax.rsqrt(jnp.mean(x * x, -1, keepdims=True) + EPS) * g


def _col_chunk(n):
    for c in (512, 384, 256, 128):
        if n % c == 0:
            return c
    raise ValueError(n)


def _proj_in_body(x_ref, g_ref, w_ref, tc_ref, ta_ref, tb_ref, *o_refs, modes, kscale, outs, n_prev=0):
    o_refs = o_refs[n_prev:]
    h = _rms(x_ref[...], g_ref[...]).astype(BF16)
    n = w_ref.shape[1]
    cw = _col_chunk(n)
    for c0 in range(0, n, cw):
        y = jnp.dot(h, w_ref[:, c0:c0 + cw], preferred_element_type=F32)
        for j in range(cw // LANE):
            col = c0 + j * LANE
            blk = y[:, j * LANE:(j + 1) * LANE]
            mode = modes[col // LANE]
            if mode in (ROPE_RET_Q, ROPE_RET_K):
                blk = blk * tc_ref[...] + pltpu.roll(blk, LANE // 2, 1) * ta_ref[...]
                if mode == ROPE_RET_K:
                    blk = blk * kscale
            elif mode == ROPE_NSA:
                blk = (blk * tc_ref[...] + pltpu.roll(blk, LANE - ROPE_DIM // 2, 1) * ta_ref[...]
                       + pltpu.roll(blk, ROPE_DIM // 2, 1) * tb_ref[...])
            for o_ref, (oc, ow, osc, tw) in zip(o_refs, outs):
                if oc <= col < oc + ow:
                    v = blk if osc == 1.0 else blk * osc
                    cs = slice(col - oc, col - oc + LANE)
                    if tw is None:
                        o_ref[:, cs] = v.astype(o_ref.dtype)
                    elif tw == PAD_HEADS:
                        low = lax.broadcasted_iota(jnp.int32, v.shape, 1) < HEAD_DIM
                        c2 = 2 * (col - oc)
                        o_ref[:, c2:c2 + LANE] = jnp.where(low, v, 0.0).astype(o_ref.dtype)
                        o_ref[:, c2 + LANE:c2 + 2 * LANE] = jnp.where(low, pltpu.roll(v, HEAD_DIM, 1), 0.0).astype(o_ref.dtype)
                    elif tw == 0:
                        o_ref[cs, :] = v.T.astype(o_ref.dtype)
                    else:
                        for s in range(v.shape[0] // tw):
                            o_ref[s, cs, :] = v[s * tw:(s + 1) * tw, :].T.astype(o_ref.dtype)


def proj_in(x, g, w, tabs, modes, outs, kscale=1.0, tm=512, seq_len=None, layer=0, n_layers=1, stacked=None):
    m, d = x.shape
    n = w.shape[1]
    tm = min(tm, m)
    nt = tabs[0].shape[0] // tm
    tab_spec = pl.BlockSpec((tm, LANE), lambda i: (i % nt, 0))
    tps = (seq_len // tm) if seq_len else 1
    stacked = stacked or {}
    specs, shapes, prevs, aliases = [], [], [], {}
    for k, (_, ow, _, dt, tw) in enumerate(outs):
        if tw is None:
            specs.append(pl.BlockSpec((tm, ow), lambda i: (i, 0)))
            shapes.append(jax.ShapeDtypeStruct((m, ow), dt))
        elif tw == PAD_HEADS:
            specs.append(pl.BlockSpec((tm, 2 * ow), lambda i: (i, 0)))
            shapes.append(jax.ShapeDtypeStruct((m, 2 * ow), dt))
        elif tw == 0 and k in stacked:
            specs.append(pl.BlockSpec((None, None, ow, tm), lambda i: (layer, i // tps, 0, i % tps)))
            shapes.append(jax.ShapeDtypeStruct((n_layers, m // seq_len, ow, seq_len), dt))
            if stacked[k] is not None:
                aliases[6 + len(prevs)] = k
                prevs.append(stacked[k])
        elif tw == 0:
            specs.append(pl.BlockSpec((None, ow, tm), lambda i: (i // tps, 0, i % tps)))
            shapes.append(jax.ShapeDtypeStruct((m // seq_len, ow, seq_len), dt))
        else:
            specs.append(pl.BlockSpec((None, tm // tw, ow, tw), lambda i: (i // tps, i % tps, 0, 0)))
            shapes.append(jax.ShapeDtypeStruct((m // seq_len, seq_len // tw, ow, tw), dt))
    return pl.pallas_call(
        functools.partial(_proj_in_body, modes=tuple(modes), kscale=kscale, n_prev=len(prevs),
                          outs=tuple((o[0], o[1], o[2], o[4]) for o in outs)),
        grid=(m // tm,),
        in_specs=[pl.BlockSpec((tm, d), lambda i: (i, 0)),
                  pl.BlockSpec((1, d), lambda i: (0, 0)),
                  pl.BlockSpec((d, n), lambda i: (0, 0)),
                  tab_spec, tab_spec, tab_spec] + [pl.BlockSpec(memory_space=pl.ANY)] * len(prevs),
        out_specs=specs,
        out_shape=shapes,
        input_output_aliases=aliases,
        compiler_params=_params("parallel"),
        name="proj_in",
    )(x, g.reshape(1, d), w, *tabs, *prevs)


def _ret_rope_tabs(pos):
    inv = 1.0 / (RET_THETA ** jnp.linspace(0.0, 1.0, LANE // 2))
    ang = pos.astype(F32)[:, None] * inv[None, :]
    cos, sin = jnp.cos(ang), jnp.sin(ang)
    return jnp.concatenate([cos, cos], -1), jnp.concatenate([-sin, sin], -1)


def _nsa_rope_tabs(pos):
    hr = ROPE_DIM // 2
    inv = 1.0 / (ROPE_THETA ** (jnp.arange(0, ROPE_DIM, 2, dtype=F32) / ROPE_DIM))
    ang = pos.astype(F32)[:, None] * inv[None, :]
    cos, sin = jnp.cos(ang), jnp.sin(ang)
    n = pos.shape[0]
    rest = HEAD_DIM - ROPE_DIM
    c = jnp.concatenate([cos, cos, jnp.ones((n, rest), F32)], -1)
    a = jnp.concatenate([-sin, jnp.zeros((n, hr + rest), F32)], -1)
    b = jnp.concatenate([jnp.zeros((n, hr), F32), sin, jnp.zeros((n, rest), F32)], -1)
    rep = LANE // HEAD_DIM
    return jnp.tile(c, (1, rep)), jnp.tile(a, (1, rep)), jnp.tile(b, (1, rep))


def _ret_tabs(c, n_valid, heads):
    log_g = jnp.log(1.0 - 2.0 ** (-5.0 - jnp.arange(heads, dtype=F32)))
    idx = jnp.arange(c, dtype=F32)
    diff = idx[:, None] - idx[None, :]
    ok = (diff >= 0) & (idx[None, :] < n_valid)
    dmask = jnp.where(ok, jnp.exp(log_g[:, None, None] * jnp.maximum(diff, 0.0)), 0.0)
    qdec = jnp.exp(log_g[:, None] * (idx + 1.0))
    kdec = jnp.where(idx < n_valid, jnp.exp(log_g[:, None] * (n_valid - 1.0 - idx)), 0.0)
    cdec = jnp.exp(log_g * n_valid)
    bc = lambda t: jnp.broadcast_to(t[..., None], t.shape + (LANE,))
    dm = dmask if c == LANE else jnp.pad(dmask, ((0, 0), (0, 0), (0, LANE - c)))
    return dm, bc(qdec), bc(kdec), jnp.broadcast_to(cdec[:, None, None], (heads, 8, LANE))


def _layer_norm_rows(x, g, b):
    xc = x - jnp.mean(x, -1, keepdims=True)
    return xc * lax.rsqrt(jnp.mean(xc * xc, -1, keepdims=True) + EPS) * g + b


def _rms_unit(x):
    return x * lax.rsqrt(jnp.mean(x * x, -1, keepdims=True) + EPS)


def _even_seq_body(q_ref, k_ref, v_ref, g_ref, u_ref, sv_ref, dm_ref, qd_ref, kd_ref, cd_ref,
                   lng_ref, lnb_ref, wm_ref, sgb_ref, o_ref, st_ref, s_scr, *, tq, heads, groups):
    t = pl.program_id(1)
    c = RET_CHUNK

    @pl.when(t == 0)
    def _():
        s_scr[...] = jnp.zeros(s_scr.shape, F32)

    for ci in range(tq // c):
        rows = slice(ci * c, (ci + 1) * c)
        for h in range(heads):
            cols = slice(h * LANE, (h + 1) * LANE)
            qc, kc, vc = q_ref[rows, cols], k_ref[rows, cols], v_ref[rows, cols]
            vb = vc.astype(BF16)
            s = s_scr[h]
            inner = lax.dot_general(qc.astype(BF16), kc.astype(BF16), (((1,), (1,)), ((), ())),
                                    preferred_element_type=F32) * dm_ref[h]
            o = (jnp.dot(inner.astype(BF16), vb, preferred_element_type=F32)
                 + jnp.dot((qc * qd_ref[h]).astype(BF16), s.astype(BF16), preferred_element_type=F32))
            s_scr[h] = s * cd_ref[h][0:1, :] + lax.dot_general(
                (kc * kd_ref[h]).astype(BF16), vb, (((0,), (0,)), ((), ())), preferred_element_type=F32)
            gg = g_ref[rows, cols]
            o_ref[rows, cols] = gg * jax.nn.sigmoid(gg) * _rms_unit(o)
        for gi in range(groups):
            cols = slice(gi * LANE, (gi + 1) * LANE)
            svn = _layer_norm_rows(sv_ref[rows, cols], lng_ref[:, cols], lnb_ref[:, cols])
            mixed = jnp.dot(wm_ref[gi], svn.astype(BF16), preferred_element_type=F32) + sgb_ref[gi]
            o_ref[rows, heads * LANE + gi * LANE:heads * LANE + (gi + 1) * LANE] = u_ref[rows, cols] * mixed

    @pl.when(t == pl.num_programs(1) - 1)
    def _():
        st_ref[...] = s_scr[...]


def even_seq(p, nseq, seq_len, ln_g, ln_b, sg_w, sg_b, heads, groups, tq=512):
    m = p.shape[0]
    w = heads * LANE
    c = RET_CHUNK
    nt = seq_len // tq
    dm, qd, kd, cd = _ret_tabs(c, c, heads)
    wm = jnp.tril(sg_w[:, :c, :c]).astype(BF16)
    sgb = jnp.broadcast_to(sg_b[:, :c, None], (groups, c, LANE))
    part = lambda j: pl.BlockSpec((tq, w), lambda b, t: (b * nt + t, j))
    full = lambda a: pl.BlockSpec(a.shape, lambda b, t: (0,) * a.ndim)
    return pl.pallas_call(
        functools.partial(_even_seq_body, tq=tq, heads=heads, groups=groups),
        grid=(nseq, nt),
        in_specs=[part(j) for j in range(6)] + [full(dm), full(qd), full(kd), full(cd),
                                                pl.BlockSpec((1, w), lambda b, t: (0, 0)),
                                                pl.BlockSpec((1, w), lambda b, t: (0, 0)),
                                                full(wm), full(sgb)],
        out_specs=[pl.BlockSpec((tq, 2 * w), lambda b, t: (b * nt + t, 0)),
                   pl.BlockSpec((None, heads, LANE, LANE), lambda b, t: (b, 0, 0, 0))],
        out_shape=[jax.ShapeDtypeStruct((m, 2 * w), F32),
                   jax.ShapeDtypeStruct((nseq, heads, LANE, LANE), F32)],
        scratch_shapes=[pltpu.VMEM((heads, LANE, LANE), F32)],
        compiler_params=_params("arbitrary", "arbitrary"),
        name="even_seq",
    )(p, p, p, p, p, p, dm, qd, kd, cd, ln_g.reshape(1, w), ln_b.reshape(1, w), wm, sgb)


def _even_seq_sample_body(*refs, nb, n, heads, groups, has_prev):
    (q_ref, k_ref, v_ref, g_ref, u_ref, sv_ref, s0_ref, dm_ref, qd_ref, kd_ref, cd_ref,
     lng_ref, lnb_ref, wm_ref, sgb_ref, o_ref, st_ref, svn_ref) = refs[1:] if has_prev else refs
    np_ = SAMPLE_PAD

    def one(b, carry):
        rows = pl.ds(pl.multiple_of(b * np_, np_), np_)
        for h in range(heads):
            cols = slice(h * LANE, (h + 1) * LANE)
            q, k, v = q_ref[rows, cols], k_ref[rows, cols], v_ref[rows, cols]
            s = s0_ref[b, h]
            o = jnp.dot((q * qd_ref[h]).astype(BF16), s.astype(BF16), preferred_element_type=F32)
            dm = dm_ref[h]
            for j in range(n):
                inner = jnp.sum(q * k[j:j + 1, :], axis=-1, keepdims=True)
                o = o + (inner * dm[:, j:j + 1]) * v[j:j + 1, :]
            st_ref[b, h] = s * cd_ref[h][0:1, :] + lax.dot_general(
                (k * kd_ref[h]).astype(BF16), v.astype(BF16), (((0,), (0,)), ((), ())),
                preferred_element_type=F32)
            gg = g_ref[rows, cols]
            o_ref[rows, cols] = gg * jax.nn.sigmoid(gg) * _rms_unit(o)
        for gi in range(groups):
            cols = slice(gi * LANE, (gi + 1) * LANE)
            svn = _layer_norm_rows(sv_ref[rows, cols], lng_ref[:, cols], lnb_ref[:, cols])
            svn_ref[rows, cols] = svn
            wm = wm_ref[gi]
            mixed = sgb_ref[gi]
            for j in range(n):
                mixed = mixed + wm[:, j:j + 1] * svn[j:j + 1, :]
            o_ref[rows, heads * LANE + gi * LANE:heads * LANE + (gi + 1) * LANE] = u_ref[rows, cols] * mixed
        return carry

    lax.fori_loop(0, nb, one, 0)


def even_seq_sample(p, s0, layer, n, ln_g, ln_b, sg_w, sg_b, heads, groups, prev=None, nb=8):
    m = p.shape[0]
    db = m // SAMPLE_PAD
    w = heads * LANE
    dm, qd, kd, cd = _ret_tabs(SAMPLE_PAD, n, heads)
    wm = jnp.pad(jnp.tril(sg_w[:, :n, :n]), ((0, 0), (0, SAMPLE_PAD - n), (0, LANE - n)))
    sgb = jnp.broadcast_to(jnp.pad(sg_b[:, :n], ((0, 0), (0, SAMPLE_PAD - n)))[:, :, None], (groups, SAMPLE_PAD, LANE))
    rows = nb * SAMPLE_PAD
    part = lambda j: pl.BlockSpec((rows, w), lambda i: (i, j))
    full = lambda a: pl.BlockSpec(a.shape, lambda i: (0,) * a.ndim)
    st_spec = pl.BlockSpec((None, nb, heads, LANE, LANE), lambda i: (layer, i, 0, 0, 0))
    has_prev = prev is not None
    return pl.pallas_call(
        functools.partial(_even_seq_sample_body, nb=nb, n=n, heads=heads, groups=groups, has_prev=has_prev),
        grid=(db // nb,),
        in_specs=([pl.BlockSpec(memory_space=pl.ANY)] if has_prev else [])
        + [part(j) for j in range(6)] + [st_spec, full(dm), full(qd), full(kd), full(cd),
                                         pl.BlockSpec((1, w), lambda i: (0, 0)),
                                         pl.BlockSpec((1, w), lambda i: (0, 0)),
                                         full(wm), full(sgb)],
        out_specs=[pl.BlockSpec((rows, 2 * w), lambda i: (i, 0)), st_spec,
                   pl.BlockSpec((rows, w), lambda i: (i, 0))],
        out_shape=[jax.ShapeDtypeStruct((m, 2 * w), F32),
                   jax.ShapeDtypeStruct(s0.shape, F32),
                   jax.ShapeDtypeStruct((m, w), F32)],
        input_output_aliases={0: 1} if has_prev else {},
        compiler_params=_params("parallel"),
        name="even_seq_sample",
    )(*([prev] if has_prev else []), p, p, p, p, p, p, s0, dm, qd, kd, cd, ln_g.reshape(1, w), ln_b.reshape(1, w),
      wm, sgb)


def _proj_out_body(*refs, n_a):
    w_ref, g_ref, x_ref, o_ref = refs[n_a:]
    a = refs[0][...]
    for r in refs[1:n_a]:
        a = a + r[...]
    y = jnp.dot(a.astype(BF16), w_ref[...], preferred_element_type=F32)
    o_ref[...] = x_ref[...] + _rms(y, g_ref[...])


def proj_out(a_list, w, g, x, tm=512):
    m, d = x.shape
    k = w.shape[0]
    tm = min(tm, m)
    row = lambda i: (i, 0)
    return pl.pallas_call(
        functools.partial(_proj_out_body, n_a=len(a_list)),
        grid=(m // tm,),
        in_specs=[pl.BlockSpec((tm, k), row) for _ in a_list]
        + [pl.BlockSpec((k, d), lambda i: (0, 0)),
           pl.BlockSpec((1, d), lambda i: (0, 0)),
           pl.BlockSpec((tm, d), row)],
        out_specs=pl.BlockSpec((tm, d), row),
        out_shape=jax.ShapeDtypeStruct((m, d), F32),
        compiler_params=_params("parallel"),
        name="proj_out",
    )(*a_list, w, g.reshape(1, d), x)


FFN_HALO = 16
FFN_CHUNK = 2048


def _ffn_body(*refs, tm, tiles_per_seq, sample, nf_static):
    if sample:
        (x_ref, gpre_ref, wa_ref, wb_ref, cwa_ref, cwb_ref, cba_ref, cbb_ref, wo_ref, gpost_ref,
         e1a_ref, e1b_ref, e2a_ref, e2b_ref, o_ref, st_ref, h_scr, upa_scr, upb_scr, acc_scr) = refs
    else:
        (x_ref, halo_ref, gpre_ref, wa_ref, wb_ref, cwa_ref, cwb_ref, cba_ref, cbb_ref, wo_ref, gpost_ref,
         o_ref, st_ref, h_scr, upa_scr, upb_scr, acc_scr) = refs
    i = pl.program_id(0)
    f = pl.program_id(1)
    nf = pl.num_programs(1)
    hl = FFN_HALO

    @pl.when(f == 0)
    def _():
        if sample:
            h_scr[0:hl, :] = jnp.zeros((hl, h_scr.shape[1]), BF16)
        else:
            hh = _rms(halo_ref[...], gpre_ref[...])
            h_scr[0:hl, :] = jnp.where(i % tiles_per_seq == 0, 0.0, hh).astype(BF16)
        h_scr[hl:, :] = _rms(x_ref[...], gpre_ref[...]).astype(BF16)

    h = h_scr[...]
    if sample:
        t = lax.broadcasted_iota(jnp.int32, (tm, 1), 0) % SAMPLE_PAD
        m1 = t >= 1
        m2 = t >= 2

    def conv(up_scr, cw_ref, cb_ref, e1_ref, e2_ref, cols):
        s2 = up_scr[pl.ds(hl - 2, tm), cols]
        s1 = up_scr[pl.ds(hl - 1, tm), cols]
        s0 = up_scr[pl.ds(hl, tm), cols]
        if sample:
            s2 = jnp.where(m2, s2, 0.0) + e2_ref[:, cols]
            s1 = jnp.where(m1, s1, 0.0) + e1_ref[:, cols]
        return cb_ref[:, cols] + s2 * cw_ref[0:1, cols] + s1 * cw_ref[1:2, cols] + s0 * cw_ref[2:3, cols]

    contrib = None
    for c0 in range(0, upa_scr.shape[1], FFN_CHUNK):
        cols = slice(c0, min(c0 + FFN_CHUNK, upa_scr.shape[1]))
        upa_scr[:, cols] = jnp.dot(h, wa_ref[:, cols], preferred_element_type=F32)
        upb_scr[:, cols] = jnp.dot(h, wb_ref[:, cols], preferred_element_type=F32)
        a = conv(upa_scr, cwa_ref, cba_ref, e1a_ref if sample else None, e2a_ref if sample else None, cols)
        b = conv(upb_scr, cwb_ref, cbb_ref, e1b_ref if sample else None, e2b_ref if sample else None, cols)
        act = (jax.nn.gelu(a) * b).astype(BF16)
        part = jnp.dot(act, wo_ref[cols, :], preferred_element_type=F32)
        contrib = part if contrib is None else contrib + part

    @pl.when(f == 0)
    def _():
        acc_scr[...] = contrib

    @pl.when(f > 0)
    def _():
        acc_scr[...] += contrib

    @pl.when(f == nf - 1)
    def _():
        o_ref[...] = x_ref[...] + _rms(acc_scr[...], gpost_ref[...])

    fw = upa_scr.shape[1]
    rows = st_ref.shape[0]
    last = True if sample else (i % tiles_per_seq == tiles_per_seq - 1)
    for j in range(nf_static):
        @pl.when(jnp.logical_and(f == j, last))
        def _(j=j):
            st_ref[:, j * fw:(j + 1) * fw] = upa_scr[pl.ds(hl + tm - rows, rows), :]
            st_ref[:, (nf_static + j) * fw:(nf_static + j + 1) * fw] = upb_scr[pl.ds(hl + tm - rows, rows), :]


def conv_ffn(x, seq_len, gpre, w_in, conv_w, conv_b, w_out, gpost, prev=None, tm=512):
    m, d = x.shape
    ff = w_out.shape[0]
    fw = 1408 if ff % 1408 == 0 else ff
    nf = ff // fw
    sample = prev is not None
    tm = min(tm, m)
    tps = max(seq_len // tm, 1)
    row = lambda i, f: (i, 0)
    const = lambda i, f: (0, 0)
    cola = lambda i, f: (0, f)
    colb = lambda i, f: (0, nf + f)
    in_specs = [pl.BlockSpec((tm, d), row)]
    args = [x]
    if not sample:
        hb = tm // FFN_HALO
        in_specs.append(pl.BlockSpec((FFN_HALO, d), lambda i, f: (jnp.maximum(i * hb - 1, 0), 0)))
        args.append(x)
    in_specs += [pl.BlockSpec((1, d), const),
                 pl.BlockSpec((d, fw), cola), pl.BlockSpec((d, fw), colb),
                 pl.BlockSpec((CONV_W, fw), cola), pl.BlockSpec((CONV_W, fw), colb),
                 pl.BlockSpec((1, fw), cola), pl.BlockSpec((1, fw), colb),
                 pl.BlockSpec((fw, d), lambda i, f: (f, 0)),
                 pl.BlockSpec((1, d), const)]
    args += [gpre.reshape(1, d), w_in, w_in, conv_w, conv_w, conv_b.reshape(1, -1), conv_b.reshape(1, -1),
             w_out, gpost.reshape(1, d)]
    if sample:
        e1, e2 = prev
        in_specs += [pl.BlockSpec((tm, fw), lambda i, f: (i, f)), pl.BlockSpec((tm, fw), lambda i, f: (i, nf + f)),
                     pl.BlockSpec((tm, fw), lambda i, f: (i, f)), pl.BlockSpec((tm, fw), lambda i, f: (i, nf + f))]
        args += [e1, e1, e2, e2]
        st_shape = jax.ShapeDtypeStruct((m, 2 * ff), F32)
        st_spec = pl.BlockSpec((tm, 2 * ff), lambda i, f: (i, 0))
    else:
        nseq = m // seq_len
        st_shape = jax.ShapeDtypeStruct((nseq * 8, 2 * ff), F32)
        st_spec = pl.BlockSpec((8, 2 * ff), lambda i, f: (i // tps, 0))
    return pl.pallas_call(
        functools.partial(_ffn_body, tm=tm, tiles_per_seq=tps, sample=sample, nf_static=nf),
        grid=(m // tm, nf),
        in_specs=in_specs,
        out_specs=[pl.BlockSpec((tm, d), row), st_spec],
        out_shape=[jax.ShapeDtypeStruct((m, d), F32), st_shape],
        scratch_shapes=[pltpu.VMEM((tm + FFN_HALO, d), BF16),
                        pltpu.VMEM((tm + FFN_HALO, fw), F32),
                        pltpu.VMEM((tm + FFN_HALO, fw), F32),
                        pltpu.VMEM((tm, d), F32)],
        compiler_params=_params("arbitrary", "arbitrary"),
        name="conv_ffn_sample" if sample else "conv_ffn",
    )(*args)


def _ffn_prev(prev):
    db, _, w = prev.shape
    z = jnp.zeros((db, SAMPLE_PAD, w), prev.dtype)
    e1 = z.at[:, 0].set(prev[:, 1])
    e2 = z.at[:, 0:2].set(prev)
    return e1.reshape(db * SAMPLE_PAD, w), e2.reshape(db * SAMPLE_PAD, w)


def _odd_modes():
    return [ROPE_NSA] * 8 + [ROPE_NSA, ROPE_NSA, ROPE_NONE, ROPE_NONE] * 3 + [ROPE_NONE]


SEL_TK = 512
WIN_TK = 128


def _odd_outs(prompt):
    if not prompt:
        return [(0, 1024, ATTN_SCALE, F32, None), (1024, 512, 1.0, F32, None), (1536, 512, 1.0, F32, None),
                (2048, 512, 1.0, F32, None), (2560, 128, 1.0, F32, None)]
    return [(0, 1024, ATTN_SCALE * LOG2E, BF16, 0), (1024, 512, 1.0, F32, None), (1024, 512, 1.0, F32, 0),
            (1536, 512, 1.0, F32, 0), (2048, 512, 1.0, F32, 0), (2560, 128, 1.0, F32, 0),
            (1536, 256, 1.0, BF16, PAD_HEADS), (1792, 256, 1.0, BF16, SEL_TK),
            (2048, 256, 1.0, BF16, None), (2304, 256, 1.0, BF16, WIN_TK)]


SUBS = CMP_LEN // CMP_STRIDE
ROW_W = 2 * NSA_KV * HEAD_DIM
SUB_W = CMP_STRIDE * ROW_W


def _compress_weights(phi_k, phi_v, pe_k, pe_v):
    def one(phi, pe):
        p4 = phi.reshape(SUBS, CMP_STRIDE, HEAD_DIM, HEAD_DIM)
        w = jnp.einsum('hlde,gG->lgdhGe', p4, jnp.eye(2, dtype=phi.dtype))
        w = w.reshape(CMP_STRIDE * 2 * HEAD_DIM, SUBS * 2 * HEAD_DIM)
        pr = jnp.broadcast_to(pe.reshape(SUBS, CMP_STRIDE, 1, HEAD_DIM), (SUBS, CMP_STRIDE, 2, HEAD_DIM))
        pr = jnp.pad(pr.reshape(SUBS, -1), ((0, 16 - SUBS), (0, 0)))
        return w, pr
    wk, pk = one(phi_k, pe_k)
    wv, pv = one(phi_v, pe_v)
    return jnp.stack([wk, wv]).astype(BF16), jnp.stack([pk, pv]).astype(BF16)


def _compress_column(xj, j, w_ref, pe_ref, o_ref, ab_scr):
    nsub = o_ref.shape[0]
    half = LANE
    xe = jnp.concatenate([xj, pe_ref[j // 2]], axis=0)
    ab_scr[...] = jnp.dot(xe, w_ref[j // 2], preferred_element_type=F32)
    bias = ab_scr[nsub:nsub + 1, 0:half] + ab_scr[nsub + 1:nsub + 2, half:2 * half]
    o_ref[:, j * LANE:(j + 1) * LANE] = (
        ab_scr[0:nsub, 0:half] + ab_scr[pl.ds(1, nsub), half:2 * half] + bias).astype(o_ref.dtype)


def _compress_body(x_ref, w_ref, pe_ref, o_ref, ab_scr):
    for j in range(ROW_W // LANE):
        xj = jnp.concatenate([x_ref[:, l * ROW_W + j * LANE:l * ROW_W + (j + 1) * LANE]
                              for l in range(CMP_STRIDE)], axis=1).astype(BF16)
        _compress_column(xj, j, w_ref, pe_ref, o_ref, ab_scr)


def _compress_pages_body(pt_ref, *refs, n_x):
    x_refs = refs[:n_x]
    perm_ref, w_ref, pe_ref, o_ref, ab_scr, xs_scr = refs[n_x:]
    page = x_refs[0].shape[-1]
    sub_pp = page // CMP_STRIDE
    kvw = NSA_KV * HEAD_DIM
    for k, xr in enumerate(x_refs):
        for c in range(2):
            t = xr[c].reshape(kvw, page).astype(BF16)
            out = lax.dot_general(perm_ref[...], t, (((1,), (1,)), ((), ())), preferred_element_type=F32)
            for gp in range(kvw // LANE):
                for l in range(CMP_STRIDE):
                    xs_scr[c * (kvw // LANE) + gp, k * sub_pp:(k + 1) * sub_pp, l * LANE:(l + 1) * LANE] = (
                        out[l * sub_pp:(l + 1) * sub_pp, gp * LANE:(gp + 1) * LANE])
    for j in range(ROW_W // LANE):
        _compress_column(xs_scr[j].astype(BF16), j, w_ref, pe_ref, o_ref, ab_scr)


def compress_prompt(rows, nseq, wts, pes):
    nsub = rows.shape[0] // nseq // CMP_STRIDE
    x = rows.reshape(nseq * nsub, SUB_W)
    return pl.pallas_call(
        _compress_body,
        grid=(nseq,),
        in_specs=[pl.BlockSpec((nsub, SUB_W), lambda b: (b, 0)),
                  pl.BlockSpec(wts.shape, lambda b: (0, 0, 0)),
                  pl.BlockSpec(pes.shape, lambda b: (0, 0, 0))],
        out_specs=pl.BlockSpec((nsub, ROW_W), lambda b: (b, 0)),
        out_shape=jax.ShapeDtypeStruct((nseq * nsub, ROW_W), BF16),
        scratch_shapes=[pltpu.VMEM((nsub + 16, 2 * LANE), F32)],
        compiler_params=_params("parallel"),
        name="compress_prompt",
    )(x, wts, pes)


def compress_pages(pool, layer, page_table, wts, pes, nb=2):
    db, n_pages = page_table.shape
    page = pool.shape[-1]
    sub_pp = page // CMP_STRIDE
    nsub = nb * n_pages * sub_pp
    out_row = jnp.arange(page)
    perm = (jnp.arange(page)[None, :] == ((out_row % sub_pp) * CMP_STRIDE + out_row // sub_pp)[:, None]).astype(BF16)
    specs = [pl.BlockSpec((None, None) + pool.shape[2:],
                          functools.partial(lambda i, pt, s, k: (layer, pt[(i * nb + s) * n_pages + k], 0, 0, 0, 0),
                                            s=s, k=k))
             for s in range(nb) for k in range(n_pages)]
    grid_spec = pltpu.PrefetchScalarGridSpec(
        num_scalar_prefetch=1,
        grid=(db // nb,),
        in_specs=specs + [pl.BlockSpec(perm.shape, lambda i, pt: (0, 0)),
                          pl.BlockSpec(wts.shape, lambda i, pt: (0, 0, 0)),
                          pl.BlockSpec(pes.shape, lambda i, pt: (0, 0, 0))],
        out_specs=pl.BlockSpec((nsub, ROW_W), lambda i, pt: (i, 0)),
        scratch_shapes=[pltpu.VMEM((nsub + 16, 2 * LANE), F32),
                        pltpu.VMEM((ROW_W // LANE, nsub, CMP_STRIDE * LANE), F32)],
    )
    return pl.pallas_call(
        functools.partial(_compress_pages_body, n_x=nb * n_pages),
        grid_spec=grid_spec,
        out_shape=jax.ShapeDtypeStruct((db * n_pages * sub_pp, ROW_W), BF16),
        compiler_params=_params("parallel"),
        name="compress_pages",
    )(page_table.reshape(-1), *([pool] * (nb * n_pages)), perm, wts, pes)


def _cover(nsub, nc, n_sel):
    c_start = jnp.arange(nsub) * CMP_STRIDE
    s_start = jnp.arange(SEL_BLK) * SEL_BLK
    ok = ((c_start[:, None] < s_start[None, :] + SEL_BLK) & (c_start[:, None] + CMP_LEN > s_start[None, :])
          & (jnp.arange(nsub)[:, None] < nc) & (jnp.arange(SEL_BLK)[None, :] < n_sel))
    return ok.astype(BF16)


def _expand(n_keys, tk):
    key = jnp.arange(n_keys).reshape(n_keys // tk, 1, tk)
    return (key // SEL_BLK == jnp.arange(SEL_BLK)[None, :, None]).astype(BF16)


def _masked_softmax(s, ok):
    sm = jnp.where(ok, s, NEG)
    ex = jnp.exp(sm - jnp.max(sm, -1, keepdims=True))
    return ex / jnp.sum(ex, -1, keepdims=True)


def _importance(psum, cov_ref, tpos, n_sel):
    hi = psum.astype(BF16)
    lo = (psum - hi.astype(F32)).astype(BF16)
    imp = (jnp.dot(hi, cov_ref[...], preferred_element_type=F32)
           + jnp.dot(lo, cov_ref[...], preferred_element_type=F32))
    jl = lax.broadcasted_iota(jnp.int32, imp.shape, 1)
    cur = tpos // SEL_BLK
    forced = (jl == 0) | (jl == cur) | (jl == cur - 1)
    causal = jl * SEL_BLK <= tpos
    imp = jnp.where(forced, FORCE, jnp.where(causal, imp, -1.0))
    return jnp.where(jl < n_sel, imp, -2.0)


def _topk_masks(imps):
    out = []
    for pair in range(len(imps) // 2):
        xt = jnp.concatenate([imps[2 * pair], imps[2 * pair + 1]], axis=1).T
        halves = []
        for hh in range(2):
            xa = xt[SEL_BLK * hh:SEL_BLK * (hh + 1)]
            jrow = lax.broadcasted_iota(jnp.int32, xa.shape, 0)
            cnt = jnp.zeros(xa.shape, F32)
            for i in range(SEL_BLK):
                xi = xa[i:i + 1, :]
                tie = jnp.where(jrow > i, 1.0, 0.0)
                cnt = cnt + jnp.where(xi > xa, 1.0, jnp.where(xi == xa, tie, 0.0))
            halves.append(jnp.where(cnt < SEL_TOPK, 1.0, 0.0))
        sel2 = jnp.concatenate(halves, axis=0).T
        out += [sel2[:, 0:SEL_BLK], sel2[:, SEL_BLK:2 * SEL_BLK]]
    return out


def _topk_masks_rows(imps):
    out = []
    for x in imps:
        jl = lax.broadcasted_iota(jnp.int32, x.shape, 1)
        cnt = jnp.zeros(x.shape, F32)
        for i in range(SEL_BLK):
            xi = x[:, i:i + 1]
            tie = jnp.where(jl > i, 1.0, 0.0)
            cnt = cnt + jnp.where(xi > x, 1.0, jnp.where(xi == x, tie, 0.0))
        out.append(jnp.where(cnt < SEL_TOPK, 1.0, 0.0))
    return out


_NT = (((1,), (1,)), ((), ()))


def _nsa_prompt_body(q_ref, kc_ref, ks_ref, kw_ref, gt_ref, cov_ref, e_ref, o_ref, *, tq, nc, n_sel):
    t0 = pl.program_id(1) * tq
    tpos = t0 + lax.broadcasted_iota(jnp.int32, (tq, 1), 0)
    sig = jax.nn.sigmoid(gt_ref[...])
    nsub = kc_ref.shape[0]
    rep, hd, kvw = NSA_REP, HEAD_DIM, NSA_KV * HEAD_DIM
    mcol = lax.broadcasted_iota(jnp.int32, (tq, nsub), 1)
    valid = (mcol * CMP_STRIDE + CMP_LEN - 1 <= tpos) & (mcol < nc)
    validf = jnp.where(valid, 1.0, 0.0)

    q_st, o_cmp, imps = [], [], []
    for g in range(NSA_KV):
        qg = jnp.concatenate([q_ref[:, (g * rep + r) * hd:(g * rep + r + 1) * hd] for r in range(rep)], axis=0)
        q_st.append(qg)
        s = lax.dot_general(qg, kc_ref[:, g * hd:(g + 1) * hd], _NT, preferred_element_type=F32)
        p = _masked_softmax(s.reshape(rep, tq, nsub), valid[None]) * validf[None]
        o_cmp.append(jnp.dot(p.reshape(rep * tq, nsub).astype(BF16), kc_ref[:, kvw + g * hd:kvw + (g + 1) * hd],
                             preferred_element_type=F32))
        imps.append(_importance(jnp.sum(p, axis=0), cov_ref, tpos, n_sel))
    sels = _topk_masks(imps)

    tk = ks_ref.shape[-1]
    wt = kw_ref.shape[-1]
    kt_d = t0 // tk
    kpos_d = kt_d * tk + lax.broadcasted_iota(jnp.int32, (tq, tk), 1)
    bias_d = jnp.where(kpos_d <= tpos, 0.0, NEG)
    n_wt = (WINDOW + tq) // wt
    wt0 = jnp.maximum(t0 - WINDOW, 0) // wt
    kpos_w = wt0 * wt + lax.broadcasted_iota(jnp.int32, (tq, n_wt * wt), 1)
    bias_w = jnp.where((kpos_w <= tpos) & (kpos_w > tpos - WINDOW), 0.0, NEG)

    def biased(s, bias):
        return (s.reshape(rep, tq, s.shape[-1]) + bias[None]).reshape(s.shape)

    def lanes(x):
        return jnp.broadcast_to(x, (x.shape[0], LANE))

    krows = [slice(g * hd, (g + 1) * hd) for g in range(NSA_KV)]
    vrows = [slice(kvw + g * hd, kvw + (g + 1) * hd) for g in range(NSA_KV)]

    o_win = []
    for g in range(NSA_KV):
        k_w = jnp.concatenate([kw_ref[wt0 + i, krows[g], :] for i in range(n_wt)], axis=1)
        v_w = jnp.concatenate([kw_ref[wt0 + i, vrows[g], :] for i in range(n_wt)], axis=1)
        s = biased(jnp.dot(q_st[g], k_w, preferred_element_type=F32), bias_w)
        p = jnp.exp(s - jnp.max(s, -1, keepdims=True))
        o_win.append((lax.dot_general(p.astype(BF16), v_w, _NT, preferred_element_type=F32)
                      / jnp.sum(p, -1, keepdims=True)).reshape(rep, tq, hd))

    q_aug = []
    for g in range(NSA_KV):
        selneg = jnp.where(sels[g] > 0.5, 0.0, NEG)
        q_aug.append(jnp.concatenate(
            [jnp.concatenate([q_ref[:, (g * rep + r) * hd:(g * rep + r + 1) * hd].astype(F32), selneg], axis=1)
             for r in range(rep)], axis=0).astype(BF16))

    def scores(kt, g):
        k_aug = jnp.concatenate([ks_ref[kt, krows[g], :], e_ref[kt]], axis=0)
        return jnp.dot(q_aug[g], k_aug, preferred_element_type=F32)

    init = []
    for g in range(NSA_KV):
        s = biased(scores(kt_d, g), bias_d)
        m0 = jnp.max(s, -1, keepdims=True)
        p = jnp.exp(s - m0)
        l0 = jnp.sum(p, -1, keepdims=True)
        acc0 = lax.dot_general(p.astype(BF16), ks_ref[kt_d, vrows[g], :], _NT, preferred_element_type=F32)
        init.append((lanes(m0), lanes(l0), acc0))

    def step(kt, carry):
        out = []
        for g in range(NSA_KV):
            m_i, l_i, acc = carry[g]
            s = scores(kt, g)
            m_new = jnp.maximum(m_i, lanes(jnp.max(s, -1, keepdims=True)))
            alpha = jnp.exp(m_i - m_new)
            p = jnp.exp(s - jnp.tile(m_new, (1, tk // LANE)))
            l_new = alpha * l_i + lanes(jnp.sum(p, -1, keepdims=True))
            pv = lax.dot_general(p.astype(BF16), ks_ref[kt, vrows[g], :], _NT, preferred_element_type=F32)
            out.append((m_new, l_new, alpha[:, 0:hd] * acc + pv))
        return tuple(out)

    final = lax.fori_loop(0, kt_d, step, tuple(init))

    nh = NSA_KV * rep
    for g in range(NSA_KV):
        _, l_s, acc_s = final[g]
        o_sel = (acc_s / l_s[:, 0:hd]).reshape(rep, tq, hd)
        o_c = o_cmp[g].reshape(rep, tq, hd)
        for r in range(rep):
            h = g * rep + r
            o_ref[:, h * hd:(h + 1) * hd] = (sig[:, h:h + 1] * o_c[r] + sig[:, nh + h:nh + h + 1] * o_sel[r]
                                            + sig[:, 2 * nh + h:2 * nh + h + 1] * o_win[g][r])


def nsa_prompt(q, kcvc, ks, kw, gates, nseq, seq_len, tq=128):
    m = q.shape[0]
    nsub = kcvc.shape[0] // nseq
    tk, wt = ks.shape[-1], kw.shape[-1]
    nc = (seq_len - CMP_LEN) // CMP_STRIDE + 1
    n_sel = seq_len // SEL_BLK
    assert n_sel <= SEL_BLK and seq_len >= WINDOW + tq and tk % tq == 0 and tq % wt == 0 and WINDOW % wt == 0
    nq = seq_len // tq
    cov = _cover(nsub, nc, n_sel)
    exp = _expand(seq_len, tk)
    tile = lambda w: pl.BlockSpec((tq, w), lambda b, t: (b * nq + t, 0))
    whole = lambda a: pl.BlockSpec((None,) + a.shape[1:], lambda b, t: (b, 0, 0, 0))
    return pl.pallas_call(
        functools.partial(_nsa_prompt_body, tq=tq, nc=nc, n_sel=n_sel),
        grid=(nseq, nq),
        in_specs=[tile(q.shape[1]), pl.BlockSpec((nsub, ROW_W), lambda b, t: (b, 0)), whole(ks), whole(kw), tile(LANE),
                  pl.BlockSpec(cov.shape, lambda b, t: (0, 0)),
                  pl.BlockSpec(exp.shape, lambda b, t: (0, 0, 0))],
        out_specs=tile(q.shape[1]),
        out_shape=jax.ShapeDtypeStruct((m, q.shape[1]), F32),
        compiler_params=_params("parallel", "parallel"),
        name="nsa_prompt",
    )(q, kcvc, ks, kw, gates, cov, exp)


def _softmax_cols(s):
    p = jnp.exp2(s - jnp.max(s, axis=0, keepdims=True))
    return p, jnp.sum(p, axis=0, keepdims=True)


def _nsa_prompt_t_body(q_ref, kc_ref, ka_ref, vs_ref, kw_ref, vw_ref, gt_ref, cov_ref, et_ref, o_ref, *, tq, nc, n_sel):
    t0 = pl.program_id(1) * tq
    tl = t0 + lax.broadcasted_iota(jnp.int32, (1, tq), 1)
    sig = jax.nn.sigmoid(gt_ref[...])
    nsub = kc_ref.shape[0]
    rep, hd = NSA_REP, HEAD_DIM
    nh = NSA_KV * rep
    tk = vs_ref.shape[-1]
    wt = vw_ref.shape[-1]
    wide = lambda x: jnp.concatenate([x] * rep, axis=1)
    q4 = [jnp.concatenate([q_ref[(g * rep + r) * hd:(g * rep + r + 1) * hd, :] for r in range(rep)], axis=1)
          for g in range(NSA_KV)]

    mrow = lax.broadcasted_iota(jnp.int32, (nsub, 1), 0)
    valid = (mrow * CMP_STRIDE + CMP_LEN - 1 <= tl) & (mrow < nc)
    validf = wide(jnp.where(valid, 1.0, 0.0))
    jrow = lax.broadcasted_iota(jnp.int32, (SEL_BLK, tq), 0)
    cur = tl // SEL_BLK
    forced = (jrow == 0) | (jrow == cur) | (jrow == cur - 1)
    causal = jrow * SEL_BLK <= tl
    o_cmp, sel_neg = [], []
    for g in range(NSA_KV):
        kc = kc_ref[:, g * hd:(g + 1) * hd]
        vc_t = kc_ref[:, NSA_KV * hd + g * hd:NSA_KV * hd + (g + 1) * hd].astype(F32).T.astype(BF16)
        s = jnp.dot(kc, q4[g], preferred_element_type=F32)
        s = jnp.where(validf > 0.5, s, NEG)
        ex, l = _softmax_cols(s)
        p = ex / l * validf
        o_cmp.append(jnp.dot(vc_t, p.astype(BF16), preferred_element_type=F32))
        psum = p[:, 0:tq]
        for r in range(1, rep):
            psum = psum + p[:, r * tq:(r + 1) * tq]
        hi = psum.astype(BF16)
        lo = (psum - hi.astype(F32)).astype(BF16)
        imp = (jnp.dot(cov_ref[...], hi, preferred_element_type=F32)
               + jnp.dot(cov_ref[...], lo, preferred_element_type=F32))
        imp = jnp.where(forced, FORCE, jnp.where(causal, imp, -1.0))
        imp = jnp.where(jrow < n_sel, imp, -2.0)
        cnt = jnp.zeros(imp.shape, F32)
        for i in range(SEL_BLK):
            xi = imp[i:i + 1, :]
            tie = jnp.where(jrow > i, 1.0, 0.0)
            cnt = cnt + jnp.where(xi > imp, 1.0, jnp.where(xi == imp, tie, 0.0))
        sel_neg.append(jnp.where(cnt < SEL_TOPK, 0.0, NEG).astype(BF16))

    n_wt = (WINDOW + tq) // wt
    w0 = jnp.maximum(t0 - WINDOW, 0)
    wt0 = w0 // wt
    wrows = n_wt * wt
    kpos_w = w0 + lax.broadcasted_iota(jnp.int32, (wrows, 1), 0)
    bias_w = wide(jnp.where((kpos_w <= tl) & (kpos_w > tl - WINDOW), 0.0, NEG))
    o_win = []
    for g in range(NSA_KV):
        k_w = kw_ref[pl.ds(pl.multiple_of(w0, wt), wrows), g * hd:(g + 1) * hd]
        v_w = jnp.concatenate([vw_ref[wt0 + i, g * hd:(g + 1) * hd, :] for i in range(n_wt)], axis=1)
        p, l = _softmax_cols(jnp.dot(k_w, q4[g], preferred_element_type=F32) + bias_w)
        o_win.append(jnp.dot(v_w, p.astype(BF16), preferred_element_type=F32) / l)

    q_aug = [jnp.concatenate(
        [jnp.concatenate([q_ref[(g * rep + r) * hd:(g * rep + r + 1) * hd, :], sel_neg[g]], axis=0)
         for r in range(rep)], axis=1) for g in range(NSA_KV)]

    def keys(kt, g):
        rows = pl.ds(pl.multiple_of(kt * tk, tk), tk)
        return ka_ref[rows, g * LANE:(g + 1) * LANE] + et_ref[rows, :]

    kt_d = t0 // tk
    kpos_d = kt_d * tk + lax.broadcasted_iota(jnp.int32, (tk, 1), 0)
    bias_d = wide(jnp.where(kpos_d <= tl, 0.0, NEG))
    init = []
    for g in range(NSA_KV):
        s = jnp.dot(keys(kt_d, g), q_aug[g], preferred_element_type=F32) + bias_d
        m0 = jnp.max(s, axis=0, keepdims=True)
        p = jnp.exp(s - m0)
        init.append((m0, jnp.sum(p, axis=0, keepdims=True),
                     jnp.dot(vs_ref[kt_d, g * hd:(g + 1) * hd, :], p.astype(BF16), preferred_element_type=F32)))

    def step(kt, carry):
        ng = NSA_KV
        s, soft, out = [None] * ng, [None] * ng, [None] * ng

        def scores(g):
            s[g] = jnp.dot(keys(kt, g), q_aug[g], preferred_element_type=F32)

        def softmax(g):
            m_i, l_i, _ = carry[g]
            m_new = jnp.maximum(m_i, jnp.max(s[g], axis=0, keepdims=True))
            alpha = jnp.exp(m_i - m_new)
            p = jnp.exp(s[g] - m_new)
            soft[g] = (m_new, alpha, alpha * l_i + jnp.sum(p, axis=0, keepdims=True), p.astype(BF16))

        def values(g):
            m_new, alpha, l_new, p = soft[g]
            pv = jnp.dot(vs_ref[kt, g * hd:(g + 1) * hd, :], p, preferred_element_type=F32)
            out[g] = (m_new, l_new, alpha * carry[g][2] + pv)

        scores(0)
        for g in range(ng):
            if g + 1 < ng:
                scores(g + 1)
            softmax(g)
            if g >= 1:
                values(g - 1)
        values(ng - 1)
        return tuple(out)

    final = lax.fori_loop(0, kt_d, step, tuple(init))

    for g in range(NSA_KV):
        _, l_s, acc_s = final[g]
        o_sel = acc_s / l_s
        heads = []
        for r in range(rep):
            h = g * rep + r
            cols = slice(r * tq, (r + 1) * tq)
            heads.append(sig[h:h + 1, :] * o_cmp[g][:, cols] + sig[nh + h:nh + h + 1, :] * o_sel[:, cols]
                         + sig[2 * nh + h:2 * nh + h + 1, :] * o_win[g][:, cols])
        for pair in range(rep // 2):
            o_ref[:, (g * rep + 2 * pair) * hd:(g * rep + 2 * pair + 2) * hd] = (
                jnp.concatenate([heads[2 * pair], heads[2 * pair + 1]], axis=0).T)


def _run_skewed(tasks):
    n = len(tasks)
    tasks[0][0]()
    for i in range(n):
        if i + 1 < n:
            tasks[i + 1][0]()
        tasks[i][1]()
        if i >= 1:
            tasks[i - 1][2]()
    tasks[n - 1][2]()


def _nsa_prompt_lanes_body(q_ref, kc_ref, ka_ref, vs_ref, kw_ref, vw_ref, gt_ref, cov_ref, et_ref, o_ref, *,
                           tq, nc, n_sel):
    t0 = pl.program_id(1) * tq
    tl = t0 + lax.broadcasted_iota(jnp.int32, (1, tq), 1)
    sig = jax.nn.sigmoid(gt_ref[...])
    nsub = kc_ref.shape[0]
    rep, hd, ng = NSA_REP, HEAD_DIM, NSA_KV
    nh = ng * rep
    tk = vs_ref.shape[-1]
    wt = vw_ref.shape[-1]
    wide = lambda x: jnp.concatenate([x] * rep, axis=1)
    head_rows = lambda g, r: q_ref[(g * rep + r) * hd:(g * rep + r + 1) * hd, :]
    q4 = [jnp.concatenate([head_rows(g, r) for r in range(rep)], axis=1) for g in range(ng)]

    mrow = lax.broadcasted_iota(jnp.int32, (nsub, 1), 0)
    validf = wide(jnp.where((mrow * CMP_STRIDE + CMP_LEN - 1 <= tl) & (mrow < nc), 1.0, 0.0))
    jrow = lax.broadcasted_iota(jnp.int32, (SEL_BLK, tq), 0)
    cur = tl // SEL_BLK
    forced = (jrow == 0) | (jrow == cur) | (jrow == cur - 1)
    causal = jrow * SEL_BLK <= tl
    s_cmp, p_cmp, o_cmp, sel_neg = [None] * ng, [None] * ng, [None] * ng, [None] * ng

    def cmp_scores(g):
        s_cmp[g] = jnp.dot(kc_ref[:, g * hd:(g + 1) * hd], q4[g], preferred_element_type=F32)

    def cmp_softmax(g):
        ex, l = _softmax_cols(jnp.where(validf > 0.5, s_cmp[g], NEG))
        p = ex / l * validf
        p_cmp[g] = p.astype(BF16)
        psum = p[:, 0:tq]
        for r in range(1, rep):
            psum = psum + p[:, r * tq:(r + 1) * tq]
        hi = psum.astype(BF16)
        lo = (psum - hi.astype(F32)).astype(BF16)
        imp = (jnp.dot(cov_ref[...], hi, preferred_element_type=F32)
               + jnp.dot(cov_ref[...], lo, preferred_element_type=F32))
        imp = jnp.where(forced, FORCE, jnp.where(causal, imp, -1.0))
        imp = jnp.where(jrow < n_sel, imp, -2.0)
        sub = 8
        slabs = [imp[v * sub:(v + 1) * sub, :] for v in range(SEL_BLK // sub)]
        cnts = [jnp.zeros((sub, tq), F32) for _ in slabs]
        srow = lax.broadcasted_iota(jnp.int32, (sub, tq), 0)
        for i in range(SEL_BLK):
            xi = jnp.broadcast_to(imp[i:i + 1, :], (sub, tq))
            for v, x in enumerate(slabs):
                if v > i // sub:
                    hit = jnp.where(xi >= x, 1.0, 0.0)
                elif v < i // sub:
                    hit = jnp.where(xi > x, 1.0, 0.0)
                else:
                    tie = jnp.where(srow > i % sub, 1.0, 0.0)
                    hit = jnp.where(xi > x, 1.0, jnp.where(xi == x, tie, 0.0))
                cnts[v] = cnts[v] + hit
        cnt = jnp.concatenate(cnts, axis=0)
        sel_neg[g] = jnp.where(cnt < SEL_TOPK, 0.0, NEG).astype(BF16)

    def cmp_values(g):
        vc_t = kc_ref[:, ng * hd + g * hd:ng * hd + (g + 1) * hd].astype(F32).T.astype(BF16)
        o_cmp[g] = jnp.dot(vc_t, p_cmp[g], preferred_element_type=F32)

    n_wt = (WINDOW + tq) // wt
    w0 = jnp.maximum(t0 - WINDOW, 0)
    wt0 = w0 // wt
    wrows = n_wt * wt
    kpos_w = w0 + lax.broadcasted_iota(jnp.int32, (wrows, 1), 0)
    bias_w = wide(jnp.where((kpos_w <= tl) & (kpos_w > tl - WINDOW), 0.0, NEG))
    s_win, p_win, o_win = [None] * ng, [None] * ng, [None] * ng

    def win_scores(g):
        k_w = kw_ref[pl.ds(pl.multiple_of(w0, wt), wrows), g * hd:(g + 1) * hd]
        s_win[g] = jnp.dot(k_w, q4[g], preferred_element_type=F32) + bias_w

    def win_softmax(g):
        p, l = _softmax_cols(s_win[g])
        p_win[g] = (p.astype(BF16), l)

    def win_values(g):
        p, l = p_win[g]
        v_w = jnp.concatenate([vw_ref[wt0 + i, g * hd:(g + 1) * hd, :] for i in range(n_wt)], axis=1)
        o_win[g] = jnp.dot(v_w, p, preferred_element_type=F32) / l

    q_aug = [None] * ng

    def keys(kt, g):
        rows = pl.ds(pl.multiple_of(kt * tk, tk), tk)
        return ka_ref[rows, g * LANE:(g + 1) * LANE] + et_ref[rows, :]

    def sweep_tasks(kt, carry, out, bias):
        s, soft = [None] * ng, [None] * ng

        def scores(g):
            if q_aug[g] is None:
                q_aug[g] = jnp.concatenate([jnp.concatenate([head_rows(g, r), sel_neg[g]], axis=0)
                                            for r in range(rep)], axis=1)
            s[g] = jnp.dot(keys(kt, g), q_aug[g], preferred_element_type=F32)
            if bias is not None:
                s[g] = s[g] + bias

        def softmax(g):
            m_i, l_i, _ = carry[g]
            m_new = jnp.maximum(m_i, jnp.max(s[g], axis=0, keepdims=True))
            alpha = jnp.exp2(m_i - m_new)
            p = jnp.exp2(s[g] - m_new)
            soft[g] = (m_new, alpha, alpha * l_i + jnp.sum(p, axis=0, keepdims=True), p.astype(BF16))

        def values(g):
            m_new, alpha, l_new, p = soft[g]
            pv = jnp.dot(vs_ref[kt, g * hd:(g + 1) * hd, :], p, preferred_element_type=F32)
            out[g] = (m_new, l_new, alpha * carry[g][2] + pv)

        return [(functools.partial(scores, g), functools.partial(softmax, g), functools.partial(values, g))
                for g in range(ng)]

    kt_d = t0 // tk
    kpos_d = kt_d * tk + lax.broadcasted_iota(jnp.int32, (tk, 1), 0)
    bias_d = wide(jnp.where(kpos_d <= tl, 0.0, NEG))
    empty = (jnp.full((1, rep * tq), NEG, F32), jnp.zeros((1, rep * tq), F32), jnp.zeros((hd, rep * tq), F32))
    init = [None] * ng
    tasks = []
    for g in range(ng):
        tasks.append((functools.partial(cmp_scores, g), functools.partial(cmp_softmax, g),
                      functools.partial(cmp_values, g)))
        tasks.append((functools.partial(win_scores, g), functools.partial(win_softmax, g),
                      functools.partial(win_values, g)))
    _run_skewed(tasks + sweep_tasks(kt_d, [empty] * ng, init, bias_d))

    def step(kt, carry):
        out = [None] * ng
        _run_skewed(sweep_tasks(kt, carry, out, None))
        return tuple(out)

    final = lax.fori_loop(0, kt_d, step, tuple(init))

    for g in range(ng):
        _, l_s, acc_s = final[g]
        o_sel = acc_s / l_s
        heads = []
        for r in range(rep):
            h = g * rep + r
            cols = slice(r * tq, (r + 1) * tq)
            heads.append(sig[h:h + 1, :] * o_cmp[g][:, cols] + sig[nh + h:nh + h + 1, :] * o_sel[:, cols]
                         + sig[2 * nh + h:2 * nh + h + 1, :] * o_win[g][:, cols])
        for pair in range(rep // 2):
            o_ref[:, (g * rep + 2 * pair) * hd:(g * rep + 2 * pair + 2) * hd] = (
                jnp.concatenate([heads[2 * pair], heads[2 * pair + 1]], axis=0).T)


def nsa_prompt_t(q_t, kcvc, ka, vs_t, kw, vw_t, gates_t, nseq, seq_len, tq=128):
    m = ka.shape[0]
    nsub = kcvc.shape[0] // nseq
    tk, wt = vs_t.shape[-1], vw_t.shape[-1]
    nc = (seq_len - CMP_LEN) // CMP_STRIDE + 1
    n_sel = seq_len // SEL_BLK
    assert n_sel <= SEL_BLK and seq_len >= WINDOW + tq and tk % tq == 0 and tq % wt == 0 and WINDOW % wt == 0
    nq = seq_len // tq
    width = q_t.shape[1]
    cov_t = _cover(nsub, nc, n_sel).T
    key = jnp.arange(seq_len)
    et = jnp.concatenate([jnp.zeros((seq_len, HEAD_DIM), BF16),
                          (key[:, None] // SEL_BLK == jnp.arange(SEL_BLK)[None, :]).astype(BF16)], axis=1)
    cols = lambda a: pl.BlockSpec((None, a.shape[1], tq), lambda b, t: (b, 0, t))
    seq_rows = lambda a: pl.BlockSpec((seq_len, a.shape[1]), lambda b, t: (b, 0))
    whole = lambda a: pl.BlockSpec((None,) + a.shape[1:], lambda b, t: (b, 0, 0, 0))
    return pl.pallas_call(
        functools.partial(_nsa_prompt_lanes_body, tq=tq, nc=nc, n_sel=n_sel),
        grid=(nseq, nq),
        in_specs=[cols(q_t), pl.BlockSpec((nsub, ROW_W), lambda b, t: (b, 0)), seq_rows(ka), whole(vs_t),
                  seq_rows(kw), whole(vw_t), cols(gates_t),
                  pl.BlockSpec(cov_t.shape, lambda b, t: (0, 0)),
                  pl.BlockSpec(et.shape, lambda b, t: (0, 0))],
        out_specs=pl.BlockSpec((tq, width), lambda b, t: (b * nq + t, 0)),
        out_shape=jax.ShapeDtypeStruct((m, width), F32),
        compiler_params=_params("parallel", "parallel"),
        name="nsa_prompt",
    )(q_t, kcvc, ka, vs_t, kw, vw_t, gates_t, cov_t, et)


def _joint_softmax(s_a, ok_a, s_b, ok_b):
    s_a = jnp.where(ok_a, s_a, NEG)
    s_b = jnp.where(ok_b, s_b, NEG)
    mx = jnp.maximum(jnp.max(s_a, -1, keepdims=True), jnp.max(s_b, -1, keepdims=True))
    e_a = jnp.exp(s_a - mx)
    e_b = jnp.exp(s_b - mx)
    return e_a, e_b, jnp.sum(e_a, -1, keepdims=True) + jnp.sum(e_b, -1, keepdims=True)


def _nsa_sample_body(pt_ref, q_ref, kc_ref, *rest, n, n_pages, past_len, nc, n_sel):
    page_refs = rest[:n_pages]
    ns_ref, nw_ref, win_ref, gt_ref, cov_ref, e_ref = rest[n_pages:n_pages + 6]
    o_ref, wo_ref = rest[-2:]
    np_, rep, hd, kvw, nkv = SAMPLE_PAD, NSA_REP, HEAD_DIM, NSA_KV * HEAD_DIM, NSA_KV
    tpos = past_len + lax.broadcasted_iota(jnp.int32, (np_, 1), 0)
    sig = jax.nn.sigmoid(gt_ref[...])
    q = q_ref[...]

    blocks = []
    for g in range(nkv):
        for r in range(rep):
            h = g * rep + r
            parts = [q[:, h * hd:(h + 1) * hd]]
            if g > 0:
                parts.insert(0, jnp.zeros((np_, g * hd), F32))
            if g < nkv - 1:
                parts.append(jnp.zeros((np_, (nkv - 1 - g) * hd), F32))
            blocks.append(jnp.concatenate(parts, axis=1))
    qbd = jnp.concatenate(blocks, axis=0).astype(BF16)
    nrow = nkv * rep * np_

    def grp(x):
        return x.reshape(nkv, rep, np_, x.shape[-1])

    def scores(keys):
        return grp(lax.dot_general(qbd, keys, _NT, preferred_element_type=F32))

    def scores_t(keys_t):
        return grp(jnp.dot(qbd, keys_t, preferred_element_type=F32))

    new_lane = lax.broadcasted_iota(jnp.int32, (np_, np_), 1)
    new_pos = past_len + new_lane

    nsub = kc_ref.shape[0]
    mcol = lax.broadcasted_iota(jnp.int32, (np_, nsub), 1)
    valid = (mcol * CMP_STRIDE + CMP_LEN - 1 <= tpos) & (mcol < nc)
    p = _masked_softmax(scores(kc_ref[:, 0:kvw]), valid[None, None]) * jnp.where(valid, 1.0, 0.0)[None, None]
    o_cmp = jnp.dot(p.reshape(nrow, nsub).astype(BF16), kc_ref[:, kvw:2 * kvw], preferred_element_type=F32)
    sels = _topk_masks_rows([_importance(jnp.sum(p[g], axis=0), cov_ref, tpos, n_sel) for g in range(nkv)])

    page = page_refs[0].shape[-1]
    k_all = jnp.concatenate([pr[0].reshape(kvw, page) for pr in page_refs], axis=1).astype(BF16)
    v_all = jnp.concatenate([pr[1].reshape(kvw, page) for pr in page_refs], axis=1).astype(BF16)
    kpos = lax.broadcasted_iota(jnp.int32, (np_, past_len), 1)
    ok_c = jnp.stack([jnp.where(kpos <= tpos, jnp.dot(sels[g].astype(BF16), e_ref[0], preferred_element_type=F32), 0.0)
                      for g in range(nkv)]) > 0.5
    new_ok = (new_pos <= tpos) & (new_lane < n)
    ok_n = jnp.stack([jnp.where(new_ok, jnp.concatenate(
        [sels[g][:, (past_len + i) // SEL_BLK:(past_len + i) // SEL_BLK + 1] for i in range(np_)], axis=1), 0.0)
        for g in range(nkv)]) > 0.5
    ns = ns_ref[...]
    e_c, e_n, l_s = _joint_softmax(scores_t(k_all), ok_c[:, None], scores(ns[:, 0:kvw].astype(BF16)), ok_n[:, None])
    o_sel = lax.dot_general(e_c.reshape(nrow, past_len).astype(BF16), v_all, _NT, preferred_element_type=F32)
    e_n = e_n.reshape(nrow, np_)
    for i in range(n):
        o_sel = o_sel + e_n[:, i:i + 1] * ns[i:i + 1, kvw:2 * kvw]
    o_sel = o_sel / l_s.reshape(nrow, 1)

    w_buf = win_ref.shape[-1]
    k_w = win_ref[0].reshape(kvw, w_buf)
    v_w = win_ref[1].reshape(kvw, w_buf)
    kpos_w = past_len - w_buf + lax.broadcasted_iota(jnp.int32, (np_, w_buf), 1)
    ok_w = (kpos_w <= tpos) & (kpos_w > tpos - WINDOW)
    ok_wn = (new_pos <= tpos) & (new_pos > tpos - WINDOW) & (new_lane < n)
    nw = nw_ref[...]
    e_w, e_wn, l_w = _joint_softmax(scores_t(k_w.astype(BF16)), ok_w[None, None],
                                    scores(nw[:, 0:kvw].astype(BF16)), ok_wn[None, None])
    o_win = lax.dot_general(e_w.reshape(nrow, w_buf).astype(BF16), v_w.astype(BF16), _NT,
                            preferred_element_type=F32)
    e_wn = e_wn.reshape(nrow, np_)
    for i in range(n):
        o_win = o_win + e_wn[:, i:i + 1] * nw[i:i + 1, kvw:2 * kvw]
    o_win = o_win / l_w.reshape(nrow, 1)

    nh = nkv * rep
    for g in range(nkv):
        for r in range(rep):
            h = g * rep + r
            rows = slice(h * np_, (h + 1) * np_)
            cols = slice(g * hd, (g + 1) * hd)
            o_ref[:, h * hd:(h + 1) * hd] = (sig[:, h:h + 1] * o_cmp[rows, cols]
                                            + sig[:, nh + h:nh + h + 1] * o_sel[rows, cols]
                                            + sig[:, 2 * nh + h:2 * nh + h + 1] * o_win[rows, cols])

    body = w_buf - LANE
    for c, old in enumerate((k_w, v_w)):
        shifted = pltpu.roll(old, w_buf - n, 1)
        new_t = jnp.concatenate([nw[:, c * kvw:(c + 1) * kvw].T, jnp.zeros((kvw, LANE - np_), F32)], axis=1)
        lane = lax.broadcasted_iota(jnp.int32, (kvw, LANE), 1)
        wo_ref[c * kvw:(c + 1) * kvw, 0:body] = shifted[:, 0:body]
        wo_ref[c * kvw:(c + 1) * kvw, body:w_buf] = jnp.where(lane < LANE - n, shifted[:, body:w_buf],
                                                              pltpu.roll(new_t, LANE - n, 1))


def nsa_sample(q, kcvc, pool, layer, page_table, rows_s, rows_w, win, gates, n, prev=None):
    db, n_pages = page_table.shape
    page = pool.shape[-1]
    past_len = n_pages * page
    nsub = kcvc.shape[0] // db
    w_buf = win.shape[-1]
    assert w_buf == WINDOW and n <= SAMPLE_PAD and past_len % SEL_BLK == 0
    nc = (past_len + n - CMP_LEN) // CMP_STRIDE + 1
    n_sel = -(-(past_len + n) // SEL_BLK)
    assert n_sel <= SEL_BLK and nc <= nsub
    cov = _cover(nsub, nc, n_sel)
    exp = _expand(past_len, past_len)
    rows8 = lambda w: pl.BlockSpec((SAMPLE_PAD, w), lambda i, pt: (i, 0))
    pages = [pl.BlockSpec((None, None) + pool.shape[2:],
                          functools.partial(lambda i, pt, k: (layer, pt[i * n_pages + k], 0, 0, 0, 0), k=k))
             for k in range(n_pages)]
    grid_spec = pltpu.PrefetchScalarGridSpec(
        num_scalar_prefetch=1,
        grid=(db,),
        in_specs=[rows8(q.shape[1]), pl.BlockSpec((nsub, ROW_W), lambda i, pt: (i, 0))] + pages
        + [rows8(ROW_W), rows8(ROW_W),
           pl.BlockSpec((None, None) + win.shape[2:], lambda i, pt: (layer, i, 0, 0, 0, 0)),
           rows8(LANE),
           pl.BlockSpec(cov.shape, lambda i, pt: (0, 0)),
           pl.BlockSpec(exp.shape, lambda i, pt: (0, 0, 0))]
        + ([pl.BlockSpec(memory_space=pl.ANY)] if prev is not None else []),
        out_specs=[rows8(q.shape[1]),
                   pl.BlockSpec((None, None, ROW_W, w_buf), lambda i, pt: (layer, i, 0, 0))],
    )
    args = (page_table.reshape(-1), q, kcvc, *([pool] * n_pages), rows_s, rows_w, win, gates, cov, exp)
    return pl.pallas_call(
        functools.partial(_nsa_sample_body, n=n, n_pages=n_pages, past_len=past_len, nc=nc, n_sel=n_sel),
        grid_spec=grid_spec,
        out_shape=[jax.ShapeDtypeStruct(q.shape, F32),
                   jax.ShapeDtypeStruct((win.shape[0], db, ROW_W, w_buf), F32)],
        input_output_aliases={len(args): 1} if prev is not None else {},
        compiler_params=_params("parallel"),
        name="nsa_sample",
    )(*args, *([prev] if prev is not None else []))


def kernel(x_prompt, x_sample, state_ret, cache_cmp, cache_sel, state_win, state_ffn, page_table, norm_mix_pre, norm_mix_post, norm_ffn_pre, norm_ffn_post, e_w_in, e_w_out, e_sg_ln_g, e_sg_ln_b, e_sg_w, e_sg_b, o_w_in, o_w_out, o_pe_k, o_pe_v, o_phi_k, o_phi_v, f_w_in, f_conv_w, f_conv_b, f_w_out):
    b, s, d = x_prompt.shape
    db, n, _ = x_sample.shape
    depth = norm_mix_pre.shape[0]
    heads = state_ret.shape[2]
    groups = e_sg_w.shape[1]
    n_pages = page_table.shape[1]
    page = cache_cmp.shape[2]
    past_len = n_pages * page
    pad = SAMPLE_PAD
    assert n <= pad and n >= CONV_W - 1 and n < CMP_STRIDE and state_ret.shape[3] == LANE

    xp = x_prompt.reshape(b * s, d)
    xs = jnp.pad(x_sample, ((0, 0), (0, pad - n), (0, 0))).reshape(db * pad, d)
    pos_p = jnp.arange(s, dtype=jnp.int32)
    pos_s = jnp.tile(past_len + jnp.arange(pad, dtype=jnp.int32), db)
    ret_p, ret_s = _ret_rope_tabs(pos_p), _ret_rope_tabs(pos_s)
    nsa_p, nsa_s = _nsa_rope_tabs(pos_p), _nsa_rope_tabs(pos_s)
    even_modes = [ROPE_RET_Q] * heads + [ROPE_RET_K] * heads + [ROPE_NONE] * (4 * heads)
    even_outs = [(0, 6 * heads * LANE, 1.0, F32, None)]
    kscale = LANE ** -0.5
    odd_cols = o_w_in.shape[2]
    odd_pad = -(-odd_cols // LANE) * LANE
    rows_last = lambda a: jnp.transpose(a, (0, 1, 3, 4, 5, 2))
    rows_first = lambda a: jnp.transpose(a.reshape(a.shape[:2] + kv_shape + (a.shape[-1],)), (0, 1, 5, 2, 3, 4))
    pool_c, pool_s, win = rows_last(cache_cmp), rows_last(cache_sel), rows_last(state_win)
    kv_shape = cache_cmp.shape[3:]
    keep = min(WINDOW, s)

    out = {k: [] for k in ("ret_p", "sgv", "cmp_s", "sel_s", "win_p", "ffn_p", "ffn_s")}
    ret_s_all = cmp_all = sel_all = win_all = None
    n_odd = o_w_in.shape[0]
    for l in range(depth):
        i = l // 2
        if l % 2 == 0:
            w_in = e_w_in[i].astype(BF16)
            w_out = e_w_out[i].astype(BF16)
            (pp,) = proj_in(xp, norm_mix_pre[l], w_in, (ret_p[0], ret_p[1], ret_p[1]), even_modes, even_outs, kscale)
            (ps,) = proj_in(xs, norm_mix_pre[l], w_in, (ret_s[0], ret_s[1], ret_s[1]), even_modes, even_outs, kscale)
            cat_p, st_p = even_seq(pp, b, s, e_sg_ln_g[i], e_sg_ln_b[i], e_sg_w[i], e_sg_b[i], heads, groups)
            cat_s, ret_s_all, svn_s = even_seq_sample(ps, state_ret, i, n, e_sg_ln_g[i], e_sg_ln_b[i], e_sg_w[i],
                                                      e_sg_b[i], heads, groups, prev=ret_s_all)
            xp = proj_out([cat_p], w_out, norm_mix_post[l], xp)
            xs = proj_out([cat_s], w_out, norm_mix_post[l], xs)
            out["ret_p"].append(st_p)
            out["sgv"].append(svn_s.reshape(db, pad, -1)[:, :n])
        else:
            w_in = jnp.pad(o_w_in[i], ((0, 0), (0, odd_pad - odd_cols))).astype(BF16)
            w_out = o_w_out[i].astype(BF16)
            wts, pes = _compress_weights(o_phi_k[i], o_phi_v[i], o_pe_k[i], o_pe_v[i])
            q_p, rc_p, cmp_all, sel_all, win_t, g_p, ka_p, vs_p, kw_p, vw_p = proj_in(
                xp, norm_mix_pre[l], w_in, nsa_p, _odd_modes(), _odd_outs(True), seq_len=s,
                layer=i, n_layers=n_odd, stacked={2: cmp_all, 3: sel_all})
            q_s, rc_s, rs_s, rw_s, g_s = proj_in(xs, norm_mix_pre[l], w_in, nsa_s, _odd_modes(), _odd_outs(False))
            kc_p = compress_prompt(rc_p, b, wts, pes)
            o_p = nsa_prompt_t(q_p, kc_p, ka_p, vs_p, kw_p, vw_p, g_p, b, s)
            kc_s = compress_pages(pool_c, i, page_table, wts, pes)
            o_s, win_all = nsa_sample(q_s, kc_s, pool_s, i, page_table, rs_s, rw_s, win, g_s, n, prev=win_all)
            xp = proj_out([o_p], w_out, norm_mix_post[l], xp)
            xs = proj_out([o_s], w_out, norm_mix_post[l], xs)
            out["win_p"].append(win_t[:, :, s - keep:])
            out["cmp_s"].append(rc_s.reshape((db, pad) + kv_shape)[:, :n])
            out["sel_s"].append(rs_s.reshape((db, pad) + kv_shape)[:, :n])
        w_in = f_w_in[l].astype(BF16)
        w_out = f_w_out[l].astype(BF16)
        xp, st_p = conv_ffn(xp, s, norm_ffn_pre[l], w_in, f_conv_w[l], f_conv_b[l], w_out, norm_ffn_post[l])
        xs, st_s = conv_ffn(xs, pad, norm_ffn_pre[l], w_in, f_conv_w[l], f_conv_b[l], w_out, norm_ffn_post[l],
                            prev=_ffn_prev(state_ffn[l]), tm=256)
        out["ffn_p"].append(st_p.reshape(b, 8, -1)[:, 8 - (CONV_W - 1):])
        out["ffn_s"].append(st_s.reshape(db, pad, -1)[:, n - (CONV_W - 1):n])

    stack = lambda k: jnp.stack(out[k])
    return (xp.reshape(b, s, d), xs.reshape(db, pad, d)[:, :n], stack("ret_p"), ret_s_all, stack("sgv"),
            rows_first(cmp_all), stack("cmp_s"), rows_first(sel_all), stack("sel_s"), rows_first(stack("win_p")),
            rows_first(win_all), stack("ffn_p"), stack("ffn_s"))
```

```python
import functools

import jax
import jax.numpy as jnp
from jax import lax
from jax.experimental import pallas as pl
from jax.experimental.pallas import tpu as pltpu

F32 = jnp.float32
BF16 = jnp.bfloat16

EPS = 1e-6
NEG = -1e30
FORCE = 1e9

LANE = 128
VMEM_LIMIT = 56 * 1024 * 1024

RET_CHUNK = 128
RET_THETA = 10000.0
SG_CHUNK = 128
HEAD_DIM = 64
NSA_KV = 4
NSA_REP = 4
CMP_LEN = 32
CMP_STRIDE = 16
SEL_BLK = 64
SEL_TOPK = 16
WINDOW = 512
ROPE_DIM = HEAD_DIM // 4
ROPE_THETA = 500000.0
ATTN_SCALE = HEAD_DIM ** -0.5
LOG2E = 1.4426950408889634
CONV_W = 3

SAMPLE_PAD = 8

ROPE_NONE, ROPE_RET_Q, ROPE_RET_K, ROPE_NSA = 0, 1, 2, 3
PAD_HEADS = -1


def _params(*sem):
    return pltpu.CompilerParams(dimension_semantics=sem, vmem_limit_bytes=VMEM_LIMIT)


def _rms(x, g):
    return x * lax.rsqrt(jnp.mean(x * x, -1, keepdims=True) + EPS) * g


def _col_chunk(n):
    for c in (512, 384, 256, 128):
        if n % c == 0:
            return c
    raise ValueError(n)


def _proj_in_body(x_ref, g_ref, w_ref, tc_ref, ta_ref, tb_ref, *o_refs, modes, kscale, outs, n_prev=0):
    o_refs = o_refs[n_prev:]
    h = _rms(x_ref[...], g_ref[...]).astype(BF16)
    n = w_ref.shape[1]
    cw = _col_chunk(n)
    for c0 in range(0, n, cw):
        y = jnp.dot(h, w_ref[:, c0:c0 + cw], preferred_element_type=F32)
        for j in range(cw // LANE):
            col = c0 + j * LANE
            blk = y[:, j * LANE:(j + 1) * LANE]
            mode = modes[col // LANE]
            if mode in (ROPE_RET_Q, ROPE_RET_K):
                blk = blk * tc_ref[...] + pltpu.roll(blk, LANE // 2, 1) * ta_ref[...]
                if mode == ROPE_RET_K:
                    blk = blk * kscale
            elif mode == ROPE_NSA:
                blk = (blk * tc_ref[...] + pltpu.roll(blk, LANE - ROPE_DIM // 2, 1) * ta_ref[...]
                       + pltpu.roll(blk, ROPE_DIM // 2, 1) * tb_ref[...])
            for o_ref, (oc, ow, osc, tw) in zip(o_refs, outs):
                if oc <= col < oc + ow:
                    v = blk if osc == 1.0 else blk * osc
                    cs = slice(col - oc, col - oc + LANE)
                    if tw is None:
                        o_ref[:, cs] = v.astype(o_ref.dtype)
                    elif tw == PAD_HEADS:
                        low = lax.broadcasted_iota(jnp.int32, v.shape, 1) < HEAD_DIM
                        c2 = 2 * (col - oc)
                        o_ref[:, c2:c2 + LANE] = jnp.where(low, v, 0.0).astype(o_ref.dtype)
                        o_ref[:, c2 + LANE:c2 + 2 * LANE] = jnp.where(low, pltpu.roll(v, HEAD_DIM, 1), 0.0).astype(o_ref.dtype)
                    elif tw == 0:
                        o_ref[cs, :] = v.T.astype(o_ref.dtype)
                    else:
                        for s in range(v.shape[0] // tw):
                            o_ref[s, cs, :] = v[s * tw:(s + 1) * tw, :].T.astype(o_ref.dtype)


def proj_in(x, g, w, tabs, modes, outs, kscale=1.0, tm=512, seq_len=None, layer=0, n_layers=1, stacked=None):
    m, d = x.shape
    n = w.shape[1]
    tm = min(tm, m)
    nt = tabs[0].shape[0] // tm
    tab_spec = pl.BlockSpec((tm, LANE), lambda i: (i % nt, 0))
    tps = (seq_len // tm) if seq_len else 1
    stacked = stacked or {}
    specs, shapes, prevs, aliases = [], [], [], {}
    for k, (_, ow, _, dt, tw) in enumerate(outs):
        if tw is None:
            specs.append(pl.BlockSpec((tm, ow), lambda i: (i, 0)))
            shapes.append(jax.ShapeDtypeStruct((m, ow), dt))
        elif tw == PAD_HEADS:
            specs.append(pl.BlockSpec((tm, 2 * ow), lambda i: (i, 0)))
            shapes.append(jax.ShapeDtypeStruct((m, 2 * ow), dt))
        elif tw == 0 and k in stacked:
            specs.append(pl.BlockSpec((None, None, ow, tm), lambda i: (layer, i // tps, 0, i % tps)))
            shapes.append(jax.ShapeDtypeStruct((n_layers, m // seq_len, ow, seq_len), dt))
            if stacked[k] is not None:
                aliases[6 + len(prevs)] = k
                prevs.append(stacked[k])
        elif tw == 0:
            specs.append(pl.BlockSpec((None, ow, tm), lambda i: (i // tps, 0, i % tps)))
            shapes.append(jax.ShapeDtypeStruct((m // seq_len, ow, seq_len), dt))
        else:
            specs.append(pl.BlockSpec((None, tm // tw, ow, tw), lambda i: (i // tps, i % tps, 0, 0)))
            shapes.append(jax.ShapeDtypeStruct((m // seq_len, seq_len // tw, ow, tw), dt))
    return pl.pallas_call(
        functools.partial(_proj_in_body, modes=tuple(modes), kscale=kscale, n_prev=len(prevs),
                          outs=tuple((o[0], o[1], o[2], o[4]) for o in outs)),
        grid=(m // tm,),
        in_specs=[pl.BlockSpec((tm, d), lambda i: (i, 0)),
                  pl.BlockSpec((1, d), lambda i: (0, 0)),
                  pl.BlockSpec((d, n), lambda i: (0, 0)),
                  tab_spec, tab_spec, tab_spec] + [pl.BlockSpec(memory_space=pl.ANY)] * len(prevs),
        out_specs=specs,
        out_shape=shapes,
        input_output_aliases=aliases,
        compiler_params=_params("parallel"),
        name="proj_in",
    )(x, g.reshape(1, d), w, *tabs, *prevs)


def _ret_rope_tabs(pos):
    inv = 1.0 / (RET_THETA ** jnp.linspace(0.0, 1.0, LANE // 2))
    ang = pos.astype(F32)[:, None] * inv[None, :]
    cos, sin = jnp.cos(ang), jnp.sin(ang)
    return jnp.concatenate([cos, cos], -1), jnp.concatenate([-sin, sin], -1)


def _nsa_rope_tabs(pos):
    hr = ROPE_DIM // 2
    inv = 1.0 / (ROPE_THETA ** (jnp.arange(0, ROPE_DIM, 2, dtype=F32) / ROPE_DIM))
    ang = pos.astype(F32)[:, None] * inv[None, :]
    cos, sin = jnp.cos(ang), jnp.sin(ang)
    n = pos.shape[0]
    rest = HEAD_DIM - ROPE_DIM
    c = jnp.concatenate([cos, cos, jnp.ones((n, rest), F32)], -1)
    a = jnp.concatenate([-sin, jnp.zeros((n, hr + rest), F32)], -1)
    b = jnp.concatenate([jnp.zeros((n, hr), F32), sin, jnp.zeros((n, rest), F32)], -1)
    rep = LANE // HEAD_DIM
    return jnp.tile(c, (1, rep)), jnp.tile(a, (1, rep)), jnp.tile(b, (1, rep))


def _ret_tabs(c, n_valid, heads):
    log_g = jnp.log(1.0 - 2.0 ** (-5.0 - jnp.arange(heads, dtype=F32)))
    idx = jnp.arange(c, dtype=F32)
    diff = idx[:, None] - idx[None, :]
    ok = (diff >= 0) & (idx[None, :] < n_valid)
    dmask = jnp.where(ok, jnp.exp(log_g[:, None, None] * jnp.maximum(diff, 0.0)), 0.0)
    qdec = jnp.exp(log_g[:, None] * (idx + 1.0))
    kdec = jnp.where(idx < n_valid, jnp.exp(log_g[:, None] * (n_valid - 1.0 - idx)), 0.0)
    cdec = jnp.exp(log_g * n_valid)
    bc = lambda t: jnp.broadcast_to(t[..., None], t.shape + (LANE,))
    dm = dmask if c == LANE else jnp.pad(dmask, ((0, 0), (0, 0), (0, LANE - c)))
    return dm, bc(qdec), bc(kdec), jnp.broadcast_to(cdec[:, None, None], (heads, 8, LANE))


def _layer_norm_rows(x, g, b):
    xc = x - jnp.mean(x, -1, keepdims=True)
    return xc * lax.rsqrt(jnp.mean(xc * xc, -1, keepdims=True) + EPS) * g + b


def _rms_unit(x):
    return x * lax.rsqrt(jnp.mean(x * x, -1, keepdims=True) + EPS)


def _even_seq_body(q_ref, k_ref, v_ref, g_ref, u_ref, sv_ref, dm_ref, qd_ref, kd_ref, cd_ref,
                   lng_ref, lnb_ref, wm_ref, sgb_ref, o_ref, st_ref, s_scr, *, tq, heads, groups):
    t = pl.program_id(1)
    c = RET_CHUNK

    @pl.when(t == 0)
    def _():
        s_scr[...] = jnp.zeros(s_scr.shape, F32)

    for ci in range(tq // c):
        rows = slice(ci * c, (ci + 1) * c)
        for h in range(heads):
            cols = slice(h * LANE, (h + 1) * LANE)
            qc, kc, vc = q_ref[rows, cols], k_ref[rows, cols], v_ref[rows, cols]
            vb = vc.astype(BF16)
            s = s_scr[h]
            inner = lax.dot_general(qc.astype(BF16), kc.astype(BF16), (((1,), (1,)), ((), ())),
                                    preferred_element_type=F32) * dm_ref[h]
            o = (jnp.dot(inner.astype(BF16), vb, preferred_element_type=F32)
                 + jnp.dot((qc * qd_ref[h]).astype(BF16), s.astype(BF16), preferred_element_type=F32))
            s_scr[h] = s * cd_ref[h][0:1, :] + lax.dot_general(
                (kc * kd_ref[h]).astype(BF16), vb, (((0,), (0,)), ((), ())), preferred_element_type=F32)
            gg = g_ref[rows, cols]
            o_ref[rows, cols] = gg * jax.nn.sigmoid(gg) * _rms_unit(o)
        for gi in range(groups):
            cols = slice(gi * LANE, (gi + 1) * LANE)
            svn = _layer_norm_rows(sv_ref[rows, cols], lng_ref[:, cols], lnb_ref[:, cols])
            mixed = jnp.dot(wm_ref[gi], svn.astype(BF16), preferred_element_type=F32) + sgb_ref[gi]
            o_ref[rows, heads * LANE + gi * LANE:heads * LANE + (gi + 1) * LANE] = u_ref[rows, cols] * mixed

    @pl.when(t == pl.num_programs(1) - 1)
    def _():
        st_ref[...] = s_scr[...]


def even_seq(p, nseq, seq_len, ln_g, ln_b, sg_w, sg_b, heads, groups, tq=512):
    m = p.shape[0]
    w = heads * LANE
    c = RET_CHUNK
    nt = seq_len // tq
    dm, qd, kd, cd = _ret_tabs(c, c, heads)
    wm = jnp.tril(sg_w[:, :c, :c]).astype(BF16)
    sgb = jnp.broadcast_to(sg_b[:, :c, None], (groups, c, LANE))
    part = lambda j: pl.BlockSpec((tq, w), lambda b, t: (b * nt + t, j))
    full = lambda a: pl.BlockSpec(a.shape, lambda b, t: (0,) * a.ndim)
    return pl.pallas_call(
        functools.partial(_even_seq_body, tq=tq, heads=heads, groups=groups),
        grid=(nseq, nt),
        in_specs=[part(j) for j in range(6)] + [full(dm), full(qd), full(kd), full(cd),
                                                pl.BlockSpec((1, w), lambda b, t: (0, 0)),
                                                pl.BlockSpec((1, w), lambda b, t: (0, 0)),
                                                full(wm), full(sgb)],
        out_specs=[pl.BlockSpec((tq, 2 * w), lambda b, t: (b * nt + t, 0)),
                   pl.BlockSpec((None, heads, LANE, LANE), lambda b, t: (b, 0, 0, 0))],
        out_shape=[jax.ShapeDtypeStruct((m, 2 * w), F32),
                   jax.ShapeDtypeStruct((nseq, heads, LANE, LANE), F32)],
        scratch_shapes=[pltpu.VMEM((heads, LANE, LANE), F32)],
        compiler_params=_params("arbitrary", "arbitrary"),
        name="even_seq",
    )(p, p, p, p, p, p, dm, qd, kd, cd, ln_g.reshape(1, w), ln_b.reshape(1, w), wm, sgb)


def _even_seq_sample_body(*refs, nb, n, heads, groups, has_prev):
    (q_ref, k_ref, v_ref, g_ref, u_ref, sv_ref, s0_ref, dm_ref, qd_ref, kd_ref, cd_ref,
     lng_ref, lnb_ref, wm_ref, sgb_ref, o_ref, st_ref, svn_ref) = refs[1:] if has_prev else refs
    np_ = SAMPLE_PAD

    def one(b, carry):
        rows = pl.ds(pl.multiple_of(b * np_, np_), np_)
        for h in range(heads):
            cols = slice(h * LANE, (h + 1) * LANE)
            q, k, v = q_ref[rows, cols], k_ref[rows, cols], v_ref[rows, cols]
            s = s0_ref[b, h]
            o = jnp.dot((q * qd_ref[h]).astype(BF16), s.astype(BF16), preferred_element_type=F32)
            dm = dm_ref[h]
            for j in range(n):
                inner = jnp.sum(q * k[j:j + 1, :], axis=-1, keepdims=True)
                o = o + (inner * dm[:, j:j + 1]) * v[j:j + 1, :]
            st_ref[b, h] = s * cd_ref[h][0:1, :] + lax.dot_general(
                (k * kd_ref[h]).astype(BF16), v.astype(BF16), (((0,), (0,)), ((), ())),
                preferred_element_type=F32)
            gg = g_ref[rows, cols]
            o_ref[rows, cols] = gg * jax.nn.sigmoid(gg) * _rms_unit(o)
        for gi in range(groups):
            cols = slice(gi * LANE, (gi + 1) * LANE)
            svn = _layer_norm_rows(sv_ref[rows, cols], lng_ref[:, cols], lnb_ref[:, cols])
            svn_ref[rows, cols] = svn
            wm = wm_ref[gi]
            mixed = sgb_ref[gi]
            for j in range(n):
                mixed = mixed + wm[:, j:j + 1] * svn[j:j + 1, :]
            o_ref[rows, heads * LANE + gi * LANE:heads * LANE + (gi + 1) * LANE] = u_ref[rows, cols] * mixed
        return carry

    lax.fori_loop(0, nb, one, 0)


def even_seq_sample(p, s0, layer, n, ln_g, ln_b, sg_w, sg_b, heads, groups, prev=None, nb=8):
    m = p.shape[0]
    db = m // SAMPLE_PAD
    w = heads * LANE
    dm, qd, kd, cd = _ret_tabs(SAMPLE_PAD, n, heads)
    wm = jnp.pad(jnp.tril(sg_w[:, :n, :n]), ((0, 0), (0, SAMPLE_PAD - n), (0, LANE - n)))
    sgb = jnp.broadcast_to(jnp.pad(sg_b[:, :n], ((0, 0), (0, SAMPLE_PAD - n)))[:, :, None], (groups, SAMPLE_PAD, LANE))
    rows = nb * SAMPLE_PAD
    part = lambda j: pl.BlockSpec((rows, w), lambda i: (i, j))
    full = lambda a: pl.BlockSpec(a.shape, lambda i: (0,) * a.ndim)
    st_spec = pl.BlockSpec((None, nb, heads, LANE, LANE), lambda i: (layer, i, 0, 0, 0))
    has_prev = prev is not None
    return pl.pallas_call(
        functools.partial(_even_seq_sample_body, nb=nb, n=n, heads=heads, groups=groups, has_prev=has_prev),
        grid=(db // nb,),
        in_specs=([pl.BlockSpec(memory_space=pl.ANY)] if has_prev else [])
        + [part(j) for j in range(6)] + [st_spec, full(dm), full(qd), full(kd), full(cd),
                                         pl.BlockSpec((1, w), lambda i: (0, 0)),
                                         pl.BlockSpec((1, w), lambda i: (0, 0)),
                                         full(wm), full(sgb)],
        out_specs=[pl.BlockSpec((rows, 2 * w), lambda i: (i, 0)), st_spec,
                   pl.BlockSpec((rows, w), lambda i: (i, 0))],
        out_shape=[jax.ShapeDtypeStruct((m, 2 * w), F32),
                   jax.ShapeDtypeStruct(s0.shape, F32),
                   jax.ShapeDtypeStruct((m, w), F32)],
        input_output_aliases={0: 1} if has_prev else {},
        compiler_params=_params("parallel"),
        name="even_seq_sample",
    )(*([prev] if has_prev else []), p, p, p, p, p, p, s0, dm, qd, kd, cd, ln_g.reshape(1, w), ln_b.reshape(1, w),
      wm, sgb)


def _proj_out_body(*refs, n_a):
    w_ref, g_ref, x_ref, o_ref = refs[n_a:]
    a = refs[0][...]
    for r in refs[1:n_a]:
        a = a + r[...]
    y = jnp.dot(a.astype(BF16), w_ref[...], preferred_element_type=F32)
    o_ref[...] = x_ref[...] + _rms(y, g_ref[...])


def proj_out(a_list, w, g, x, tm=512):
    m, d = x.shape
    k = w.shape[0]
    tm = min(tm, m)
    row = lambda i: (i, 0)
    return pl.pallas_call(
        functools.partial(_proj_out_body, n_a=len(a_list)),
        grid=(m // tm,),
        in_specs=[pl.BlockSpec((tm, k), row) for _ in a_list]
        + [pl.BlockSpec((k, d), lambda i: (0, 0)),
           pl.BlockSpec((1, d), lambda i: (0, 0)),
           pl.BlockSpec((tm, d), row)],
        out_specs=pl.BlockSpec((tm, d), row),
        out_shape=jax.ShapeDtypeStruct((m, d), F32),
        compiler_params=_params("parallel"),
        name="proj_out",
    )(*a_list, w, g.reshape(1, d), x)


FFN_HALO = 16

def _ffn_body(*refs, tm, tiles_per_seq, sample, nf_static):
    if sample:
        (x_ref, gpre_ref, wa_ref, wb_ref, cwa_ref, cwb_ref, cba_ref, cbb_ref, wo_ref, gpost_ref,
         pa_ref, pb_ref, o_ref, st_ref, h_scr, upa_scr, upb_scr, acc_scr) = refs
    else:
        (x_ref, halo_ref, gpre_ref, wa_ref, wb_ref, cwa_ref, cwb_ref, cba_ref, cbb_ref, wo_ref, gpost_ref,
         o_ref, st_ref, h_scr, upa_scr, upb_scr, acc_scr) = refs
    i = pl.program_id(0)
    f = pl.program_id(1)
    nf = pl.num_programs(1)
    hl = FFN_HALO

    @pl.when(f == 0)
    def _():
        if sample:
            h_scr[0:hl, :] = jnp.zeros((hl, h_scr.shape[1]), BF16)
        else:
            hh = _rms(halo_ref[...], gpre_ref[...])
            h_scr[0:hl, :] = jnp.where(i % tiles_per_seq == 0, 0.0, hh).astype(BF16)
        h_scr[hl:, :] = _rms(x_ref[...], gpre_ref[...]).astype(BF16)

    h = h_scr[...]
    if sample:
        t = lax.broadcasted_iota(jnp.int32, (tm, 1), 0) % SAMPLE_PAD
        m1 = t >= 1
        m2 = t >= 2

    def conv(up_scr, cw_ref, cb_ref, p_ref):
        s2 = up_scr[pl.ds(hl - 2, tm), :]
        s1 = up_scr[pl.ds(hl - 1, tm), :]
        s0 = up_scr[pl.ds(hl, tm), :]
        if sample:
            p0, p1 = p_ref[:, 0, :][:, None, :], p_ref[:, 1, :][:, None, :]
            t3 = lax.broadcasted_iota(jnp.int32, (tm // SAMPLE_PAD, SAMPLE_PAD, p0.shape[-1]), 1)
            e2 = jnp.where(t3 == 0, p0, jnp.where(t3 == 1, p1, 0.0)).reshape(tm, p0.shape[-1])
            e1 = jnp.where(t3 == 0, p1, 0.0).reshape(tm, p0.shape[-1])
            s2 = jnp.where(m2, s2, 0.0) + e2
            s1 = jnp.where(m1, s1, 0.0) + e1
        return cb_ref[...] + s2 * cw_ref[0:1, :] + s1 * cw_ref[1:2, :] + s0 * cw_ref[2:3, :]

    upa_scr[...] = jnp.dot(h, wa_ref[...], preferred_element_type=F32)
    upb_scr[...] = jnp.dot(h, wb_ref[...], preferred_element_type=F32)
    a = conv(upa_scr, cwa_ref, cba_ref, pa_ref if sample else None)
    b = conv(upb_scr, cwb_ref, cbb_ref, pb_ref if sample else None)
    act = (jax.nn.gelu(a) * b).astype(BF16)
    contrib = jnp.dot(act, wo_ref[...], preferred_element_type=F32)

    @pl.when(f == 0)
    def _():
        acc_scr[...] = contrib

    @pl.when(f > 0)
    def _():
        acc_scr[...] += contrib

    @pl.when(f == nf - 1)
    def _():
        o_ref[...] = x_ref[...] + _rms(acc_scr[...], gpost_ref[...])

    fw = upa_scr.shape[1]
    rows = st_ref.shape[0]
    last = True if sample else (i % tiles_per_seq == tiles_per_seq - 1)
    for j in range(nf_static):
        @pl.when(jnp.logical_and(f == j, last))
        def _(j=j):
            st_ref[:, j * fw:(j + 1) * fw] = upa_scr[pl.ds(hl + tm - rows, rows), :]
            st_ref[:, (nf_static + j) * fw:(nf_static + j + 1) * fw] = upb_scr[pl.ds(hl + tm - rows, rows), :]


def conv_ffn(x, seq_len, gpre, w_in, conv_w, conv_b, w_out, gpost, layer, prev=None, tm=512):
    m, d = x.shape
    ff = w_out.shape[1]
    fw = 1408 if ff % 1408 == 0 else ff
    nf = ff // fw
    sample = prev is not None
    tm = min(tm, m)
    tps = max(seq_len // tm, 1)
    row = lambda i, f: (i, 0)
    const = lambda i, f: (0, 0)
    cola = lambda i, f: (layer, 0, f)
    colb = lambda i, f: (layer, 0, nf + f)
    in_specs = [pl.BlockSpec((tm, d), row)]
    args = [x]
    if not sample:
        hb = tm // FFN_HALO
        in_specs.append(pl.BlockSpec((FFN_HALO, d), lambda i, f: (jnp.maximum(i * hb - 1, 0), 0)))
        args.append(x)
    in_specs += [pl.BlockSpec((1, d), const),
                 pl.BlockSpec((None, d, fw), cola), pl.BlockSpec((None, d, fw), colb),
                 pl.BlockSpec((None, CONV_W, fw), cola), pl.BlockSpec((None, CONV_W, fw), colb),
                 pl.BlockSpec((None, 1, fw), cola), pl.BlockSpec((None, 1, fw), colb),
                 pl.BlockSpec((None, fw, d), lambda i, f: (layer, f, 0)),
                 pl.BlockSpec((1, d), const)]
    args += [gpre.reshape(1, d), w_in, w_in, conv_w, conv_w, conv_b, conv_b, w_out, gpost.reshape(1, d)]
    if sample:
        nsq = tm // SAMPLE_PAD
        in_specs += [pl.BlockSpec((None, nsq, CONV_W - 1, fw), lambda i, f: (layer, i, 0, f)),
                     pl.BlockSpec((None, nsq, CONV_W - 1, fw), lambda i, f: (layer, i, 0, nf + f))]
        args += [prev, prev]
        st_shape = jax.ShapeDtypeStruct((m, 2 * ff), F32)
        st_spec = pl.BlockSpec((tm, 2 * ff), lambda i, f: (i, 0))
    else:
        nseq = m // seq_len
        st_shape = jax.ShapeDtypeStruct((nseq * 8, 2 * ff), F32)
        st_spec = pl.BlockSpec((8, 2 * ff), lambda i, f: (i // tps, 0))
    return pl.pallas_call(
        functools.partial(_ffn_body, tm=tm, tiles_per_seq=tps, sample=sample, nf_static=nf),
        grid=(m // tm, nf),
        in_specs=in_specs,
        out_specs=[pl.BlockSpec((tm, d), row), st_spec],
        out_shape=[jax.ShapeDtypeStruct((m, d), F32), st_shape],
        scratch_shapes=[pltpu.VMEM((tm + FFN_HALO, d), BF16),
                        pltpu.VMEM((tm + FFN_HALO, fw), F32),
                        pltpu.VMEM((tm + FFN_HALO, fw), F32),
                        pltpu.VMEM((tm, d), F32)],
        compiler_params=_params("arbitrary", "arbitrary"),
        name="conv_ffn_sample" if sample else "conv_ffn",
    )(*args)


def _odd_modes():
    return [ROPE_NSA] * 8 + [ROPE_NSA, ROPE_NSA, ROPE_NONE, ROPE_NONE] * 3 + [ROPE_NONE]


SEL_TK = 512
SUM_ROWS = 16
WIN_TK = 128


def _odd_outs(prompt):
    if not prompt:
        return [(0, 1024, ATTN_SCALE, F32, None), (1024, 512, 1.0, F32, None), (1536, 512, 1.0, F32, None),
                (2048, 512, 1.0, F32, None), (2560, 128, 1.0, F32, None)]
    return [(0, 1024, ATTN_SCALE * LOG2E, BF16, 0), (1024, 512, 1.0, F32, None), (1024, 512, 1.0, F32, 0),
            (1536, 512, 1.0, F32, 0), (2048, 512, 1.0, F32, 0), (2560, 128, 1.0, F32, 0),
            (1536, 256, 1.0, BF16, PAD_HEADS), (1792, 256, 1.0, BF16, SEL_TK),
            (2048, 256, 1.0, BF16, None), (2304, 256, 1.0, BF16, WIN_TK)]


SUBS = CMP_LEN // CMP_STRIDE
ROW_W = 2 * NSA_KV * HEAD_DIM
SUB_W = CMP_STRIDE * ROW_W


def _compress_weights(phi_k, phi_v, pe_k, pe_v):
    def one(phi, pe):
        p4 = phi.reshape(SUBS, CMP_STRIDE, HEAD_DIM, HEAD_DIM)
        w = jnp.einsum('hlde,gG->lgdhGe', p4, jnp.eye(2, dtype=phi.dtype))
        w = w.reshape(CMP_STRIDE * 2 * HEAD_DIM, SUBS * 2 * HEAD_DIM)
        pr = jnp.broadcast_to(pe.reshape(SUBS, CMP_STRIDE, 1, HEAD_DIM), (SUBS, CMP_STRIDE, 2, HEAD_DIM))
        pr = jnp.pad(pr.reshape(SUBS, -1), ((0, 16 - SUBS), (0, 0)))
        return w, pr
    wk, pk = one(phi_k, pe_k)
    wv, pv = one(phi_v, pe_v)
    return jnp.stack([wk, wv]).astype(BF16), jnp.stack([pk, pv]).astype(BF16)


def _compress_column(xj, j, w_ref, pe_ref, o_ref, ab_scr):
    nsub = o_ref.shape[0]
    half = LANE
    xe = jnp.concatenate([xj, pe_ref[j // 2]], axis=0)
    ab_scr[...] = jnp.dot(xe, w_ref[j // 2], preferred_element_type=F32)
    bias = ab_scr[nsub:nsub + 1, 0:half] + ab_scr[nsub + 1:nsub + 2, half:2 * half]
    o_ref[:, j * LANE:(j + 1) * LANE] = (
        ab_scr[0:nsub, 0:half] + ab_scr[pl.ds(1, nsub), half:2 * half] + bias).astype(o_ref.dtype)


def _compress_body(x_ref, w_ref, pe_ref, o_ref, ab_scr):
    for j in range(ROW_W // LANE):
        xj = jnp.concatenate([x_ref[:, l * ROW_W + j * LANE:l * ROW_W + (j + 1) * LANE]
                              for l in range(CMP_STRIDE)], axis=1).astype(BF16)
        _compress_column(xj, j, w_ref, pe_ref, o_ref, ab_scr)


def _compress_pages_body(pt_ref, *refs, n_x):
    x_refs = refs[:n_x]
    perm_ref, w_ref, pe_ref, o_ref, ab_scr, xs_scr = refs[n_x:]
    page = x_refs[0].shape[-1]
    sub_pp = page // CMP_STRIDE
    kvw = NSA_KV * HEAD_DIM
    for k, xr in enumerate(x_refs):
        for c in range(2):
            t = xr[c].reshape(kvw, page).astype(BF16)
            out = lax.dot_general(perm_ref[...], t, (((1,), (1,)), ((), ())), preferred_element_type=F32)
            for gp in range(kvw // LANE):
                for l in range(CMP_STRIDE):
                    xs_scr[c * (kvw // LANE) + gp, k * sub_pp:(k + 1) * sub_pp, l * LANE:(l + 1) * LANE] = (
                        out[l * sub_pp:(l + 1) * sub_pp, gp * LANE:(gp + 1) * LANE])
    for j in range(ROW_W // LANE):
        _compress_column(xs_scr[j].astype(BF16), j, w_ref, pe_ref, o_ref, ab_scr)


def compress_prompt(rows, nseq, wts, pes):
    nsub = rows.shape[0] // nseq // CMP_STRIDE
    x = rows.reshape(nseq * nsub, SUB_W)
    return pl.pallas_call(
        _compress_body,
        grid=(nseq,),
        in_specs=[pl.BlockSpec((nsub, SUB_W), lambda b: (b, 0)),
                  pl.BlockSpec(wts.shape, lambda b: (0, 0, 0)),
                  pl.BlockSpec(pes.shape, lambda b: (0, 0, 0))],
        out_specs=pl.BlockSpec((nsub, ROW_W), lambda b: (b, 0)),
        out_shape=jax.ShapeDtypeStruct((nseq * nsub, ROW_W), BF16),
        scratch_shapes=[pltpu.VMEM((nsub + 16, 2 * LANE), F32)],
        compiler_params=_params("parallel"),
        name="compress_prompt",
    )(x, wts, pes)


def compress_pages(pool, layer, page_table, wts, pes, nb=2):
    db, n_pages = page_table.shape
    page = pool.shape[-1]
    sub_pp = page // CMP_STRIDE
    nsub = nb * n_pages * sub_pp
    out_row = jnp.arange(page)
    perm = (jnp.arange(page)[None, :] == ((out_row % sub_pp) * CMP_STRIDE + out_row // sub_pp)[:, None]).astype(BF16)
    specs = [pl.BlockSpec((None, None) + pool.shape[2:],
                          functools.partial(lambda i, pt, s, k: (layer, pt[(i * nb + s) * n_pages + k], 0, 0, 0, 0),
                                            s=s, k=k))
             for s in range(nb) for k in range(n_pages)]
    grid_spec = pltpu.PrefetchScalarGridSpec(
        num_scalar_prefetch=1,
        grid=(db // nb,),
        in_specs=specs + [pl.BlockSpec(perm.shape, lambda i, pt: (0, 0)),
                          pl.BlockSpec(wts.shape, lambda i, pt: (0, 0, 0)),
                          pl.BlockSpec(pes.shape, lambda i, pt: (0, 0, 0))],
        out_specs=pl.BlockSpec((nsub, ROW_W), lambda i, pt: (i, 0)),
        scratch_shapes=[pltpu.VMEM((nsub + 16, 2 * LANE), F32),
                        pltpu.VMEM((ROW_W // LANE, nsub, CMP_STRIDE * LANE), F32)],
    )
    return pl.pallas_call(
        functools.partial(_compress_pages_body, n_x=nb * n_pages),
        grid_spec=grid_spec,
        out_shape=jax.ShapeDtypeStruct((db * n_pages * sub_pp, ROW_W), BF16),
        compiler_params=_params("parallel"),
        name="compress_pages",
    )(page_table.reshape(-1), *([pool] * (nb * n_pages)), perm, wts, pes)


def _cover(nsub, nc, n_sel):
    c_start = jnp.arange(nsub) * CMP_STRIDE
    s_start = jnp.arange(SEL_BLK) * SEL_BLK
    ok = ((c_start[:, None] < s_start[None, :] + SEL_BLK) & (c_start[:, None] + CMP_LEN > s_start[None, :])
          & (jnp.arange(nsub)[:, None] < nc) & (jnp.arange(SEL_BLK)[None, :] < n_sel))
    return ok.astype(BF16)


def _expand(n_keys, tk):
    key = jnp.arange(n_keys).reshape(n_keys // tk, 1, tk)
    return (key // SEL_BLK == jnp.arange(SEL_BLK)[None, :, None]).astype(BF16)


def _masked_softmax(s, ok):
    sm = jnp.where(ok, s, NEG)
    ex = jnp.exp(sm - jnp.max(sm, -1, keepdims=True))
    return ex / jnp.sum(ex, -1, keepdims=True)


def _importance(psum, cov_ref, tpos, n_sel):
    hi = psum.astype(BF16)
    lo = (psum - hi.astype(F32)).astype(BF16)
    imp = (jnp.dot(hi, cov_ref[...], preferred_element_type=F32)
           + jnp.dot(lo, cov_ref[...], preferred_element_type=F32))
    jl = lax.broadcasted_iota(jnp.int32, imp.shape, 1)
    cur = tpos // SEL_BLK
    forced = (jl == 0) | (jl == cur) | (jl == cur - 1)
    causal = jl * SEL_BLK <= tpos
    imp = jnp.where(forced, FORCE, jnp.where(causal, imp, -1.0))
    return jnp.where(jl < n_sel, imp, -2.0)


def _topk_masks(imps):
    out = []
    for pair in range(len(imps) // 2):
        xt = jnp.concatenate([imps[2 * pair], imps[2 * pair + 1]], axis=1).T
        halves = []
        for hh in range(2):
            xa = xt[SEL_BLK * hh:SEL_BLK * (hh + 1)]
            jrow = lax.broadcasted_iota(jnp.int32, xa.shape, 0)
            cnt = jnp.zeros(xa.shape, F32)
            for i in range(SEL_BLK):
                xi = xa[i:i + 1, :]
                tie = jnp.where(jrow > i, 1.0, 0.0)
                cnt = cnt + jnp.where(xi > xa, 1.0, jnp.where(xi == xa, tie, 0.0))
            halves.append(jnp.where(cnt < SEL_TOPK, 1.0, 0.0))
        sel2 = jnp.concatenate(halves, axis=0).T
        out += [sel2[:, 0:SEL_BLK], sel2[:, SEL_BLK:2 * SEL_BLK]]
    return out


def _topk_masks_rows(imps):
    out = []
    for x in imps:
        jl = lax.broadcasted_iota(jnp.int32, x.shape, 1)
        cnt = jnp.zeros(x.shape, F32)
        for i in range(SEL_BLK):
            xi = x[:, i:i + 1]
            tie = jnp.where(jl > i, 1.0, 0.0)
            cnt = cnt + jnp.where(xi > x, 1.0, jnp.where(xi == x, tie, 0.0))
        out.append(jnp.where(cnt < SEL_TOPK, 1.0, 0.0))
    return out


_NT = (((1,), (1,)), ((), ()))


def _nsa_prompt_body(q_ref, kc_ref, ks_ref, kw_ref, gt_ref, cov_ref, e_ref, o_ref, *, tq, nc, n_sel):
    t0 = pl.program_id(1) * tq
    tpos = t0 + lax.broadcasted_iota(jnp.int32, (tq, 1), 0)
    sig = jax.nn.sigmoid(gt_ref[...])
    nsub = kc_ref.shape[0]
    rep, hd, kvw = NSA_REP, HEAD_DIM, NSA_KV * HEAD_DIM
    mcol = lax.broadcasted_iota(jnp.int32, (tq, nsub), 1)
    valid = (mcol * CMP_STRIDE + CMP_LEN - 1 <= tpos) & (mcol < nc)
    validf = jnp.where(valid, 1.0, 0.0)

    q_st, o_cmp, imps = [], [], []
    for g in range(NSA_KV):
        qg = jnp.concatenate([q_ref[:, (g * rep + r) * hd:(g * rep + r + 1) * hd] for r in range(rep)], axis=0)
        q_st.append(qg)
        s = lax.dot_general(qg, kc_ref[:, g * hd:(g + 1) * hd], _NT, preferred_element_type=F32)
        p = _masked_softmax(s.reshape(rep, tq, nsub), valid[None]) * validf[None]
        o_cmp.append(jnp.dot(p.reshape(rep * tq, nsub).astype(BF16), kc_ref[:, kvw + g * hd:kvw + (g + 1) * hd],
                             preferred_element_type=F32))
        imps.append(_importance(jnp.sum(p, axis=0), cov_ref, tpos, n_sel))
    sels = _topk_masks(imps)

    tk = ks_ref.shape[-1]
    wt = kw_ref.shape[-1]
    kt_d = t0 // tk
    kpos_d = kt_d * tk + lax.broadcasted_iota(jnp.int32, (tq, tk), 1)
    bias_d = jnp.where(kpos_d <= tpos, 0.0, NEG)
    n_wt = (WINDOW + tq) // wt
    wt0 = jnp.maximum(t0 - WINDOW, 0) // wt
    kpos_w = wt0 * wt + lax.broadcasted_iota(jnp.int32, (tq, n_wt * wt), 1)
    bias_w = jnp.where((kpos_w <= tpos) & (kpos_w > tpos - WINDOW), 0.0, NEG)

    def biased(s, bias):
        return (s.reshape(rep, tq, s.shape[-1]) + bias[None]).reshape(s.shape)

    def lanes(x):
        return jnp.broadcast_to(x, (x.shape[0], LANE))

    krows = [slice(g * hd, (g + 1) * hd) for g in range(NSA_KV)]
    vrows = [slice(kvw + g * hd, kvw + (g + 1) * hd) for g in range(NSA_KV)]

    o_win = []
    for g in range(NSA_KV):
        k_w = jnp.concatenate([kw_ref[wt0 + i, krows[g], :] for i in range(n_wt)], axis=1)
        v_w = jnp.concatenate([kw_ref[wt0 + i, vrows[g], :] for i in range(n_wt)], axis=1)
        s = biased(jnp.dot(q_st[g], k_w, preferred_element_type=F32), bias_w)
        p = jnp.exp(s - jnp.max(s, -1, keepdims=True))
        o_win.append((lax.dot_general(p.astype(BF16), v_w, _NT, preferred_element_type=F32)
                      / jnp.sum(p, -1, keepdims=True)).reshape(rep, tq, hd))

    q_aug = []
    for g in range(NSA_KV):
        selneg = jnp.where(sels[g] > 0.5, 0.0, NEG)
        q_aug.append(jnp.concatenate(
            [jnp.concatenate([q_ref[:, (g * rep + r) * hd:(g * rep + r + 1) * hd].astype(F32), selneg], axis=1)
             for r in range(rep)], axis=0).astype(BF16))

    def scores(kt, g):
        k_aug = jnp.concatenate([ks_ref[kt, krows[g], :], e_ref[kt]], axis=0)
        return jnp.dot(q_aug[g], k_aug, preferred_element_type=F32)

    init = []
    for g in range(NSA_KV):
        s = biased(scores(kt_d, g), bias_d)
        m0 = jnp.max(s, -1, keepdims=True)
        p = jnp.exp(s - m0)
        l0 = jnp.sum(p, -1, keepdims=True)
        acc0 = lax.dot_general(p.astype(BF16), ks_ref[kt_d, vrows[g], :], _NT, preferred_element_type=F32)
        init.append((lanes(m0), lanes(l0), acc0))

    def step(kt, carry):
        out = []
        for g in range(NSA_KV):
            m_i, l_i, acc = carry[g]
            s = scores(kt, g)
            m_new = jnp.maximum(m_i, lanes(jnp.max(s, -1, keepdims=True)))
            alpha = jnp.exp(m_i - m_new)
            p = jnp.exp(s - jnp.tile(m_new, (1, tk // LANE)))
            l_new = alpha * l_i + lanes(jnp.sum(p, -1, keepdims=True))
            pv = lax.dot_general(p.astype(BF16), ks_ref[kt, vrows[g], :], _NT, preferred_element_type=F32)
            out.append((m_new, l_new, alpha[:, 0:hd] * acc + pv))
        return tuple(out)

    final = lax.fori_loop(0, kt_d, step, tuple(init))

    nh = NSA_KV * rep
    for g in range(NSA_KV):
        _, l_s, acc_s = final[g]
        o_sel = (acc_s / l_s[:, 0:hd]).reshape(rep, tq, hd)
        o_c = o_cmp[g].reshape(rep, tq, hd)
        for r in range(rep):
            h = g * rep + r
            o_ref[:, h * hd:(h + 1) * hd] = (sig[:, h:h + 1] * o_c[r] + sig[:, nh + h:nh + h + 1] * o_sel[r]
                                            + sig[:, 2 * nh + h:2 * nh + h + 1] * o_win[g][r])


def nsa_prompt(q, kcvc, ks, kw, gates, nseq, seq_len, tq=128):
    m = q.shape[0]
    nsub = kcvc.shape[0] // nseq
    tk, wt = ks.shape[-1], kw.shape[-1]
    nc = (seq_len - CMP_LEN) // CMP_STRIDE + 1
    n_sel = seq_len // SEL_BLK
    assert n_sel <= SEL_BLK and seq_len >= WINDOW + tq and tk % tq == 0 and tq % wt == 0 and WINDOW % wt == 0
    nq = seq_len // tq
    cov = _cover(nsub, nc, n_sel)
    exp = _expand(seq_len, tk)
    tile = lambda w: pl.BlockSpec((tq, w), lambda b, t: (b * nq + t, 0))
    whole = lambda a: pl.BlockSpec((None,) + a.shape[1:], lambda b, t: (b, 0, 0, 0))
    return pl.pallas_call(
        functools.partial(_nsa_prompt_body, tq=tq, nc=nc, n_sel=n_sel),
        grid=(nseq, nq),
        in_specs=[tile(q.shape[1]), pl.BlockSpec((nsub, ROW_W), lambda b, t: (b, 0)), whole(ks), whole(kw), tile(LANE),
                  pl.BlockSpec(cov.shape, lambda b, t: (0, 0)),
                  pl.BlockSpec(exp.shape, lambda b, t: (0, 0, 0))],
        out_specs=tile(q.shape[1]),
        out_shape=jax.ShapeDtypeStruct((m, q.shape[1]), F32),
        compiler_params=_params("parallel", "parallel"),
        name="nsa_prompt",
    )(q, kcvc, ks, kw, gates, cov, exp)


def _softmax_cols(s):
    p = jnp.exp2(s - jnp.max(s, axis=0, keepdims=True))
    return p, jnp.sum(p, axis=0, keepdims=True)


def _nsa_prompt_t_body(q_ref, kc_ref, ka_ref, vs_ref, kw_ref, vw_ref, gt_ref, cov_ref, et_ref, o_ref, *, tq, nc, n_sel):
    t0 = pl.program_id(1) * tq
    tl = t0 + lax.broadcasted_iota(jnp.int32, (1, tq), 1)
    sig = jax.nn.sigmoid(gt_ref[...])
    nsub = kc_ref.shape[0]
    rep, hd = NSA_REP, HEAD_DIM
    nh = NSA_KV * rep
    tk = vs_ref.shape[-1]
    wt = vw_ref.shape[-1]
    wide = lambda x: jnp.concatenate([x] * rep, axis=1)
    q4 = [jnp.concatenate([q_ref[(g * rep + r) * hd:(g * rep + r + 1) * hd, :] for r in range(rep)], axis=1)
          for g in range(NSA_KV)]

    mrow = lax.broadcasted_iota(jnp.int32, (nsub, 1), 0)
    valid = (mrow * CMP_STRIDE + CMP_LEN - 1 <= tl) & (mrow < nc)
    validf = wide(jnp.where(valid, 1.0, 0.0))
    jrow = lax.broadcasted_iota(jnp.int32, (SEL_BLK, tq), 0)
    cur = tl // SEL_BLK
    forced = (jrow == 0) | (jrow == cur) | (jrow == cur - 1)
    causal = jrow * SEL_BLK <= tl
    o_cmp, sel_neg = [], []
    for g in range(NSA_KV):
        kc = kc_ref[:, g * hd:(g + 1) * hd]
        vc_t = kc_ref[:, NSA_KV * hd + g * hd:NSA_KV * hd + (g + 1) * hd].astype(F32).T.astype(BF16)
        s = jnp.dot(kc, q4[g], preferred_element_type=F32)
        s = jnp.where(validf > 0.5, s, NEG)
        ex, l = _softmax_cols(s)
        p = ex / l * validf
        o_cmp.append(jnp.dot(vc_t, p.astype(BF16), preferred_element_type=F32))
        psum = p[:, 0:tq]
        for r in range(1, rep):
            psum = psum + p[:, r * tq:(r + 1) * tq]
        hi = psum.astype(BF16)
        lo = (psum - hi.astype(F32)).astype(BF16)
        imp = (jnp.dot(cov_ref[...], hi, preferred_element_type=F32)
               + jnp.dot(cov_ref[...], lo, preferred_element_type=F32))
        imp = jnp.where(forced, FORCE, jnp.where(causal, imp, -1.0))
        imp = jnp.where(jrow < n_sel, imp, -2.0)
        cnt = jnp.zeros(imp.shape, F32)
        for i in range(SEL_BLK):
            xi = imp[i:i + 1, :]
            tie = jnp.where(jrow > i, 1.0, 0.0)
            cnt = cnt + jnp.where(xi > imp, 1.0, jnp.where(xi == imp, tie, 0.0))
        sel_neg.append(jnp.where(cnt < SEL_TOPK, 0.0, NEG).astype(BF16))

    n_wt = (WINDOW + tq) // wt
    w0 = jnp.maximum(t0 - WINDOW, 0)
    wt0 = w0 // wt
    wrows = n_wt * wt
    kpos_w = w0 + lax.broadcasted_iota(jnp.int32, (wrows, 1), 0)
    bias_w = wide(jnp.where((kpos_w <= tl) & (kpos_w > tl - WINDOW), 0.0, NEG))
    o_win = []
    for g in range(NSA_KV):
        k_w = kw_ref[pl.ds(pl.multiple_of(w0, wt), wrows), g * hd:(g + 1) * hd]
        v_w = jnp.concatenate([vw_ref[wt0 + i, g * hd:(g + 1) * hd, :] for i in range(n_wt)], axis=1)
        p, l = _softmax_cols(jnp.dot(k_w, q4[g], preferred_element_type=F32) + bias_w)
        o_win.append(jnp.dot(v_w, p.astype(BF16), preferred_element_type=F32) / l)

    q_aug = [jnp.concatenate(
        [jnp.concatenate([q_ref[(g * rep + r) * hd:(g * rep + r + 1) * hd, :], sel_neg[g]], axis=0)
         for r in range(rep)], axis=1) for g in range(NSA_KV)]

    def keys(kt, g):
        rows = pl.ds(pl.multiple_of(kt * tk, tk), tk)
        return ka_ref[rows, g * LANE:(g + 1) * LANE] + et_ref[rows, :]

    kt_d = t0 // tk
    kpos_d = kt_d * tk + lax.broadcasted_iota(jnp.int32, (tk, 1), 0)
    bias_d = wide(jnp.where(kpos_d <= tl, 0.0, NEG))
    init = []
    for g in range(NSA_KV):
        s = jnp.dot(keys(kt_d, g), q_aug[g], preferred_element_type=F32) + bias_d
        m0 = jnp.max(s, axis=0, keepdims=True)
        p = jnp.exp(s - m0)
        init.append((m0, jnp.sum(p, axis=0, keepdims=True),
                     jnp.dot(vs_ref[kt_d, g * hd:(g + 1) * hd, :], p.astype(BF16), preferred_element_type=F32)))

    def step(kt, carry):
        ng = NSA_KV
        s, soft, out = [None] * ng, [None] * ng, [None] * ng

        def scores(g):
            s[g] = jnp.dot(keys(kt, g), q_aug[g], preferred_element_type=F32)

        def softmax(g):
            m_i, l_i, _ = carry[g]
            m_new = jnp.maximum(m_i, jnp.max(s[g], axis=0, keepdims=True))
            alpha = jnp.exp(m_i - m_new)
            p = jnp.exp(s[g] - m_new)
            soft[g] = (m_new, alpha, alpha * l_i + jnp.sum(p, axis=0, keepdims=True), p.astype(BF16))

        def values(g):
            m_new, alpha, l_new, p = soft[g]
            pv = jnp.dot(vs_ref[kt, g * hd:(g + 1) * hd, :], p, preferred_element_type=F32)
            out[g] = (m_new, l_new, alpha * carry[g][2] + pv)

        scores(0)
        for g in range(ng):
            if g + 1 < ng:
                scores(g + 1)
            softmax(g)
            if g >= 1:
                values(g - 1)
        values(ng - 1)
        return tuple(out)

    final = lax.fori_loop(0, kt_d, step, tuple(init))

    for g in range(NSA_KV):
        _, l_s, acc_s = final[g]
        o_sel = acc_s / l_s
        heads = []
        for r in range(rep):
            h = g * rep + r
            cols = slice(r * tq, (r + 1) * tq)
            heads.append(sig[h:h + 1, :] * o_cmp[g][:, cols] + sig[nh + h:nh + h + 1, :] * o_sel[:, cols]
                         + sig[2 * nh + h:2 * nh + h + 1, :] * o_win[g][:, cols])
        for pair in range(rep // 2):
            o_ref[:, (g * rep + 2 * pair) * hd:(g * rep + 2 * pair + 2) * hd] = (
                jnp.concatenate([heads[2 * pair], heads[2 * pair + 1]], axis=0).T)


def _run_skewed(tasks):
    n = len(tasks)
    tasks[0][0]()
    for i in range(n):
        if i + 1 < n:
            tasks[i + 1][0]()
        tasks[i][1]()
        if i >= 1:
            tasks[i - 1][2]()
    tasks[n - 1][2]()


def _nsa_prompt_lanes_body(q_ref, kc_ref, ka_ref, vs_ref, kw_ref, vw_ref, gt_ref, cov_ref, et_ref, o_ref, *,
                           tq, nc, n_sel):
    t0 = pl.program_id(1) * tq
    tl = t0 + lax.broadcasted_iota(jnp.int32, (1, tq), 1)
    sig = jax.nn.sigmoid(gt_ref[...])
    nsub = kc_ref.shape[0]
    rep, hd, ng = NSA_REP, HEAD_DIM, NSA_KV
    nh = ng * rep
    tk = vs_ref.shape[-1]
    wt = vw_ref.shape[-1]
    wide = lambda x: jnp.concatenate([x] * rep, axis=1)
    head_rows = lambda g, r: q_ref[(g * rep + r) * hd:(g * rep + r + 1) * hd, :]
    q4 = [jnp.concatenate([head_rows(g, r) for r in range(rep)], axis=1) for g in range(ng)]

    mrow = lax.broadcasted_iota(jnp.int32, (nsub, 1), 0)
    validf = wide(jnp.where((mrow * CMP_STRIDE + CMP_LEN - 1 <= tl) & (mrow < nc), 1.0, 0.0))
    jrow = lax.broadcasted_iota(jnp.int32, (SEL_BLK, tq), 0)
    cur = tl // SEL_BLK
    forced = (jrow == 0) | (jrow == cur) | (jrow == cur - 1)
    causal = jrow * SEL_BLK <= tl
    s_cmp, p_cmp, o_cmp, sel_neg = [None] * ng, [None] * ng, [None] * ng, [None] * ng

    def cmp_scores(g):
        s_cmp[g] = jnp.dot(kc_ref[:, g * hd:(g + 1) * hd], q4[g], preferred_element_type=F32)

    def cmp_softmax(g):
        ex, l = _softmax_cols(jnp.where(validf > 0.5, s_cmp[g], NEG))
        p = ex / l * validf
        p_cmp[g] = p.astype(BF16)
        psum = p[:, 0:tq]
        for r in range(1, rep):
            psum = psum + p[:, r * tq:(r + 1) * tq]
        hi = psum.astype(BF16)
        lo = (psum - hi.astype(F32)).astype(BF16)
        imp = (jnp.dot(cov_ref[...], hi, preferred_element_type=F32)
               + jnp.dot(cov_ref[...], lo, preferred_element_type=F32))
        imp = jnp.where(forced, FORCE, jnp.where(causal, imp, -1.0))
        imp = jnp.where(jrow < n_sel, imp, -2.0)
        sub = 8
        slabs = [imp[v * sub:(v + 1) * sub, :] for v in range(SEL_BLK // sub)]
        cnts = [jnp.zeros((sub, tq), F32) for _ in slabs]
        srow = lax.broadcasted_iota(jnp.int32, (sub, tq), 0)
        for i in range(SEL_BLK):
            xi = jnp.broadcast_to(imp[i:i + 1, :], (sub, tq))
            for v, x in enumerate(slabs):
                if v > i // sub:
                    hit = jnp.where(xi >= x, 1.0, 0.0)
                elif v < i // sub:
                    hit = jnp.where(xi > x, 1.0, 0.0)
                else:
                    tie = jnp.where(srow > i % sub, 1.0, 0.0)
                    hit = jnp.where(xi > x, 1.0, jnp.where(xi == x, tie, 0.0))
                cnts[v] = cnts[v] + hit
        cnt = jnp.concatenate(cnts, axis=0)
        sel_neg[g] = jnp.where(cnt < SEL_TOPK, 0.0, NEG).astype(BF16)

    def cmp_values(g):
        vc_t = kc_ref[:, ng * hd + g * hd:ng * hd + (g + 1) * hd].astype(F32).T.astype(BF16)
        o_cmp[g] = jnp.dot(vc_t, p_cmp[g], preferred_element_type=F32)

    n_wt = (WINDOW + tq) // wt
    w0 = jnp.maximum(t0 - WINDOW, 0)
    wt0 = w0 // wt
    wrows = n_wt * wt
    kpos_w = w0 + lax.broadcasted_iota(jnp.int32, (wrows, 1), 0)
    bias_w = wide(jnp.where((kpos_w <= tl) & (kpos_w > tl - WINDOW), 0.0, NEG))
    s_win, p_win, o_win = [None] * ng, [None] * ng, [None] * ng

    def ones_row(n):
        return jnp.where(lax.broadcasted_iota(jnp.int32, (SUM_ROWS, n), 0) == 0, 1.0, 0.0).astype(BF16)

    def win_scores(g):
        k_w = kw_ref[pl.ds(pl.multiple_of(w0, wt), wrows), g * hd:(g + 1) * hd]
        s_win[g] = jnp.dot(k_w, q4[g], preferred_element_type=F32) + bias_w

    def win_softmax(g):
        p_win[g] = jnp.exp2(s_win[g] - jnp.max(s_win[g], axis=0, keepdims=True)).astype(BF16)

    def win_values(g):
        v_w = jnp.concatenate([vw_ref[wt0 + i, g * hd:(g + 1) * hd, :] for i in range(n_wt)], axis=1)
        o = jnp.dot(jnp.concatenate([v_w, ones_row(wrows)], axis=0), p_win[g], preferred_element_type=F32)
        o_win[g] = o[0:hd] / o[hd:hd + 1]

    q_aug = [None] * ng

    def keys(kt, g):
        rows = pl.ds(pl.multiple_of(kt * tk, tk), tk)
        return ka_ref[rows, g * LANE:(g + 1) * LANE] + et_ref[rows, :]

    def sweep_tasks(kt, carry, out, bias):
        s, soft = [None] * ng, [None] * ng

        def scores(g):
            if q_aug[g] is None:
                q_aug[g] = jnp.concatenate([jnp.concatenate([head_rows(g, r), sel_neg[g]], axis=0)
                                            for r in range(rep)], axis=1)
            s[g] = jnp.dot(keys(kt, g), q_aug[g], preferred_element_type=F32)
            if bias is not None:
                s[g] = s[g] + bias

        def softmax(g):
            m_i = carry[g][0]
            m_new = jnp.maximum(m_i, jnp.max(s[g], axis=0, keepdims=True))
            soft[g] = (m_new, jnp.exp2(m_i - m_new), jnp.exp2(s[g] - m_new).astype(BF16))

        def values(g):
            m_new, alpha, p = soft[g]
            v_aug = jnp.concatenate([vs_ref[kt, g * hd:(g + 1) * hd, :], ones_tk], axis=0)
            out[g] = (m_new, alpha * carry[g][1] + jnp.dot(v_aug, p, preferred_element_type=F32))

        return [(functools.partial(scores, g), functools.partial(softmax, g), functools.partial(values, g))
                for g in range(ng)]

    kt_d = t0 // tk
    kpos_d = kt_d * tk + lax.broadcasted_iota(jnp.int32, (tk, 1), 0)
    bias_d = wide(jnp.where(kpos_d <= tl, 0.0, NEG))
    ones_tk = ones_row(tk)
    empty = (jnp.full((1, rep * tq), NEG, F32), jnp.zeros((hd + SUM_ROWS, rep * tq), F32))
    init = [None] * ng
    tasks = []
    for g in range(ng):
        tasks.append((functools.partial(cmp_scores, g), functools.partial(cmp_softmax, g),
                      functools.partial(cmp_values, g)))
        tasks.append((functools.partial(win_scores, g), functools.partial(win_softmax, g),
                      functools.partial(win_values, g)))
    _run_skewed(tasks + sweep_tasks(kt_d, [empty] * ng, init, bias_d))

    def step(kt, carry):
        out = [None] * ng
        _run_skewed(sweep_tasks(kt, carry, out, None))
        return tuple(out)

    final = lax.fori_loop(0, kt_d, step, tuple(init))

    for g in range(ng):
        acc_s = final[g][1]
        o_sel = acc_s[0:hd] / acc_s[hd:hd + 1]
        heads = []
        for r in range(rep):
            h = g * rep + r
            cols = slice(r * tq, (r + 1) * tq)
            heads.append(sig[h:h + 1, :] * o_cmp[g][:, cols] + sig[nh + h:nh + h + 1, :] * o_sel[:, cols]
                         + sig[2 * nh + h:2 * nh + h + 1, :] * o_win[g][:, cols])
        for pair in range(rep // 2):
            o_ref[:, (g * rep + 2 * pair) * hd:(g * rep + 2 * pair + 2) * hd] = (
                jnp.concatenate([heads[2 * pair], heads[2 * pair + 1]], axis=0).T)


def nsa_prompt_t(q_t, kcvc, ka, vs_t, kw, vw_t, gates_t, nseq, seq_len, tq=128):
    m = ka.shape[0]
    nsub = kcvc.shape[0] // nseq
    tk, wt = vs_t.shape[-1], vw_t.shape[-1]
    nc = (seq_len - CMP_LEN) // CMP_STRIDE + 1
    n_sel = seq_len // SEL_BLK
    assert n_sel <= SEL_BLK and seq_len >= WINDOW + tq and tk % tq == 0 and tq % wt == 0 and WINDOW % wt == 0
    nq = seq_len // tq
    width = q_t.shape[1]
    cov_t = _cover(nsub, nc, n_sel).T
    key = jnp.arange(seq_len)
    et = jnp.concatenate([jnp.zeros((seq_len, HEAD_DIM), BF16),
                          (key[:, None] // SEL_BLK == jnp.arange(SEL_BLK)[None, :]).astype(BF16)], axis=1)
    cols = lambda a: pl.BlockSpec((None, a.shape[1], tq), lambda b, t: (b, 0, t))
    seq_rows = lambda a: pl.BlockSpec((seq_len, a.shape[1]), lambda b, t: (b, 0))
    whole = lambda a: pl.BlockSpec((None,) + a.shape[1:], lambda b, t: (b, 0, 0, 0))
    return pl.pallas_call(
        functools.partial(_nsa_prompt_lanes_body, tq=tq, nc=nc, n_sel=n_sel),
        grid=(nseq, nq),
        in_specs=[cols(q_t), pl.BlockSpec((nsub, ROW_W), lambda b, t: (b, 0)), seq_rows(ka), whole(vs_t),
                  seq_rows(kw), whole(vw_t), cols(gates_t),
                  pl.BlockSpec(cov_t.shape, lambda b, t: (0, 0)),
                  pl.BlockSpec(et.shape, lambda b, t: (0, 0))],
        out_specs=pl.BlockSpec((tq, width), lambda b, t: (b * nq + t, 0)),
        out_shape=jax.ShapeDtypeStruct((m, width), F32),
        compiler_params=_params("parallel", "parallel"),
        name="nsa_prompt",
    )(q_t, kcvc, ka, vs_t, kw, vw_t, gates_t, cov_t, et)


def _joint_softmax(s_a, ok_a, s_b, ok_b):
    s_a = jnp.where(ok_a, s_a, NEG)
    s_b = jnp.where(ok_b, s_b, NEG)
    mx = jnp.maximum(jnp.max(s_a, -1, keepdims=True), jnp.max(s_b, -1, keepdims=True))
    e_a = jnp.exp(s_a - mx)
    e_b = jnp.exp(s_b - mx)
    return e_a, e_b, jnp.sum(e_a, -1, keepdims=True) + jnp.sum(e_b, -1, keepdims=True)


def _nsa_sample_body(pt_ref, q_ref, kc_ref, *rest, n, n_pages, past_len, nc, n_sel):
    page_refs = rest[:n_pages]
    ns_ref, nw_ref, win_ref, gt_ref, cov_ref, e_ref = rest[n_pages:n_pages + 6]
    o_ref, wo_ref = rest[-2:]
    np_, rep, hd, kvw, nkv = SAMPLE_PAD, NSA_REP, HEAD_DIM, NSA_KV * HEAD_DIM, NSA_KV
    tpos = past_len + lax.broadcasted_iota(jnp.int32, (np_, 1), 0)
    sig = jax.nn.sigmoid(gt_ref[...])
    q = q_ref[...]

    blocks = []
    for g in range(nkv):
        for r in range(rep):
            h = g * rep + r
            parts = [q[:, h * hd:(h + 1) * hd]]
            if g > 0:
                parts.insert(0, jnp.zeros((np_, g * hd), F32))
            if g < nkv - 1:
                parts.append(jnp.zeros((np_, (nkv - 1 - g) * hd), F32))
            blocks.append(jnp.concatenate(parts, axis=1))
    qbd = jnp.concatenate(blocks, axis=0).astype(BF16)
    nrow = nkv * rep * np_

    def grp(x):
        return x.reshape(nkv, rep, np_, x.shape[-1])

    def scores(keys):
        return grp(lax.dot_general(qbd, keys, _NT, preferred_element_type=F32))

    def scores_t(keys_t):
        return grp(jnp.dot(qbd, keys_t, preferred_element_type=F32))

    new_lane = lax.broadcasted_iota(jnp.int32, (np_, np_), 1)
    new_pos = past_len + new_lane

    nsub = kc_ref.shape[0]
    mcol = lax.broadcasted_iota(jnp.int32, (np_, nsub), 1)
    valid = (mcol * CMP_STRIDE + CMP_LEN - 1 <= tpos) & (mcol < nc)
    p = _masked_softmax(scores(kc_ref[:, 0:kvw]), valid[None, None]) * jnp.where(valid, 1.0, 0.0)[None, None]
    o_cmp = jnp.dot(p.reshape(nrow, nsub).astype(BF16), kc_ref[:, kvw:2 * kvw], preferred_element_type=F32)
    sels = _topk_masks_rows([_importance(jnp.sum(p[g], axis=0), cov_ref, tpos, n_sel) for g in range(nkv)])

    page = page_refs[0].shape[-1]
    k_all = jnp.concatenate([pr[0].reshape(kvw, page) for pr in page_refs], axis=1).astype(BF16)
    v_all = jnp.concatenate([pr[1].reshape(kvw, page) for pr in page_refs], axis=1).astype(BF16)
    kpos = lax.broadcasted_iota(jnp.int32, (np_, past_len), 1)
    ok_c = jnp.stack([jnp.where(kpos <= tpos, jnp.dot(sels[g].astype(BF16), e_ref[0], preferred_element_type=F32), 0.0)
                      for g in range(nkv)]) > 0.5
    new_ok = (new_pos <= tpos) & (new_lane < n)
    ok_n = jnp.stack([jnp.where(new_ok, jnp.concatenate(
        [sels[g][:, (past_len + i) // SEL_BLK:(past_len + i) // SEL_BLK + 1] for i in range(np_)], axis=1), 0.0)
        for g in range(nkv)]) > 0.5
    ns = ns_ref[...]
    e_c, e_n, l_s = _joint_softmax(scores_t(k_all), ok_c[:, None], scores(ns[:, 0:kvw].astype(BF16)), ok_n[:, None])
    o_sel = lax.dot_general(e_c.reshape(nrow, past_len).astype(BF16), v_all, _NT, preferred_element_type=F32)
    e_n = e_n.reshape(nrow, np_)
    for i in range(n):
        o_sel = o_sel + e_n[:, i:i + 1] * ns[i:i + 1, kvw:2 * kvw]
    o_sel = o_sel / l_s.reshape(nrow, 1)

    w_buf = win_ref.shape[-1]
    k_w = win_ref[0].reshape(kvw, w_buf)
    v_w = win_ref[1].reshape(kvw, w_buf)
    kpos_w = past_len - w_buf + lax.broadcasted_iota(jnp.int32, (np_, w_buf), 1)
    ok_w = (kpos_w <= tpos) & (kpos_w > tpos - WINDOW)
    ok_wn = (new_pos <= tpos) & (new_pos > tpos - WINDOW) & (new_lane < n)
    nw = nw_ref[...]
    e_w, e_wn, l_w = _joint_softmax(scores_t(k_w.astype(BF16)), ok_w[None, None],
                                    scores(nw[:, 0:kvw].astype(BF16)), ok_wn[None, None])
    o_win = lax.dot_general(e_w.reshape(nrow, w_buf).astype(BF16), v_w.astype(BF16), _NT,
                            preferred_element_type=F32)
    e_wn = e_wn.reshape(nrow, np_)
    for i in range(n):
        o_win = o_win + e_wn[:, i:i + 1] * nw[i:i + 1, kvw:2 * kvw]
    o_win = o_win / l_w.reshape(nrow, 1)

    nh = nkv * rep
    for g in range(nkv):
        for r in range(rep):
            h = g * rep + r
            rows = slice(h * np_, (h + 1) * np_)
            cols = slice(g * hd, (g + 1) * hd)
            o_ref[:, h * hd:(h + 1) * hd] = (sig[:, h:h + 1] * o_cmp[rows, cols]
                                            + sig[:, nh + h:nh + h + 1] * o_sel[rows, cols]
                                            + sig[:, 2 * nh + h:2 * nh + h + 1] * o_win[rows, cols])

    body = w_buf - LANE
    for c, old in enumerate((k_w, v_w)):
        shifted = pltpu.roll(old, w_buf - n, 1)
        new_t = jnp.concatenate([nw[:, c * kvw:(c + 1) * kvw].T, jnp.zeros((kvw, LANE - np_), F32)], axis=1)
        lane = lax.broadcasted_iota(jnp.int32, (kvw, LANE), 1)
        wo_ref[c * kvw:(c + 1) * kvw, 0:body] = shifted[:, 0:body]
        wo_ref[c * kvw:(c + 1) * kvw, body:w_buf] = jnp.where(lane < LANE - n, shifted[:, body:w_buf],
                                                              pltpu.roll(new_t, LANE - n, 1))


def nsa_sample(q, kcvc, pool, layer, page_table, rows_s, rows_w, win, gates, n, prev=None):
    db, n_pages = page_table.shape
    page = pool.shape[-1]
    past_len = n_pages * page
    nsub = kcvc.shape[0] // db
    w_buf = win.shape[-1]
    assert w_buf == WINDOW and n <= SAMPLE_PAD and past_len % SEL_BLK == 0
    nc = (past_len + n - CMP_LEN) // CMP_STRIDE + 1
    n_sel = -(-(past_len + n) // SEL_BLK)
    assert n_sel <= SEL_BLK and nc <= nsub
    cov = _cover(nsub, nc, n_sel)
    exp = _expand(past_len, past_len)
    rows8 = lambda w: pl.BlockSpec((SAMPLE_PAD, w), lambda i, pt: (i, 0))
    pages = [pl.BlockSpec((None, None) + pool.shape[2:],
                          functools.partial(lambda i, pt, k: (layer, pt[i * n_pages + k], 0, 0, 0, 0), k=k))
             for k in range(n_pages)]
    grid_spec = pltpu.PrefetchScalarGridSpec(
        num_scalar_prefetch=1,
        grid=(db,),
        in_specs=[rows8(q.shape[1]), pl.BlockSpec((nsub, ROW_W), lambda i, pt: (i, 0))] + pages
        + [rows8(ROW_W), rows8(ROW_W),
           pl.BlockSpec((None, None) + win.shape[2:], lambda i, pt: (layer, i, 0, 0, 0, 0)),
           rows8(LANE),
           pl.BlockSpec(cov.shape, lambda i, pt: (0, 0)),
           pl.BlockSpec(exp.shape, lambda i, pt: (0, 0, 0))]
        + ([pl.BlockSpec(memory_space=pl.ANY)] if prev is not None else []),
        out_specs=[rows8(q.shape[1]),
                   pl.BlockSpec((None, None, ROW_W, w_buf), lambda i, pt: (layer, i, 0, 0))],
    )
    args = (page_table.reshape(-1), q, kcvc, *([pool] * n_pages), rows_s, rows_w, win, gates, cov, exp)
    return pl.pallas_call(
        functools.partial(_nsa_sample_body, n=n, n_pages=n_pages, past_len=past_len, nc=nc, n_sel=n_sel),
        grid_spec=grid_spec,
        out_shape=[jax.ShapeDtypeStruct(q.shape, F32),
                   jax.ShapeDtypeStruct((win.shape[0], db, ROW_W, w_buf), F32)],
        input_output_aliases={len(args): 1} if prev is not None else {},
        compiler_params=_params("parallel"),
        name="nsa_sample",
    )(*args, *([prev] if prev is not None else []))


def kernel(x_prompt, x_sample, state_ret, cache_cmp, cache_sel, state_win, state_ffn, page_table, norm_mix_pre, norm_mix_post, norm_ffn_pre, norm_ffn_post, e_w_in, e_w_out, e_sg_ln_g, e_sg_ln_b, e_sg_w, e_sg_b, o_w_in, o_w_out, o_pe_k, o_pe_v, o_phi_k, o_phi_v, f_w_in, f_conv_w, f_conv_b, f_w_out):
    b, s, d = x_prompt.shape
    db, n, _ = x_sample.shape
    depth = norm_mix_pre.shape[0]
    heads = state_ret.shape[2]
    groups = e_sg_w.shape[1]
    n_pages = page_table.shape[1]
    page = cache_cmp.shape[2]
    past_len = n_pages * page
    pad = SAMPLE_PAD
    assert n <= pad and n >= CONV_W - 1 and n < CMP_STRIDE and state_ret.shape[3] == LANE

    xp = x_prompt.reshape(b * s, d)
    xs = jnp.pad(x_sample, ((0, 0), (0, pad - n), (0, 0))).reshape(db * pad, d)
    pos_p = jnp.arange(s, dtype=jnp.int32)
    pos_s = jnp.tile(past_len + jnp.arange(pad, dtype=jnp.int32), db)
    ret_p, ret_s = _ret_rope_tabs(pos_p), _ret_rope_tabs(pos_s)
    nsa_p, nsa_s = _nsa_rope_tabs(pos_p), _nsa_rope_tabs(pos_s)
    even_modes = [ROPE_RET_Q] * heads + [ROPE_RET_K] * heads + [ROPE_NONE] * (4 * heads)
    even_outs = [(0, 6 * heads * LANE, 1.0, F32, None)]
    kscale = LANE ** -0.5
    odd_cols = o_w_in.shape[2]
    odd_pad = -(-odd_cols // LANE) * LANE
    rows_last = lambda a: jnp.transpose(a, (0, 1, 3, 4, 5, 2))
    rows_first = lambda a: jnp.transpose(a.reshape(a.shape[:2] + kv_shape + (a.shape[-1],)), (0, 1, 5, 2, 3, 4))
    pool_c, pool_s, win = rows_last(cache_cmp), rows_last(cache_sel), rows_last(state_win)
    kv_shape = cache_cmp.shape[3:]
    keep = min(WINDOW, s)

    out = {k: [] for k in ("ret_p", "sgv", "cmp_s", "sel_s", "win_p", "ffn_p", "ffn_s")}
    ret_s_all = cmp_all = sel_all = win_all = None
    n_odd = o_w_in.shape[0]
    ffn_w_in, ffn_w_out = f_w_in.astype(BF16), f_w_out.astype(BF16)
    ffn_cb = f_conv_b[:, None, :]
    for l in range(depth):
        i = l // 2
        if l % 2 == 0:
            w_in = e_w_in[i].astype(BF16)
            w_out = e_w_out[i].astype(BF16)
            (pp,) = proj_in(xp, norm_mix_pre[l], w_in, (ret_p[0], ret_p[1], ret_p[1]), even_modes, even_outs, kscale)
            (ps,) = proj_in(xs, norm_mix_pre[l], w_in, (ret_s[0], ret_s[1], ret_s[1]), even_modes, even_outs, kscale)
            cat_p, st_p = even_seq(pp, b, s, e_sg_ln_g[i], e_sg_ln_b[i], e_sg_w[i], e_sg_b[i], heads, groups)
            cat_s, ret_s_all, svn_s = even_seq_sample(ps, state_ret, i, n, e_sg_ln_g[i], e_sg_ln_b[i], e_sg_w[i],
                                                      e_sg_b[i], heads, groups, prev=ret_s_all)
            xp = proj_out([cat_p], w_out, norm_mix_post[l], xp)
            xs = proj_out([cat_s], w_out, norm_mix_post[l], xs)
            out["ret_p"].append(st_p)
            out["sgv"].append(svn_s.reshape(db, pad, -1)[:, :n])
        else:
            w_in = jnp.pad(o_w_in[i], ((0, 0), (0, odd_pad - odd_cols))).astype(BF16)
            w_out = o_w_out[i].astype(BF16)
            wts, pes = _compress_weights(o_phi_k[i], o_phi_v[i], o_pe_k[i], o_pe_v[i])
            q_p, rc_p, cmp_all, sel_all, win_t, g_p, ka_p, vs_p, kw_p, vw_p = proj_in(
                xp, norm_mix_pre[l], w_in, nsa_p, _odd_modes(), _odd_outs(True), seq_len=s,
                layer=i, n_layers=n_odd, stacked={2: cmp_all, 3: sel_all})
            q_s, rc_s, rs_s, rw_s, g_s = proj_in(xs, norm_mix_pre[l], w_in, nsa_s, _odd_modes(), _odd_outs(False))
            kc_p = compress_prompt(rc_p, b, wts, pes)
            o_p = nsa_prompt_t(q_p, kc_p, ka_p, vs_p, kw_p, vw_p, g_p, b, s)
            kc_s = compress_pages(pool_c, i, page_table, wts, pes)
            o_s, win_all = nsa_sample(q_s, kc_s, pool_s, i, page_table, rs_s, rw_s, win, g_s, n, prev=win_all)
            xp = proj_out([o_p], w_out, norm_mix_post[l], xp)
            xs = proj_out([o_s], w_out, norm_mix_post[l], xs)
            out["win_p"].append(win_t[:, :, s - keep:])
            out["cmp_s"].append(rc_s.reshape((db, pad) + kv_shape)[:, :n])
            out["sel_s"].append(rs_s.reshape((db, pad) + kv_shape)[:, :n])
        xp, st_p = conv_ffn(xp, s, norm_ffn_pre[l], ffn_w_in, f_conv_w, ffn_cb, ffn_w_out, norm_ffn_post[l], l)
        xs, st_s = conv_ffn(xs, pad, norm_ffn_pre[l], ffn_w_in, f_conv_w, ffn_cb, ffn_w_out, norm_ffn_post[l], l,
                            prev=state_ffn, tm=256)
        out["ffn_p"].append(st_p.reshape(b, 8, -1)[:, 8 - (CONV_W - 1):])
        out["ffn_s"].append(st_s.reshape(db, pad, -1)[:, n - (CONV_W - 1):n])

    stack = lambda k: jnp.stack(out[k])
    return (xp.reshape(b, s, d), xs.reshape(db, pad, d)[:, :n], stack("ret_p"), ret_s_all, stack("sgv"),
            rows_first(cmp_all), stack("cmp_s"), rows_first(sel_all), stack("sel_s"), rows_first(stack("win_p")),
            rows_first(win_all), stack("ffn_p"), stack("ffn_s"))
```

```python
import functools

import jax
import jax.numpy as jnp
from jax import lax
from jax.experimental import pallas as pl
from jax.experimental.pallas import tpu as pltpu

F32 = jnp.float32
BF16 = jnp.bfloat16

EPS = 1e-6
NEG = -1e30
FORCE = 1e9

LANE = 128
VMEM_LIMIT = 56 * 1024 * 1024

RET_CHUNK = 128
RET_THETA = 10000.0
SG_CHUNK = 128
HEAD_DIM = 64
NSA_KV = 4
NSA_REP = 4
CMP_LEN = 32
CMP_STRIDE = 16
SEL_BLK = 64
SEL_TOPK = 16
WINDOW = 512
ROPE_DIM = HEAD_DIM // 4
ROPE_THETA = 500000.0
ATTN_SCALE = HEAD_DIM ** -0.5
LOG2E = 1.4426950408889634
CONV_W = 3

SAMPLE_PAD = 8

ROPE_NONE, ROPE_RET_Q, ROPE_RET_K, ROPE_NSA = 0, 1, 2, 3
PAD_HEADS = -1


def _params(*sem):
    return pltpu.CompilerParams(dimension_semantics=sem, vmem_limit_bytes=VMEM_LIMIT)


def _rms(x, g):
    return x * lax.rsqrt(jnp.mean(x * x, -1, keepdims=True) + EPS) * g


def _col_chunk(n):
    for c in (512, 384, 256, 128):
        if n % c == 0:
            return c
    raise ValueError(n)


def _proj_in_body(x_ref, g_ref, w_ref, tc_ref, ta_ref, tb_ref, *o_refs, modes, kscale, outs, n_prev=0):
    o_refs = o_refs[n_prev:]
    h = _rms(x_ref[...], g_ref[...]).astype(BF16)
    n = w_ref.shape[1]
    cw = _col_chunk(n)
    for c0 in range(0, n, cw):
        y = jnp.dot(h, w_ref[:, c0:c0 + cw], preferred_element_type=F32)
        for j in range(cw // LANE):
            col = c0 + j * LANE
            blk = y[:, j * LANE:(j + 1) * LANE]
            mode = modes[col // LANE]
            if mode in (ROPE_RET_Q, ROPE_RET_K):
                blk = blk * tc_ref[...] + pltpu.roll(blk, LANE // 2, 1) * ta_ref[...]
                if mode == ROPE_RET_K:
                    blk = blk * kscale
            elif mode == ROPE_NSA:
                blk = (blk * tc_ref[...] + pltpu.roll(blk, LANE - ROPE_DIM // 2, 1) * ta_ref[...]
                       + pltpu.roll(blk, ROPE_DIM // 2, 1) * tb_ref[...])
            for o_ref, (oc, ow, osc, tw, fill) in zip(o_refs, outs):
                if oc <= col < oc + ow:
                    v = blk if osc == 1.0 else blk * osc
                    cs = slice(col - oc, col - oc + LANE)
                    if tw is None:
                        o_ref[:, cs] = v.astype(o_ref.dtype)
                    elif tw == PAD_HEADS:
                        low = lax.broadcasted_iota(jnp.int32, v.shape, 1) < HEAD_DIM
                        c2 = 2 * (col - oc)
                        o_ref[:, c2:c2 + LANE] = jnp.where(low, v, 0.0).astype(o_ref.dtype)
                        o_ref[:, c2 + LANE:c2 + 2 * LANE] = jnp.where(low, pltpu.roll(v, HEAD_DIM, 1), 0.0).astype(o_ref.dtype)
                    elif tw == 0 and fill is not None:
                        for slot in range(fill[1]):
                            o_ref[slot, cs, :] = (v.T if slot == fill[0] else jnp.zeros(v.T.shape, F32)).astype(o_ref.dtype)
                    elif tw == 0:
                        o_ref[cs, :] = v.T.astype(o_ref.dtype)
                    else:
                        for s in range(v.shape[0] // tw):
                            o_ref[s, cs, :] = v[s * tw:(s + 1) * tw, :].T.astype(o_ref.dtype)


def proj_in(x, g, w, tabs, modes, outs, kscale=1.0, tm=512, seq_len=None, layer=0, n_layers=1, stacked=None):
    m, d = x.shape
    n = w.shape[1]
    tm = min(tm, m)
    nt = tabs[0].shape[0] // tm
    tab_spec = pl.BlockSpec((tm, LANE), lambda i: (i % nt, 0))
    tps = (seq_len // tm) if seq_len else 1
    stacked = stacked or {}
    specs, shapes, prevs, aliases, fills = [], [], [], {}, []
    for k, (_, ow, _, dt, tw) in enumerate(outs):
        fills.append(None)
        if tw is None:
            specs.append(pl.BlockSpec((tm, ow), lambda i: (i, 0)))
            shapes.append(jax.ShapeDtypeStruct((m, ow), dt))
        elif tw == PAD_HEADS:
            specs.append(pl.BlockSpec((tm, 2 * ow), lambda i: (i, 0)))
            shapes.append(jax.ShapeDtypeStruct((m, 2 * ow), dt))
        elif tw == 0 and k in stacked:
            shapes.append(jax.ShapeDtypeStruct((n_layers, m // seq_len, ow, seq_len), dt))
            if stacked[k] is None:
                specs.append(pl.BlockSpec((n_layers, None, ow, tm), lambda i: (0, i // tps, 0, i % tps)))
                fills[-1] = (layer, n_layers)
            else:
                specs.append(pl.BlockSpec((None, None, ow, tm), lambda i: (layer, i // tps, 0, i % tps)))
                aliases[6 + len(prevs)] = k
                prevs.append(stacked[k])
        elif tw == 0:
            specs.append(pl.BlockSpec((None, ow, tm), lambda i: (i // tps, 0, i % tps)))
            shapes.append(jax.ShapeDtypeStruct((m // seq_len, ow, seq_len), dt))
        else:
            specs.append(pl.BlockSpec((None, tm // tw, ow, tw), lambda i: (i // tps, i % tps, 0, 0)))
            shapes.append(jax.ShapeDtypeStruct((m // seq_len, seq_len // tw, ow, tw), dt))
    return pl.pallas_call(
        functools.partial(_proj_in_body, modes=tuple(modes), kscale=kscale, n_prev=len(prevs),
                          outs=tuple((o[0], o[1], o[2], o[4], fl) for o, fl in zip(outs, fills))),
        grid=(m // tm,),
        in_specs=[pl.BlockSpec((tm, d), lambda i: (i, 0)),
                  pl.BlockSpec((1, d), lambda i: (0, 0)),
                  pl.BlockSpec((d, n), lambda i: (0, 0)),
                  tab_spec, tab_spec, tab_spec] + [pl.BlockSpec(memory_space=pl.ANY)] * len(prevs),
        out_specs=specs,
        out_shape=shapes,
        input_output_aliases=aliases,
        compiler_params=_params("parallel"),
        name="proj_in",
    )(x, g.reshape(1, d), w, *tabs, *prevs)


def _ret_rope_tabs(pos):
    inv = 1.0 / (RET_THETA ** jnp.linspace(0.0, 1.0, LANE // 2))
    ang = pos.astype(F32)[:, None] * inv[None, :]
    cos, sin = jnp.cos(ang), jnp.sin(ang)
    return jnp.concatenate([cos, cos], -1), jnp.concatenate([-sin, sin], -1)


def _nsa_rope_tabs(pos):
    hr = ROPE_DIM // 2
    inv = 1.0 / (ROPE_THETA ** (jnp.arange(0, ROPE_DIM, 2, dtype=F32) / ROPE_DIM))
    ang = pos.astype(F32)[:, None] * inv[None, :]
    cos, sin = jnp.cos(ang), jnp.sin(ang)
    n = pos.shape[0]
    rest = HEAD_DIM - ROPE_DIM
    c = jnp.concatenate([cos, cos, jnp.ones((n, rest), F32)], -1)
    a = jnp.concatenate([-sin, jnp.zeros((n, hr + rest), F32)], -1)
    b = jnp.concatenate([jnp.zeros((n, hr), F32), sin, jnp.zeros((n, rest), F32)], -1)
    rep = LANE // HEAD_DIM
    return jnp.tile(c, (1, rep)), jnp.tile(a, (1, rep)), jnp.tile(b, (1, rep))


def _ret_tabs(c, n_valid, heads):
    log_g = jnp.log(1.0 - 2.0 ** (-5.0 - jnp.arange(heads, dtype=F32)))
    idx = jnp.arange(c, dtype=F32)
    diff = idx[:, None] - idx[None, :]
    ok = (diff >= 0) & (idx[None, :] < n_valid)
    dmask = jnp.where(ok, jnp.exp(log_g[:, None, None] * jnp.maximum(diff, 0.0)), 0.0)
    qdec = jnp.exp(log_g[:, None] * (idx + 1.0))
    kdec = jnp.where(idx < n_valid, jnp.exp(log_g[:, None] * (n_valid - 1.0 - idx)), 0.0)
    cdec = jnp.exp(log_g * n_valid)
    bc = lambda t: jnp.broadcast_to(t[..., None], t.shape + (LANE,))
    dm = dmask if c == LANE else jnp.pad(dmask, ((0, 0), (0, 0), (0, LANE - c)))
    return dm, bc(qdec), bc(kdec), jnp.broadcast_to(cdec[:, None, None], (heads, 8, LANE))


def _layer_norm_rows(x, g, b):
    xc = x - jnp.mean(x, -1, keepdims=True)
    return xc * lax.rsqrt(jnp.mean(xc * xc, -1, keepdims=True) + EPS) * g + b


def _rms_unit(x):
    return x * lax.rsqrt(jnp.mean(x * x, -1, keepdims=True) + EPS)


def _even_seq_body(q_ref, k_ref, v_ref, g_ref, u_ref, sv_ref, dm_ref, qd_ref, kd_ref, cd_ref,
                   lng_ref, lnb_ref, wm_ref, sgb_ref, o_ref, st_ref, s_scr, *, tq, heads, groups):
    t = pl.program_id(1)
    c = RET_CHUNK

    @pl.when(t == 0)
    def _():
        s_scr[...] = jnp.zeros(s_scr.shape, F32)

    for ci in range(tq // c):
        rows = slice(ci * c, (ci + 1) * c)
        for h in range(heads):
            cols = slice(h * LANE, (h + 1) * LANE)
            qc, kc, vc = q_ref[rows, cols], k_ref[rows, cols], v_ref[rows, cols]
            vb = vc.astype(BF16)
            s = s_scr[h]
            inner = lax.dot_general(qc.astype(BF16), kc.astype(BF16), (((1,), (1,)), ((), ())),
                                    preferred_element_type=F32) * dm_ref[h]
            o = (jnp.dot(inner.astype(BF16), vb, preferred_element_type=F32)
                 + jnp.dot((qc * qd_ref[h]).astype(BF16), s.astype(BF16), preferred_element_type=F32))
            s_scr[h] = s * cd_ref[h][0:1, :] + lax.dot_general(
                (kc * kd_ref[h]).astype(BF16), vb, (((0,), (0,)), ((), ())), preferred_element_type=F32)
            gg = g_ref[rows, cols]
            o_ref[rows, cols] = gg * jax.nn.sigmoid(gg) * _rms_unit(o)
        for gi in range(groups):
            cols = slice(gi * LANE, (gi + 1) * LANE)
            svn = _layer_norm_rows(sv_ref[rows, cols], lng_ref[:, cols], lnb_ref[:, cols])
            mixed = jnp.dot(wm_ref[gi], svn.astype(BF16), preferred_element_type=F32) + sgb_ref[gi]
            o_ref[rows, heads * LANE + gi * LANE:heads * LANE + (gi + 1) * LANE] = u_ref[rows, cols] * mixed

    @pl.when(t == pl.num_programs(1) - 1)
    def _():
        st_ref[...] = s_scr[...]


def even_seq(p, nseq, seq_len, ln_g, ln_b, sg_w, sg_b, heads, groups, tq=512):
    m = p.shape[0]
    w = heads * LANE
    c = RET_CHUNK
    nt = seq_len // tq
    dm, qd, kd, cd = _ret_tabs(c, c, heads)
    wm = jnp.tril(sg_w[:, :c, :c]).astype(BF16)
    sgb = jnp.broadcast_to(sg_b[:, :c, None], (groups, c, LANE))
    part = lambda j: pl.BlockSpec((tq, w), lambda b, t: (b * nt + t, j))
    full = lambda a: pl.BlockSpec(a.shape, lambda b, t: (0,) * a.ndim)
    return pl.pallas_call(
        functools.partial(_even_seq_body, tq=tq, heads=heads, groups=groups),
        grid=(nseq, nt),
        in_specs=[part(j) for j in range(6)] + [full(dm), full(qd), full(kd), full(cd),
                                                pl.BlockSpec((1, w), lambda b, t: (0, 0)),
                                                pl.BlockSpec((1, w), lambda b, t: (0, 0)),
                                                full(wm), full(sgb)],
        out_specs=[pl.BlockSpec((tq, 2 * w), lambda b, t: (b * nt + t, 0)),
                   pl.BlockSpec((None, heads, LANE, LANE), lambda b, t: (b, 0, 0, 0))],
        out_shape=[jax.ShapeDtypeStruct((m, 2 * w), F32),
                   jax.ShapeDtypeStruct((nseq, heads, LANE, LANE), F32)],
        scratch_shapes=[pltpu.VMEM((heads, LANE, LANE), F32)],
        compiler_params=_params("arbitrary", "arbitrary"),
        name="even_seq",
    )(p, p, p, p, p, p, dm, qd, kd, cd, ln_g.reshape(1, w), ln_b.reshape(1, w), wm, sgb)


def _even_seq_sample_body(*refs, nb, n, heads, groups, has_prev, fill):
    (q_ref, k_ref, v_ref, g_ref, u_ref, sv_ref, s0_ref, dm_ref, qd_ref, kd_ref, cd_ref,
     lng_ref, lnb_ref, wm_ref, sgb_ref, o_ref, st_all_ref, svn_ref) = refs[1:] if has_prev else refs
    np_ = SAMPLE_PAD
    if fill is None:
        st_ref = st_all_ref
    else:
        st_ref = st_all_ref.at[fill[0]]
        for slot in range(fill[1]):
            if slot != fill[0]:
                st_all_ref[slot] = jnp.zeros(st_all_ref.shape[1:], F32)

    def one(b, carry):
        rows = pl.ds(pl.multiple_of(b * np_, np_), np_)
        for h in range(heads):
            cols = slice(h * LANE, (h + 1) * LANE)
            q, k, v = q_ref[rows, cols], k_ref[rows, cols], v_ref[rows, cols]
            s = s0_ref[b, h]
            o = jnp.dot((q * qd_ref[h]).astype(BF16), s.astype(BF16), preferred_element_type=F32)
            dm = dm_ref[h]
            for j in range(n):
                inner = jnp.sum(q * k[j:j + 1, :], axis=-1, keepdims=True)
                o = o + (inner * dm[:, j:j + 1]) * v[j:j + 1, :]
            st_ref[b, h] = s * cd_ref[h][0:1, :] + lax.dot_general(
                (k * kd_ref[h]).astype(BF16), v.astype(BF16), (((0,), (0,)), ((), ())),
                preferred_element_type=F32)
            gg = g_ref[rows, cols]
            o_ref[rows, cols] = gg * jax.nn.sigmoid(gg) * _rms_unit(o)
        for gi in range(groups):
            cols = slice(gi * LANE, (gi + 1) * LANE)
            svn = _layer_norm_rows(sv_ref[rows, cols], lng_ref[:, cols], lnb_ref[:, cols])
            svn_ref[rows, cols] = svn
            wm = wm_ref[gi]
            mixed = sgb_ref[gi]
            for j in range(n):
                mixed = mixed + wm[:, j:j + 1] * svn[j:j + 1, :]
            o_ref[rows, heads * LANE + gi * LANE:heads * LANE + (gi + 1) * LANE] = u_ref[rows, cols] * mixed
        return carry

    lax.fori_loop(0, nb, one, 0)


def even_seq_sample(p, s0, layer, n, ln_g, ln_b, sg_w, sg_b, heads, groups, prev=None, nb=8):
    m = p.shape[0]
    db = m // SAMPLE_PAD
    w = heads * LANE
    dm, qd, kd, cd = _ret_tabs(SAMPLE_PAD, n, heads)
    wm = jnp.pad(jnp.tril(sg_w[:, :n, :n]), ((0, 0), (0, SAMPLE_PAD - n), (0, LANE - n)))
    sgb = jnp.broadcast_to(jnp.pad(sg_b[:, :n], ((0, 0), (0, SAMPLE_PAD - n)))[:, :, None], (groups, SAMPLE_PAD, LANE))
    rows = nb * SAMPLE_PAD
    part = lambda j: pl.BlockSpec((rows, w), lambda i: (i, j))
    full = lambda a: pl.BlockSpec(a.shape, lambda i: (0,) * a.ndim)
    st_spec = pl.BlockSpec((None, nb, heads, LANE, LANE), lambda i: (layer, i, 0, 0, 0))
    has_prev = prev is not None
    n_layers = s0.shape[0]
    fill = None if has_prev else (layer, n_layers)
    st_out = st_spec if has_prev else pl.BlockSpec((n_layers, nb, heads, LANE, LANE), lambda i: (0, i, 0, 0, 0))
    return pl.pallas_call(
        functools.partial(_even_seq_sample_body, nb=nb, n=n, heads=heads, groups=groups, has_prev=has_prev,
                          fill=fill),
        grid=(db // nb,),
        in_specs=([pl.BlockSpec(memory_space=pl.ANY)] if has_prev else [])
        + [part(j) for j in range(6)] + [st_spec, full(dm), full(qd), full(kd), full(cd),
                                         pl.BlockSpec((1, w), lambda i: (0, 0)),
                                         pl.BlockSpec((1, w), lambda i: (0, 0)),
                                         full(wm), full(sgb)],
        out_specs=[pl.BlockSpec((rows, 2 * w), lambda i: (i, 0)), st_out,
                   pl.BlockSpec((rows, w), lambda i: (i, 0))],
        out_shape=[jax.ShapeDtypeStruct((m, 2 * w), F32),
                   jax.ShapeDtypeStruct(s0.shape, F32),
                   jax.ShapeDtypeStruct((m, w), F32)],
        input_output_aliases={0: 1} if has_prev else {},
        compiler_params=_params("parallel"),
        name="even_seq_sample",
    )(*([prev] if has_prev else []), p, p, p, p, p, p, s0, dm, qd, kd, cd, ln_g.reshape(1, w), ln_b.reshape(1, w),
      wm, sgb)


def _proj_out_body(*refs, n_a):
    w_ref, g_ref, x_ref, o_ref = refs[n_a:]
    a = refs[0][...]
    for r in refs[1:n_a]:
        a = a + r[...]
    y = jnp.dot(a.astype(BF16), w_ref[...], preferred_element_type=F32)
    o_ref[...] = x_ref[...] + _rms(y, g_ref[...])


def proj_out(a_list, w, g, x, tm=512):
    m, d = x.shape
    k = w.shape[0]
    tm = min(tm, m)
    row = lambda i: (i, 0)
    return pl.pallas_call(
        functools.partial(_proj_out_body, n_a=len(a_list)),
        grid=(m // tm,),
        in_specs=[pl.BlockSpec((tm, k), row) for _ in a_list]
        + [pl.BlockSpec((k, d), lambda i: (0, 0)),
           pl.BlockSpec((1, d), lambda i: (0, 0)),
           pl.BlockSpec((tm, d), row)],
        out_specs=pl.BlockSpec((tm, d), row),
        out_shape=jax.ShapeDtypeStruct((m, d), F32),
        compiler_params=_params("parallel"),
        name="proj_out",
    )(*a_list, w, g.reshape(1, d), x)


FFN_HALO = 16

def _ffn_body(*refs, tm, tiles_per_seq, sample, nf_static):
    if sample:
        (x_ref, gpre_ref, wa_ref, wb_ref, cwa_ref, cwb_ref, cba_ref, cbb_ref, wo_ref, gpost_ref,
         pa_ref, pb_ref, o_ref, st_ref, h_scr, upa_scr, upb_scr, acc_scr) = refs
    else:
        (x_ref, halo_ref, gpre_ref, wa_ref, wb_ref, cwa_ref, cwb_ref, cba_ref, cbb_ref, wo_ref, gpost_ref,
         o_ref, st_ref, h_scr, upa_scr, upb_scr, acc_scr) = refs
    i = pl.program_id(0)
    f = pl.program_id(1)
    nf = pl.num_programs(1)
    hl = FFN_HALO

    @pl.when(f == 0)
    def _():
        if sample:
            h_scr[0:hl, :] = jnp.zeros((hl, h_scr.shape[1]), BF16)
        else:
            hh = _rms(halo_ref[...], gpre_ref[...])
            h_scr[0:hl, :] = jnp.where(i % tiles_per_seq == 0, 0.0, hh).astype(BF16)
        h_scr[hl:, :] = _rms(x_ref[...], gpre_ref[...]).astype(BF16)

    h = h_scr[...]
    if sample:
        t = lax.broadcasted_iota(jnp.int32, (tm, 1), 0) % SAMPLE_PAD
        m1 = t >= 1
        m2 = t >= 2

    def conv(up_scr, cw_ref, cb_ref, p_ref):
        s2 = up_scr[pl.ds(hl - 2, tm), :]
        s1 = up_scr[pl.ds(hl - 1, tm), :]
        s0 = up_scr[pl.ds(hl, tm), :]
        if sample:
            p0, p1 = p_ref[:, 0, :][:, None, :], p_ref[:, 1, :][:, None, :]
            t3 = lax.broadcasted_iota(jnp.int32, (tm // SAMPLE_PAD, SAMPLE_PAD, p0.shape[-1]), 1)
            e2 = jnp.where(t3 == 0, p0, jnp.where(t3 == 1, p1, 0.0)).reshape(tm, p0.shape[-1])
            e1 = jnp.where(t3 == 0, p1, 0.0).reshape(tm, p0.shape[-1])
            s2 = jnp.where(m2, s2, 0.0) + e2
            s1 = jnp.where(m1, s1, 0.0) + e1
        return cb_ref[...] + s2 * cw_ref[0:1, :] + s1 * cw_ref[1:2, :] + s0 * cw_ref[2:3, :]

    upa_scr[...] = jnp.dot(h, wa_ref[...], preferred_element_type=F32)
    upb_scr[...] = jnp.dot(h, wb_ref[...], preferred_element_type=F32)
    a = conv(upa_scr, cwa_ref, cba_ref, pa_ref if sample else None)
    b = conv(upb_scr, cwb_ref, cbb_ref, pb_ref if sample else None)
    act = (jax.nn.gelu(a) * b).astype(BF16)
    contrib = jnp.dot(act, wo_ref[...], preferred_element_type=F32)

    @pl.when(f == 0)
    def _():
        acc_scr[...] = contrib

    @pl.when(f > 0)
    def _():
        acc_scr[...] += contrib

    @pl.when(f == nf - 1)
    def _():
        o_ref[...] = x_ref[...] + _rms(acc_scr[...], gpost_ref[...])

    fw = upa_scr.shape[1]
    rows = st_ref.shape[0]
    last = True if sample else (i % tiles_per_seq == tiles_per_seq - 1)
    for j in range(nf_static):
        @pl.when(jnp.logical_and(f == j, last))
        def _(j=j):
            st_ref[:, j * fw:(j + 1) * fw] = upa_scr[pl.ds(hl + tm - rows, rows), :]
            st_ref[:, (nf_static + j) * fw:(nf_static + j + 1) * fw] = upb_scr[pl.ds(hl + tm - rows, rows), :]


def conv_ffn(x, seq_len, gpre, w_in, conv_w, conv_b, w_out, gpost, layer, prev=None, tm=512):
    m, d = x.shape
    ff = w_out.shape[1]
    fw = 1408 if ff % 1408 == 0 else ff
    nf = ff // fw
    sample = prev is not None
    tm = min(tm, m)
    tps = max(seq_len // tm, 1)
    row = lambda i, f: (i, 0)
    const = lambda i, f: (0, 0)
    cola = lambda i, f: (layer, 0, f)
    colb = lambda i, f: (layer, 0, nf + f)
    in_specs = [pl.BlockSpec((tm, d), row)]
    args = [x]
    if not sample:
        hb = tm // FFN_HALO
        in_specs.append(pl.BlockSpec((FFN_HALO, d), lambda i, f: (jnp.maximum(i * hb - 1, 0), 0)))
        args.append(x)
    in_specs += [pl.BlockSpec((1, d), const),
                 pl.BlockSpec((None, d, fw), cola), pl.BlockSpec((None, d, fw), colb),
                 pl.BlockSpec((None, CONV_W, fw), cola), pl.BlockSpec((None, CONV_W, fw), colb),
                 pl.BlockSpec((None, 1, fw), cola), pl.BlockSpec((None, 1, fw), colb),
                 pl.BlockSpec((None, fw, d), lambda i, f: (layer, f, 0)),
                 pl.BlockSpec((1, d), const)]
    args += [gpre.reshape(1, d), w_in, w_in, conv_w, conv_w, conv_b, conv_b, w_out, gpost.reshape(1, d)]
    if sample:
        nsq = tm // SAMPLE_PAD
        in_specs += [pl.BlockSpec((None, nsq, CONV_W - 1, fw), lambda i, f: (layer, i, 0, f)),
                     pl.BlockSpec((None, nsq, CONV_W - 1, fw), lambda i, f: (layer, i, 0, nf + f))]
        args += [prev, prev]
        st_shape = jax.ShapeDtypeStruct((m, 2 * ff), F32)
        st_spec = pl.BlockSpec((tm, 2 * ff), lambda i, f: (i, 0))
    else:
        nseq = m // seq_len
        st_shape = jax.ShapeDtypeStruct((nseq * 8, 2 * ff), F32)
        st_spec = pl.BlockSpec((8, 2 * ff), lambda i, f: (i // tps, 0))
    return pl.pallas_call(
        functools.partial(_ffn_body, tm=tm, tiles_per_seq=tps, sample=sample, nf_static=nf),
        grid=(m // tm, nf),
        in_specs=in_specs,
        out_specs=[pl.BlockSpec((tm, d), row), st_spec],
        out_shape=[jax.ShapeDtypeStruct((m, d), F32), st_shape],
        scratch_shapes=[pltpu.VMEM((tm + FFN_HALO, d), BF16),
                        pltpu.VMEM((tm + FFN_HALO, fw), F32),
                        pltpu.VMEM((tm + FFN_HALO, fw), F32),
                        pltpu.VMEM((tm, d), F32)],
        compiler_params=_params("arbitrary", "arbitrary"),
        name="conv_ffn_sample" if sample else "conv_ffn",
    )(*args)


def _odd_modes():
    return [ROPE_NSA] * 8 + [ROPE_NSA, ROPE_NSA, ROPE_NONE, ROPE_NONE] * 3 + [ROPE_NONE]


SEL_TK = 512
SUM_ROWS = 16
WIN_TK = 128


def _odd_outs(prompt):
    if not prompt:
        return [(0, 1024, ATTN_SCALE, F32, None), (1024, 512, 1.0, F32, None), (1536, 512, 1.0, F32, None),
                (2048, 512, 1.0, F32, None), (2560, 128, 1.0, F32, None)]
    return [(0, 1024, ATTN_SCALE * LOG2E, BF16, 0), (1024, 512, 1.0, F32, None), (1024, 512, 1.0, F32, 0),
            (1536, 512, 1.0, F32, 0), (2048, 512, 1.0, F32, 0), (2560, 128, 1.0, F32, 0),
            (1536, 256, 1.0, BF16, PAD_HEADS), (1792, 256, 1.0, BF16, SEL_TK),
            (2048, 256, 1.0, BF16, None), (2304, 256, 1.0, BF16, WIN_TK)]


SUBS = CMP_LEN // CMP_STRIDE
ROW_W = 2 * NSA_KV * HEAD_DIM
SUB_W = CMP_STRIDE * ROW_W


def _compress_weights(phi_k, phi_v, pe_k, pe_v):
    def one(phi, pe):
        p4 = phi.reshape(SUBS, CMP_STRIDE, HEAD_DIM, HEAD_DIM)
        w = jnp.einsum('hlde,gG->lgdhGe', p4, jnp.eye(2, dtype=phi.dtype))
        w = w.reshape(CMP_STRIDE * 2 * HEAD_DIM, SUBS * 2 * HEAD_DIM)
        pr = jnp.broadcast_to(pe.reshape(SUBS, CMP_STRIDE, 1, HEAD_DIM), (SUBS, CMP_STRIDE, 2, HEAD_DIM))
        pr = jnp.pad(pr.reshape(SUBS, -1), ((0, 16 - SUBS), (0, 0)))
        return w, pr
    wk, pk = one(phi_k, pe_k)
    wv, pv = one(phi_v, pe_v)
    return jnp.stack([wk, wv]).astype(BF16), jnp.stack([pk, pv]).astype(BF16)


def _compress_column(xj, j, w_ref, pe_ref, o_ref, ab_scr):
    nsub = o_ref.shape[0]
    half = LANE
    xe = jnp.concatenate([xj, pe_ref[j // 2]], axis=0)
    ab_scr[...] = jnp.dot(xe, w_ref[j // 2], preferred_element_type=F32)
    bias = ab_scr[nsub:nsub + 1, 0:half] + ab_scr[nsub + 1:nsub + 2, half:2 * half]
    o_ref[:, j * LANE:(j + 1) * LANE] = (
        ab_scr[0:nsub, 0:half] + ab_scr[pl.ds(1, nsub), half:2 * half] + bias).astype(o_ref.dtype)


def _compress_body(x_ref, w_ref, pe_ref, o_ref, ab_scr):
    for j in range(ROW_W // LANE):
        xj = jnp.concatenate([x_ref[:, l * ROW_W + j * LANE:l * ROW_W + (j + 1) * LANE]
                              for l in range(CMP_STRIDE)], axis=1).astype(BF16)
        _compress_column(xj, j, w_ref, pe_ref, o_ref, ab_scr)


def _compress_pages_body(pt_ref, *refs, n_x):
    x_refs = refs[:n_x]
    perm_ref, w_ref, pe_ref, o_ref, ab_scr, xs_scr = refs[n_x:]
    page = x_refs[0].shape[-1]
    sub_pp = page // CMP_STRIDE
    kvw = NSA_KV * HEAD_DIM
    for k, xr in enumerate(x_refs):
        for c in range(2):
            t = xr[c].reshape(kvw, page).astype(BF16)
            out = lax.dot_general(perm_ref[...], t, (((1,), (1,)), ((), ())), preferred_element_type=F32)
            for gp in range(kvw // LANE):
                for l in range(CMP_STRIDE):
                    xs_scr[c * (kvw // LANE) + gp, k * sub_pp:(k + 1) * sub_pp, l * LANE:(l + 1) * LANE] = (
                        out[l * sub_pp:(l + 1) * sub_pp, gp * LANE:(gp + 1) * LANE])
    for j in range(ROW_W // LANE):
        _compress_column(xs_scr[j].astype(BF16), j, w_ref, pe_ref, o_ref, ab_scr)


def compress_prompt(rows, nseq, wts, pes):
    nsub = rows.shape[0] // nseq // CMP_STRIDE
    x = rows.reshape(nseq * nsub, SUB_W)
    return pl.pallas_call(
        _compress_body,
        grid=(nseq,),
        in_specs=[pl.BlockSpec((nsub, SUB_W), lambda b: (b, 0)),
                  pl.BlockSpec(wts.shape, lambda b: (0, 0, 0)),
                  pl.BlockSpec(pes.shape, lambda b: (0, 0, 0))],
        out_specs=pl.BlockSpec((nsub, ROW_W), lambda b: (b, 0)),
        out_shape=jax.ShapeDtypeStruct((nseq * nsub, ROW_W), BF16),
        scratch_shapes=[pltpu.VMEM((nsub + 16, 2 * LANE), F32)],
        compiler_params=_params("parallel"),
        name="compress_prompt",
    )(x, wts, pes)


def compress_pages(pool, layer, page_table, wts, pes, nb=2):
    db, n_pages = page_table.shape
    page = pool.shape[-1]
    sub_pp = page // CMP_STRIDE
    nsub = nb * n_pages * sub_pp
    out_row = jnp.arange(page)
    perm = (jnp.arange(page)[None, :] == ((out_row % sub_pp) * CMP_STRIDE + out_row // sub_pp)[:, None]).astype(BF16)
    specs = [pl.BlockSpec((None, None) + pool.shape[2:],
                          functools.partial(lambda i, pt, s, k: (layer, pt[(i * nb + s) * n_pages + k], 0, 0, 0, 0),
                                            s=s, k=k))
             for s in range(nb) for k in range(n_pages)]
    grid_spec = pltpu.PrefetchScalarGridSpec(
        num_scalar_prefetch=1,
        grid=(db // nb,),
        in_specs=specs + [pl.BlockSpec(perm.shape, lambda i, pt: (0, 0)),
                          pl.BlockSpec(wts.shape, lambda i, pt: (0, 0, 0)),
                          pl.BlockSpec(pes.shape, lambda i, pt: (0, 0, 0))],
        out_specs=pl.BlockSpec((nsub, ROW_W), lambda i, pt: (i, 0)),
        scratch_shapes=[pltpu.VMEM((nsub + 16, 2 * LANE), F32),
                        pltpu.VMEM((ROW_W // LANE, nsub, CMP_STRIDE * LANE), F32)],
    )
    return pl.pallas_call(
        functools.partial(_compress_pages_body, n_x=nb * n_pages),
        grid_spec=grid_spec,
        out_shape=jax.ShapeDtypeStruct((db * n_pages * sub_pp, ROW_W), BF16),
        compiler_params=_params("parallel"),
        name="compress_pages",
    )(page_table.reshape(-1), *([pool] * (nb * n_pages)), perm, wts, pes)


def _cover(nsub, nc, n_sel):
    c_start = jnp.arange(nsub) * CMP_STRIDE
    s_start = jnp.arange(SEL_BLK) * SEL_BLK
    ok = ((c_start[:, None] < s_start[None, :] + SEL_BLK) & (c_start[:, None] + CMP_LEN > s_start[None, :])
          & (jnp.arange(nsub)[:, None] < nc) & (jnp.arange(SEL_BLK)[None, :] < n_sel))
    return ok.astype(BF16)


def _expand(n_keys, tk):
    key = jnp.arange(n_keys).reshape(n_keys // tk, 1, tk)
    return (key // SEL_BLK == jnp.arange(SEL_BLK)[None, :, None]).astype(BF16)


def _masked_softmax(s, ok):
    sm = jnp.where(ok, s, NEG)
    ex = jnp.exp(sm - jnp.max(sm, -1, keepdims=True))
    return ex / jnp.sum(ex, -1, keepdims=True)


def _importance(psum, cov_ref, tpos, n_sel):
    hi = psum.astype(BF16)
    lo = (psum - hi.astype(F32)).astype(BF16)
    imp = (jnp.dot(hi, cov_ref[...], preferred_element_type=F32)
           + jnp.dot(lo, cov_ref[...], preferred_element_type=F32))
    jl = lax.broadcasted_iota(jnp.int32, imp.shape, 1)
    cur = tpos // SEL_BLK
    forced = (jl == 0) | (jl == cur) | (jl == cur - 1)
    causal = jl * SEL_BLK <= tpos
    imp = jnp.where(forced, FORCE, jnp.where(causal, imp, -1.0))
    return jnp.where(jl < n_sel, imp, -2.0)


def _topk_masks_rows(imps):
    out = []
    for x in imps:
        jl = lax.broadcasted_iota(jnp.int32, x.shape, 1)
        cnt = jnp.zeros(x.shape, F32)
        for i in range(SEL_BLK):
            xi = x[:, i:i + 1]
            tie = jnp.where(jl > i, 1.0, 0.0)
            cnt = cnt + jnp.where(xi > x, 1.0, jnp.where(xi == x, tie, 0.0))
        out.append(jnp.where(cnt < SEL_TOPK, 1.0, 0.0))
    return out


_NT = (((1,), (1,)), ((), ()))


def _softmax_cols(s):
    p = jnp.exp2(s - jnp.max(s, axis=0, keepdims=True))
    return p, jnp.sum(p, axis=0, keepdims=True)


def _run_skewed(tasks):
    n = len(tasks)
    tasks[0][0]()
    for i in range(n):
        if i + 1 < n:
            tasks[i + 1][0]()
        tasks[i][1]()
        if i >= 1:
            tasks[i - 1][2]()
    tasks[n - 1][2]()


def _nsa_prompt_lanes_body(q_ref, kc_ref, ka_ref, vs_ref, kw_ref, vw_ref, gt_ref, cov_ref, et_ref, o_ref, *,
                           tq, nc, n_sel):
    t0 = pl.program_id(1) * tq
    tl = t0 + lax.broadcasted_iota(jnp.int32, (1, tq), 1)
    sig = jax.nn.sigmoid(gt_ref[...])
    nsub = kc_ref.shape[0]
    rep, hd, ng = NSA_REP, HEAD_DIM, NSA_KV
    nh = ng * rep
    tk = vs_ref.shape[-1]
    wt = vw_ref.shape[-1]
    wide = lambda x: jnp.concatenate([x] * rep, axis=1)
    head_rows = lambda g, r: q_ref[(g * rep + r) * hd:(g * rep + r + 1) * hd, :]
    q4 = [jnp.concatenate([head_rows(g, r) for r in range(rep)], axis=1) for g in range(ng)]

    mrow = lax.broadcasted_iota(jnp.int32, (nsub, 1), 0)
    validf = wide(jnp.where((mrow * CMP_STRIDE + CMP_LEN - 1 <= tl) & (mrow < nc), 1.0, 0.0))
    jrow = lax.broadcasted_iota(jnp.int32, (SEL_BLK, tq), 0)
    cur = tl // SEL_BLK
    forced = (jrow == 0) | (jrow == cur) | (jrow == cur - 1)
    causal = jrow * SEL_BLK <= tl
    s_cmp, p_cmp, o_cmp, sel_neg = [None] * ng, [None] * ng, [None] * ng, [None] * ng

    def cmp_scores(g):
        s_cmp[g] = jnp.dot(kc_ref[:, g * hd:(g + 1) * hd], q4[g], preferred_element_type=F32)

    def cmp_softmax(g):
        ex, l = _softmax_cols(jnp.where(validf > 0.5, s_cmp[g], NEG))
        p = ex / l * validf
        p_cmp[g] = p.astype(BF16)
        psum = p[:, 0:tq]
        for r in range(1, rep):
            psum = psum + p[:, r * tq:(r + 1) * tq]
        hi = psum.astype(BF16)
        lo = (psum - hi.astype(F32)).astype(BF16)
        imp = (jnp.dot(cov_ref[...], hi, preferred_element_type=F32)
               + jnp.dot(cov_ref[...], lo, preferred_element_type=F32))
        imp = jnp.where(forced, FORCE, jnp.where(causal, imp, -1.0))
        imp = jnp.where(jrow < n_sel, imp, -2.0)
        sub = 8
        slabs = [imp[v * sub:(v + 1) * sub, :] for v in range(SEL_BLK // sub)]
        cnts = [jnp.zeros((sub, tq), F32) for _ in slabs]
        srow = lax.broadcasted_iota(jnp.int32, (sub, tq), 0)
        for i in range(SEL_BLK):
            xi = jnp.broadcast_to(imp[i:i + 1, :], (sub, tq))
            for v, x in enumerate(slabs):
                if v > i // sub:
                    hit = jnp.where(xi >= x, 1.0, 0.0)
                elif v < i // sub:
                    hit = jnp.where(xi > x, 1.0, 0.0)
                else:
                    tie = jnp.where(srow > i % sub, 1.0, 0.0)
                    hit = jnp.where(xi > x, 1.0, jnp.where(xi == x, tie, 0.0))
                cnts[v] = cnts[v] + hit
        cnt = jnp.concatenate(cnts, axis=0)
        sel_neg[g] = jnp.where(cnt < SEL_TOPK, 0.0, NEG).astype(BF16)

    def cmp_values(g):
        vc_t = kc_ref[:, ng * hd + g * hd:ng * hd + (g + 1) * hd].astype(F32).T.astype(BF16)
        o_cmp[g] = jnp.dot(vc_t, p_cmp[g], preferred_element_type=F32)

    n_wt = (WINDOW + tq) // wt
    w0 = jnp.maximum(t0 - WINDOW, 0)
    wt0 = w0 // wt
    wrows = n_wt * wt
    kpos_w = w0 + lax.broadcasted_iota(jnp.int32, (wrows, 1), 0)
    bias_w = wide(jnp.where((kpos_w <= tl) & (kpos_w > tl - WINDOW), 0.0, NEG))
    s_win, p_win, o_win = [None] * ng, [None] * ng, [None] * ng

    def ones_row(n):
        return jnp.where(lax.broadcasted_iota(jnp.int32, (SUM_ROWS, n), 0) == 0, 1.0, 0.0).astype(BF16)

    def win_scores(g):
        k_w = kw_ref[pl.ds(pl.multiple_of(w0, wt), wrows), g * hd:(g + 1) * hd]
        s_win[g] = jnp.dot(k_w, q4[g], preferred_element_type=F32) + bias_w

    def win_softmax(g):
        p_win[g] = jnp.exp2(s_win[g] - jnp.max(s_win[g], axis=0, keepdims=True)).astype(BF16)

    def win_values(g):
        v_w = jnp.concatenate([vw_ref[wt0 + i, g * hd:(g + 1) * hd, :] for i in range(n_wt)], axis=1)
        o = jnp.dot(jnp.concatenate([v_w, ones_row(wrows)], axis=0), p_win[g], preferred_element_type=F32)
        o_win[g] = o[0:hd] / o[hd:hd + 1]

    q_aug = [None] * ng

    def keys(kt, g):
        rows = pl.ds(pl.multiple_of(kt * tk, tk), tk)
        return ka_ref[rows, g * LANE:(g + 1) * LANE] + et_ref[rows, :]

    def sweep_tasks(kt, carry, out, bias):
        s, soft = [None] * ng, [None] * ng

        def scores(g):
            if q_aug[g] is None:
                q_aug[g] = jnp.concatenate([jnp.concatenate([head_rows(g, r), sel_neg[g]], axis=0)
                                            for r in range(rep)], axis=1)
            s[g] = jnp.dot(keys(kt, g), q_aug[g], preferred_element_type=F32)
            if bias is not None:
                s[g] = s[g] + bias

        def softmax(g):
            m_i = carry[g][0]
            m_new = jnp.maximum(m_i, jnp.max(s[g], axis=0, keepdims=True))
            soft[g] = (m_new, jnp.exp2(m_i - m_new), jnp.exp2(s[g] - m_new).astype(BF16))

        def values(g):
            m_new, alpha, p = soft[g]
            v_aug = jnp.concatenate([vs_ref[kt, g * hd:(g + 1) * hd, :], ones_tk], axis=0)
            out[g] = (m_new, alpha * carry[g][1] + jnp.dot(v_aug, p, preferred_element_type=F32))

        return [(functools.partial(scores, g), functools.partial(softmax, g), functools.partial(values, g))
                for g in range(ng)]

    kt_d = t0 // tk
    kpos_d = kt_d * tk + lax.broadcasted_iota(jnp.int32, (tk, 1), 0)
    bias_d = wide(jnp.where(kpos_d <= tl, 0.0, NEG))
    ones_tk = ones_row(tk)
    empty = (jnp.full((1, rep * tq), NEG, F32), jnp.zeros((hd + SUM_ROWS, rep * tq), F32))
    init = [None] * ng
    tasks = []
    for g in range(ng):
        tasks.append((functools.partial(cmp_scores, g), functools.partial(cmp_softmax, g),
                      functools.partial(cmp_values, g)))
        tasks.append((functools.partial(win_scores, g), functools.partial(win_softmax, g),
                      functools.partial(win_values, g)))
    _run_skewed(tasks + sweep_tasks(kt_d, [empty] * ng, init, bias_d))

    def step(kt, carry):
        out = [None] * ng
        _run_skewed(sweep_tasks(kt, carry, out, None))
        return tuple(out)

    final = lax.fori_loop(0, kt_d, step, tuple(init))

    for g in range(ng):
        acc_s = final[g][1]
        o_sel = acc_s[0:hd] / acc_s[hd:hd + 1]
        heads = []
        for r in range(rep):
            h = g * rep + r
            cols = slice(r * tq, (r + 1) * tq)
            heads.append(sig[h:h + 1, :] * o_cmp[g][:, cols] + sig[nh + h:nh + h + 1, :] * o_sel[:, cols]
                         + sig[2 * nh + h:2 * nh + h + 1, :] * o_win[g][:, cols])
        for pair in range(rep // 2):
            o_ref[:, (g * rep + 2 * pair) * hd:(g * rep + 2 * pair + 2) * hd] = (
                jnp.concatenate([heads[2 * pair], heads[2 * pair + 1]], axis=0).T)


def nsa_prompt_t(q_t, kcvc, ka, vs_t, kw, vw_t, gates_t, nseq, seq_len, tq=128):
    m = ka.shape[0]
    nsub = kcvc.shape[0] // nseq
    tk, wt = vs_t.shape[-1], vw_t.shape[-1]
    nc = (seq_len - CMP_LEN) // CMP_STRIDE + 1
    n_sel = seq_len // SEL_BLK
    assert n_sel <= SEL_BLK and seq_len >= WINDOW + tq and tk % tq == 0 and tq % wt == 0 and WINDOW % wt == 0
    nq = seq_len // tq
    width = q_t.shape[1]
    cov_t = _cover(nsub, nc, n_sel).T
    key = jnp.arange(seq_len)
    et = jnp.concatenate([jnp.zeros((seq_len, HEAD_DIM), BF16),
                          (key[:, None] // SEL_BLK == jnp.arange(SEL_BLK)[None, :]).astype(BF16)], axis=1)
    cols = lambda a: pl.BlockSpec((None, a.shape[1], tq), lambda b, t: (b, 0, t))
    seq_rows = lambda a: pl.BlockSpec((seq_len, a.shape[1]), lambda b, t: (b, 0))
    whole = lambda a: pl.BlockSpec((None,) + a.shape[1:], lambda b, t: (b, 0, 0, 0))
    return pl.pallas_call(
        functools.partial(_nsa_prompt_lanes_body, tq=tq, nc=nc, n_sel=n_sel),
        grid=(nseq, nq),
        in_specs=[cols(q_t), pl.BlockSpec((nsub, ROW_W), lambda b, t: (b, 0)), seq_rows(ka), whole(vs_t),
                  seq_rows(kw), whole(vw_t), cols(gates_t),
                  pl.BlockSpec(cov_t.shape, lambda b, t: (0, 0)),
                  pl.BlockSpec(et.shape, lambda b, t: (0, 0))],
        out_specs=pl.BlockSpec((tq, width), lambda b, t: (b * nq + t, 0)),
        out_shape=jax.ShapeDtypeStruct((m, width), F32),
        compiler_params=_params("parallel", "parallel"),
        name="nsa_prompt",
    )(q_t, kcvc, ka, vs_t, kw, vw_t, gates_t, cov_t, et)


def _joint_softmax(s_a, ok_a, s_b, ok_b):
    s_a = jnp.where(ok_a, s_a, NEG)
    s_b = jnp.where(ok_b, s_b, NEG)
    mx = jnp.maximum(jnp.max(s_a, -1, keepdims=True), jnp.max(s_b, -1, keepdims=True))
    e_a = jnp.exp(s_a - mx)
    e_b = jnp.exp(s_b - mx)
    return e_a, e_b, jnp.sum(e_a, -1, keepdims=True) + jnp.sum(e_b, -1, keepdims=True)


def _nsa_sample_body(pt_ref, q_ref, kc_ref, *rest, n, n_pages, past_len, nc, n_sel, fill):
    page_refs = rest[:n_pages]
    ns_ref, nw_ref, win_ref, gt_ref, cov_ref, e_ref = rest[n_pages:n_pages + 6]
    o_ref, wo_ref = rest[-2:]
    if fill is not None:
        for slot in range(fill[1]):
            if slot != fill[0]:
                wo_ref[slot] = jnp.zeros(wo_ref.shape[1:], F32)
        wo_ref = wo_ref.at[fill[0]]
    np_, rep, hd, kvw, nkv = SAMPLE_PAD, NSA_REP, HEAD_DIM, NSA_KV * HEAD_DIM, NSA_KV
    tpos = past_len + lax.broadcasted_iota(jnp.int32, (np_, 1), 0)
    sig = jax.nn.sigmoid(gt_ref[...])
    q = q_ref[...]

    blocks = []
    for g in range(nkv):
        for r in range(rep):
            h = g * rep + r
            parts = [q[:, h * hd:(h + 1) * hd]]
            if g > 0:
                parts.insert(0, jnp.zeros((np_, g * hd), F32))
            if g < nkv - 1:
                parts.append(jnp.zeros((np_, (nkv - 1 - g) * hd), F32))
            blocks.append(jnp.concatenate(parts, axis=1))
    qbd = jnp.concatenate(blocks, axis=0).astype(BF16)
    nrow = nkv * rep * np_

    def grp(x):
        return x.reshape(nkv, rep, np_, x.shape[-1])

    def scores(keys):
        return grp(lax.dot_general(qbd, keys, _NT, preferred_element_type=F32))

    def scores_t(keys_t):
        return grp(jnp.dot(qbd, keys_t, preferred_element_type=F32))

    new_lane = lax.broadcasted_iota(jnp.int32, (np_, np_), 1)
    new_pos = past_len + new_lane

    nsub = kc_ref.shape[0]
    mcol = lax.broadcasted_iota(jnp.int32, (np_, nsub), 1)
    valid = (mcol * CMP_STRIDE + CMP_LEN - 1 <= tpos) & (mcol < nc)
    p = _masked_softmax(scores(kc_ref[:, 0:kvw]), valid[None, None]) * jnp.where(valid, 1.0, 0.0)[None, None]
    o_cmp = jnp.dot(p.reshape(nrow, nsub).astype(BF16), kc_ref[:, kvw:2 * kvw], preferred_element_type=F32)
    sels = _topk_masks_rows([_importance(jnp.sum(p[g], axis=0), cov_ref, tpos, n_sel) for g in range(nkv)])

    page = page_refs[0].shape[-1]
    k_all = jnp.concatenate([pr[0].reshape(kvw, page) for pr in page_refs], axis=1).astype(BF16)
    v_all = jnp.concatenate([pr[1].reshape(kvw, page) for pr in page_refs], axis=1).astype(BF16)
    kpos = lax.broadcasted_iota(jnp.int32, (np_, past_len), 1)
    ok_c = jnp.stack([jnp.where(kpos <= tpos, jnp.dot(sels[g].astype(BF16), e_ref[0], preferred_element_type=F32), 0.0)
                      for g in range(nkv)]) > 0.5
    new_ok = (new_pos <= tpos) & (new_lane < n)
    ok_n = jnp.stack([jnp.where(new_ok, jnp.concatenate(
        [sels[g][:, (past_len + i) // SEL_BLK:(past_len + i) // SEL_BLK + 1] for i in range(np_)], axis=1), 0.0)
        for g in range(nkv)]) > 0.5
    ns = ns_ref[...]
    e_c, e_n, l_s = _joint_softmax(scores_t(k_all), ok_c[:, None], scores(ns[:, 0:kvw].astype(BF16)), ok_n[:, None])
    o_sel = lax.dot_general(e_c.reshape(nrow, past_len).astype(BF16), v_all, _NT, preferred_element_type=F32)
    e_n = e_n.reshape(nrow, np_)
    for i in range(n):
        o_sel = o_sel + e_n[:, i:i + 1] * ns[i:i + 1, kvw:2 * kvw]
    o_sel = o_sel / l_s.reshape(nrow, 1)

    w_buf = win_ref.shape[-1]
    k_w = win_ref[0].reshape(kvw, w_buf)
    v_w = win_ref[1].reshape(kvw, w_buf)
    kpos_w = past_len - w_buf + lax.broadcasted_iota(jnp.int32, (np_, w_buf), 1)
    ok_w = (kpos_w <= tpos) & (kpos_w > tpos - WINDOW)
    ok_wn = (new_pos <= tpos) & (new_pos > tpos - WINDOW) & (new_lane < n)
    nw = nw_ref[...]
    e_w, e_wn, l_w = _joint_softmax(scores_t(k_w.astype(BF16)), ok_w[None, None],
                                    scores(nw[:, 0:kvw].astype(BF16)), ok_wn[None, None])
    o_win = lax.dot_general(e_w.reshape(nrow, w_buf).astype(BF16), v_w.astype(BF16), _NT,
                            preferred_element_type=F32)
    e_wn = e_wn.reshape(nrow, np_)
    for i in range(n):
        o_win = o_win + e_wn[:, i:i + 1] * nw[i:i + 1, kvw:2 * kvw]
    o_win = o_win / l_w.reshape(nrow, 1)

    nh = nkv * rep
    for g in range(nkv):
        for r in range(rep):
            h = g * rep + r
            rows = slice(h * np_, (h + 1) * np_)
            cols = slice(g * hd, (g + 1) * hd)
            o_ref[:, h * hd:(h + 1) * hd] = (sig[:, h:h + 1] * o_cmp[rows, cols]
                                            + sig[:, nh + h:nh + h + 1] * o_sel[rows, cols]
                                            + sig[:, 2 * nh + h:2 * nh + h + 1] * o_win[rows, cols])

    body = w_buf - LANE
    for c, old in enumerate((k_w, v_w)):
        shifted = pltpu.roll(old, w_buf - n, 1)
        new_t = jnp.concatenate([nw[:, c * kvw:(c + 1) * kvw].T, jnp.zeros((kvw, LANE - np_), F32)], axis=1)
        lane = lax.broadcasted_iota(jnp.int32, (kvw, LANE), 1)
        wo_ref[c * kvw:(c + 1) * kvw, 0:body] = shifted[:, 0:body]
        wo_ref[c * kvw:(c + 1) * kvw, body:w_buf] = jnp.where(lane < LANE - n, shifted[:, body:w_buf],
                                                              pltpu.roll(new_t, LANE - n, 1))


def nsa_sample(q, kcvc, pool, layer, page_table, rows_s, rows_w, win, gates, n, prev=None):
    db, n_pages = page_table.shape
    page = pool.shape[-1]
    past_len = n_pages * page
    nsub = kcvc.shape[0] // db
    w_buf = win.shape[-1]
    assert w_buf == WINDOW and n <= SAMPLE_PAD and past_len % SEL_BLK == 0
    nc = (past_len + n - CMP_LEN) // CMP_STRIDE + 1
    n_sel = -(-(past_len + n) // SEL_BLK)
    assert n_sel <= SEL_BLK and nc <= nsub
    cov = _cover(nsub, nc, n_sel)
    exp = _expand(past_len, past_len)
    rows8 = lambda w: pl.BlockSpec((SAMPLE_PAD, w), lambda i, pt: (i, 0))
    pages = [pl.BlockSpec((None, None) + pool.shape[2:],
                          functools.partial(lambda i, pt, k: (layer, pt[i * n_pages + k], 0, 0, 0, 0), k=k))
             for k in range(n_pages)]
    grid_spec = pltpu.PrefetchScalarGridSpec(
        num_scalar_prefetch=1,
        grid=(db,),
        in_specs=[rows8(q.shape[1]), pl.BlockSpec((nsub, ROW_W), lambda i, pt: (i, 0))] + pages
        + [rows8(ROW_W), rows8(ROW_W),
           pl.BlockSpec((None, None) + win.shape[2:], lambda i, pt: (layer, i, 0, 0, 0, 0)),
           rows8(LANE),
           pl.BlockSpec(cov.shape, lambda i, pt: (0, 0)),
           pl.BlockSpec(exp.shape, lambda i, pt: (0, 0, 0))]
        + ([pl.BlockSpec(memory_space=pl.ANY)] if prev is not None else []),
        out_specs=[rows8(q.shape[1]),
                   pl.BlockSpec((None, None, ROW_W, w_buf), lambda i, pt: (layer, i, 0, 0)) if prev is not None
                   else pl.BlockSpec((win.shape[0], None, ROW_W, w_buf), lambda i, pt: (0, i, 0, 0))],
    )
    args = (page_table.reshape(-1), q, kcvc, *([pool] * n_pages), rows_s, rows_w, win, gates, cov, exp)
    return pl.pallas_call(
        functools.partial(_nsa_sample_body, n=n, n_pages=n_pages, past_len=past_len, nc=nc, n_sel=n_sel,
                          fill=None if prev is not None else (layer, win.shape[0])),
        grid_spec=grid_spec,
        out_shape=[jax.ShapeDtypeStruct(q.shape, F32),
                   jax.ShapeDtypeStruct((win.shape[0], db, ROW_W, w_buf), F32)],
        input_output_aliases={len(args): 1} if prev is not None else {},
        compiler_params=_params("parallel"),
        name="nsa_sample",
    )(*args, *([prev] if prev is not None else []))


def kernel(x_prompt, x_sample, state_ret, cache_cmp, cache_sel, state_win, state_ffn, page_table, norm_mix_pre, norm_mix_post, norm_ffn_pre, norm_ffn_post, e_w_in, e_w_out, e_sg_ln_g, e_sg_ln_b, e_sg_w, e_sg_b, o_w_in, o_w_out, o_pe_k, o_pe_v, o_phi_k, o_phi_v, f_w_in, f_conv_w, f_conv_b, f_w_out):
    b, s, d = x_prompt.shape
    db, n, _ = x_sample.shape
    depth = norm_mix_pre.shape[0]
    heads = state_ret.shape[2]
    groups = e_sg_w.shape[1]
    n_pages = page_table.shape[1]
    page = cache_cmp.shape[2]
    past_len = n_pages * page
    pad = SAMPLE_PAD
    assert n <= pad and n >= CONV_W - 1 and n < CMP_STRIDE and state_ret.shape[3] == LANE

    xp = x_prompt.reshape(b * s, d)
    xs = jnp.pad(x_sample, ((0, 0), (0, pad - n), (0, 0))).reshape(db * pad, d)
    pos_p = jnp.arange(s, dtype=jnp.int32)
    pos_s = jnp.tile(past_len + jnp.arange(pad, dtype=jnp.int32), db)
    ret_p, ret_s = _ret_rope_tabs(pos_p), _ret_rope_tabs(pos_s)
    nsa_p, nsa_s = _nsa_rope_tabs(pos_p), _nsa_rope_tabs(pos_s)
    even_modes = [ROPE_RET_Q] * heads + [ROPE_RET_K] * heads + [ROPE_NONE] * (4 * heads)
    even_outs = [(0, 6 * heads * LANE, 1.0, F32, None)]
    kscale = LANE ** -0.5
    odd_cols = o_w_in.shape[2]
    odd_pad = -(-odd_cols // LANE) * LANE
    rows_last = lambda a: jnp.transpose(a, (0, 1, 3, 4, 5, 2))
    rows_first = lambda a: jnp.transpose(a.reshape(a.shape[:2] + kv_shape + (a.shape[-1],)), (0, 1, 5, 2, 3, 4))
    pool_c, pool_s, win = rows_last(cache_cmp), rows_last(cache_sel), rows_last(state_win)
    kv_shape = cache_cmp.shape[3:]
    keep = min(WINDOW, s)

    out = {k: [] for k in ("ret_p", "sgv", "cmp_s", "sel_s", "win_p", "ffn_p", "ffn_s")}
    ret_s_all = cmp_all = sel_all = win_all = None
    n_odd = o_w_in.shape[0]
    ffn_w_in, ffn_w_out = f_w_in.astype(BF16), f_w_out.astype(BF16)
    ffn_cb = f_conv_b[:, None, :]
    for l in range(depth):
        i = l // 2
        if l % 2 == 0:
            w_in = e_w_in[i].astype(BF16)
            w_out = e_w_out[i].astype(BF16)
            (pp,) = proj_in(xp, norm_mix_pre[l], w_in, (ret_p[0], ret_p[1], ret_p[1]), even_modes, even_outs, kscale)
            (ps,) = proj_in(xs, norm_mix_pre[l], w_in, (ret_s[0], ret_s[1], ret_s[1]), even_modes, even_outs, kscale)
            cat_p, st_p = even_seq(pp, b, s, e_sg_ln_g[i], e_sg_ln_b[i], e_sg_w[i], e_sg_b[i], heads, groups)
            cat_s, ret_s_all, svn_s = even_seq_sample(ps, state_ret, i, n, e_sg_ln_g[i], e_sg_ln_b[i], e_sg_w[i],
                                                      e_sg_b[i], heads, groups, prev=ret_s_all)
            xp = proj_out([cat_p], w_out, norm_mix_post[l], xp)
            xs = proj_out([cat_s], w_out, norm_mix_post[l], xs)
            out["ret_p"].append(st_p)
            out["sgv"].append(svn_s.reshape(db, pad, -1)[:, :n])
        else:
            w_in = jnp.pad(o_w_in[i], ((0, 0), (0, odd_pad - odd_cols))).astype(BF16)
            w_out = o_w_out[i].astype(BF16)
            wts, pes = _compress_weights(o_phi_k[i], o_phi_v[i], o_pe_k[i], o_pe_v[i])
            q_p, rc_p, cmp_all, sel_all, win_t, g_p, ka_p, vs_p, kw_p, vw_p = proj_in(
                xp, norm_mix_pre[l], w_in, nsa_p, _odd_modes(), _odd_outs(True), seq_len=s,
                layer=i, n_layers=n_odd, stacked={2: cmp_all, 3: sel_all})
            q_s, rc_s, rs_s, rw_s, g_s = proj_in(xs, norm_mix_pre[l], w_in, nsa_s, _odd_modes(), _odd_outs(False))
            kc_p = compress_prompt(rc_p, b, wts, pes)
            o_p = nsa_prompt_t(q_p, kc_p, ka_p, vs_p, kw_p, vw_p, g_p, b, s)
            kc_s = compress_pages(pool_c, i, page_table, wts, pes)
            o_s, win_all = nsa_sample(q_s, kc_s, pool_s, i, page_table, rs_s, rw_s, win, g_s, n, prev=win_all)
            xp = proj_out([o_p], w_out, norm_mix_post[l], xp)
            xs = proj_out([o_s], w_out, norm_mix_post[l], xs)
            out["win_p"].append(win_t[:, :, s - keep:])
            out["cmp_s"].append(rc_s.reshape((db, pad) + kv_shape)[:, :n])
            out["sel_s"].append(rs_s.reshape((db, pad) + kv_shape)[:, :n])
        xp, st_p = conv_ffn(xp, s, norm_ffn_pre[l], ffn_w_in, f_conv_w, ffn_cb, ffn_w_out, norm_ffn_post[l], l)
        xs, st_s = conv_ffn(xs, pad, norm_ffn_pre[l], ffn_w_in, f_conv_w, ffn_cb, ffn_w_out, norm_ffn_post[l], l,
                            prev=state_ffn, tm=256)
        out["ffn_p"].append(st_p.reshape(b, 8, -1)[:, 8 - (CONV_W - 1):])
        out["ffn_s"].append(st_s.reshape(db, pad, -1)[:, n - (CONV_W - 1):n])

    stack = lambda k: jnp.stack(out[k])
    return (xp.reshape(b, s, d), xs.reshape(db, pad, d)[:, :n], stack("ret_p"), ret_s_all, stack("sgv"),
            rows_first(cmp_all), stack("cmp_s"), rows_first(sel_all), stack("sel_s"), rows_first(stack("win_p")),
            rows_first(win_all), stack("ffn_p"), stack("ffn_s"))
```

```python
import functools

import jax
import jax.numpy as jnp
from jax import lax
from jax.experimental import pallas as pl
from jax.experimental.pallas import tpu as pltpu

F32 = jnp.float32
BF16 = jnp.bfloat16

EPS = 1e-6
NEG = -1e30
FORCE = 1e9

LANE = 128
VMEM_LIMIT = 56 * 1024 * 1024

RET_CHUNK = 128
RET_THETA = 10000.0
SG_CHUNK = 128
HEAD_DIM = 64
NSA_KV = 4
NSA_REP = 4
CMP_LEN = 32
CMP_STRIDE = 16
SEL_BLK = 64
SEL_TOPK = 16
WINDOW = 512
ROPE_DIM = HEAD_DIM // 4
ROPE_THETA = 500000.0
ATTN_SCALE = HEAD_DIM ** -0.5
LOG2E = 1.4426950408889634
CONV_W = 3

SAMPLE_PAD = 8

ROPE_NONE, ROPE_RET_Q, ROPE_RET_K, ROPE_NSA = 0, 1, 2, 3
PAD_HEADS = -1


def _params(*sem):
    return pltpu.CompilerParams(dimension_semantics=sem, vmem_limit_bytes=VMEM_LIMIT)


def _rms(x, g):
    return x * lax.rsqrt(jnp.mean(x * x, -1, keepdims=True) + EPS) * g


def _col_chunk(n):
    for c in (512, 384, 256, 128):
        if n % c == 0:
            return c
    raise ValueError(n)


def _proj_in_body(x_ref, g_ref, w_ref, tc_ref, ta_ref, tb_ref, *o_refs, modes, kscale, outs, n_prev=0):
    o_refs = o_refs[n_prev:]
    h = _rms(x_ref[...], g_ref[...]).astype(BF16)
    n = w_ref.shape[1]
    cw = _col_chunk(n)
    for c0 in range(0, n, cw):
        y = jnp.dot(h, w_ref[:, c0:c0 + cw], preferred_element_type=F32)
        for j in range(cw // LANE):
            col = c0 + j * LANE
            blk = y[:, j * LANE:(j + 1) * LANE]
            mode = modes[col // LANE]
            if mode in (ROPE_RET_Q, ROPE_RET_K):
                blk = blk * tc_ref[...] + pltpu.roll(blk, LANE // 2, 1) * ta_ref[...]
                if mode == ROPE_RET_K:
                    blk = blk * kscale
            elif mode == ROPE_NSA:
                blk = (blk * tc_ref[...] + pltpu.roll(blk, LANE - ROPE_DIM // 2, 1) * ta_ref[...]
                       + pltpu.roll(blk, ROPE_DIM // 2, 1) * tb_ref[...])
            for o_ref, (oc, ow, osc, tw, fill) in zip(o_refs, outs):
                if oc <= col < oc + ow:
                    v = blk if osc == 1.0 else blk * osc
                    cs = slice(col - oc, col - oc + LANE)
                    if tw is None:
                        o_ref[:, cs] = v.astype(o_ref.dtype)
                    elif tw == PAD_HEADS:
                        low = lax.broadcasted_iota(jnp.int32, v.shape, 1) < HEAD_DIM
                        c2 = 2 * (col - oc)
                        o_ref[:, c2:c2 + LANE] = jnp.where(low, v, 0.0).astype(o_ref.dtype)
                        o_ref[:, c2 + LANE:c2 + 2 * LANE] = jnp.where(low, pltpu.roll(v, HEAD_DIM, 1), 0.0).astype(o_ref.dtype)
                    elif tw == 0 and fill is not None:
                        for slot in range(fill[1]):
                            o_ref[slot, cs, :] = (v.T if slot == fill[0] else jnp.zeros(v.T.shape, F32)).astype(o_ref.dtype)
                    elif tw == 0:
                        o_ref[cs, :] = v.T.astype(o_ref.dtype)
                    else:
                        for s in range(v.shape[0] // tw):
                            o_ref[s, cs, :] = v[s * tw:(s + 1) * tw, :].T.astype(o_ref.dtype)


def proj_in(x, g, w, tabs, modes, outs, kscale=1.0, tm=512, seq_len=None, layer=0, n_layers=1, stacked=None):
    m, d = x.shape
    n = w.shape[1]
    tm = min(tm, m)
    nt = tabs[0].shape[0] // tm
    tab_spec = pl.BlockSpec((tm, LANE), lambda i: (i % nt, 0))
    tps = (seq_len // tm) if seq_len else 1
    stacked = stacked or {}
    specs, shapes, prevs, aliases, fills = [], [], [], {}, []
    for k, (_, ow, _, dt, tw) in enumerate(outs):
        fills.append(None)
        if tw is None:
            specs.append(pl.BlockSpec((tm, ow), lambda i: (i, 0)))
            shapes.append(jax.ShapeDtypeStruct((m, ow), dt))
        elif tw == PAD_HEADS:
            specs.append(pl.BlockSpec((tm, 2 * ow), lambda i: (i, 0)))
            shapes.append(jax.ShapeDtypeStruct((m, 2 * ow), dt))
        elif tw == 0 and k in stacked:
            shapes.append(jax.ShapeDtypeStruct((n_layers, m // seq_len, ow, seq_len), dt))
            if stacked[k] is None:
                specs.append(pl.BlockSpec((n_layers, None, ow, tm), lambda i: (0, i // tps, 0, i % tps)))
                fills[-1] = (layer, n_layers)
            else:
                specs.append(pl.BlockSpec((None, None, ow, tm), lambda i: (layer, i // tps, 0, i % tps)))
                aliases[6 + len(prevs)] = k
                prevs.append(stacked[k])
        elif tw == 0:
            specs.append(pl.BlockSpec((None, ow, tm), lambda i: (i // tps, 0, i % tps)))
            shapes.append(jax.ShapeDtypeStruct((m // seq_len, ow, seq_len), dt))
        else:
            specs.append(pl.BlockSpec((None, tm // tw, ow, tw), lambda i: (i // tps, i % tps, 0, 0)))
            shapes.append(jax.ShapeDtypeStruct((m // seq_len, seq_len // tw, ow, tw), dt))
    return pl.pallas_call(
        functools.partial(_proj_in_body, modes=tuple(modes), kscale=kscale, n_prev=len(prevs),
                          outs=tuple((o[0], o[1], o[2], o[4], fl) for o, fl in zip(outs, fills))),
        grid=(m // tm,),
        in_specs=[pl.BlockSpec((tm, d), lambda i: (i, 0)),
                  pl.BlockSpec((1, d), lambda i: (0, 0)),
                  pl.BlockSpec((d, n), lambda i: (0, 0)),
                  tab_spec, tab_spec, tab_spec] + [pl.BlockSpec(memory_space=pl.ANY)] * len(prevs),
        out_specs=specs,
        out_shape=shapes,
        input_output_aliases=aliases,
        compiler_params=_params("parallel"),
        name="proj_in",
    )(x, g.reshape(1, d), w, *tabs, *prevs)


def _ret_rope_tabs(pos):
    inv = 1.0 / (RET_THETA ** jnp.linspace(0.0, 1.0, LANE // 2))
    ang = pos.astype(F32)[:, None] * inv[None, :]
    cos, sin = jnp.cos(ang), jnp.sin(ang)
    return jnp.concatenate([cos, cos], -1), jnp.concatenate([-sin, sin], -1)


def _nsa_rope_tabs(pos):
    hr = ROPE_DIM // 2
    inv = 1.0 / (ROPE_THETA ** (jnp.arange(0, ROPE_DIM, 2, dtype=F32) / ROPE_DIM))
    ang = pos.astype(F32)[:, None] * inv[None, :]
    cos, sin = jnp.cos(ang), jnp.sin(ang)
    n = pos.shape[0]
    rest = HEAD_DIM - ROPE_DIM
    c = jnp.concatenate([cos, cos, jnp.ones((n, rest), F32)], -1)
    a = jnp.concatenate([-sin, jnp.zeros((n, hr + rest), F32)], -1)
    b = jnp.concatenate([jnp.zeros((n, hr), F32), sin, jnp.zeros((n, rest), F32)], -1)
    rep = LANE // HEAD_DIM
    return jnp.tile(c, (1, rep)), jnp.tile(a, (1, rep)), jnp.tile(b, (1, rep))


def _ret_tabs(c, n_valid, heads):
    log_g = jnp.log(1.0 - 2.0 ** (-5.0 - jnp.arange(heads, dtype=F32)))
    idx = jnp.arange(c, dtype=F32)
    diff = idx[:, None] - idx[None, :]
    ok = (diff >= 0) & (idx[None, :] < n_valid)
    dmask = jnp.where(ok, jnp.exp(log_g[:, None, None] * jnp.maximum(diff, 0.0)), 0.0)
    qdec = jnp.exp(log_g[:, None] * (idx + 1.0))
    kdec = jnp.where(idx < n_valid, jnp.exp(log_g[:, None] * (n_valid - 1.0 - idx)), 0.0)
    cdec = jnp.exp(log_g * n_valid)
    bc = lambda t: jnp.broadcast_to(t[..., None], t.shape + (LANE,))
    dm = dmask if c == LANE else jnp.pad(dmask, ((0, 0), (0, 0), (0, LANE - c)))
    return dm, bc(qdec), bc(kdec), jnp.broadcast_to(cdec[:, None, None], (heads, 8, LANE))


def _layer_norm_rows(x, g, b):
    xc = x - jnp.mean(x, -1, keepdims=True)
    return xc * lax.rsqrt(jnp.mean(xc * xc, -1, keepdims=True) + EPS) * g + b


def _rms_unit(x):
    return x * lax.rsqrt(jnp.mean(x * x, -1, keepdims=True) + EPS)


def _even_seq_body(q_ref, k_ref, v_ref, g_ref, u_ref, sv_ref, dm_ref, qd_ref, kd_ref, cd_ref,
                   lng_ref, lnb_ref, wm_ref, sgb_ref, o_ref, st_ref, s_scr, *, tq, heads, groups):
    t = pl.program_id(1)
    c = RET_CHUNK

    @pl.when(t == 0)
    def _():
        s_scr[...] = jnp.zeros(s_scr.shape, F32)

    for ci in range(tq // c):
        rows = slice(ci * c, (ci + 1) * c)
        for h in range(heads):
            cols = slice(h * LANE, (h + 1) * LANE)
            qc, kc, vc = q_ref[rows, cols], k_ref[rows, cols], v_ref[rows, cols]
            vb = vc.astype(BF16)
            s = s_scr[h]
            inner = lax.dot_general(qc.astype(BF16), kc.astype(BF16), (((1,), (1,)), ((), ())),
                                    preferred_element_type=F32) * dm_ref[h]
            o = (jnp.dot(inner.astype(BF16), vb, preferred_element_type=F32)
                 + jnp.dot((qc * qd_ref[h]).astype(BF16), s.astype(BF16), preferred_element_type=F32))
            s_scr[h] = s * cd_ref[h][0:1, :] + lax.dot_general(
                (kc * kd_ref[h]).astype(BF16), vb, (((0,), (0,)), ((), ())), preferred_element_type=F32)
            gg = g_ref[rows, cols]
            o_ref[rows, cols] = gg * jax.nn.sigmoid(gg) * _rms_unit(o)
        for gi in range(groups):
            cols = slice(gi * LANE, (gi + 1) * LANE)
            svn = _layer_norm_rows(sv_ref[rows, cols], lng_ref[:, cols], lnb_ref[:, cols])
            mixed = jnp.dot(wm_ref[gi], svn.astype(BF16), preferred_element_type=F32) + sgb_ref[gi]
            o_ref[rows, heads * LANE + gi * LANE:heads * LANE + (gi + 1) * LANE] = u_ref[rows, cols] * mixed

    @pl.when(t == pl.num_programs(1) - 1)
    def _():
        st_ref[...] = s_scr[...]


def even_seq(p, nseq, seq_len, ln_g, ln_b, sg_w, sg_b, heads, groups, tq=512):
    m = p.shape[0]
    w = heads * LANE
    c = RET_CHUNK
    nt = seq_len // tq
    dm, qd, kd, cd = _ret_tabs(c, c, heads)
    wm = jnp.tril(sg_w[:, :c, :c]).astype(BF16)
    sgb = jnp.broadcast_to(sg_b[:, :c, None], (groups, c, LANE))
    part = lambda j: pl.BlockSpec((tq, w), lambda b, t: (b * nt + t, j))
    full = lambda a: pl.BlockSpec(a.shape, lambda b, t: (0,) * a.ndim)
    return pl.pallas_call(
        functools.partial(_even_seq_body, tq=tq, heads=heads, groups=groups),
        grid=(nseq, nt),
        in_specs=[part(j) for j in range(6)] + [full(dm), full(qd), full(kd), full(cd),
                                                pl.BlockSpec((1, w), lambda b, t: (0, 0)),
                                                pl.BlockSpec((1, w), lambda b, t: (0, 0)),
                                                full(wm), full(sgb)],
        out_specs=[pl.BlockSpec((tq, 2 * w), lambda b, t: (b * nt + t, 0)),
                   pl.BlockSpec((None, heads, LANE, LANE), lambda b, t: (b, 0, 0, 0))],
        out_shape=[jax.ShapeDtypeStruct((m, 2 * w), F32),
                   jax.ShapeDtypeStruct((nseq, heads, LANE, LANE), F32)],
        scratch_shapes=[pltpu.VMEM((heads, LANE, LANE), F32)],
        compiler_params=_params("arbitrary", "arbitrary"),
        name="even_seq",
    )(p, p, p, p, p, p, dm, qd, kd, cd, ln_g.reshape(1, w), ln_b.reshape(1, w), wm, sgb)


def _even_seq_sample_body(*refs, nb, n, heads, groups, has_prev, fill):
    (q_ref, k_ref, v_ref, g_ref, u_ref, sv_ref, s0_ref, dm_ref, qd_ref, kd_ref, cd_ref,
     lng_ref, lnb_ref, wm_ref, sgb_ref, o_ref, st_all_ref, svn_ref) = refs[1:] if has_prev else refs
    np_ = SAMPLE_PAD
    if fill is None:
        st_ref = st_all_ref
    else:
        st_ref = st_all_ref.at[fill[0]]
        for slot in range(fill[1]):
            if slot != fill[0]:
                st_all_ref[slot] = jnp.zeros(st_all_ref.shape[1:], F32)

    def one(b, carry):
        rows = pl.ds(pl.multiple_of(b * np_, np_), np_)
        for h in range(heads):
            cols = slice(h * LANE, (h + 1) * LANE)
            q, k, v = q_ref[rows, cols], k_ref[rows, cols], v_ref[rows, cols]
            s = s0_ref[b, h]
            o = jnp.dot((q * qd_ref[h]).astype(BF16), s.astype(BF16), preferred_element_type=F32)
            dm = dm_ref[h]
            for j in range(n):
                inner = jnp.sum(q * k[j:j + 1, :], axis=-1, keepdims=True)
                o = o + (inner * dm[:, j:j + 1]) * v[j:j + 1, :]
            st_ref[b, h] = s * cd_ref[h][0:1, :] + lax.dot_general(
                (k * kd_ref[h]).astype(BF16), v.astype(BF16), (((0,), (0,)), ((), ())),
                preferred_element_type=F32)
            gg = g_ref[rows, cols]
            o_ref[rows, cols] = gg * jax.nn.sigmoid(gg) * _rms_unit(o)
        for gi in range(groups):
            cols = slice(gi * LANE, (gi + 1) * LANE)
            svn = _layer_norm_rows(sv_ref[rows, cols], lng_ref[:, cols], lnb_ref[:, cols])
            svn_ref[rows, cols] = svn
            wm = wm_ref[gi]
            mixed = sgb_ref[gi]
            for j in range(n):
                mixed = mixed + wm[:, j:j + 1] * svn[j:j + 1, :]
            o_ref[rows, heads * LANE + gi * LANE:heads * LANE + (gi + 1) * LANE] = u_ref[rows, cols] * mixed
        return carry

    lax.fori_loop(0, nb, one, 0)


def even_seq_sample(p, s0, layer, n, ln_g, ln_b, sg_w, sg_b, heads, groups, prev=None, nb=8):
    m = p.shape[0]
    db = m // SAMPLE_PAD
    w = heads * LANE
    dm, qd, kd, cd = _ret_tabs(SAMPLE_PAD, n, heads)
    wm = jnp.pad(jnp.tril(sg_w[:, :n, :n]), ((0, 0), (0, SAMPLE_PAD - n), (0, LANE - n)))
    sgb = jnp.broadcast_to(jnp.pad(sg_b[:, :n], ((0, 0), (0, SAMPLE_PAD - n)))[:, :, None], (groups, SAMPLE_PAD, LANE))
    rows = nb * SAMPLE_PAD
    part = lambda j: pl.BlockSpec((rows, w), lambda i: (i, j))
    full = lambda a: pl.BlockSpec(a.shape, lambda i: (0,) * a.ndim)
    st_spec = pl.BlockSpec((None, nb, heads, LANE, LANE), lambda i: (layer, i, 0, 0, 0))
    has_prev = prev is not None
    n_layers = s0.shape[0]
    fill = None if has_prev else (layer, n_layers)
    st_out = st_spec if has_prev else pl.BlockSpec((n_layers, nb, heads, LANE, LANE), lambda i: (0, i, 0, 0, 0))
    return pl.pallas_call(
        functools.partial(_even_seq_sample_body, nb=nb, n=n, heads=heads, groups=groups, has_prev=has_prev,
                          fill=fill),
        grid=(db // nb,),
        in_specs=([pl.BlockSpec(memory_space=pl.ANY)] if has_prev else [])
        + [part(j) for j in range(6)] + [st_spec, full(dm), full(qd), full(kd), full(cd),
                                         pl.BlockSpec((1, w), lambda i: (0, 0)),
                                         pl.BlockSpec((1, w), lambda i: (0, 0)),
                                         full(wm), full(sgb)],
        out_specs=[pl.BlockSpec((rows, 2 * w), lambda i: (i, 0)), st_out,
                   pl.BlockSpec((rows, w), lambda i: (i, 0))],
        out_shape=[jax.ShapeDtypeStruct((m, 2 * w), F32),
                   jax.ShapeDtypeStruct(s0.shape, F32),
                   jax.ShapeDtypeStruct((m, w), F32)],
        input_output_aliases={0: 1} if has_prev else {},
        compiler_params=_params("parallel"),
        name="even_seq_sample",
    )(*([prev] if has_prev else []), p, p, p, p, p, p, s0, dm, qd, kd, cd, ln_g.reshape(1, w), ln_b.reshape(1, w),
      wm, sgb)


FFN_HALO = 16


def _ffn_body(*refs, tm, tiles_per_seq, sample, nf_static):
    if sample:
        (x_ref, a_ref, wm_ref, gmix_ref, gpre_ref, wa_ref, wb_ref, cwa_ref, cwb_ref, cba_ref, cbb_ref, wo_ref,
         gpost_ref, pa_ref, pb_ref, o_ref, st_ref, h_scr, upa_scr, upb_scr, acc_scr, x1_scr) = refs
    else:
        (x_ref, xh_ref, a_ref, ah_ref, wm_ref, gmix_ref, gpre_ref, wa_ref, wb_ref, cwa_ref, cwb_ref, cba_ref, cbb_ref,
         wo_ref, gpost_ref, o_ref, st_ref, h_scr, upa_scr, upb_scr, acc_scr, x1_scr) = refs
    i = pl.program_id(0)
    f = pl.program_id(1)
    nf = pl.num_programs(1)
    hl = FFN_HALO

    @pl.when(f == 0)
    def _():
        if sample:
            a, x = a_ref[...], x_ref[...]
        else:
            a = jnp.concatenate([ah_ref[...], a_ref[...]], axis=0)
            x = jnp.concatenate([xh_ref[...], x_ref[...]], axis=0)
        x1 = x + _rms(jnp.dot(a.astype(BF16), wm_ref[...], preferred_element_type=F32), gmix_ref[...])
        hn = _rms(x1, gpre_ref[...])
        if sample:
            x1_scr[...] = x1
            h_scr[0:hl, :] = jnp.zeros((hl, h_scr.shape[1]), BF16)
            h_scr[hl:, :] = hn.astype(BF16)
        else:
            x1_scr[...] = x1[hl:, :]
            h_scr[0:hl, :] = jnp.where(i % tiles_per_seq == 0, 0.0, hn[0:hl, :]).astype(BF16)
            h_scr[hl:, :] = hn[hl:, :].astype(BF16)

    h = h_scr[...]
    if sample:
        t = lax.broadcasted_iota(jnp.int32, (tm, 1), 0) % SAMPLE_PAD
        m1 = t >= 1
        m2 = t >= 2

    def conv(up_scr, cw_ref, cb_ref, p_ref):
        s2 = up_scr[pl.ds(hl - 2, tm), :]
        s1 = up_scr[pl.ds(hl - 1, tm), :]
        s0 = up_scr[pl.ds(hl, tm), :]
        if sample:
            p0, p1 = p_ref[:, 0, :][:, None, :], p_ref[:, 1, :][:, None, :]
            t3 = lax.broadcasted_iota(jnp.int32, (tm // SAMPLE_PAD, SAMPLE_PAD, p0.shape[-1]), 1)
            e2 = jnp.where(t3 == 0, p0, jnp.where(t3 == 1, p1, 0.0)).reshape(tm, p0.shape[-1])
            e1 = jnp.where(t3 == 0, p1, 0.0).reshape(tm, p0.shape[-1])
            s2 = jnp.where(m2, s2, 0.0) + e2
            s1 = jnp.where(m1, s1, 0.0) + e1
        return cb_ref[...] + s2 * cw_ref[0:1, :] + s1 * cw_ref[1:2, :] + s0 * cw_ref[2:3, :]

    upa_scr[...] = jnp.dot(h, wa_ref[...], preferred_element_type=F32)
    upb_scr[...] = jnp.dot(h, wb_ref[...], preferred_element_type=F32)
    a = conv(upa_scr, cwa_ref, cba_ref, pa_ref if sample else None)
    b = conv(upb_scr, cwb_ref, cbb_ref, pb_ref if sample else None)
    act = (jax.nn.gelu(a) * b).astype(BF16)
    contrib = jnp.dot(act, wo_ref[...], preferred_element_type=F32)

    @pl.when(f == 0)
    def _():
        acc_scr[...] = contrib

    @pl.when(f > 0)
    def _():
        acc_scr[...] += contrib

    @pl.when(f == nf - 1)
    def _():
        o_ref[...] = x1_scr[...] + _rms(acc_scr[...], gpost_ref[...])

    fw = upa_scr.shape[1]
    rows = st_ref.shape[0]
    last = True if sample else (i % tiles_per_seq == tiles_per_seq - 1)
    for j in range(nf_static):
        @pl.when(jnp.logical_and(f == j, last))
        def _(j=j):
            st_ref[:, j * fw:(j + 1) * fw] = upa_scr[pl.ds(hl + tm - rows, rows), :]
            st_ref[:, (nf_static + j) * fw:(nf_static + j + 1) * fw] = upb_scr[pl.ds(hl + tm - rows, rows), :]


def mix_out_ffn(x, a, w_mix, g_mix, seq_len, gpre, w_in, conv_w, conv_b, w_out, gpost, layer, prev=None, tm=512):
    m, d = x.shape
    ka = a.shape[1]
    ff = w_out.shape[1]
    fw = 1408 if ff % 1408 == 0 else ff
    nf = ff // fw
    sample = prev is not None
    tm = min(tm, m)
    tps = max(seq_len // tm, 1)
    row = lambda i, f: (i, 0)
    const = lambda i, f: (0, 0)
    cola = lambda i, f: (layer, 0, f)
    colb = lambda i, f: (layer, 0, nf + f)
    hb = tm // FFN_HALO
    halo = lambda i, f: (jnp.maximum(i * hb - 1, 0), 0)
    if sample:
        in_specs = [pl.BlockSpec((tm, d), row), pl.BlockSpec((tm, ka), row)]
        args = [x, a]
    else:
        in_specs = [pl.BlockSpec((tm, d), row), pl.BlockSpec((FFN_HALO, d), halo),
                    pl.BlockSpec((tm, ka), row), pl.BlockSpec((FFN_HALO, ka), halo)]
        args = [x, x, a, a]
    in_specs += [pl.BlockSpec((ka, d), const), pl.BlockSpec((1, d), const)]
    args += [w_mix, g_mix.reshape(1, d)]
    in_specs += [pl.BlockSpec((1, d), const),
                 pl.BlockSpec((None, d, fw), cola), pl.BlockSpec((None, d, fw), colb),
                 pl.BlockSpec((None, CONV_W, fw), cola), pl.BlockSpec((None, CONV_W, fw), colb),
                 pl.BlockSpec((None, 1, fw), cola), pl.BlockSpec((None, 1, fw), colb),
                 pl.BlockSpec((None, fw, d), lambda i, f: (layer, f, 0)),
                 pl.BlockSpec((1, d), const)]
    args += [gpre.reshape(1, d), w_in, w_in, conv_w, conv_w, conv_b, conv_b, w_out, gpost.reshape(1, d)]
    if sample:
        nsq = tm // SAMPLE_PAD
        in_specs += [pl.BlockSpec((None, nsq, CONV_W - 1, fw), lambda i, f: (layer, i, 0, f)),
                     pl.BlockSpec((None, nsq, CONV_W - 1, fw), lambda i, f: (layer, i, 0, nf + f))]
        args += [prev, prev]
        st_shape = jax.ShapeDtypeStruct((m, 2 * ff), F32)
        st_spec = pl.BlockSpec((tm, 2 * ff), lambda i, f: (i, 0))
    else:
        nseq = m // seq_len
        st_shape = jax.ShapeDtypeStruct((nseq * 8, 2 * ff), F32)
        st_spec = pl.BlockSpec((8, 2 * ff), lambda i, f: (i // tps, 0))
    return pl.pallas_call(
        functools.partial(_ffn_body, tm=tm, tiles_per_seq=tps, sample=sample, nf_static=nf),
        grid=(m // tm, nf),
        in_specs=in_specs,
        out_specs=[pl.BlockSpec((tm, d), row), st_spec],
        out_shape=[jax.ShapeDtypeStruct((m, d), F32), st_shape],
        scratch_shapes=[pltpu.VMEM((tm + FFN_HALO, d), BF16),
                        pltpu.VMEM((tm + FFN_HALO, fw), F32),
                        pltpu.VMEM((tm + FFN_HALO, fw), F32),
                        pltpu.VMEM((tm, d), F32),
                        pltpu.VMEM((tm, d), F32)],
        compiler_params=_params("arbitrary", "arbitrary"),
        name="conv_ffn_sample" if sample else "conv_ffn",
    )(*args)


def _odd_modes():
    return [ROPE_NSA] * 8 + [ROPE_NSA, ROPE_NSA, ROPE_NONE, ROPE_NONE] * 3 + [ROPE_NONE]


SEL_TK = 512
SUM_ROWS = 16
WIN_TK = 128


def _odd_outs(prompt):
    if not prompt:
        return [(0, 1024, ATTN_SCALE, F32, None), (1024, 512, 1.0, F32, None), (1536, 512, 1.0, F32, None),
                (2048, 512, 1.0, F32, None), (2560, 128, 1.0, F32, None)]
    return [(0, 1024, ATTN_SCALE * LOG2E, BF16, 0), (1024, 512, 1.0, F32, None), (1024, 512, 1.0, F32, 0),
            (1536, 512, 1.0, F32, 0), (2048, 512, 1.0, F32, 0), (2560, 128, 1.0, F32, 0),
            (1536, 256, 1.0, BF16, PAD_HEADS), (1792, 256, 1.0, BF16, SEL_TK),
            (2048, 256, 1.0, BF16, None), (2304, 256, 1.0, BF16, WIN_TK)]


SUBS = CMP_LEN // CMP_STRIDE
ROW_W = 2 * NSA_KV * HEAD_DIM
SUB_W = CMP_STRIDE * ROW_W


def _compress_weights(phi_k, phi_v, pe_k, pe_v):
    def one(phi, pe):
        p4 = phi.reshape(SUBS, CMP_STRIDE, HEAD_DIM, HEAD_DIM)
        w = jnp.einsum('hlde,gG->lgdhGe', p4, jnp.eye(2, dtype=phi.dtype))
        w = w.reshape(CMP_STRIDE * 2 * HEAD_DIM, SUBS * 2 * HEAD_DIM)
        pr = jnp.broadcast_to(pe.reshape(SUBS, CMP_STRIDE, 1, HEAD_DIM), (SUBS, CMP_STRIDE, 2, HEAD_DIM))
        pr = jnp.pad(pr.reshape(SUBS, -1), ((0, 16 - SUBS), (0, 0)))
        return w, pr
    wk, pk = one(phi_k, pe_k)
    wv, pv = one(phi_v, pe_v)
    return jnp.stack([wk, wv]).astype(BF16), jnp.stack([pk, pv]).astype(BF16)


def _compress_column(xj, j, w_ref, pe_ref, o_ref, ab_scr):
    nsub = o_ref.shape[0]
    half = LANE
    xe = jnp.concatenate([xj, pe_ref[j // 2]], axis=0)
    ab_scr[...] = jnp.dot(xe, w_ref[j // 2], preferred_element_type=F32)
    bias = ab_scr[nsub:nsub + 1, 0:half] + ab_scr[nsub + 1:nsub + 2, half:2 * half]
    o_ref[:, j * LANE:(j + 1) * LANE] = (
        ab_scr[0:nsub, 0:half] + ab_scr[pl.ds(1, nsub), half:2 * half] + bias).astype(o_ref.dtype)


def _compress_body(x_ref, w_ref, pe_ref, o_ref, ab_scr):
    for j in range(ROW_W // LANE):
        xj = jnp.concatenate([x_ref[:, l * ROW_W + j * LANE:l * ROW_W + (j + 1) * LANE]
                              for l in range(CMP_STRIDE)], axis=1).astype(BF16)
        _compress_column(xj, j, w_ref, pe_ref, o_ref, ab_scr)


def _compress_pages_body(pt_ref, *refs, n_x):
    x_refs = refs[:n_x]
    perm_ref, w_ref, pe_ref, o_ref, ab_scr, xs_scr = refs[n_x:]
    page = x_refs[0].shape[-1]
    sub_pp = page // CMP_STRIDE
    kvw = NSA_KV * HEAD_DIM
    for k, xr in enumerate(x_refs):
        for c in range(2):
            t = xr[c].reshape(kvw, page).astype(BF16)
            out = lax.dot_general(perm_ref[...], t, (((1,), (1,)), ((), ())), preferred_element_type=F32)
            for gp in range(kvw // LANE):
                for l in range(CMP_STRIDE):
                    xs_scr[c * (kvw // LANE) + gp, k * sub_pp:(k + 1) * sub_pp, l * LANE:(l + 1) * LANE] = (
                        out[l * sub_pp:(l + 1) * sub_pp, gp * LANE:(gp + 1) * LANE])
    for j in range(ROW_W // LANE):
        _compress_column(xs_scr[j].astype(BF16), j, w_ref, pe_ref, o_ref, ab_scr)


def compress_prompt(rows, nseq, wts, pes):
    nsub = rows.shape[0] // nseq // CMP_STRIDE
    x = rows.reshape(nseq * nsub, SUB_W)
    return pl.pallas_call(
        _compress_body,
        grid=(nseq,),
        in_specs=[pl.BlockSpec((nsub, SUB_W), lambda b: (b, 0)),
                  pl.BlockSpec(wts.shape, lambda b: (0, 0, 0)),
                  pl.BlockSpec(pes.shape, lambda b: (0, 0, 0))],
        out_specs=pl.BlockSpec((nsub, ROW_W), lambda b: (b, 0)),
        out_shape=jax.ShapeDtypeStruct((nseq * nsub, ROW_W), BF16),
        scratch_shapes=[pltpu.VMEM((nsub + 16, 2 * LANE), F32)],
        compiler_params=_params("parallel"),
        name="compress_prompt",
    )(x, wts, pes)


def compress_pages(pool, layer, page_table, wts, pes, nb=2):
    db, n_pages = page_table.shape
    page = pool.shape[-1]
    sub_pp = page // CMP_STRIDE
    nsub = nb * n_pages * sub_pp
    out_row = jnp.arange(page)
    perm = (jnp.arange(page)[None, :] == ((out_row % sub_pp) * CMP_STRIDE + out_row // sub_pp)[:, None]).astype(BF16)
    specs = [pl.BlockSpec((None, None) + pool.shape[2:],
                          functools.partial(lambda i, pt, s, k: (layer, pt[(i * nb + s) * n_pages + k], 0, 0, 0, 0),
                                            s=s, k=k))
             for s in range(nb) for k in range(n_pages)]
    grid_spec = pltpu.PrefetchScalarGridSpec(
        num_scalar_prefetch=1,
        grid=(db // nb,),
        in_specs=specs + [pl.BlockSpec(perm.shape, lambda i, pt: (0, 0)),
                          pl.BlockSpec(wts.shape, lambda i, pt: (0, 0, 0)),
                          pl.BlockSpec(pes.shape, lambda i, pt: (0, 0, 0))],
        out_specs=pl.BlockSpec((nsub, ROW_W), lambda i, pt: (i, 0)),
        scratch_shapes=[pltpu.VMEM((nsub + 16, 2 * LANE), F32),
                        pltpu.VMEM((ROW_W // LANE, nsub, CMP_STRIDE * LANE), F32)],
    )
    return pl.pallas_call(
        functools.partial(_compress_pages_body, n_x=nb * n_pages),
        grid_spec=grid_spec,
        out_shape=jax.ShapeDtypeStruct((db * n_pages * sub_pp, ROW_W), BF16),
        compiler_params=_params("parallel"),
        name="compress_pages",
    )(page_table.reshape(-1), *([pool] * (nb * n_pages)), perm, wts, pes)


def _cover(nsub, nc, n_sel):
    c_start = jnp.arange(nsub) * CMP_STRIDE
    s_start = jnp.arange(SEL_BLK) * SEL_BLK
    ok = ((c_start[:, None] < s_start[None, :] + SEL_BLK) & (c_start[:, None] + CMP_LEN > s_start[None, :])
          & (jnp.arange(nsub)[:, None] < nc) & (jnp.arange(SEL_BLK)[None, :] < n_sel))
    return ok.astype(BF16)


def _expand(n_keys, tk):
    key = jnp.arange(n_keys).reshape(n_keys // tk, 1, tk)
    return (key // SEL_BLK == jnp.arange(SEL_BLK)[None, :, None]).astype(BF16)


def _masked_softmax(s, ok):
    sm = jnp.where(ok, s, NEG)
    ex = jnp.exp(sm - jnp.max(sm, -1, keepdims=True))
    return ex / jnp.sum(ex, -1, keepdims=True)


def _importance(psum, cov_ref, tpos, n_sel):
    hi = psum.astype(BF16)
    lo = (psum - hi.astype(F32)).astype(BF16)
    imp = (jnp.dot(hi, cov_ref[...], preferred_element_type=F32)
           + jnp.dot(lo, cov_ref[...], preferred_element_type=F32))
    jl = lax.broadcasted_iota(jnp.int32, imp.shape, 1)
    cur = tpos // SEL_BLK
    forced = (jl == 0) | (jl == cur) | (jl == cur - 1)
    causal = jl * SEL_BLK <= tpos
    imp = jnp.where(forced, FORCE, jnp.where(causal, imp, -1.0))
    return jnp.where(jl < n_sel, imp, -2.0)


def _topk_masks_rows(imps):
    out = []
    for x in imps:
        jl = lax.broadcasted_iota(jnp.int32, x.shape, 1)
        cnt = jnp.zeros(x.shape, F32)
        for i in range(SEL_BLK):
            xi = x[:, i:i + 1]
            tie = jnp.where(jl > i, 1.0, 0.0)
            cnt = cnt + jnp.where(xi > x, 1.0, jnp.where(xi == x, tie, 0.0))
        out.append(jnp.where(cnt < SEL_TOPK, 1.0, 0.0))
    return out


_NT = (((1,), (1,)), ((), ()))


def _softmax_cols(s):
    p = jnp.exp2(s - jnp.max(s, axis=0, keepdims=True))
    return p, jnp.sum(p, axis=0, keepdims=True)


def _run_skewed(tasks):
    n = len(tasks)
    tasks[0][0]()
    for i in range(n):
        if i + 1 < n:
            tasks[i + 1][0]()
        tasks[i][1]()
        if i >= 1:
            tasks[i - 1][2]()
    tasks[n - 1][2]()


def _nsa_prompt_lanes_body(q_ref, kc_ref, ka_ref, vs_ref, kw_ref, vw_ref, gt_ref, cov_ref, et_ref, o_ref, *,
                           tq, nc, n_sel):
    t0 = pl.program_id(1) * tq
    tl = t0 + lax.broadcasted_iota(jnp.int32, (1, tq), 1)
    sig = jax.nn.sigmoid(gt_ref[...])
    nsub = kc_ref.shape[0]
    rep, hd, ng = NSA_REP, HEAD_DIM, NSA_KV
    nh = ng * rep
    tk = vs_ref.shape[-1]
    wt = vw_ref.shape[-1]
    wide = lambda x: jnp.concatenate([x] * rep, axis=1)
    head_rows = lambda g, r: q_ref[(g * rep + r) * hd:(g * rep + r + 1) * hd, :]
    q4 = [jnp.concatenate([head_rows(g, r) for r in range(rep)], axis=1) for g in range(ng)]

    mrow = lax.broadcasted_iota(jnp.int32, (nsub, 1), 0)
    validf = wide(jnp.where((mrow * CMP_STRIDE + CMP_LEN - 1 <= tl) & (mrow < nc), 1.0, 0.0))
    jrow = lax.broadcasted_iota(jnp.int32, (SEL_BLK, tq), 0)
    cur = tl // SEL_BLK
    forced = (jrow == 0) | (jrow == cur) | (jrow == cur - 1)
    causal = jrow * SEL_BLK <= tl
    s_cmp, p_cmp, o_cmp, sel_neg = [None] * ng, [None] * ng, [None] * ng, [None] * ng

    def cmp_scores(g):
        s_cmp[g] = jnp.dot(kc_ref[:, g * hd:(g + 1) * hd], q4[g], preferred_element_type=F32)

    def cmp_softmax(g):
        ex, l = _softmax_cols(jnp.where(validf > 0.5, s_cmp[g], NEG))
        p = ex / l * validf
        p_cmp[g] = p.astype(BF16)
        psum = p[:, 0:tq]
        for r in range(1, rep):
            psum = psum + p[:, r * tq:(r + 1) * tq]
        hi = psum.astype(BF16)
        lo = (psum - hi.astype(F32)).astype(BF16)
        imp = (jnp.dot(cov_ref[...], hi, preferred_element_type=F32)
               + jnp.dot(cov_ref[...], lo, preferred_element_type=F32))
        imp = jnp.where(forced, FORCE, jnp.where(causal, imp, -1.0))
        imp = jnp.where(jrow < n_sel, imp, -2.0)
        sub = 8
        slabs = [imp[v * sub:(v + 1) * sub, :] for v in range(SEL_BLK // sub)]
        cnts = [jnp.zeros((sub, tq), F32) for _ in slabs]
        srow = lax.broadcasted_iota(jnp.int32, (sub, tq), 0)
        for i in range(SEL_BLK):
            xi = jnp.broadcast_to(imp[i:i + 1, :], (sub, tq))
            for v, x in enumerate(slabs):
                if v > i // sub:
                    hit = jnp.where(xi >= x, 1.0, 0.0)
                elif v < i // sub:
                    hit = jnp.where(xi > x, 1.0, 0.0)
                else:
                    tie = jnp.where(srow > i % sub, 1.0, 0.0)
                    hit = jnp.where(xi > x, 1.0, jnp.where(xi == x, tie, 0.0))
                cnts[v] = cnts[v] + hit
        cnt = jnp.concatenate(cnts, axis=0)
        sel_neg[g] = jnp.where(cnt < SEL_TOPK, 0.0, NEG).astype(BF16)

    def cmp_values(g):
        vc_t = kc_ref[:, ng * hd + g * hd:ng * hd + (g + 1) * hd].astype(F32).T.astype(BF16)
        o_cmp[g] = jnp.dot(vc_t, p_cmp[g], preferred_element_type=F32)

    n_wt = (WINDOW + tq) // wt
    w0 = jnp.maximum(t0 - WINDOW, 0)
    wt0 = w0 // wt
    wrows = n_wt * wt
    kpos_w = w0 + lax.broadcasted_iota(jnp.int32, (wrows, 1), 0)
    bias_w = wide(jnp.where((kpos_w <= tl) & (kpos_w > tl - WINDOW), 0.0, NEG))
    s_win, p_win, o_win = [None] * ng, [None] * ng, [None] * ng

    def ones_row(n):
        return jnp.where(lax.broadcasted_iota(jnp.int32, (SUM_ROWS, n), 0) == 0, 1.0, 0.0).astype(BF16)

    def win_scores(g):
        k_w = kw_ref[pl.ds(pl.multiple_of(w0, wt), wrows), g * hd:(g + 1) * hd]
        s_win[g] = jnp.dot(k_w, q4[g], preferred_element_type=F32) + bias_w

    def win_softmax(g):
        p_win[g] = jnp.exp2(s_win[g] - jnp.max(s_win[g], axis=0, keepdims=True)).astype(BF16)

    def win_values(g):
        v_w = jnp.concatenate([vw_ref[wt0 + i, g * hd:(g + 1) * hd, :] for i in range(n_wt)], axis=1)
        o = jnp.dot(jnp.concatenate([v_w, ones_row(wrows)], axis=0), p_win[g], preferred_element_type=F32)
        o_win[g] = o[0:hd] / o[hd:hd + 1]

    q_aug = [None] * ng

    def keys(kt, g):
        rows = pl.ds(pl.multiple_of(kt * tk, tk), tk)
        return ka_ref[rows, g * LANE:(g + 1) * LANE] + et_ref[rows, :]

    def sweep_tasks(kt, carry, out, bias):
        s, soft = [None] * ng, [None] * ng

        def scores(g):
            if q_aug[g] is None:
                q_aug[g] = jnp.concatenate([jnp.concatenate([head_rows(g, r), sel_neg[g]], axis=0)
                                            for r in range(rep)], axis=1)
            s[g] = jnp.dot(keys(kt, g), q_aug[g], preferred_element_type=F32)
            if bias is not None:
                s[g] = s[g] + bias

        def softmax(g):
            m_i = carry[g][0]
            m_new = jnp.maximum(m_i, jnp.max(s[g], axis=0, keepdims=True))
            soft[g] = (m_new, jnp.exp2(m_i - m_new), jnp.exp2(s[g] - m_new).astype(BF16))

        def values(g):
            m_new, alpha, p = soft[g]
            v_aug = jnp.concatenate([vs_ref[kt, g * hd:(g + 1) * hd, :], ones_tk], axis=0)
            out[g] = (m_new, alpha * carry[g][1] + jnp.dot(v_aug, p, preferred_element_type=F32))

        return [(functools.partial(scores, g), functools.partial(softmax, g), functools.partial(values, g))
                for g in range(ng)]

    kt_d = t0 // tk
    kpos_d = kt_d * tk + lax.broadcasted_iota(jnp.int32, (tk, 1), 0)
    bias_d = wide(jnp.where(kpos_d <= tl, 0.0, NEG))
    ones_tk = ones_row(tk)
    empty = (jnp.full((1, rep * tq), NEG, F32), jnp.zeros((hd + SUM_ROWS, rep * tq), F32))
    init = [None] * ng
    tasks = []
    for g in range(ng):
        tasks.append((functools.partial(cmp_scores, g), functools.partial(cmp_softmax, g),
                      functools.partial(cmp_values, g)))
        tasks.append((functools.partial(win_scores, g), functools.partial(win_softmax, g),
                      functools.partial(win_values, g)))
    _run_skewed(tasks + sweep_tasks(kt_d, [empty] * ng, init, bias_d))

    def step(kt, carry):
        out = [None] * ng
        _run_skewed(sweep_tasks(kt, carry, out, None))
        return tuple(out)

    final = lax.fori_loop(0, kt_d, step, tuple(init))

    for g in range(ng):
        acc_s = final[g][1]
        o_sel = acc_s[0:hd] / acc_s[hd:hd + 1]
        heads = []
        for r in range(rep):
            h = g * rep + r
            cols = slice(r * tq, (r + 1) * tq)
            heads.append(sig[h:h + 1, :] * o_cmp[g][:, cols] + sig[nh + h:nh + h + 1, :] * o_sel[:, cols]
                         + sig[2 * nh + h:2 * nh + h + 1, :] * o_win[g][:, cols])
        for pair in range(rep // 2):
            o_ref[:, (g * rep + 2 * pair) * hd:(g * rep + 2 * pair + 2) * hd] = (
                jnp.concatenate([heads[2 * pair], heads[2 * pair + 1]], axis=0).T)


def nsa_prompt_t(q_t, kcvc, ka, vs_t, kw, vw_t, gates_t, nseq, seq_len, tq=128):
    m = ka.shape[0]
    nsub = kcvc.shape[0] // nseq
    tk, wt = vs_t.shape[-1], vw_t.shape[-1]
    nc = (seq_len - CMP_LEN) // CMP_STRIDE + 1
    n_sel = seq_len // SEL_BLK
    assert n_sel <= SEL_BLK and seq_len >= WINDOW + tq and tk % tq == 0 and tq % wt == 0 and WINDOW % wt == 0
    nq = seq_len // tq
    width = q_t.shape[1]
    cov_t = _cover(nsub, nc, n_sel).T
    key = jnp.arange(seq_len)
    et = jnp.concatenate([jnp.zeros((seq_len, HEAD_DIM), BF16),
                          (key[:, None] // SEL_BLK == jnp.arange(SEL_BLK)[None, :]).astype(BF16)], axis=1)
    cols = lambda a: pl.BlockSpec((None, a.shape[1], tq), lambda b, t: (b, 0, t))
    seq_rows = lambda a: pl.BlockSpec((seq_len, a.shape[1]), lambda b, t: (b, 0))
    whole = lambda a: pl.BlockSpec((None,) + a.shape[1:], lambda b, t: (b, 0, 0, 0))
    return pl.pallas_call(
        functools.partial(_nsa_prompt_lanes_body, tq=tq, nc=nc, n_sel=n_sel),
        grid=(nseq, nq),
        in_specs=[cols(q_t), pl.BlockSpec((nsub, ROW_W), lambda b, t: (b, 0)), seq_rows(ka), whole(vs_t),
                  seq_rows(kw), whole(vw_t), cols(gates_t),
                  pl.BlockSpec(cov_t.shape, lambda b, t: (0, 0)),
                  pl.BlockSpec(et.shape, lambda b, t: (0, 0))],
        out_specs=pl.BlockSpec((tq, width), lambda b, t: (b * nq + t, 0)),
        out_shape=jax.ShapeDtypeStruct((m, width), F32),
        compiler_params=_params("parallel", "parallel"),
        name="nsa_prompt",
    )(q_t, kcvc, ka, vs_t, kw, vw_t, gates_t, cov_t, et)


def _joint_softmax(s_a, ok_a, s_b, ok_b):
    s_a = jnp.where(ok_a, s_a, NEG)
    s_b = jnp.where(ok_b, s_b, NEG)
    mx = jnp.maximum(jnp.max(s_a, -1, keepdims=True), jnp.max(s_b, -1, keepdims=True))
    e_a = jnp.exp(s_a - mx)
    e_b = jnp.exp(s_b - mx)
    return e_a, e_b, jnp.sum(e_a, -1, keepdims=True) + jnp.sum(e_b, -1, keepdims=True)


def _nsa_sample_body(pt_ref, q_ref, kc_ref, *rest, n, n_pages, past_len, nc, n_sel, fill):
    page_refs = rest[:n_pages]
    ns_ref, nw_ref, win_ref, gt_ref, cov_ref, e_ref = rest[n_pages:n_pages + 6]
    o_ref, wo_ref = rest[-2:]
    if fill is not None:
        for slot in range(fill[1]):
            if slot != fill[0]:
                wo_ref[slot] = jnp.zeros(wo_ref.shape[1:], F32)
        wo_ref = wo_ref.at[fill[0]]
    np_, rep, hd, kvw, nkv = SAMPLE_PAD, NSA_REP, HEAD_DIM, NSA_KV * HEAD_DIM, NSA_KV
    tpos = past_len + lax.broadcasted_iota(jnp.int32, (np_, 1), 0)
    sig = jax.nn.sigmoid(gt_ref[...])
    q = q_ref[...]

    blocks = []
    for g in range(nkv):
        for r in range(rep):
            h = g * rep + r
            parts = [q[:, h * hd:(h + 1) * hd]]
            if g > 0:
                parts.insert(0, jnp.zeros((np_, g * hd), F32))
            if g < nkv - 1:
                parts.append(jnp.zeros((np_, (nkv - 1 - g) * hd), F32))
            blocks.append(jnp.concatenate(parts, axis=1))
    qbd = jnp.concatenate(blocks, axis=0).astype(BF16)
    nrow = nkv * rep * np_

    def grp(x):
        return x.reshape(nkv, rep, np_, x.shape[-1])

    def scores(keys):
        return grp(lax.dot_general(qbd, keys, _NT, preferred_element_type=F32))

    def scores_t(keys_t):
        return grp(jnp.dot(qbd, keys_t, preferred_element_type=F32))

    new_lane = lax.broadcasted_iota(jnp.int32, (np_, np_), 1)
    new_pos = past_len + new_lane

    nsub = kc_ref.shape[0]
    mcol = lax.broadcasted_iota(jnp.int32, (np_, nsub), 1)
    valid = (mcol * CMP_STRIDE + CMP_LEN - 1 <= tpos) & (mcol < nc)
    p = _masked_softmax(scores(kc_ref[:, 0:kvw]), valid[None, None]) * jnp.where(valid, 1.0, 0.0)[None, None]
    o_cmp = jnp.dot(p.reshape(nrow, nsub).astype(BF16), kc_ref[:, kvw:2 * kvw], preferred_element_type=F32)
    sels = _topk_masks_rows([_importance(jnp.sum(p[g], axis=0), cov_ref, tpos, n_sel) for g in range(nkv)])

    page = page_refs[0].shape[-1]
    k_all = jnp.concatenate([pr[0].reshape(kvw, page) for pr in page_refs], axis=1).astype(BF16)
    v_all = jnp.concatenate([pr[1].reshape(kvw, page) for pr in page_refs], axis=1).astype(BF16)
    kpos = lax.broadcasted_iota(jnp.int32, (np_, past_len), 1)
    ok_c = jnp.stack([jnp.where(kpos <= tpos, jnp.dot(sels[g].astype(BF16), e_ref[0], preferred_element_type=F32), 0.0)
                      for g in range(nkv)]) > 0.5
    new_ok = (new_pos <= tpos) & (new_lane < n)
    ok_n = jnp.stack([jnp.where(new_ok, jnp.concatenate(
        [sels[g][:, (past_len + i) // SEL_BLK:(past_len + i) // SEL_BLK + 1] for i in range(np_)], axis=1), 0.0)
        for g in range(nkv)]) > 0.5
    ns = ns_ref[...]
    e_c, e_n, l_s = _joint_softmax(scores_t(k_all), ok_c[:, None], scores(ns[:, 0:kvw].astype(BF16)), ok_n[:, None])
    o_sel = lax.dot_general(e_c.reshape(nrow, past_len).astype(BF16), v_all, _NT, preferred_element_type=F32)
    e_n = e_n.reshape(nrow, np_)
    for i in range(n):
        o_sel = o_sel + e_n[:, i:i + 1] * ns[i:i + 1, kvw:2 * kvw]
    o_sel = o_sel / l_s.reshape(nrow, 1)

    w_buf = win_ref.shape[-1]
    k_w = win_ref[0].reshape(kvw, w_buf)
    v_w = win_ref[1].reshape(kvw, w_buf)
    kpos_w = past_len - w_buf + lax.broadcasted_iota(jnp.int32, (np_, w_buf), 1)
    ok_w = (kpos_w <= tpos) & (kpos_w > tpos - WINDOW)
    ok_wn = (new_pos <= tpos) & (new_pos > tpos - WINDOW) & (new_lane < n)
    nw = nw_ref[...]
    e_w, e_wn, l_w = _joint_softmax(scores_t(k_w.astype(BF16)), ok_w[None, None],
                                    scores(nw[:, 0:kvw].astype(BF16)), ok_wn[None, None])
    o_win = lax.dot_general(e_w.reshape(nrow, w_buf).astype(BF16), v_w.astype(BF16), _NT,
                            preferred_element_type=F32)
    e_wn = e_wn.reshape(nrow, np_)
    for i in range(n):
        o_win = o_win + e_wn[:, i:i + 1] * nw[i:i + 1, kvw:2 * kvw]
    o_win = o_win / l_w.reshape(nrow, 1)

    nh = nkv * rep
    for g in range(nkv):
        for r in range(rep):
            h = g * rep + r
            rows = slice(h * np_, (h + 1) * np_)
            cols = slice(g * hd, (g + 1) * hd)
            o_ref[:, h * hd:(h + 1) * hd] = (sig[:, h:h + 1] * o_cmp[rows, cols]
                                            + sig[:, nh + h:nh + h + 1] * o_sel[rows, cols]
                                            + sig[:, 2 * nh + h:2 * nh + h + 1] * o_win[rows, cols])

    body = w_buf - LANE
    for c, old in enumerate((k_w, v_w)):
        shifted = pltpu.roll(old, w_buf - n, 1)
        new_t = jnp.concatenate([nw[:, c * kvw:(c + 1) * kvw].T, jnp.zeros((kvw, LANE - np_), F32)], axis=1)
        lane = lax.broadcasted_iota(jnp.int32, (kvw, LANE), 1)
        wo_ref[c * kvw:(c + 1) * kvw, 0:body] = shifted[:, 0:body]
        wo_ref[c * kvw:(c + 1) * kvw, body:w_buf] = jnp.where(lane < LANE - n, shifted[:, body:w_buf],
                                                              pltpu.roll(new_t, LANE - n, 1))


def nsa_sample(q, kcvc, pool, layer, page_table, rows_s, rows_w, win, gates, n, prev=None):
    db, n_pages = page_table.shape
    page = pool.shape[-1]
    past_len = n_pages * page
    nsub = kcvc.shape[0] // db
    w_buf = win.shape[-1]
    assert w_buf == WINDOW and n <= SAMPLE_PAD and past_len % SEL_BLK == 0
    nc = (past_len + n - CMP_LEN) // CMP_STRIDE + 1
    n_sel = -(-(past_len + n) // SEL_BLK)
    assert n_sel <= SEL_BLK and nc <= nsub
    cov = _cover(nsub, nc, n_sel)
    exp = _expand(past_len, past_len)
    rows8 = lambda w: pl.BlockSpec((SAMPLE_PAD, w), lambda i, pt: (i, 0))
    pages = [pl.BlockSpec((None, None) + pool.shape[2:],
                          functools.partial(lambda i, pt, k: (layer, pt[i * n_pages + k], 0, 0, 0, 0), k=k))
             for k in range(n_pages)]
    grid_spec = pltpu.PrefetchScalarGridSpec(
        num_scalar_prefetch=1,
        grid=(db,),
        in_specs=[rows8(q.shape[1]), pl.BlockSpec((nsub, ROW_W), lambda i, pt: (i, 0))] + pages
        + [rows8(ROW_W), rows8(ROW_W),
           pl.BlockSpec((None, None) + win.shape[2:], lambda i, pt: (layer, i, 0, 0, 0, 0)),
           rows8(LANE),
           pl.BlockSpec(cov.shape, lambda i, pt: (0, 0)),
           pl.BlockSpec(exp.shape, lambda i, pt: (0, 0, 0))]
        + ([pl.BlockSpec(memory_space=pl.ANY)] if prev is not None else []),
        out_specs=[rows8(q.shape[1]),
                   pl.BlockSpec((None, None, ROW_W, w_buf), lambda i, pt: (layer, i, 0, 0)) if prev is not None
                   else pl.BlockSpec((win.shape[0], None, ROW_W, w_buf), lambda i, pt: (0, i, 0, 0))],
    )
    args = (page_table.reshape(-1), q, kcvc, *([pool] * n_pages), rows_s, rows_w, win, gates, cov, exp)
    return pl.pallas_call(
        functools.partial(_nsa_sample_body, n=n, n_pages=n_pages, past_len=past_len, nc=nc, n_sel=n_sel,
                          fill=None if prev is not None else (layer, win.shape[0])),
        grid_spec=grid_spec,
        out_shape=[jax.ShapeDtypeStruct(q.shape, F32),
                   jax.ShapeDtypeStruct((win.shape[0], db, ROW_W, w_buf), F32)],
        input_output_aliases={len(args): 1} if prev is not None else {},
        compiler_params=_params("parallel"),
        name="nsa_sample",
    )(*args, *([prev] if prev is not None else []))


def kernel(x_prompt, x_sample, state_ret, cache_cmp, cache_sel, state_win, state_ffn, page_table, norm_mix_pre, norm_mix_post, norm_ffn_pre, norm_ffn_post, e_w_in, e_w_out, e_sg_ln_g, e_sg_ln_b, e_sg_w, e_sg_b, o_w_in, o_w_out, o_pe_k, o_pe_v, o_phi_k, o_phi_v, f_w_in, f_conv_w, f_conv_b, f_w_out):
    b, s, d = x_prompt.shape
    db, n, _ = x_sample.shape
    depth = norm_mix_pre.shape[0]
    heads = state_ret.shape[2]
    groups = e_sg_w.shape[1]
    n_pages = page_table.shape[1]
    page = cache_cmp.shape[2]
    past_len = n_pages * page
    pad = SAMPLE_PAD
    assert n <= pad and n >= CONV_W - 1 and n < CMP_STRIDE and state_ret.shape[3] == LANE

    xp = x_prompt.reshape(b * s, d)
    xs = jnp.pad(x_sample, ((0, 0), (0, pad - n), (0, 0))).reshape(db * pad, d)
    pos_p = jnp.arange(s, dtype=jnp.int32)
    pos_s = jnp.tile(past_len + jnp.arange(pad, dtype=jnp.int32), db)
    ret_p, ret_s = _ret_rope_tabs(pos_p), _ret_rope_tabs(pos_s)
    nsa_p, nsa_s = _nsa_rope_tabs(pos_p), _nsa_rope_tabs(pos_s)
    even_modes = [ROPE_RET_Q] * heads + [ROPE_RET_K] * heads + [ROPE_NONE] * (4 * heads)
    even_outs = [(0, 6 * heads * LANE, 1.0, F32, None)]
    kscale = LANE ** -0.5
    odd_cols = o_w_in.shape[2]
    odd_pad = -(-odd_cols // LANE) * LANE
    rows_last = lambda a: jnp.transpose(a, (0, 1, 3, 4, 5, 2))
    rows_first = lambda a: jnp.transpose(a.reshape(a.shape[:2] + kv_shape + (a.shape[-1],)), (0, 1, 5, 2, 3, 4))
    pool_c, pool_s, win = rows_last(cache_cmp), rows_last(cache_sel), rows_last(state_win)
    kv_shape = cache_cmp.shape[3:]
    keep = min(WINDOW, s)

    out = {k: [] for k in ("ret_p", "sgv", "cmp_s", "sel_s", "win_p", "ffn_p", "ffn_s")}
    ret_s_all = cmp_all = sel_all = win_all = None
    n_odd = o_w_in.shape[0]
    ffn_w_in, ffn_w_out = f_w_in.astype(BF16), f_w_out.astype(BF16)
    ffn_cb = f_conv_b[:, None, :]
    for l in range(depth):
        i = l // 2
        if l % 2 == 0:
            w_in = e_w_in[i].astype(BF16)
            w_out = e_w_out[i].astype(BF16)
            (pp,) = proj_in(xp, norm_mix_pre[l], w_in, (ret_p[0], ret_p[1], ret_p[1]), even_modes, even_outs, kscale)
            (ps,) = proj_in(xs, norm_mix_pre[l], w_in, (ret_s[0], ret_s[1], ret_s[1]), even_modes, even_outs, kscale)
            cat_p, st_p = even_seq(pp, b, s, e_sg_ln_g[i], e_sg_ln_b[i], e_sg_w[i], e_sg_b[i], heads, groups)
            cat_s, ret_s_all, svn_s = even_seq_sample(ps, state_ret, i, n, e_sg_ln_g[i], e_sg_ln_b[i], e_sg_w[i],
                                                      e_sg_b[i], heads, groups, prev=ret_s_all)
            a_p, a_s = cat_p, cat_s
            out["ret_p"].append(st_p)
            out["sgv"].append(svn_s.reshape(db, pad, -1)[:, :n])
        else:
            w_in = jnp.pad(o_w_in[i], ((0, 0), (0, odd_pad - odd_cols))).astype(BF16)
            w_out = o_w_out[i].astype(BF16)
            wts, pes = _compress_weights(o_phi_k[i], o_phi_v[i], o_pe_k[i], o_pe_v[i])
            q_p, rc_p, cmp_all, sel_all, win_t, g_p, ka_p, vs_p, kw_p, vw_p = proj_in(
                xp, norm_mix_pre[l], w_in, nsa_p, _odd_modes(), _odd_outs(True), seq_len=s,
                layer=i, n_layers=n_odd, stacked={2: cmp_all, 3: sel_all})
            q_s, rc_s, rs_s, rw_s, g_s = proj_in(xs, norm_mix_pre[l], w_in, nsa_s, _odd_modes(), _odd_outs(False))
            kc_p = compress_prompt(rc_p, b, wts, pes)
            o_p = nsa_prompt_t(q_p, kc_p, ka_p, vs_p, kw_p, vw_p, g_p, b, s)
            kc_s = compress_pages(pool_c, i, page_table, wts, pes)
            a_s, win_all = nsa_sample(q_s, kc_s, pool_s, i, page_table, rs_s, rw_s, win, g_s, n, prev=win_all)
            a_p = o_p
            out["win_p"].append(win_t[:, :, s - keep:])
            out["cmp_s"].append(rc_s.reshape((db, pad) + kv_shape)[:, :n])
            out["sel_s"].append(rs_s.reshape((db, pad) + kv_shape)[:, :n])
        xp, st_p = mix_out_ffn(xp, a_p, w_out, norm_mix_post[l], s, norm_ffn_pre[l], ffn_w_in, f_conv_w, ffn_cb,
                               ffn_w_out, norm_ffn_post[l], l)
        xs, st_s = mix_out_ffn(xs, a_s, w_out, norm_mix_post[l], pad, norm_ffn_pre[l], ffn_w_in, f_conv_w, ffn_cb,
                               ffn_w_out, norm_ffn_post[l], l, prev=state_ffn, tm=256)
        out["ffn_p"].append(st_p.reshape(b, 8, -1)[:, 8 - (CONV_W - 1):])
        out["ffn_s"].append(st_s.reshape(db, pad, -1)[:, n - (CONV_W - 1):n])

    stack = lambda k: jnp.stack(out[k])
    return (xp.reshape(b, s, d), xs.reshape(db, pad, d)[:, :n], stack("ret_p"), ret_s_all, stack("sgv"),
            rows_first(cmp_all), stack("cmp_s"), rows_first(sel_all), stack("sel_s"), rows_first(stack("win_p")),
            rows_first(win_all), stack("ffn_p"), stack("ffn_s"))
```

```python
import functools

import jax
import jax.numpy as jnp
from jax import lax
from jax.experimental import pallas as pl
from jax.experimental.pallas import tpu as pltpu

F32 = jnp.float32
BF16 = jnp.bfloat16

EPS = 1e-6
NEG = -1e30
FORCE = 1e9

LANE = 128
VMEM_LIMIT = 56 * 1024 * 1024

RET_CHUNK = 128
RET_THETA = 10000.0
SG_CHUNK = 128
HEAD_DIM = 64
NSA_KV = 4
NSA_REP = 4
CMP_LEN = 32
CMP_STRIDE = 16
SEL_BLK = 64
SEL_TOPK = 16
WINDOW = 512
ROPE_DIM = HEAD_DIM // 4
ROPE_THETA = 500000.0
ATTN_SCALE = HEAD_DIM ** -0.5
LOG2E = 1.4426950408889634
CONV_W = 3

SAMPLE_PAD = 8

ROPE_NONE, ROPE_RET_Q, ROPE_RET_K, ROPE_NSA = 0, 1, 2, 3
PAD_HEADS = -1


def _params(*sem):
    return pltpu.CompilerParams(dimension_semantics=sem, vmem_limit_bytes=VMEM_LIMIT)


def _rms(x, g):
    return x * lax.rsqrt(jnp.mean(x * x, -1, keepdims=True) + EPS) * g


def _col_chunk(n):
    for c in (512, 384, 256, 128):
        if n % c == 0:
            return c
    raise ValueError(n)


def _proj_in_body(x_ref, g_ref, w_ref, tc_ref, ta_ref, tb_ref, *o_refs, modes, kscale, outs, n_prev=0):
    o_refs = o_refs[n_prev:]
    h = _rms(x_ref[...], g_ref[...]).astype(BF16)
    n = w_ref.shape[1]
    cw = _col_chunk(n)
    for c0 in range(0, n, cw):
        y = jnp.dot(h, w_ref[:, c0:c0 + cw], preferred_element_type=F32)
        for j in range(cw // LANE):
            col = c0 + j * LANE
            blk = y[:, j * LANE:(j + 1) * LANE]
            mode = modes[col // LANE]
            if mode in (ROPE_RET_Q, ROPE_RET_K):
                blk = blk * tc_ref[...] + pltpu.roll(blk, LANE // 2, 1) * ta_ref[...]
                if mode == ROPE_RET_K:
                    blk = blk * kscale
            elif mode == ROPE_NSA:
                blk = (blk * tc_ref[...] + pltpu.roll(blk, LANE - ROPE_DIM // 2, 1) * ta_ref[...]
                       + pltpu.roll(blk, ROPE_DIM // 2, 1) * tb_ref[...])
            for o_ref, (oc, ow, osc, tw, fill) in zip(o_refs, outs):
                if oc <= col < oc + ow:
                    v = blk if osc == 1.0 else blk * osc
                    cs = slice(col - oc, col - oc + LANE)
                    if tw is None:
                        o_ref[:, cs] = v.astype(o_ref.dtype)
                    elif tw == PAD_HEADS:
                        low = lax.broadcasted_iota(jnp.int32, v.shape, 1) < HEAD_DIM
                        c2 = 2 * (col - oc)
                        o_ref[:, c2:c2 + LANE] = jnp.where(low, v, 0.0).astype(o_ref.dtype)
                        o_ref[:, c2 + LANE:c2 + 2 * LANE] = jnp.where(low, pltpu.roll(v, HEAD_DIM, 1), 0.0).astype(o_ref.dtype)
                    elif tw == 0 and fill is not None:
                        for slot in range(fill[1]):
                            o_ref[slot, cs, :] = (v.T if slot == fill[0] else jnp.zeros(v.T.shape, F32)).astype(o_ref.dtype)
                    elif tw == 0:
                        o_ref[cs, :] = v.T.astype(o_ref.dtype)
                    else:
                        for s in range(v.shape[0] // tw):
                            o_ref[s, cs, :] = v[s * tw:(s + 1) * tw, :].T.astype(o_ref.dtype)


def proj_in(x, g, w, tabs, modes, outs, kscale=1.0, tm=512, seq_len=None, layer=0, n_layers=1, stacked=None):
    m, d = x.shape
    n = w.shape[1]
    tm = min(tm, m)
    nt = tabs[0].shape[0] // tm
    tab_spec = pl.BlockSpec((tm, LANE), lambda i: (i % nt, 0))
    tps = (seq_len // tm) if seq_len else 1
    stacked = stacked or {}
    specs, shapes, prevs, aliases, fills = [], [], [], {}, []
    for k, (_, ow, _, dt, tw) in enumerate(outs):
        fills.append(None)
        if tw is None:
            specs.append(pl.BlockSpec((tm, ow), lambda i: (i, 0)))
            shapes.append(jax.ShapeDtypeStruct((m, ow), dt))
        elif tw == PAD_HEADS:
            specs.append(pl.BlockSpec((tm, 2 * ow), lambda i: (i, 0)))
            shapes.append(jax.ShapeDtypeStruct((m, 2 * ow), dt))
        elif tw == 0 and k in stacked:
            shapes.append(jax.ShapeDtypeStruct((n_layers, m // seq_len, ow, seq_len), dt))
            if stacked[k] is None:
                specs.append(pl.BlockSpec((n_layers, None, ow, tm), lambda i: (0, i // tps, 0, i % tps)))
                fills[-1] = (layer, n_layers)
            else:
                specs.append(pl.BlockSpec((None, None, ow, tm), lambda i: (layer, i // tps, 0, i % tps)))
                aliases[6 + len(prevs)] = k
                prevs.append(stacked[k])
        elif tw == 0:
            specs.append(pl.BlockSpec((None, ow, tm), lambda i: (i // tps, 0, i % tps)))
            shapes.append(jax.ShapeDtypeStruct((m // seq_len, ow, seq_len), dt))
        else:
            specs.append(pl.BlockSpec((None, tm // tw, ow, tw), lambda i: (i // tps, i % tps, 0, 0)))
            shapes.append(jax.ShapeDtypeStruct((m // seq_len, seq_len // tw, ow, tw), dt))
    return pl.pallas_call(
        functools.partial(_proj_in_body, modes=tuple(modes), kscale=kscale, n_prev=len(prevs),
                          outs=tuple((o[0], o[1], o[2], o[4], fl) for o, fl in zip(outs, fills))),
        grid=(m // tm,),
        in_specs=[pl.BlockSpec((tm, d), lambda i: (i, 0)),
                  pl.BlockSpec((1, d), lambda i: (0, 0)),
                  pl.BlockSpec((d, n), lambda i: (0, 0)),
                  tab_spec, tab_spec, tab_spec] + [pl.BlockSpec(memory_space=pl.ANY)] * len(prevs),
        out_specs=specs,
        out_shape=shapes,
        input_output_aliases=aliases,
        compiler_params=_params("parallel"),
        name="proj_in",
    )(x, g.reshape(1, d), w, *tabs, *prevs)


def _ret_rope_tabs(pos):
    inv = 1.0 / (RET_THETA ** jnp.linspace(0.0, 1.0, LANE // 2))
    ang = pos.astype(F32)[:, None] * inv[None, :]
    cos, sin = jnp.cos(ang), jnp.sin(ang)
    return jnp.concatenate([cos, cos], -1), jnp.concatenate([-sin, sin], -1)


def _nsa_rope_tabs(pos):
    hr = ROPE_DIM // 2
    inv = 1.0 / (ROPE_THETA ** (jnp.arange(0, ROPE_DIM, 2, dtype=F32) / ROPE_DIM))
    ang = pos.astype(F32)[:, None] * inv[None, :]
    cos, sin = jnp.cos(ang), jnp.sin(ang)
    n = pos.shape[0]
    rest = HEAD_DIM - ROPE_DIM
    c = jnp.concatenate([cos, cos, jnp.ones((n, rest), F32)], -1)
    a = jnp.concatenate([-sin, jnp.zeros((n, hr + rest), F32)], -1)
    b = jnp.concatenate([jnp.zeros((n, hr), F32), sin, jnp.zeros((n, rest), F32)], -1)
    rep = LANE // HEAD_DIM
    return jnp.tile(c, (1, rep)), jnp.tile(a, (1, rep)), jnp.tile(b, (1, rep))


def _ret_tabs(c, n_valid, heads):
    log_g = jnp.log(1.0 - 2.0 ** (-5.0 - jnp.arange(heads, dtype=F32)))
    idx = jnp.arange(c, dtype=F32)
    diff = idx[:, None] - idx[None, :]
    ok = (diff >= 0) & (idx[None, :] < n_valid)
    dmask = jnp.where(ok, jnp.exp(log_g[:, None, None] * jnp.maximum(diff, 0.0)), 0.0)
    qdec = jnp.exp(log_g[:, None] * (idx + 1.0))
    kdec = jnp.where(idx < n_valid, jnp.exp(log_g[:, None] * (n_valid - 1.0 - idx)), 0.0)
    cdec = jnp.exp(log_g * n_valid)
    bc = lambda t: jnp.broadcast_to(t[..., None], t.shape + (LANE,))
    dm = dmask if c == LANE else jnp.pad(dmask, ((0, 0), (0, 0), (0, LANE - c)))
    return dm, bc(qdec), bc(kdec), jnp.broadcast_to(cdec[:, None, None], (heads, 8, LANE))


def _layer_norm_rows(x, g, b):
    xc = x - jnp.mean(x, -1, keepdims=True)
    return xc * lax.rsqrt(jnp.mean(xc * xc, -1, keepdims=True) + EPS) * g + b


def _rms_unit(x):
    return x * lax.rsqrt(jnp.mean(x * x, -1, keepdims=True) + EPS)


def _even_seq_body(q_ref, k_ref, v_ref, g_ref, u_ref, sv_ref, dm_ref, qd_ref, kd_ref, cd_ref,
                   lng_ref, lnb_ref, wm_ref, sgb_ref, o_ref, st_ref, s_scr, *, tq, heads, groups):
    t = pl.program_id(1)
    c = RET_CHUNK

    @pl.when(t == 0)
    def _():
        s_scr[...] = jnp.zeros(s_scr.shape, F32)

    hcols = [slice(h * LANE, (h + 1) * LANE) for h in range(heads)]
    gcols = [slice(gi * LANE, (gi + 1) * LANE) for gi in range(groups)]
    state = [s_scr[h] for h in range(heads)]
    for ci in range(tq // c):
        rows = slice(ci * c, (ci + 1) * c)
        stage = []
        for h in range(heads):
            qc, kc, vc = q_ref[rows, hcols[h]], k_ref[rows, hcols[h]], v_ref[rows, hcols[h]]
            vb = vc.astype(BF16)
            inner = lax.dot_general(qc.astype(BF16), kc.astype(BF16), (((1,), (1,)), ((), ())),
                                    preferred_element_type=F32)
            carry_in = jnp.dot((qc * qd_ref[h]).astype(BF16), state[h].astype(BF16), preferred_element_type=F32)
            update = lax.dot_general((kc * kd_ref[h]).astype(BF16), vb, (((0,), (0,)), ((), ())),
                                     preferred_element_type=F32)
            stage.append((inner, carry_in, update, vb))
        outs = []
        for h in range(heads):
            inner, carry_in, update, vb = stage[h]
            outs.append(jnp.dot((inner * dm_ref[h]).astype(BF16), vb, preferred_element_type=F32) + carry_in)
            state[h] = state[h] * cd_ref[h][0:1, :] + update
        for h in range(heads):
            gg = g_ref[rows, hcols[h]]
            o_ref[rows, hcols[h]] = gg * jax.nn.sigmoid(gg) * _rms_unit(outs[h])
        svns = [_layer_norm_rows(sv_ref[rows, gcols[gi]], lng_ref[:, gcols[gi]], lnb_ref[:, gcols[gi]])
                for gi in range(groups)]
        for gi in range(groups):
            mixed = jnp.dot(wm_ref[gi], svns[gi].astype(BF16), preferred_element_type=F32) + sgb_ref[gi]
            o_ref[rows, heads * LANE + gi * LANE:heads * LANE + (gi + 1) * LANE] = u_ref[rows, gcols[gi]] * mixed
    for h in range(heads):
        s_scr[h] = state[h]

    @pl.when(t == pl.num_programs(1) - 1)
    def _():
        st_ref[...] = s_scr[...]


def even_seq(p, nseq, seq_len, ln_g, ln_b, sg_w, sg_b, heads, groups, tq=512):
    m = p.shape[0]
    w = heads * LANE
    c = RET_CHUNK
    nt = seq_len // tq
    dm, qd, kd, cd = _ret_tabs(c, c, heads)
    wm = jnp.tril(sg_w[:, :c, :c]).astype(BF16)
    sgb = jnp.broadcast_to(sg_b[:, :c, None], (groups, c, LANE))
    part = lambda j: pl.BlockSpec((tq, w), lambda b, t: (b * nt + t, j))
    full = lambda a: pl.BlockSpec(a.shape, lambda b, t: (0,) * a.ndim)
    return pl.pallas_call(
        functools.partial(_even_seq_body, tq=tq, heads=heads, groups=groups),
        grid=(nseq, nt),
        in_specs=[part(j) for j in range(6)] + [full(dm), full(qd), full(kd), full(cd),
                                                pl.BlockSpec((1, w), lambda b, t: (0, 0)),
                                                pl.BlockSpec((1, w), lambda b, t: (0, 0)),
                                                full(wm), full(sgb)],
        out_specs=[pl.BlockSpec((tq, 2 * w), lambda b, t: (b * nt + t, 0)),
                   pl.BlockSpec((None, heads, LANE, LANE), lambda b, t: (b, 0, 0, 0))],
        out_shape=[jax.ShapeDtypeStruct((m, 2 * w), F32),
                   jax.ShapeDtypeStruct((nseq, heads, LANE, LANE), F32)],
        scratch_shapes=[pltpu.VMEM((heads, LANE, LANE), F32)],
        compiler_params=_params("arbitrary", "arbitrary"),
        name="even_seq",
    )(p, p, p, p, p, p, dm, qd, kd, cd, ln_g.reshape(1, w), ln_b.reshape(1, w), wm, sgb)


def _even_seq_sample_body(*refs, nb, n, heads, groups, has_prev, fill):
    (q_ref, k_ref, v_ref, g_ref, u_ref, sv_ref, s0_ref, dm_ref, qd_ref, kd_ref, cd_ref,
     lng_ref, lnb_ref, wm_ref, sgb_ref, o_ref, st_all_ref, svn_ref) = refs[1:] if has_prev else refs
    np_ = SAMPLE_PAD
    if fill is None:
        st_ref = st_all_ref
    else:
        st_ref = st_all_ref.at[fill[0]]
        for slot in range(fill[1]):
            if slot != fill[0]:
                st_all_ref[slot] = jnp.zeros(st_all_ref.shape[1:], F32)

    def one(b, carry):
        rows = pl.ds(pl.multiple_of(b * np_, np_), np_)
        qkv, o_state = [], []
        for h in range(heads):
            cols = slice(h * LANE, (h + 1) * LANE)
            q, k, v = q_ref[rows, cols], k_ref[rows, cols], v_ref[rows, cols]
            s = s0_ref[b, h]
            qkv.append((q, k, v))
            o_state.append(jnp.dot((q * qd_ref[h]).astype(BF16), s.astype(BF16), preferred_element_type=F32))
            st_ref[b, h] = s * cd_ref[h][0:1, :] + lax.dot_general(
                (k * kd_ref[h]).astype(BF16), v.astype(BF16), (((0,), (0,)), ((), ())),
                preferred_element_type=F32)
        inner = [[jnp.sum(q * k[j:j + 1, :], axis=-1, keepdims=True) for j in range(n)] for (q, k, v) in qkv]
        for h in range(heads):
            cols = slice(h * LANE, (h + 1) * LANE)
            v = qkv[h][2]
            o = o_state[h]
            dm = dm_ref[h]
            for j in range(n):
                o = o + (inner[h][j] * dm[:, j:j + 1]) * v[j:j + 1, :]
            gg = g_ref[rows, cols]
            o_ref[rows, cols] = gg * jax.nn.sigmoid(gg) * _rms_unit(o)
        for gi in range(groups):
            cols = slice(gi * LANE, (gi + 1) * LANE)
            svn = _layer_norm_rows(sv_ref[rows, cols], lng_ref[:, cols], lnb_ref[:, cols])
            svn_ref[rows, cols] = svn
            wm = wm_ref[gi]
            mixed = sgb_ref[gi]
            for j in range(n):
                mixed = mixed + wm[:, j:j + 1] * svn[j:j + 1, :]
            o_ref[rows, heads * LANE + gi * LANE:heads * LANE + (gi + 1) * LANE] = u_ref[rows, cols] * mixed
        return carry

    lax.fori_loop(0, nb, one, 0)


def even_seq_sample(p, s0, layer, n, ln_g, ln_b, sg_w, sg_b, heads, groups, prev=None, nb=8):
    m = p.shape[0]
    db = m // SAMPLE_PAD
    w = heads * LANE
    dm, qd, kd, cd = _ret_tabs(SAMPLE_PAD, n, heads)
    wm = jnp.pad(jnp.tril(sg_w[:, :n, :n]), ((0, 0), (0, SAMPLE_PAD - n), (0, LANE - n)))
    sgb = jnp.broadcast_to(jnp.pad(sg_b[:, :n], ((0, 0), (0, SAMPLE_PAD - n)))[:, :, None], (groups, SAMPLE_PAD, LANE))
    rows = nb * SAMPLE_PAD
    part = lambda j: pl.BlockSpec((rows, w), lambda i: (i, j))
    full = lambda a: pl.BlockSpec(a.shape, lambda i: (0,) * a.ndim)
    st_spec = pl.BlockSpec((None, nb, heads, LANE, LANE), lambda i: (layer, i, 0, 0, 0))
    has_prev = prev is not None
    n_layers = s0.shape[0]
    fill = None if has_prev else (layer, n_layers)
    st_out = st_spec if has_prev else pl.BlockSpec((n_layers, nb, heads, LANE, LANE), lambda i: (0, i, 0, 0, 0))
    return pl.pallas_call(
        functools.partial(_even_seq_sample_body, nb=nb, n=n, heads=heads, groups=groups, has_prev=has_prev,
                          fill=fill),
        grid=(db // nb,),
        in_specs=([pl.BlockSpec(memory_space=pl.ANY)] if has_prev else [])
        + [part(j) for j in range(6)] + [st_spec, full(dm), full(qd), full(kd), full(cd),
                                         pl.BlockSpec((1, w), lambda i: (0, 0)),
                                         pl.BlockSpec((1, w), lambda i: (0, 0)),
                                         full(wm), full(sgb)],
        out_specs=[pl.BlockSpec((rows, 2 * w), lambda i: (i, 0)), st_out,
                   pl.BlockSpec((rows, w), lambda i: (i, 0))],
        out_shape=[jax.ShapeDtypeStruct((m, 2 * w), F32),
                   jax.ShapeDtypeStruct(s0.shape, F32),
                   jax.ShapeDtypeStruct((m, w), F32)],
        input_output_aliases={0: 1} if has_prev else {},
        compiler_params=_params("parallel"),
        name="even_seq_sample",
    )(*([prev] if has_prev else []), p, p, p, p, p, p, s0, dm, qd, kd, cd, ln_g.reshape(1, w), ln_b.reshape(1, w),
      wm, sgb)


FFN_HALO = 16


def _ffn_body(*refs, tm, tiles_per_seq, sample, nf_static):
    if sample:
        (x_ref, a_ref, wm_ref, gmix_ref, gpre_ref, wa_ref, wb_ref, cwa_ref, cwb_ref, cba_ref, cbb_ref, wo_ref,
         gpost_ref, pa_ref, pb_ref, o_ref, st_ref, h_scr, upa_scr, upb_scr, acc_scr, x1_scr) = refs
    else:
        (x_ref, xh_ref, a_ref, ah_ref, wm_ref, gmix_ref, gpre_ref, wa_ref, wb_ref, cwa_ref, cwb_ref, cba_ref, cbb_ref,
         wo_ref, gpost_ref, o_ref, st_ref, h_scr, upa_scr, upb_scr, acc_scr, x1_scr) = refs
    i = pl.program_id(0)
    f = pl.program_id(1)
    nf = pl.num_programs(1)
    hl = FFN_HALO

    @pl.when(f == 0)
    def _():
        if sample:
            a, x = a_ref[...], x_ref[...]
        else:
            a = jnp.concatenate([ah_ref[...], a_ref[...]], axis=0)
            x = jnp.concatenate([xh_ref[...], x_ref[...]], axis=0)
        x1 = x + _rms(jnp.dot(a.astype(BF16), wm_ref[...], preferred_element_type=F32), gmix_ref[...])
        hn = _rms(x1, gpre_ref[...])
        if sample:
            x1_scr[...] = x1
            h_scr[0:hl, :] = jnp.zeros((hl, h_scr.shape[1]), BF16)
            h_scr[hl:, :] = hn.astype(BF16)
        else:
            x1_scr[...] = x1[hl:, :]
            h_scr[0:hl, :] = jnp.where(i % tiles_per_seq == 0, 0.0, hn[0:hl, :]).astype(BF16)
            h_scr[hl:, :] = hn[hl:, :].astype(BF16)

    h = h_scr[...]
    if sample:
        t = lax.broadcasted_iota(jnp.int32, (tm, 1), 0) % SAMPLE_PAD
        m1 = t >= 1
        m2 = t >= 2

    def conv(up_scr, cw_ref, cb_ref, p_ref):
        s2 = up_scr[pl.ds(hl - 2, tm), :]
        s1 = up_scr[pl.ds(hl - 1, tm), :]
        s0 = up_scr[pl.ds(hl, tm), :]
        if sample:
            p0, p1 = p_ref[:, 0, :][:, None, :], p_ref[:, 1, :][:, None, :]
            t3 = lax.broadcasted_iota(jnp.int32, (tm // SAMPLE_PAD, SAMPLE_PAD, p0.shape[-1]), 1)
            e2 = jnp.where(t3 == 0, p0, jnp.where(t3 == 1, p1, 0.0)).reshape(tm, p0.shape[-1])
            e1 = jnp.where(t3 == 0, p1, 0.0).reshape(tm, p0.shape[-1])
            s2 = jnp.where(m2, s2, 0.0) + e2
            s1 = jnp.where(m1, s1, 0.0) + e1
        return cb_ref[...] + s2 * cw_ref[0:1, :] + s1 * cw_ref[1:2, :] + s0 * cw_ref[2:3, :]

    upa_scr[...] = jnp.dot(h, wa_ref[...], preferred_element_type=F32)
    upb_scr[...] = jnp.dot(h, wb_ref[...], preferred_element_type=F32)
    a = conv(upa_scr, cwa_ref, cba_ref, pa_ref if sample else None)
    b = conv(upb_scr, cwb_ref, cbb_ref, pb_ref if sample else None)
    act = (jax.nn.gelu(a) * b).astype(BF16)
    contrib = jnp.dot(act, wo_ref[...], preferred_element_type=F32)

    @pl.when(f == 0)
    def _():
        acc_scr[...] = contrib

    @pl.when(f > 0)
    def _():
        acc_scr[...] += contrib

    @pl.when(f == nf - 1)
    def _():
        o_ref[...] = x1_scr[...] + _rms(acc_scr[...], gpost_ref[...])

    fw = upa_scr.shape[1]
    rows = st_ref.shape[0]
    last = True if sample else (i % tiles_per_seq == tiles_per_seq - 1)
    for j in range(nf_static):
        @pl.when(jnp.logical_and(f == j, last))
        def _(j=j):
            st_ref[:, j * fw:(j + 1) * fw] = upa_scr[pl.ds(hl + tm - rows, rows), :]
            st_ref[:, (nf_static + j) * fw:(nf_static + j + 1) * fw] = upb_scr[pl.ds(hl + tm - rows, rows), :]


def mix_out_ffn(x, a, w_mix, g_mix, seq_len, gpre, w_in, conv_w, conv_b, w_out, gpost, layer, prev=None, tm=512):
    m, d = x.shape
    ka = a.shape[1]
    ff = w_out.shape[1]
    fw = 1408 if ff % 1408 == 0 else ff
    nf = ff // fw
    sample = prev is not None
    tm = min(tm, m)
    tps = max(seq_len // tm, 1)
    row = lambda i, f: (i, 0)
    const = lambda i, f: (0, 0)
    cola = lambda i, f: (layer, 0, f)
    colb = lambda i, f: (layer, 0, nf + f)
    hb = tm // FFN_HALO
    halo = lambda i, f: (jnp.maximum(i * hb - 1, 0), 0)
    if sample:
        in_specs = [pl.BlockSpec((tm, d), row), pl.BlockSpec((tm, ka), row)]
        args = [x, a]
    else:
        in_specs = [pl.BlockSpec((tm, d), row), pl.BlockSpec((FFN_HALO, d), halo),
                    pl.BlockSpec((tm, ka), row), pl.BlockSpec((FFN_HALO, ka), halo)]
        args = [x, x, a, a]
    in_specs += [pl.BlockSpec((ka, d), const), pl.BlockSpec((1, d), const)]
    args += [w_mix, g_mix.reshape(1, d)]
    in_specs += [pl.BlockSpec((1, d), const),
                 pl.BlockSpec((None, d, fw), cola), pl.BlockSpec((None, d, fw), colb),
                 pl.BlockSpec((None, CONV_W, fw), cola), pl.BlockSpec((None, CONV_W, fw), colb),
                 pl.BlockSpec((None, 1, fw), cola), pl.BlockSpec((None, 1, fw), colb),
                 pl.BlockSpec((None, fw, d), lambda i, f: (layer, f, 0)),
                 pl.BlockSpec((1, d), const)]
    args += [gpre.reshape(1, d), w_in, w_in, conv_w, conv_w, conv_b, conv_b, w_out, gpost.reshape(1, d)]
    if sample:
        nsq = tm // SAMPLE_PAD
        in_specs += [pl.BlockSpec((None, nsq, CONV_W - 1, fw), lambda i, f: (layer, i, 0, f)),
                     pl.BlockSpec((None, nsq, CONV_W - 1, fw), lambda i, f: (layer, i, 0, nf + f))]
        args += [prev, prev]
        st_shape = jax.ShapeDtypeStruct((m, 2 * ff), F32)
        st_spec = pl.BlockSpec((tm, 2 * ff), lambda i, f: (i, 0))
    else:
        nseq = m // seq_len
        st_shape = jax.ShapeDtypeStruct((nseq * 8, 2 * ff), F32)
        st_spec = pl.BlockSpec((8, 2 * ff), lambda i, f: (i // tps, 0))
    return pl.pallas_call(
        functools.partial(_ffn_body, tm=tm, tiles_per_seq=tps, sample=sample, nf_static=nf),
        grid=(m // tm, nf),
        in_specs=in_specs,
        out_specs=[pl.BlockSpec((tm, d), row), st_spec],
        out_shape=[jax.ShapeDtypeStruct((m, d), F32), st_shape],
        scratch_shapes=[pltpu.VMEM((tm + FFN_HALO, d), BF16),
                        pltpu.VMEM((tm + FFN_HALO, fw), F32),
                        pltpu.VMEM((tm + FFN_HALO, fw), F32),
                        pltpu.VMEM((tm, d), F32),
                        pltpu.VMEM((tm, d), F32)],
        compiler_params=_params("arbitrary", "arbitrary"),
        name="conv_ffn_sample" if sample else "conv_ffn",
    )(*args)


def _odd_modes():
    return [ROPE_NSA] * 8 + [ROPE_NSA, ROPE_NSA, ROPE_NONE, ROPE_NONE] * 3 + [ROPE_NONE]


SEL_TK = 512
SUM_ROWS = 16
WIN_TK = 128


def _odd_outs(prompt):
    if not prompt:
        return [(0, 1024, ATTN_SCALE, F32, None), (1024, 512, 1.0, F32, None), (1536, 512, 1.0, F32, None),
                (2048, 512, 1.0, F32, None), (2560, 128, 1.0, F32, None)]
    return [(0, 1024, ATTN_SCALE * LOG2E, BF16, 0), (1024, 512, 1.0, F32, None), (1024, 512, 1.0, F32, 0),
            (1536, 512, 1.0, F32, 0), (2048, 512, 1.0, F32, 0), (2560, 128, 1.0, F32, 0),
            (1536, 256, 1.0, BF16, PAD_HEADS), (1792, 256, 1.0, BF16, SEL_TK),
            (2048, 256, 1.0, BF16, None), (2304, 256, 1.0, BF16, WIN_TK)]


SUBS = CMP_LEN // CMP_STRIDE
ROW_W = 2 * NSA_KV * HEAD_DIM
SUB_W = CMP_STRIDE * ROW_W


def _compress_weights(phi_k, phi_v, pe_k, pe_v):
    def one(phi, pe):
        p4 = phi.reshape(SUBS, CMP_STRIDE, HEAD_DIM, HEAD_DIM)
        w = jnp.einsum('hlde,gG->lgdhGe', p4, jnp.eye(2, dtype=phi.dtype))
        w = w.reshape(CMP_STRIDE * 2 * HEAD_DIM, SUBS * 2 * HEAD_DIM)
        pr = jnp.broadcast_to(pe.reshape(SUBS, CMP_STRIDE, 1, HEAD_DIM), (SUBS, CMP_STRIDE, 2, HEAD_DIM))
        pr = jnp.pad(pr.reshape(SUBS, -1), ((0, 16 - SUBS), (0, 0)))
        return w, pr
    wk, pk = one(phi_k, pe_k)
    wv, pv = one(phi_v, pe_v)
    return jnp.stack([wk, wv]).astype(BF16), jnp.stack([pk, pv]).astype(BF16)


def _compress_column(xj, j, w_ref, pe_ref, o_ref, ab_scr):
    nsub = o_ref.shape[0]
    half = LANE
    xe = jnp.concatenate([xj, pe_ref[j // 2]], axis=0)
    ab_scr[...] = jnp.dot(xe, w_ref[j // 2], preferred_element_type=F32)
    bias = ab_scr[nsub:nsub + 1, 0:half] + ab_scr[nsub + 1:nsub + 2, half:2 * half]
    o_ref[:, j * LANE:(j + 1) * LANE] = (
        ab_scr[0:nsub, 0:half] + ab_scr[pl.ds(1, nsub), half:2 * half] + bias).astype(o_ref.dtype)


def _compress_body(x_ref, w_ref, pe_ref, o_ref, ab_scr):
    for j in range(ROW_W // LANE):
        xj = jnp.concatenate([x_ref[:, l * ROW_W + j * LANE:l * ROW_W + (j + 1) * LANE]
                              for l in range(CMP_STRIDE)], axis=1).astype(BF16)
        _compress_column(xj, j, w_ref, pe_ref, o_ref, ab_scr)


def _compress_pages_body(pt_ref, *refs, n_x):
    x_refs = refs[:n_x]
    perm_ref, w_ref, pe_ref, o_ref, ab_scr, xs_scr = refs[n_x:]
    page = x_refs[0].shape[-1]
    sub_pp = page // CMP_STRIDE
    kvw = NSA_KV * HEAD_DIM
    for k, xr in enumerate(x_refs):
        for c in range(2):
            t = xr[c].reshape(kvw, page).astype(BF16)
            out = lax.dot_general(perm_ref[...], t, (((1,), (1,)), ((), ())), preferred_element_type=F32)
            for gp in range(kvw // LANE):
                for l in range(CMP_STRIDE):
                    xs_scr[c * (kvw // LANE) + gp, k * sub_pp:(k + 1) * sub_pp, l * LANE:(l + 1) * LANE] = (
                        out[l * sub_pp:(l + 1) * sub_pp, gp * LANE:(gp + 1) * LANE])
    for j in range(ROW_W // LANE):
        _compress_column(xs_scr[j].astype(BF16), j, w_ref, pe_ref, o_ref, ab_scr)


def compress_prompt(rows, nseq, wts, pes):
    nsub = rows.shape[0] // nseq // CMP_STRIDE
    x = rows.reshape(nseq * nsub, SUB_W)
    return pl.pallas_call(
        _compress_body,
        grid=(nseq,),
        in_specs=[pl.BlockSpec((nsub, SUB_W), lambda b: (b, 0)),
                  pl.BlockSpec(wts.shape, lambda b: (0, 0, 0)),
                  pl.BlockSpec(pes.shape, lambda b: (0, 0, 0))],
        out_specs=pl.BlockSpec((nsub, ROW_W), lambda b: (b, 0)),
        out_shape=jax.ShapeDtypeStruct((nseq * nsub, ROW_W), BF16),
        scratch_shapes=[pltpu.VMEM((nsub + 16, 2 * LANE), F32)],
        compiler_params=_params("parallel"),
        name="compress_prompt",
    )(x, wts, pes)


def compress_pages(pool, layer, page_table, wts, pes, nb=2):
    db, n_pages = page_table.shape
    page = pool.shape[-1]
    sub_pp = page // CMP_STRIDE
    nsub = nb * n_pages * sub_pp
    out_row = jnp.arange(page)
    perm = (jnp.arange(page)[None, :] == ((out_row % sub_pp) * CMP_STRIDE + out_row // sub_pp)[:, None]).astype(BF16)
    specs = [pl.BlockSpec((None, None) + pool.shape[2:],
                          functools.partial(lambda i, pt, s, k: (layer, pt[(i * nb + s) * n_pages + k], 0, 0, 0, 0),
                                            s=s, k=k))
             for s in range(nb) for k in range(n_pages)]
    grid_spec = pltpu.PrefetchScalarGridSpec(
        num_scalar_prefetch=1,
        grid=(db // nb,),
        in_specs=specs + [pl.BlockSpec(perm.shape, lambda i, pt: (0, 0)),
                          pl.BlockSpec(wts.shape, lambda i, pt: (0, 0, 0)),
                          pl.BlockSpec(pes.shape, lambda i, pt: (0, 0, 0))],
        out_specs=pl.BlockSpec((nsub, ROW_W), lambda i, pt: (i, 0)),
        scratch_shapes=[pltpu.VMEM((nsub + 16, 2 * LANE), F32),
                        pltpu.VMEM((ROW_W // LANE, nsub, CMP_STRIDE * LANE), F32)],
    )
    return pl.pallas_call(
        functools.partial(_compress_pages_body, n_x=nb * n_pages),
        grid_spec=grid_spec,
        out_shape=jax.ShapeDtypeStruct((db * n_pages * sub_pp, ROW_W), BF16),
        compiler_params=_params("parallel"),
        name="compress_pages",
    )(page_table.reshape(-1), *([pool] * (nb * n_pages)), perm, wts, pes)


def _cover(nsub, nc, n_sel):
    c_start = jnp.arange(nsub) * CMP_STRIDE
    s_start = jnp.arange(SEL_BLK) * SEL_BLK
    ok = ((c_start[:, None] < s_start[None, :] + SEL_BLK) & (c_start[:, None] + CMP_LEN > s_start[None, :])
          & (jnp.arange(nsub)[:, None] < nc) & (jnp.arange(SEL_BLK)[None, :] < n_sel))
    return ok.astype(BF16)


def _expand(n_keys, tk):
    key = jnp.arange(n_keys).reshape(n_keys // tk, 1, tk)
    return (key // SEL_BLK == jnp.arange(SEL_BLK)[None, :, None]).astype(BF16)


def _masked_softmax(s, ok):
    sm = jnp.where(ok, s, NEG)
    ex = jnp.exp(sm - jnp.max(sm, -1, keepdims=True))
    return ex / jnp.sum(ex, -1, keepdims=True)


def _importance(psum, cov_ref, tpos, n_sel):
    hi = psum.astype(BF16)
    lo = (psum - hi.astype(F32)).astype(BF16)
    imp = (jnp.dot(hi, cov_ref[...], preferred_element_type=F32)
           + jnp.dot(lo, cov_ref[...], preferred_element_type=F32))
    jl = lax.broadcasted_iota(jnp.int32, imp.shape, 1)
    cur = tpos // SEL_BLK
    forced = (jl == 0) | (jl == cur) | (jl == cur - 1)
    causal = jl * SEL_BLK <= tpos
    imp = jnp.where(forced, FORCE, jnp.where(causal, imp, -1.0))
    return jnp.where(jl < n_sel, imp, -2.0)


def _topk_masks_rows(imps):
    out = []
    for x in imps:
        jl = lax.broadcasted_iota(jnp.int32, x.shape, 1)
        cnt = jnp.zeros(x.shape, F32)
        for i in range(SEL_BLK):
            xi = x[:, i:i + 1]
            tie = jnp.where(jl > i, 1.0, 0.0)
            cnt = cnt + jnp.where(xi > x, 1.0, jnp.where(xi == x, tie, 0.0))
        out.append(jnp.where(cnt < SEL_TOPK, 1.0, 0.0))
    return out


_NT = (((1,), (1,)), ((), ()))


def _softmax_cols(s):
    p = jnp.exp2(s - jnp.max(s, axis=0, keepdims=True))
    return p, jnp.sum(p, axis=0, keepdims=True)


def _run_skewed(tasks):
    n = len(tasks)
    tasks[0][0]()
    for i in range(n):
        if i + 1 < n:
            tasks[i + 1][0]()
        tasks[i][1]()
        if i >= 1:
            tasks[i - 1][2]()
    tasks[n - 1][2]()


def _nsa_prompt_lanes_body(q_ref, kc_ref, ka_ref, vs_ref, kw_ref, vw_ref, gt_ref, cov_ref, et_ref, o_ref, *,
                           tq, nc, n_sel):
    t0 = pl.program_id(1) * tq
    tl = t0 + lax.broadcasted_iota(jnp.int32, (1, tq), 1)
    sig = jax.nn.sigmoid(gt_ref[...])
    nsub = kc_ref.shape[0]
    rep, hd, ng = NSA_REP, HEAD_DIM, NSA_KV
    nh = ng * rep
    tk = vs_ref.shape[-1]
    wt = vw_ref.shape[-1]
    wide = lambda x: jnp.concatenate([x] * rep, axis=1)
    head_rows = lambda g, r: q_ref[(g * rep + r) * hd:(g * rep + r + 1) * hd, :]
    q4 = [jnp.concatenate([head_rows(g, r) for r in range(rep)], axis=1) for g in range(ng)]

    mrow = lax.broadcasted_iota(jnp.int32, (nsub, 1), 0)
    validf = wide(jnp.where((mrow * CMP_STRIDE + CMP_LEN - 1 <= tl) & (mrow < nc), 1.0, 0.0))
    jrow = lax.broadcasted_iota(jnp.int32, (SEL_BLK, tq), 0)
    cur = tl // SEL_BLK
    forced = (jrow == 0) | (jrow == cur) | (jrow == cur - 1)
    causal = jrow * SEL_BLK <= tl
    s_cmp, p_cmp, o_cmp, sel_neg = [None] * ng, [None] * ng, [None] * ng, [None] * ng

    def cmp_scores(g):
        s_cmp[g] = jnp.dot(kc_ref[:, g * hd:(g + 1) * hd], q4[g], preferred_element_type=F32)

    def cmp_softmax(g):
        ex, l = _softmax_cols(jnp.where(validf > 0.5, s_cmp[g], NEG))
        p = ex / l * validf
        p_cmp[g] = p.astype(BF16)
        psum = p[:, 0:tq]
        for r in range(1, rep):
            psum = psum + p[:, r * tq:(r + 1) * tq]
        hi = psum.astype(BF16)
        lo = (psum - hi.astype(F32)).astype(BF16)
        imp = (jnp.dot(cov_ref[...], hi, preferred_element_type=F32)
               + jnp.dot(cov_ref[...], lo, preferred_element_type=F32))
        imp = jnp.where(forced, FORCE, jnp.where(causal, imp, -1.0))
        imp = jnp.where(jrow < n_sel, imp, -2.0)
        sub = 8
        slabs = [imp[v * sub:(v + 1) * sub, :] for v in range(SEL_BLK // sub)]
        cnts = [jnp.zeros((sub, tq), F32) for _ in slabs]
        srow = lax.broadcasted_iota(jnp.int32, (sub, tq), 0)
        for i in range(SEL_BLK):
            xi = jnp.broadcast_to(imp[i:i + 1, :], (sub, tq))
            for v, x in enumerate(slabs):
                if v > i // sub:
                    hit = jnp.where(xi >= x, 1.0, 0.0)
                elif v < i // sub:
                    hit = jnp.where(xi > x, 1.0, 0.0)
                else:
                    tie = jnp.where(srow > i % sub, 1.0, 0.0)
                    hit = jnp.where(xi > x, 1.0, jnp.where(xi == x, tie, 0.0))
                cnts[v] = cnts[v] + hit
        cnt = jnp.concatenate(cnts, axis=0)
        sel_neg[g] = jnp.where(cnt < SEL_TOPK, 0.0, NEG).astype(BF16)

    def cmp_values(g):
        vc_t = kc_ref[:, ng * hd + g * hd:ng * hd + (g + 1) * hd].astype(F32).T.astype(BF16)
        o_cmp[g] = jnp.dot(vc_t, p_cmp[g], preferred_element_type=F32)

    n_wt = (WINDOW + tq) // wt
    w0 = jnp.maximum(t0 - WINDOW, 0)
    wt0 = w0 // wt
    wrows = n_wt * wt
    kpos_w = w0 + lax.broadcasted_iota(jnp.int32, (wrows, 1), 0)
    bias_w = wide(jnp.where((kpos_w <= tl) & (kpos_w > tl - WINDOW), 0.0, NEG))
    s_win, p_win, o_win = [None] * ng, [None] * ng, [None] * ng

    def ones_row(n):
        return jnp.where(lax.broadcasted_iota(jnp.int32, (SUM_ROWS, n), 0) == 0, 1.0, 0.0).astype(BF16)

    def win_scores(g):
        k_w = kw_ref[pl.ds(pl.multiple_of(w0, wt), wrows), g * hd:(g + 1) * hd]
        s_win[g] = jnp.dot(k_w, q4[g], preferred_element_type=F32) + bias_w

    def win_softmax(g):
        p_win[g] = jnp.exp2(s_win[g] - jnp.max(s_win[g], axis=0, keepdims=True)).astype(BF16)

    def win_values(g):
        v_w = jnp.concatenate([vw_ref[wt0 + i, g * hd:(g + 1) * hd, :] for i in range(n_wt)], axis=1)
        o = jnp.dot(jnp.concatenate([v_w, ones_row(wrows)], axis=0), p_win[g], preferred_element_type=F32)
        o_win[g] = o[0:hd] / o[hd:hd + 1]

    q_aug = [None] * ng

    def keys(kt, g):
        rows = pl.ds(pl.multiple_of(kt * tk, tk), tk)
        return ka_ref[rows, g * LANE:(g + 1) * LANE] + et_ref[rows, :]

    def sweep_tasks(kt, carry, out, bias):
        s, soft = [None] * ng, [None] * ng

        def scores(g):
            if q_aug[g] is None:
                q_aug[g] = jnp.concatenate([jnp.concatenate([head_rows(g, r), sel_neg[g]], axis=0)
                                            for r in range(rep)], axis=1)
            s[g] = jnp.dot(keys(kt, g), q_aug[g], preferred_element_type=F32)
            if bias is not None:
                s[g] = s[g] + bias

        def softmax(g):
            m_i = carry[g][0]
            m_new = jnp.maximum(m_i, jnp.max(s[g], axis=0, keepdims=True))
            soft[g] = (m_new, jnp.exp2(m_i - m_new), jnp.exp2(s[g] - m_new).astype(BF16))

        def values(g):
            m_new, alpha, p = soft[g]
            v_aug = jnp.concatenate([vs_ref[kt, g * hd:(g + 1) * hd, :], ones_tk], axis=0)
            out[g] = (m_new, alpha * carry[g][1] + jnp.dot(v_aug, p, preferred_element_type=F32))

        return [(functools.partial(scores, g), functools.partial(softmax, g), functools.partial(values, g))
                for g in range(ng)]

    kt_d = t0 // tk
    kpos_d = kt_d * tk + lax.broadcasted_iota(jnp.int32, (tk, 1), 0)
    bias_d = wide(jnp.where(kpos_d <= tl, 0.0, NEG))
    ones_tk = ones_row(tk)
    empty = (jnp.full((1, rep * tq), NEG, F32), jnp.zeros((hd + SUM_ROWS, rep * tq), F32))
    init = [None] * ng
    tasks = []
    for g in range(ng):
        tasks.append((functools.partial(cmp_scores, g), functools.partial(cmp_softmax, g),
                      functools.partial(cmp_values, g)))
        tasks.append((functools.partial(win_scores, g), functools.partial(win_softmax, g),
                      functools.partial(win_values, g)))
    _run_skewed(tasks + sweep_tasks(kt_d, [empty] * ng, init, bias_d))

    def step(kt, carry):
        out = [None] * ng
        _run_skewed(sweep_tasks(kt, carry, out, None))
        return tuple(out)

    final = lax.fori_loop(0, kt_d, step, tuple(init))

    for g in range(ng):
        acc_s = final[g][1]
        o_sel = acc_s[0:hd] / acc_s[hd:hd + 1]
        heads = []
        for r in range(rep):
            h = g * rep + r
            cols = slice(r * tq, (r + 1) * tq)
            heads.append(sig[h:h + 1, :] * o_cmp[g][:, cols] + sig[nh + h:nh + h + 1, :] * o_sel[:, cols]
                         + sig[2 * nh + h:2 * nh + h + 1, :] * o_win[g][:, cols])
        for pair in range(rep // 2):
            o_ref[:, (g * rep + 2 * pair) * hd:(g * rep + 2 * pair + 2) * hd] = (
                jnp.concatenate([heads[2 * pair], heads[2 * pair + 1]], axis=0).T)


def nsa_prompt_t(q_t, kcvc, ka, vs_t, kw, vw_t, gates_t, nseq, seq_len, tq=128):
    m = ka.shape[0]
    nsub = kcvc.shape[0] // nseq
    tk, wt = vs_t.shape[-1], vw_t.shape[-1]
    nc = (seq_len - CMP_LEN) // CMP_STRIDE + 1
    n_sel = seq_len // SEL_BLK
    assert n_sel <= SEL_BLK and seq_len >= WINDOW + tq and tk % tq == 0 and tq % wt == 0 and WINDOW % wt == 0
    nq = seq_len // tq
    width = q_t.shape[1]
    cov_t = _cover(nsub, nc, n_sel).T
    key = jnp.arange(seq_len)
    et = jnp.concatenate([jnp.zeros((seq_len, HEAD_DIM), BF16),
                          (key[:, None] // SEL_BLK == jnp.arange(SEL_BLK)[None, :]).astype(BF16)], axis=1)
    cols = lambda a: pl.BlockSpec((None, a.shape[1], tq), lambda b, t: (b, 0, t))
    seq_rows = lambda a: pl.BlockSpec((seq_len, a.shape[1]), lambda b, t: (b, 0))
    whole = lambda a: pl.BlockSpec((None,) + a.shape[1:], lambda b, t: (b, 0, 0, 0))
    return pl.pallas_call(
        functools.partial(_nsa_prompt_lanes_body, tq=tq, nc=nc, n_sel=n_sel),
        grid=(nseq, nq),
        in_specs=[cols(q_t), pl.BlockSpec((nsub, ROW_W), lambda b, t: (b, 0)), seq_rows(ka), whole(vs_t),
                  seq_rows(kw), whole(vw_t), cols(gates_t),
                  pl.BlockSpec(cov_t.shape, lambda b, t: (0, 0)),
                  pl.BlockSpec(et.shape, lambda b, t: (0, 0))],
        out_specs=pl.BlockSpec((tq, width), lambda b, t: (b * nq + t, 0)),
        out_shape=jax.ShapeDtypeStruct((m, width), F32),
        compiler_params=_params("parallel", "parallel"),
        name="nsa_prompt",
    )(q_t, kcvc, ka, vs_t, kw, vw_t, gates_t, cov_t, et)


def _joint_softmax(s_a, ok_a, s_b, ok_b):
    s_a = jnp.where(ok_a, s_a, NEG)
    s_b = jnp.where(ok_b, s_b, NEG)
    mx = jnp.maximum(jnp.max(s_a, -1, keepdims=True), jnp.max(s_b, -1, keepdims=True))
    e_a = jnp.exp(s_a - mx)
    e_b = jnp.exp(s_b - mx)
    return e_a, e_b, jnp.sum(e_a, -1, keepdims=True) + jnp.sum(e_b, -1, keepdims=True)


def _nsa_sample_body(pt_ref, q_ref, kc_ref, *rest, n, n_pages, past_len, nc, n_sel, fill):
    page_refs = rest[:n_pages]
    ns_ref, nw_ref, win_ref, gt_ref, cov_ref, e_ref = rest[n_pages:n_pages + 6]
    o_ref, wo_ref = rest[-2:]
    if fill is not None:
        for slot in range(fill[1]):
            if slot != fill[0]:
                wo_ref[slot] = jnp.zeros(wo_ref.shape[1:], F32)
        wo_ref = wo_ref.at[fill[0]]
    np_, rep, hd, kvw, nkv = SAMPLE_PAD, NSA_REP, HEAD_DIM, NSA_KV * HEAD_DIM, NSA_KV
    tpos = past_len + lax.broadcasted_iota(jnp.int32, (np_, 1), 0)
    sig = jax.nn.sigmoid(gt_ref[...])
    q = q_ref[...]

    blocks = []
    for g in range(nkv):
        for r in range(rep):
            h = g * rep + r
            parts = [q[:, h * hd:(h + 1) * hd]]
            if g > 0:
                parts.insert(0, jnp.zeros((np_, g * hd), F32))
            if g < nkv - 1:
                parts.append(jnp.zeros((np_, (nkv - 1 - g) * hd), F32))
            blocks.append(jnp.concatenate(parts, axis=1))
    qbd = jnp.concatenate(blocks, axis=0).astype(BF16)
    nrow = nkv * rep * np_

    def grp(x):
        return x.reshape(nkv, rep, np_, x.shape[-1])

    def scores(keys):
        return grp(lax.dot_general(qbd, keys, _NT, preferred_element_type=F32))

    def scores_t(keys_t):
        return grp(jnp.dot(qbd, keys_t, preferred_element_type=F32))

    new_lane = lax.broadcasted_iota(jnp.int32, (np_, np_), 1)
    new_pos = past_len + new_lane

    page = page_refs[0].shape[-1]
    w_buf = win_ref.shape[-1]
    ns, nw = ns_ref[...], nw_ref[...]
    k_w = win_ref[0].reshape(kvw, w_buf)
    v_w = win_ref[1].reshape(kvw, w_buf)
    k_all = jnp.concatenate([pr[0].reshape(kvw, page) for pr in page_refs], axis=1).astype(BF16)
    v_all = jnp.concatenate([pr[1].reshape(kvw, page) for pr in page_refs], axis=1).astype(BF16)
    s_cmp = scores(kc_ref[:, 0:kvw])
    s_win, s_win_new = scores_t(k_w.astype(BF16)), scores(nw[:, 0:kvw].astype(BF16))
    s_sel, s_sel_new = scores_t(k_all), scores(ns[:, 0:kvw].astype(BF16))

    nsub = kc_ref.shape[0]
    mcol = lax.broadcasted_iota(jnp.int32, (np_, nsub), 1)
    valid = (mcol * CMP_STRIDE + CMP_LEN - 1 <= tpos) & (mcol < nc)
    p = _masked_softmax(s_cmp, valid[None, None]) * jnp.where(valid, 1.0, 0.0)[None, None]
    o_cmp = jnp.dot(p.reshape(nrow, nsub).astype(BF16), kc_ref[:, kvw:2 * kvw], preferred_element_type=F32)
    sels = _topk_masks_rows([_importance(jnp.sum(p[g], axis=0), cov_ref, tpos, n_sel) for g in range(nkv)])

    kpos = lax.broadcasted_iota(jnp.int32, (np_, past_len), 1)
    ok_c = jnp.stack([jnp.where(kpos <= tpos, jnp.dot(sels[g].astype(BF16), e_ref[0], preferred_element_type=F32), 0.0)
                      for g in range(nkv)]) > 0.5
    new_ok = (new_pos <= tpos) & (new_lane < n)
    ok_n = jnp.stack([jnp.where(new_ok, jnp.concatenate(
        [sels[g][:, (past_len + i) // SEL_BLK:(past_len + i) // SEL_BLK + 1] for i in range(np_)], axis=1), 0.0)
        for g in range(nkv)]) > 0.5
    e_c, e_n, l_s = _joint_softmax(s_sel, ok_c[:, None], s_sel_new, ok_n[:, None])
    o_sel = lax.dot_general(e_c.reshape(nrow, past_len).astype(BF16), v_all, _NT, preferred_element_type=F32)
    e_n = e_n.reshape(nrow, np_)
    for i in range(n):
        o_sel = o_sel + e_n[:, i:i + 1] * ns[i:i + 1, kvw:2 * kvw]
    o_sel = o_sel / l_s.reshape(nrow, 1)

    kpos_w = past_len - w_buf + lax.broadcasted_iota(jnp.int32, (np_, w_buf), 1)
    ok_w = (kpos_w <= tpos) & (kpos_w > tpos - WINDOW)
    ok_wn = (new_pos <= tpos) & (new_pos > tpos - WINDOW) & (new_lane < n)
    e_w, e_wn, l_w = _joint_softmax(s_win, ok_w[None, None], s_win_new, ok_wn[None, None])
    o_win = lax.dot_general(e_w.reshape(nrow, w_buf).astype(BF16), v_w.astype(BF16), _NT,
                            preferred_element_type=F32)
    e_wn = e_wn.reshape(nrow, np_)
    for i in range(n):
        o_win = o_win + e_wn[:, i:i + 1] * nw[i:i + 1, kvw:2 * kvw]
    o_win = o_win / l_w.reshape(nrow, 1)

    nh = nkv * rep
    for g in range(nkv):
        for r in range(rep):
            h = g * rep + r
            rows = slice(h * np_, (h + 1) * np_)
            cols = slice(g * hd, (g + 1) * hd)
            o_ref[:, h * hd:(h + 1) * hd] = (sig[:, h:h + 1] * o_cmp[rows, cols]
                                            + sig[:, nh + h:nh + h + 1] * o_sel[rows, cols]
                                            + sig[:, 2 * nh + h:2 * nh + h + 1] * o_win[rows, cols])

    body = w_buf - LANE
    for c, old in enumerate((k_w, v_w)):
        shifted = pltpu.roll(old, w_buf - n, 1)
        new_t = jnp.concatenate([nw[:, c * kvw:(c + 1) * kvw].T, jnp.zeros((kvw, LANE - np_), F32)], axis=1)
        lane = lax.broadcasted_iota(jnp.int32, (kvw, LANE), 1)
        wo_ref[c * kvw:(c + 1) * kvw, 0:body] = shifted[:, 0:body]
        wo_ref[c * kvw:(c + 1) * kvw, body:w_buf] = jnp.where(lane < LANE - n, shifted[:, body:w_buf],
                                                              pltpu.roll(new_t, LANE - n, 1))


def nsa_sample(q, kcvc, pool, layer, page_table, rows_s, rows_w, win, gates, n, prev=None):
    db, n_pages = page_table.shape
    page = pool.shape[-1]
    past_len = n_pages * page
    nsub = kcvc.shape[0] // db
    w_buf = win.shape[-1]
    assert w_buf == WINDOW and n <= SAMPLE_PAD and past_len % SEL_BLK == 0
    nc = (past_len + n - CMP_LEN) // CMP_STRIDE + 1
    n_sel = -(-(past_len + n) // SEL_BLK)
    assert n_sel <= SEL_BLK and nc <= nsub
    cov = _cover(nsub, nc, n_sel)
    exp = _expand(past_len, past_len)
    rows8 = lambda w: pl.BlockSpec((SAMPLE_PAD, w), lambda i, pt: (i, 0))
    pages = [pl.BlockSpec((None, None) + pool.shape[2:],
                          functools.partial(lambda i, pt, k: (layer, pt[i * n_pages + k], 0, 0, 0, 0), k=k))
             for k in range(n_pages)]
    grid_spec = pltpu.PrefetchScalarGridSpec(
        num_scalar_prefetch=1,
        grid=(db,),
        in_specs=[rows8(q.shape[1]), pl.BlockSpec((nsub, ROW_W), lambda i, pt: (i, 0))] + pages
        + [rows8(ROW_W), rows8(ROW_W),
           pl.BlockSpec((None, None) + win.shape[2:], lambda i, pt: (layer, i, 0, 0, 0, 0)),
           rows8(LANE),
           pl.BlockSpec(cov.shape, lambda i, pt: (0, 0)),
           pl.BlockSpec(exp.shape, lambda i, pt: (0, 0, 0))]
        + ([pl.BlockSpec(memory_space=pl.ANY)] if prev is not None else []),
        out_specs=[rows8(q.shape[1]),
                   pl.BlockSpec((None, None, ROW_W, w_buf), lambda i, pt: (layer, i, 0, 0)) if prev is not None
                   else pl.BlockSpec((win.shape[0], None, ROW_W, w_buf), lambda i, pt: (0, i, 0, 0))],
    )
    args = (page_table.reshape(-1), q, kcvc, *([pool] * n_pages), rows_s, rows_w, win, gates, cov, exp)
    return pl.pallas_call(
        functools.partial(_nsa_sample_body, n=n, n_pages=n_pages, past_len=past_len, nc=nc, n_sel=n_sel,
                          fill=None if prev is not None else (layer, win.shape[0])),
        grid_spec=grid_spec,
        out_shape=[jax.ShapeDtypeStruct(q.shape, F32),
                   jax.ShapeDtypeStruct((win.shape[0], db, ROW_W, w_buf), F32)],
        input_output_aliases={len(args): 1} if prev is not None else {},
        compiler_params=_params("parallel"),
        name="nsa_sample",
    )(*args, *([prev] if prev is not None else []))


def kernel(x_prompt, x_sample, state_ret, cache_cmp, cache_sel, state_win, state_ffn, page_table, norm_mix_pre, norm_mix_post, norm_ffn_pre, norm_ffn_post, e_w_in, e_w_out, e_sg_ln_g, e_sg_ln_b, e_sg_w, e_sg_b, o_w_in, o_w_out, o_pe_k, o_pe_v, o_phi_k, o_phi_v, f_w_in, f_conv_w, f_conv_b, f_w_out):
    b, s, d = x_prompt.shape
    db, n, _ = x_sample.shape
    depth = norm_mix_pre.shape[0]
    heads = state_ret.shape[2]
    groups = e_sg_w.shape[1]
    n_pages = page_table.shape[1]
    page = cache_cmp.shape[2]
    past_len = n_pages * page
    pad = SAMPLE_PAD
    assert n <= pad and n >= CONV_W - 1 and n < CMP_STRIDE and state_ret.shape[3] == LANE

    xp = x_prompt.reshape(b * s, d)
    xs = jnp.pad(x_sample, ((0, 0), (0, pad - n), (0, 0))).reshape(db * pad, d)
    pos_p = jnp.arange(s, dtype=jnp.int32)
    pos_s = jnp.tile(past_len + jnp.arange(pad, dtype=jnp.int32), db)
    ret_p, ret_s = _ret_rope_tabs(pos_p), _ret_rope_tabs(pos_s)
    nsa_p, nsa_s = _nsa_rope_tabs(pos_p), _nsa_rope_tabs(pos_s)
    even_modes = [ROPE_RET_Q] * heads + [ROPE_RET_K] * heads + [ROPE_NONE] * (4 * heads)
    even_outs = [(0, 6 * heads * LANE, 1.0, F32, None)]
    kscale = LANE ** -0.5
    odd_cols = o_w_in.shape[2]
    odd_pad = -(-odd_cols // LANE) * LANE
    rows_last = lambda a: jnp.transpose(a, (0, 1, 3, 4, 5, 2))
    rows_first = lambda a: jnp.transpose(a.reshape(a.shape[:2] + kv_shape + (a.shape[-1],)), (0, 1, 5, 2, 3, 4))
    pool_c, pool_s, win = rows_last(cache_cmp), rows_last(cache_sel), rows_last(state_win)
    kv_shape = cache_cmp.shape[3:]
    keep = min(WINDOW, s)

    out = {k: [] for k in ("ret_p", "sgv", "cmp_s", "sel_s", "win_p", "ffn_p", "ffn_s")}
    ret_s_all = cmp_all = sel_all = win_all = None
    n_odd = o_w_in.shape[0]
    ffn_w_in, ffn_w_out = f_w_in.astype(BF16), f_w_out.astype(BF16)
    ffn_cb = f_conv_b[:, None, :]
    for l in range(depth):
        i = l // 2
        if l % 2 == 0:
            w_in = e_w_in[i].astype(BF16)
            w_out = e_w_out[i].astype(BF16)
            (pp,) = proj_in(xp, norm_mix_pre[l], w_in, (ret_p[0], ret_p[1], ret_p[1]), even_modes, even_outs, kscale)
            (ps,) = proj_in(xs, norm_mix_pre[l], w_in, (ret_s[0], ret_s[1], ret_s[1]), even_modes, even_outs, kscale)
            cat_p, st_p = even_seq(pp, b, s, e_sg_ln_g[i], e_sg_ln_b[i], e_sg_w[i], e_sg_b[i], heads, groups)
            cat_s, ret_s_all, svn_s = even_seq_sample(ps, state_ret, i, n, e_sg_ln_g[i], e_sg_ln_b[i], e_sg_w[i],
                                                      e_sg_b[i], heads, groups, prev=ret_s_all)
            a_p, a_s = cat_p, cat_s
            out["ret_p"].append(st_p)
            out["sgv"].append(svn_s.reshape(db, pad, -1)[:, :n])
        else:
            w_in = jnp.pad(o_w_in[i], ((0, 0), (0, odd_pad - odd_cols))).astype(BF16)
            w_out = o_w_out[i].astype(BF16)
            wts, pes = _compress_weights(o_phi_k[i], o_phi_v[i], o_pe_k[i], o_pe_v[i])
            q_p, rc_p, cmp_all, sel_all, win_t, g_p, ka_p, vs_p, kw_p, vw_p = proj_in(
                xp, norm_mix_pre[l], w_in, nsa_p, _odd_modes(), _odd_outs(True), seq_len=s,
                layer=i, n_layers=n_odd, stacked={2: cmp_all, 3: sel_all})
            q_s, rc_s, rs_s, rw_s, g_s = proj_in(xs, norm_mix_pre[l], w_in, nsa_s, _odd_modes(), _odd_outs(False))
            kc_p = compress_prompt(rc_p, b, wts, pes)
            o_p = nsa_prompt_t(q_p, kc_p, ka_p, vs_p, kw_p, vw_p, g_p, b, s)
            kc_s = compress_pages(pool_c, i, page_table, wts, pes)
            a_s, win_all = nsa_sample(q_s, kc_s, pool_s, i, page_table, rs_s, rw_s, win, g_s, n, prev=win_all)
            a_p = o_p
            out["win_p"].append(win_t[:, :, s - keep:])
            out["cmp_s"].append(rc_s.reshape((db, pad) + kv_shape)[:, :n])
            out["sel_s"].append(rs_s.reshape((db, pad) + kv_shape)[:, :n])
        xp, st_p = mix_out_ffn(xp, a_p, w_out, norm_mix_post[l], s, norm_ffn_pre[l], ffn_w_in, f_conv_w, ffn_cb,
                               ffn_w_out, norm_ffn_post[l], l)
        xs, st_s = mix_out_ffn(xs, a_s, w_out, norm_mix_post[l], pad, norm_ffn_pre[l], ffn_w_in, f_conv_w, ffn_cb,
                               ffn_w_out, norm_ffn_post[l], l, prev=state_ffn, tm=256)
        out["ffn_p"].append(st_p.reshape(b, 8, -1)[:, 8 - (CONV_W - 1):])
        out["ffn_s"].append(st_s.reshape(db, pad, -1)[:, n - (CONV_W - 1):n])

    stack = lambda k: jnp.stack(out[k])
    return (xp.reshape(b, s, d), xs.reshape(db, pad, d)[:, :n], stack("ret_p"), ret_s_all, stack("sgv"),
            rows_first(cmp_all), stack("cmp_s"), rows_first(sel_all), stack("sel_s"), rows_first(stack("win_p")),
            rows_first(win_all), stack("ffn_p"), stack("ffn_s"))
```

```python
import functools

import jax
import jax.numpy as jnp
from jax import lax
from jax.experimental import pallas as pl
from jax.experimental.pallas import tpu as pltpu

F32 = jnp.float32
BF16 = jnp.bfloat16

EPS = 1e-6
NEG = -1e30
FORCE = 1e9

LANE = 128
VMEM_LIMIT = 56 * 1024 * 1024

RET_CHUNK = 128
RET_THETA = 10000.0
SG_CHUNK = 128
HEAD_DIM = 64
NSA_KV = 4
NSA_REP = 4
CMP_LEN = 32
CMP_STRIDE = 16
SEL_BLK = 64
SEL_TOPK = 16
WINDOW = 512
ROPE_DIM = HEAD_DIM // 4
ROPE_THETA = 500000.0
ATTN_SCALE = HEAD_DIM ** -0.5
LOG2E = 1.4426950408889634
CONV_W = 3

SAMPLE_PAD = 8

ROPE_NONE, ROPE_RET_Q, ROPE_RET_K, ROPE_NSA = 0, 1, 2, 3
PAD_HEADS = -1


def _params(*sem):
    return pltpu.CompilerParams(dimension_semantics=sem, vmem_limit_bytes=VMEM_LIMIT)


def _rms(x, g):
    return x * lax.rsqrt(jnp.mean(x * x, -1, keepdims=True) + EPS) * g


def _col_chunk(n):
    for c in (512, 384, 256, 128):
        if n % c == 0:
            return c
    raise ValueError(n)


def _proj_in_body(x_ref, g_ref, w_ref, tc_ref, ta_ref, tb_ref, *o_refs, modes, kscale, outs, n_prev=0):
    o_refs = o_refs[n_prev:]
    h = _rms(x_ref[...], g_ref[...]).astype(BF16)
    n = w_ref.shape[1]
    cw = _col_chunk(n)
    for c0 in range(0, n, cw):
        y = jnp.dot(h, w_ref[:, c0:c0 + cw], preferred_element_type=F32)
        for j in range(cw // LANE):
            col = c0 + j * LANE
            blk = y[:, j * LANE:(j + 1) * LANE]
            mode = modes[col // LANE]
            if mode in (ROPE_RET_Q, ROPE_RET_K):
                blk = blk * tc_ref[...] + pltpu.roll(blk, LANE // 2, 1) * ta_ref[...]
                if mode == ROPE_RET_K:
                    blk = blk * kscale
            elif mode == ROPE_NSA:
                blk = (blk * tc_ref[...] + pltpu.roll(blk, LANE - ROPE_DIM // 2, 1) * ta_ref[...]
                       + pltpu.roll(blk, ROPE_DIM // 2, 1) * tb_ref[...])
            for o_ref, (oc, ow, osc, tw, fill) in zip(o_refs, outs):
                if oc <= col < oc + ow:
                    v = blk if osc == 1.0 else blk * osc
                    cs = slice(col - oc, col - oc + LANE)
                    if tw is None:
                        o_ref[:, cs] = v.astype(o_ref.dtype)
                    elif tw == PAD_HEADS:
                        low = lax.broadcasted_iota(jnp.int32, v.shape, 1) < HEAD_DIM
                        c2 = 2 * (col - oc)
                        o_ref[:, c2:c2 + LANE] = jnp.where(low, v, 0.0).astype(o_ref.dtype)
                        o_ref[:, c2 + LANE:c2 + 2 * LANE] = jnp.where(low, pltpu.roll(v, HEAD_DIM, 1), 0.0).astype(o_ref.dtype)
                    elif tw == 0 and fill is not None:
                        for slot in range(fill[1]):
                            o_ref[slot, cs, :] = (v.T if slot == fill[0] else jnp.zeros(v.T.shape, F32)).astype(o_ref.dtype)
                    elif tw == 0:
                        o_ref[cs, :] = v.T.astype(o_ref.dtype)
                    else:
                        for s in range(v.shape[0] // tw):
                            o_ref[s, cs, :] = v[s * tw:(s + 1) * tw, :].T.astype(o_ref.dtype)


def proj_in(x, g, w, tabs, modes, outs, kscale=1.0, tm=512, seq_len=None, layer=0, n_layers=1, stacked=None):
    m, d = x.shape
    n = w.shape[1]
    tm = min(tm, m)
    nt = tabs[0].shape[0] // tm
    tab_spec = pl.BlockSpec((tm, LANE), lambda i: (i % nt, 0))
    tps = (seq_len // tm) if seq_len else 1
    stacked = stacked or {}
    specs, shapes, prevs, aliases, fills = [], [], [], {}, []
    for k, (_, ow, _, dt, tw) in enumerate(outs):
        fills.append(None)
        if tw is None:
            specs.append(pl.BlockSpec((tm, ow), lambda i: (i, 0)))
            shapes.append(jax.ShapeDtypeStruct((m, ow), dt))
        elif tw == PAD_HEADS:
            specs.append(pl.BlockSpec((tm, 2 * ow), lambda i: (i, 0)))
            shapes.append(jax.ShapeDtypeStruct((m, 2 * ow), dt))
        elif tw == 0 and k in stacked:
            shapes.append(jax.ShapeDtypeStruct((n_layers, m // seq_len, ow, seq_len), dt))
            if stacked[k] is None:
                specs.append(pl.BlockSpec((n_layers, None, ow, tm), lambda i: (0, i // tps, 0, i % tps)))
                fills[-1] = (layer, n_layers)
            else:
                specs.append(pl.BlockSpec((None, None, ow, tm), lambda i: (layer, i // tps, 0, i % tps)))
                aliases[6 + len(prevs)] = k
                prevs.append(stacked[k])
        elif tw == 0:
            specs.append(pl.BlockSpec((None, ow, tm), lambda i: (i // tps, 0, i % tps)))
            shapes.append(jax.ShapeDtypeStruct((m // seq_len, ow, seq_len), dt))
        else:
            specs.append(pl.BlockSpec((None, tm // tw, ow, tw), lambda i: (i // tps, i % tps, 0, 0)))
            shapes.append(jax.ShapeDtypeStruct((m // seq_len, seq_len // tw, ow, tw), dt))
    return pl.pallas_call(
        functools.partial(_proj_in_body, modes=tuple(modes), kscale=kscale, n_prev=len(prevs),
                          outs=tuple((o[0], o[1], o[2], o[4], fl) for o, fl in zip(outs, fills))),
        grid=(m // tm,),
        in_specs=[pl.BlockSpec((tm, d), lambda i: (i, 0)),
                  pl.BlockSpec((1, d), lambda i: (0, 0)),
                  pl.BlockSpec((d, n), lambda i: (0, 0)),
                  tab_spec, tab_spec, tab_spec] + [pl.BlockSpec(memory_space=pl.ANY)] * len(prevs),
        out_specs=specs,
        out_shape=shapes,
        input_output_aliases=aliases,
        compiler_params=_params("parallel"),
        name="proj_in",
    )(x, g.reshape(1, d), w, *tabs, *prevs)


def _ret_rope_tabs(pos):
    inv = 1.0 / (RET_THETA ** jnp.linspace(0.0, 1.0, LANE // 2))
    ang = pos.astype(F32)[:, None] * inv[None, :]
    cos, sin = jnp.cos(ang), jnp.sin(ang)
    return jnp.concatenate([cos, cos], -1), jnp.concatenate([-sin, sin], -1)


def _nsa_rope_tabs(pos):
    hr = ROPE_DIM // 2
    inv = 1.0 / (ROPE_THETA ** (jnp.arange(0, ROPE_DIM, 2, dtype=F32) / ROPE_DIM))
    ang = pos.astype(F32)[:, None] * inv[None, :]
    cos, sin = jnp.cos(ang), jnp.sin(ang)
    n = pos.shape[0]
    rest = HEAD_DIM - ROPE_DIM
    c = jnp.concatenate([cos, cos, jnp.ones((n, rest), F32)], -1)
    a = jnp.concatenate([-sin, jnp.zeros((n, hr + rest), F32)], -1)
    b = jnp.concatenate([jnp.zeros((n, hr), F32), sin, jnp.zeros((n, rest), F32)], -1)
    rep = LANE // HEAD_DIM
    return jnp.tile(c, (1, rep)), jnp.tile(a, (1, rep)), jnp.tile(b, (1, rep))


def _ret_tabs(c, n_valid, heads):
    log_g = jnp.log(1.0 - 2.0 ** (-5.0 - jnp.arange(heads, dtype=F32)))
    idx = jnp.arange(c, dtype=F32)
    diff = idx[:, None] - idx[None, :]
    ok = (diff >= 0) & (idx[None, :] < n_valid)
    dmask = jnp.where(ok, jnp.exp(log_g[:, None, None] * jnp.maximum(diff, 0.0)), 0.0)
    qdec = jnp.exp(log_g[:, None] * (idx + 1.0))
    kdec = jnp.where(idx < n_valid, jnp.exp(log_g[:, None] * (n_valid - 1.0 - idx)), 0.0)
    cdec = jnp.exp(log_g * n_valid)
    bc = lambda t: jnp.broadcast_to(t[..., None], t.shape + (LANE,))
    dm = dmask if c == LANE else jnp.pad(dmask, ((0, 0), (0, 0), (0, LANE - c)))
    return dm, bc(qdec), bc(kdec), jnp.broadcast_to(cdec[:, None, None], (heads, 8, LANE))


def _layer_norm_rows(x, g, b):
    xc = x - jnp.mean(x, -1, keepdims=True)
    return xc * lax.rsqrt(jnp.mean(xc * xc, -1, keepdims=True) + EPS) * g + b


def _rms_unit(x):
    return x * lax.rsqrt(jnp.mean(x * x, -1, keepdims=True) + EPS)


def _even_seq_body(q_ref, k_ref, v_ref, g_ref, u_ref, sv_ref, dm_ref, qd_ref, kd_ref, cd_ref,
                   lng_ref, lnb_ref, wm_ref, sgb_ref, o_ref, st_ref, s_scr, *, tq, heads, groups):
    t = pl.program_id(1)
    c = RET_CHUNK

    @pl.when(t == 0)
    def _():
        s_scr[...] = jnp.zeros(s_scr.shape, F32)

    hcols = [slice(h * LANE, (h + 1) * LANE) for h in range(heads)]
    gcols = [slice(gi * LANE, (gi + 1) * LANE) for gi in range(groups)]
    state = [s_scr[h] for h in range(heads)]
    for ci in range(tq // c):
        rows = slice(ci * c, (ci + 1) * c)
        stage = []
        for h in range(heads):
            qc, kc, vc = q_ref[rows, hcols[h]], k_ref[rows, hcols[h]], v_ref[rows, hcols[h]]
            vb = vc.astype(BF16)
            inner = lax.dot_general(qc.astype(BF16), kc.astype(BF16), (((1,), (1,)), ((), ())),
                                    preferred_element_type=F32)
            carry_in = jnp.dot((qc * qd_ref[h]).astype(BF16), state[h].astype(BF16), preferred_element_type=F32)
            update = lax.dot_general((kc * kd_ref[h]).astype(BF16), vb, (((0,), (0,)), ((), ())),
                                     preferred_element_type=F32)
            stage.append((inner, carry_in, update, vb))
        outs = []
        for h in range(heads):
            inner, carry_in, update, vb = stage[h]
            outs.append(jnp.dot((inner * dm_ref[h]).astype(BF16), vb, preferred_element_type=F32) + carry_in)
            state[h] = state[h] * cd_ref[h][0:1, :] + update
        for h in range(heads):
            gg = g_ref[rows, hcols[h]]
            o_ref[rows, hcols[h]] = gg * jax.nn.sigmoid(gg) * _rms_unit(outs[h])
        svns = [_layer_norm_rows(sv_ref[rows, gcols[gi]], lng_ref[:, gcols[gi]], lnb_ref[:, gcols[gi]])
                for gi in range(groups)]
        for gi in range(groups):
            mixed = jnp.dot(wm_ref[gi], svns[gi].astype(BF16), preferred_element_type=F32) + sgb_ref[gi]
            o_ref[rows, heads * LANE + gi * LANE:heads * LANE + (gi + 1) * LANE] = u_ref[rows, gcols[gi]] * mixed
    for h in range(heads):
        s_scr[h] = state[h]

    @pl.when(t == pl.num_programs(1) - 1)
    def _():
        st_ref[...] = s_scr[...]


def even_seq(p, nseq, seq_len, ln_g, ln_b, sg_w, sg_b, heads, groups, tq=512):
    m = p.shape[0]
    w = heads * LANE
    c = RET_CHUNK
    nt = seq_len // tq
    dm, qd, kd, cd = _ret_tabs(c, c, heads)
    wm = jnp.tril(sg_w[:, :c, :c]).astype(BF16)
    sgb = jnp.broadcast_to(sg_b[:, :c, None], (groups, c, LANE))
    part = lambda j: pl.BlockSpec((tq, w), lambda b, t: (b * nt + t, j))
    full = lambda a: pl.BlockSpec(a.shape, lambda b, t: (0,) * a.ndim)
    return pl.pallas_call(
        functools.partial(_even_seq_body, tq=tq, heads=heads, groups=groups),
        grid=(nseq, nt),
        in_specs=[part(j) for j in range(6)] + [full(dm), full(qd), full(kd), full(cd),
                                                pl.BlockSpec((1, w), lambda b, t: (0, 0)),
                                                pl.BlockSpec((1, w), lambda b, t: (0, 0)),
                                                full(wm), full(sgb)],
        out_specs=[pl.BlockSpec((tq, 2 * w), lambda b, t: (b * nt + t, 0)),
                   pl.BlockSpec((None, heads, LANE, LANE), lambda b, t: (b, 0, 0, 0))],
        out_shape=[jax.ShapeDtypeStruct((m, 2 * w), F32),
                   jax.ShapeDtypeStruct((nseq, heads, LANE, LANE), F32)],
        scratch_shapes=[pltpu.VMEM((heads, LANE, LANE), F32)],
        compiler_params=_params("arbitrary", "arbitrary"),
        name="even_seq",
    )(p, p, p, p, p, p, dm, qd, kd, cd, ln_g.reshape(1, w), ln_b.reshape(1, w), wm, sgb)


def _even_seq_sample_body(*refs, nb, n, heads, groups, has_prev, fill):
    (q_ref, k_ref, v_ref, g_ref, u_ref, sv_ref, s0_ref, dm_ref, qd_ref, kd_ref, cd_ref,
     lng_ref, lnb_ref, wm_ref, sgb_ref, o_ref, st_all_ref, svn_ref) = refs[1:] if has_prev else refs
    np_ = SAMPLE_PAD
    if fill is None:
        st_ref = st_all_ref
    else:
        st_ref = st_all_ref.at[fill[0]]
        for slot in range(fill[1]):
            if slot != fill[0]:
                st_all_ref[slot] = jnp.zeros(st_all_ref.shape[1:], F32)

    def one(b, carry):
        rows = pl.ds(pl.multiple_of(b * np_, np_), np_)
        qkv, o_state = [], []
        for h in range(heads):
            cols = slice(h * LANE, (h + 1) * LANE)
            q, k, v = q_ref[rows, cols], k_ref[rows, cols], v_ref[rows, cols]
            s = s0_ref[b, h]
            qkv.append((q, k, v))
            o_state.append(jnp.dot((q * qd_ref[h]).astype(BF16), s.astype(BF16), preferred_element_type=F32))
            st_ref[b, h] = s * cd_ref[h][0:1, :] + lax.dot_general(
                (k * kd_ref[h]).astype(BF16), v.astype(BF16), (((0,), (0,)), ((), ())),
                preferred_element_type=F32)
        inner = [[jnp.sum(q * k[j:j + 1, :], axis=-1, keepdims=True) for j in range(n)] for (q, k, v) in qkv]
        for h in range(heads):
            cols = slice(h * LANE, (h + 1) * LANE)
            v = qkv[h][2]
            o = o_state[h]
            dm = dm_ref[h]
            for j in range(n):
                o = o + (inner[h][j] * dm[:, j:j + 1]) * v[j:j + 1, :]
            gg = g_ref[rows, cols]
            o_ref[rows, cols] = gg * jax.nn.sigmoid(gg) * _rms_unit(o)
        for gi in range(groups):
            cols = slice(gi * LANE, (gi + 1) * LANE)
            svn = _layer_norm_rows(sv_ref[rows, cols], lng_ref[:, cols], lnb_ref[:, cols])
            svn_ref[rows, cols] = svn
            wm = wm_ref[gi]
            mixed = sgb_ref[gi]
            for j in range(n):
                mixed = mixed + wm[:, j:j + 1] * svn[j:j + 1, :]
            o_ref[rows, heads * LANE + gi * LANE:heads * LANE + (gi + 1) * LANE] = u_ref[rows, cols] * mixed
        return carry

    lax.fori_loop(0, nb, one, 0)


def even_seq_sample(p, s0, layer, n, ln_g, ln_b, sg_w, sg_b, heads, groups, prev=None, nb=8):
    m = p.shape[0]
    db = m // SAMPLE_PAD
    w = heads * LANE
    dm, qd, kd, cd = _ret_tabs(SAMPLE_PAD, n, heads)
    wm = jnp.pad(jnp.tril(sg_w[:, :n, :n]), ((0, 0), (0, SAMPLE_PAD - n), (0, LANE - n)))
    sgb = jnp.broadcast_to(jnp.pad(sg_b[:, :n], ((0, 0), (0, SAMPLE_PAD - n)))[:, :, None], (groups, SAMPLE_PAD, LANE))
    rows = nb * SAMPLE_PAD
    part = lambda j: pl.BlockSpec((rows, w), lambda i: (i, j))
    full = lambda a: pl.BlockSpec(a.shape, lambda i: (0,) * a.ndim)
    st_spec = pl.BlockSpec((None, nb, heads, LANE, LANE), lambda i: (layer, i, 0, 0, 0))
    has_prev = prev is not None
    n_layers = s0.shape[0]
    fill = None if has_prev else (layer, n_layers)
    st_out = st_spec if has_prev else pl.BlockSpec((n_layers, nb, heads, LANE, LANE), lambda i: (0, i, 0, 0, 0))
    return pl.pallas_call(
        functools.partial(_even_seq_sample_body, nb=nb, n=n, heads=heads, groups=groups, has_prev=has_prev,
                          fill=fill),
        grid=(db // nb,),
        in_specs=([pl.BlockSpec(memory_space=pl.ANY)] if has_prev else [])
        + [part(j) for j in range(6)] + [st_spec, full(dm), full(qd), full(kd), full(cd),
                                         pl.BlockSpec((1, w), lambda i: (0, 0)),
                                         pl.BlockSpec((1, w), lambda i: (0, 0)),
                                         full(wm), full(sgb)],
        out_specs=[pl.BlockSpec((rows, 2 * w), lambda i: (i, 0)), st_out,
                   pl.BlockSpec((rows, w), lambda i: (i, 0))],
        out_shape=[jax.ShapeDtypeStruct((m, 2 * w), F32),
                   jax.ShapeDtypeStruct(s0.shape, F32),
                   jax.ShapeDtypeStruct((m, w), F32)],
        input_output_aliases={0: 1} if has_prev else {},
        compiler_params=_params("parallel"),
        name="even_seq_sample",
    )(*([prev] if has_prev else []), p, p, p, p, p, p, s0, dm, qd, kd, cd, ln_g.reshape(1, w), ln_b.reshape(1, w),
      wm, sgb)


FFN_HALO = 16


def _ffn_body(*refs, tm, tiles_per_seq, sample, nf_static):
    if sample:
        (x_ref, a_ref, wm_ref, gmix_ref, gpre_ref, wa_ref, wb_ref, cwa_ref, cwb_ref, cba_ref, cbb_ref, wo_ref,
         gpost_ref, pa_ref, pb_ref, o_ref, st_ref, h_scr, upa_scr, upb_scr, acc_scr, x1_scr) = refs
    else:
        (x_ref, xh_ref, a_ref, ah_ref, wm_ref, gmix_ref, gpre_ref, wa_ref, wb_ref, cwa_ref, cwb_ref, cba_ref, cbb_ref,
         wo_ref, gpost_ref, o_ref, st_ref, h_scr, upa_scr, upb_scr, acc_scr, x1_scr) = refs
    i = pl.program_id(0)
    f = pl.program_id(1)
    nf = pl.num_programs(1)
    hl = FFN_HALO

    @pl.when(f == 0)
    def _():
        if sample:
            a, x = a_ref[...], x_ref[...]
        else:
            a = jnp.concatenate([ah_ref[...], a_ref[...]], axis=0)
            x = jnp.concatenate([xh_ref[...], x_ref[...]], axis=0)
        x1 = x + _rms(jnp.dot(a.astype(BF16), wm_ref[...], preferred_element_type=F32), gmix_ref[...])
        hn = _rms(x1, gpre_ref[...])
        if sample:
            x1_scr[...] = x1
            h_scr[0:hl, :] = jnp.zeros((hl, h_scr.shape[1]), BF16)
            h_scr[hl:, :] = hn.astype(BF16)
        else:
            x1_scr[...] = x1[hl:, :]
            h_scr[0:hl, :] = jnp.where(i % tiles_per_seq == 0, 0.0, hn[0:hl, :]).astype(BF16)
            h_scr[hl:, :] = hn[hl:, :].astype(BF16)

    h = h_scr[...]
    if sample:
        t = lax.broadcasted_iota(jnp.int32, (tm, 1), 0) % SAMPLE_PAD
        m1 = t >= 1
        m2 = t >= 2

    def conv(up_scr, cw_ref, cb_ref, p_ref):
        s2 = up_scr[pl.ds(hl - 2, tm), :]
        s1 = up_scr[pl.ds(hl - 1, tm), :]
        s0 = up_scr[pl.ds(hl, tm), :]
        if sample:
            p0, p1 = p_ref[:, 0, :][:, None, :], p_ref[:, 1, :][:, None, :]
            t3 = lax.broadcasted_iota(jnp.int32, (tm // SAMPLE_PAD, SAMPLE_PAD, p0.shape[-1]), 1)
            e2 = jnp.where(t3 == 0, p0, jnp.where(t3 == 1, p1, 0.0)).reshape(tm, p0.shape[-1])
            e1 = jnp.where(t3 == 0, p1, 0.0).reshape(tm, p0.shape[-1])
            s2 = jnp.where(m2, s2, 0.0) + e2
            s1 = jnp.where(m1, s1, 0.0) + e1
        return cb_ref[...] + s2 * cw_ref[0:1, :] + s1 * cw_ref[1:2, :] + s0 * cw_ref[2:3, :]

    upa_scr[...] = jnp.dot(h, wa_ref[...], preferred_element_type=F32)
    upb_scr[...] = jnp.dot(h, wb_ref[...], preferred_element_type=F32)
    a = conv(upa_scr, cwa_ref, cba_ref, pa_ref if sample else None)
    b = conv(upb_scr, cwb_ref, cbb_ref, pb_ref if sample else None)
    act = (jax.nn.gelu(a) * b).astype(BF16)
    contrib = jnp.dot(act, wo_ref[...], preferred_element_type=F32)

    @pl.when(f == 0)
    def _():
        acc_scr[...] = contrib

    @pl.when(f > 0)
    def _():
        acc_scr[...] += contrib

    @pl.when(f == nf - 1)
    def _():
        o_ref[...] = x1_scr[...] + _rms(acc_scr[...], gpost_ref[...])

    fw = upa_scr.shape[1]
    rows = st_ref.shape[0]
    last = True if sample else (i % tiles_per_seq == tiles_per_seq - 1)
    for j in range(nf_static):
        @pl.when(jnp.logical_and(f == j, last))
        def _(j=j):
            st_ref[:, j * fw:(j + 1) * fw] = upa_scr[pl.ds(hl + tm - rows, rows), :]
            st_ref[:, (nf_static + j) * fw:(nf_static + j + 1) * fw] = upb_scr[pl.ds(hl + tm - rows, rows), :]


def mix_out_ffn(x, a, w_mix, g_mix, seq_len, gpre, w_in, conv_w, conv_b, w_out, gpost, layer, prev=None, tm=512):
    m, d = x.shape
    ka = a.shape[1]
    ff = w_out.shape[1]
    fw = 1408 if ff % 1408 == 0 else ff
    nf = ff // fw
    sample = prev is not None
    tm = min(tm, m)
    tps = max(seq_len // tm, 1)
    row = lambda i, f: (i, 0)
    const = lambda i, f: (0, 0)
    cola = lambda i, f: (layer, 0, f)
    colb = lambda i, f: (layer, 0, nf + f)
    hb = tm // FFN_HALO
    halo = lambda i, f: (jnp.maximum(i * hb - 1, 0), 0)
    if sample:
        in_specs = [pl.BlockSpec((tm, d), row), pl.BlockSpec((tm, ka), row)]
        args = [x, a]
    else:
        in_specs = [pl.BlockSpec((tm, d), row), pl.BlockSpec((FFN_HALO, d), halo),
                    pl.BlockSpec((tm, ka), row), pl.BlockSpec((FFN_HALO, ka), halo)]
        args = [x, x, a, a]
    in_specs += [pl.BlockSpec((ka, d), const), pl.BlockSpec((1, d), const)]
    args += [w_mix, g_mix.reshape(1, d)]
    in_specs += [pl.BlockSpec((1, d), const),
                 pl.BlockSpec((None, d, fw), cola), pl.BlockSpec((None, d, fw), colb),
                 pl.BlockSpec((None, CONV_W, fw), cola), pl.BlockSpec((None, CONV_W, fw), colb),
                 pl.BlockSpec((None, 1, fw), cola), pl.BlockSpec((None, 1, fw), colb),
                 pl.BlockSpec((None, fw, d), lambda i, f: (layer, f, 0)),
                 pl.BlockSpec((1, d), const)]
    args += [gpre.reshape(1, d), w_in, w_in, conv_w, conv_w, conv_b, conv_b, w_out, gpost.reshape(1, d)]
    if sample:
        nsq = tm // SAMPLE_PAD
        in_specs += [pl.BlockSpec((None, nsq, CONV_W - 1, fw), lambda i, f: (layer, i, 0, f)),
                     pl.BlockSpec((None, nsq, CONV_W - 1, fw), lambda i, f: (layer, i, 0, nf + f))]
        args += [prev, prev]
        st_shape = jax.ShapeDtypeStruct((m, 2 * ff), F32)
        st_spec = pl.BlockSpec((tm, 2 * ff), lambda i, f: (i, 0))
    else:
        nseq = m // seq_len
        st_shape = jax.ShapeDtypeStruct((nseq * 8, 2 * ff), F32)
        st_spec = pl.BlockSpec((8, 2 * ff), lambda i, f: (i // tps, 0))
    return pl.pallas_call(
        functools.partial(_ffn_body, tm=tm, tiles_per_seq=tps, sample=sample, nf_static=nf),
        grid=(m // tm, nf),
        in_specs=in_specs,
        out_specs=[pl.BlockSpec((tm, d), row), st_spec],
        out_shape=[jax.ShapeDtypeStruct((m, d), F32), st_shape],
        scratch_shapes=[pltpu.VMEM((tm + FFN_HALO, d), BF16),
                        pltpu.VMEM((tm + FFN_HALO, fw), F32),
                        pltpu.VMEM((tm + FFN_HALO, fw), F32),
                        pltpu.VMEM((tm, d), F32),
                        pltpu.VMEM((tm, d), F32)],
        compiler_params=_params("arbitrary", "arbitrary"),
        name="conv_ffn_sample" if sample else "conv_ffn",
    )(*args)


def _odd_modes():
    return [ROPE_NSA] * 8 + [ROPE_NSA, ROPE_NSA, ROPE_NONE, ROPE_NONE] * 3 + [ROPE_NONE]


SEL_TK = 512
SUM_ROWS = 16
WIN_TK = 128


def _odd_outs(prompt):
    if not prompt:
        return [(0, 1024, ATTN_SCALE, F32, None), (1024, 512, 1.0, F32, None), (1536, 512, 1.0, F32, None),
                (2048, 512, 1.0, F32, None), (2560, 128, 1.0, F32, None)]
    return [(0, 1024, ATTN_SCALE * LOG2E, BF16, 0), (1024, 512, 1.0, F32, None), (1024, 512, 1.0, F32, 0),
            (1536, 512, 1.0, F32, 0), (2048, 512, 1.0, F32, 0), (2560, 128, 1.0, F32, 0),
            (1536, 256, 1.0, BF16, PAD_HEADS), (1792, 256, 1.0, BF16, SEL_TK),
            (2048, 256, 1.0, BF16, None), (2304, 256, 1.0, BF16, WIN_TK)]


SUBS = CMP_LEN // CMP_STRIDE
ROW_W = 2 * NSA_KV * HEAD_DIM
SUB_W = CMP_STRIDE * ROW_W


def _compress_weights(phi_k, phi_v, pe_k, pe_v):
    def one(phi, pe):
        p4 = phi.reshape(SUBS, CMP_STRIDE, HEAD_DIM, HEAD_DIM)
        w = jnp.einsum('hlde,gG->lgdhGe', p4, jnp.eye(2, dtype=phi.dtype))
        w = w.reshape(CMP_STRIDE * 2 * HEAD_DIM, SUBS * 2 * HEAD_DIM)
        pr = jnp.broadcast_to(pe.reshape(SUBS, CMP_STRIDE, 1, HEAD_DIM), (SUBS, CMP_STRIDE, 2, HEAD_DIM))
        pr = jnp.pad(pr.reshape(SUBS, -1), ((0, 16 - SUBS), (0, 0)))
        return w, pr
    wk, pk = one(phi_k, pe_k)
    wv, pv = one(phi_v, pe_v)
    return jnp.stack([wk, wv]).astype(BF16), jnp.stack([pk, pv]).astype(BF16)


def _compress_column(xj, j, w_ref, pe_ref, o_ref, ab_scr):
    nsub = o_ref.shape[0]
    half = LANE
    xe = jnp.concatenate([xj, pe_ref[j // 2]], axis=0)
    ab_scr[...] = jnp.dot(xe, w_ref[j // 2], preferred_element_type=F32)
    bias = ab_scr[nsub:nsub + 1, 0:half] + ab_scr[nsub + 1:nsub + 2, half:2 * half]
    o_ref[:, j * LANE:(j + 1) * LANE] = (
        ab_scr[0:nsub, 0:half] + ab_scr[pl.ds(1, nsub), half:2 * half] + bias).astype(o_ref.dtype)


def _compress_body(x_ref, w_ref, pe_ref, o_ref, ab_scr):
    for j in range(ROW_W // LANE):
        xj = jnp.concatenate([x_ref[:, l * ROW_W + j * LANE:l * ROW_W + (j + 1) * LANE]
                              for l in range(CMP_STRIDE)], axis=1).astype(BF16)
        _compress_column(xj, j, w_ref, pe_ref, o_ref, ab_scr)


def _compress_pages_body(pt_ref, *refs, n_x):
    x_refs = refs[:n_x]
    perm_ref, w_ref, pe_ref, o_ref, ab_scr, xs_scr = refs[n_x:]
    page = x_refs[0].shape[-1]
    sub_pp = page // CMP_STRIDE
    kvw = NSA_KV * HEAD_DIM
    for k, xr in enumerate(x_refs):
        for c in range(2):
            t = xr[c].reshape(kvw, page).astype(BF16)
            out = lax.dot_general(perm_ref[...], t, (((1,), (1,)), ((), ())), preferred_element_type=F32)
            for gp in range(kvw // LANE):
                for l in range(CMP_STRIDE):
                    xs_scr[c * (kvw // LANE) + gp, k * sub_pp:(k + 1) * sub_pp, l * LANE:(l + 1) * LANE] = (
                        out[l * sub_pp:(l + 1) * sub_pp, gp * LANE:(gp + 1) * LANE])
    for j in range(ROW_W // LANE):
        _compress_column(xs_scr[j].astype(BF16), j, w_ref, pe_ref, o_ref, ab_scr)


def compress_prompt(rows, nseq, wts, pes):
    nsub = rows.shape[0] // nseq // CMP_STRIDE
    x = rows.reshape(nseq * nsub, SUB_W)
    return pl.pallas_call(
        _compress_body,
        grid=(nseq,),
        in_specs=[pl.BlockSpec((nsub, SUB_W), lambda b: (b, 0)),
                  pl.BlockSpec(wts.shape, lambda b: (0, 0, 0)),
                  pl.BlockSpec(pes.shape, lambda b: (0, 0, 0))],
        out_specs=pl.BlockSpec((nsub, ROW_W), lambda b: (b, 0)),
        out_shape=jax.ShapeDtypeStruct((nseq * nsub, ROW_W), BF16),
        scratch_shapes=[pltpu.VMEM((nsub + 16, 2 * LANE), F32)],
        compiler_params=_params("parallel"),
        name="compress_prompt",
    )(x, wts, pes)


def compress_pages(pool, layer, page_table, wts, pes, nb=2):
    db, n_pages = page_table.shape
    page = pool.shape[-1]
    sub_pp = page // CMP_STRIDE
    nsub = nb * n_pages * sub_pp
    out_row = jnp.arange(page)
    perm = (jnp.arange(page)[None, :] == ((out_row % sub_pp) * CMP_STRIDE + out_row // sub_pp)[:, None]).astype(BF16)
    specs = [pl.BlockSpec((None, None) + pool.shape[2:],
                          functools.partial(lambda i, pt, s, k: (layer, pt[(i * nb + s) * n_pages + k], 0, 0, 0, 0),
                                            s=s, k=k))
             for s in range(nb) for k in range(n_pages)]
    grid_spec = pltpu.PrefetchScalarGridSpec(
        num_scalar_prefetch=1,
        grid=(db // nb,),
        in_specs=specs + [pl.BlockSpec(perm.shape, lambda i, pt: (0, 0)),
                          pl.BlockSpec(wts.shape, lambda i, pt: (0, 0, 0)),
                          pl.BlockSpec(pes.shape, lambda i, pt: (0, 0, 0))],
        out_specs=pl.BlockSpec((nsub, ROW_W), lambda i, pt: (i, 0)),
        scratch_shapes=[pltpu.VMEM((nsub + 16, 2 * LANE), F32),
                        pltpu.VMEM((ROW_W // LANE, nsub, CMP_STRIDE * LANE), F32)],
    )
    return pl.pallas_call(
        functools.partial(_compress_pages_body, n_x=nb * n_pages),
        grid_spec=grid_spec,
        out_shape=jax.ShapeDtypeStruct((db * n_pages * sub_pp, ROW_W), BF16),
        compiler_params=_params("parallel"),
        name="compress_pages",
    )(page_table.reshape(-1), *([pool] * (nb * n_pages)), perm, wts, pes)


def _cover(nsub, nc, n_sel):
    c_start = jnp.arange(nsub) * CMP_STRIDE
    s_start = jnp.arange(SEL_BLK) * SEL_BLK
    ok = ((c_start[:, None] < s_start[None, :] + SEL_BLK) & (c_start[:, None] + CMP_LEN > s_start[None, :])
          & (jnp.arange(nsub)[:, None] < nc) & (jnp.arange(SEL_BLK)[None, :] < n_sel))
    return ok.astype(BF16)


def _expand(n_keys, tk):
    key = jnp.arange(n_keys).reshape(n_keys // tk, 1, tk)
    return (key // SEL_BLK == jnp.arange(SEL_BLK)[None, :, None]).astype(BF16)


def _masked_softmax(s, ok):
    sm = jnp.where(ok, s, NEG)
    ex = jnp.exp(sm - jnp.max(sm, -1, keepdims=True))
    return ex / jnp.sum(ex, -1, keepdims=True)


def _importance(psum, cov_ref, tpos, n_sel):
    hi = psum.astype(BF16)
    lo = (psum - hi.astype(F32)).astype(BF16)
    imp = (jnp.dot(hi, cov_ref[...], preferred_element_type=F32)
           + jnp.dot(lo, cov_ref[...], preferred_element_type=F32))
    jl = lax.broadcasted_iota(jnp.int32, imp.shape, 1)
    cur = tpos // SEL_BLK
    forced = (jl == 0) | (jl == cur) | (jl == cur - 1)
    causal = jl * SEL_BLK <= tpos
    imp = jnp.where(forced, FORCE, jnp.where(causal, imp, -1.0))
    return jnp.where(jl < n_sel, imp, -2.0)


def _topk_masks_rows(imps):
    out = []
    for x in imps:
        jl = lax.broadcasted_iota(jnp.int32, x.shape, 1)
        cnt = jnp.zeros(x.shape, F32)
        for i in range(SEL_BLK):
            xi = x[:, i:i + 1]
            tie = jnp.where(jl > i, 1.0, 0.0)
            cnt = cnt + jnp.where(xi > x, 1.0, jnp.where(xi == x, tie, 0.0))
        out.append(jnp.where(cnt < SEL_TOPK, 1.0, 0.0))
    return out


_NT = (((1,), (1,)), ((), ()))


def _softmax_cols(s):
    p = jnp.exp2(s - jnp.max(s, axis=0, keepdims=True))
    return p, jnp.sum(p, axis=0, keepdims=True)


def _run_skewed(tasks):
    n = len(tasks)
    tasks[0][0]()
    for i in range(n):
        if i + 1 < n:
            tasks[i + 1][0]()
        tasks[i][1]()
        if i >= 1:
            tasks[i - 1][2]()
    tasks[n - 1][2]()


def _nsa_prompt_lanes_body(q_ref, kc_ref, ka_ref, vs_ref, kw_ref, vw_ref, gt_ref, cov_ref, et_ref, o_ref, *,
                           tq, nc, n_sel):
    t0 = pl.program_id(1) * tq
    tl = t0 + lax.broadcasted_iota(jnp.int32, (1, tq), 1)
    sig = jax.nn.sigmoid(gt_ref[...])
    nsub = kc_ref.shape[0]
    rep, hd, ng = NSA_REP, HEAD_DIM, NSA_KV
    nh = ng * rep
    tk = vs_ref.shape[-1]
    wt = vw_ref.shape[-1]
    wide = lambda x: jnp.concatenate([x] * rep, axis=1)
    head_rows = lambda g, r: q_ref[(g * rep + r) * hd:(g * rep + r + 1) * hd, :]
    q4 = [jnp.concatenate([head_rows(g, r) for r in range(rep)], axis=1) for g in range(ng)]

    mrow = lax.broadcasted_iota(jnp.int32, (nsub, 1), 0)
    validf = wide(jnp.where((mrow * CMP_STRIDE + CMP_LEN - 1 <= tl) & (mrow < nc), 1.0, 0.0))
    jrow = lax.broadcasted_iota(jnp.int32, (SEL_BLK, tq), 0)
    cur = tl // SEL_BLK
    forced = (jrow == 0) | (jrow == cur) | (jrow == cur - 1)
    causal = jrow * SEL_BLK <= tl
    s_cmp, p_cmp, o_cmp, sel_neg = [None] * ng, [None] * ng, [None] * ng, [None] * ng

    def cmp_scores(g):
        s_cmp[g] = jnp.dot(kc_ref[:, g * hd:(g + 1) * hd], q4[g], preferred_element_type=F32)

    def cmp_softmax(g):
        ex, l = _softmax_cols(jnp.where(validf > 0.5, s_cmp[g], NEG))
        p = ex / l * validf
        p_cmp[g] = p.astype(BF16)
        psum = p[:, 0:tq]
        for r in range(1, rep):
            psum = psum + p[:, r * tq:(r + 1) * tq]
        hi = psum.astype(BF16)
        lo = (psum - hi.astype(F32)).astype(BF16)
        imp = (jnp.dot(cov_ref[...], hi, preferred_element_type=F32)
               + jnp.dot(cov_ref[...], lo, preferred_element_type=F32))
        imp = jnp.where(forced, FORCE, jnp.where(causal, imp, -1.0))
        imp = jnp.where(jrow < n_sel, imp, -2.0)
        sub = 8
        slabs = [imp[v * sub:(v + 1) * sub, :] for v in range(SEL_BLK // sub)]
        cnts = [jnp.zeros((sub, tq), F32) for _ in slabs]
        srow = lax.broadcasted_iota(jnp.int32, (sub, tq), 0)
        for i in range(SEL_BLK):
            xi = jnp.broadcast_to(imp[i:i + 1, :], (sub, tq))
            for v, x in enumerate(slabs):
                if v > i // sub:
                    hit = jnp.where(xi >= x, 1.0, 0.0)
                elif v < i // sub:
                    hit = jnp.where(xi > x, 1.0, 0.0)
                else:
                    tie = jnp.where(srow > i % sub, 1.0, 0.0)
                    hit = jnp.where(xi > x, 1.0, jnp.where(xi == x, tie, 0.0))
                cnts[v] = cnts[v] + hit
        cnt = jnp.concatenate(cnts, axis=0)
        sel_neg[g] = jnp.where(cnt < SEL_TOPK, 0.0, NEG).astype(BF16)

    def cmp_values(g):
        vc_t = kc_ref[:, ng * hd + g * hd:ng * hd + (g + 1) * hd].astype(F32).T.astype(BF16)
        o_cmp[g] = jnp.dot(vc_t, p_cmp[g], preferred_element_type=F32)

    n_wt = (WINDOW + tq) // wt
    w0 = jnp.maximum(t0 - WINDOW, 0)
    wt0 = w0 // wt
    wrows = n_wt * wt
    kpos_w = w0 + lax.broadcasted_iota(jnp.int32, (wrows, 1), 0)
    bias_w = wide(jnp.where((kpos_w <= tl) & (kpos_w > tl - WINDOW), 0.0, NEG))
    s_win, p_win, o_win = [None] * ng, [None] * ng, [None] * ng

    def ones_row(n):
        return jnp.where(lax.broadcasted_iota(jnp.int32, (SUM_ROWS, n), 0) == 0, 1.0, 0.0).astype(BF16)

    def win_scores(g):
        k_w = kw_ref[pl.ds(pl.multiple_of(w0, wt), wrows), g * hd:(g + 1) * hd]
        s_win[g] = jnp.dot(k_w, q4[g], preferred_element_type=F32) + bias_w

    def win_softmax(g):
        p_win[g] = jnp.exp2(s_win[g] - jnp.max(s_win[g], axis=0, keepdims=True)).astype(BF16)

    def win_values(g):
        v_w = jnp.concatenate([vw_ref[wt0 + i, g * hd:(g + 1) * hd, :] for i in range(n_wt)], axis=1)
        o = jnp.dot(jnp.concatenate([v_w, ones_row(wrows)], axis=0), p_win[g], preferred_element_type=F32)
        o_win[g] = o[0:hd] / o[hd:hd + 1]

    q_aug = [None] * ng

    def keys(kt, g):
        rows = pl.ds(pl.multiple_of(kt * tk, tk), tk)
        return ka_ref[rows, g * LANE:(g + 1) * LANE] + et_ref[rows, :]

    def sweep_tasks(kt, carry, out, bias):
        s, soft = [None] * ng, [None] * ng

        def scores(g):
            if q_aug[g] is None:
                q_aug[g] = jnp.concatenate([jnp.concatenate([head_rows(g, r), sel_neg[g]], axis=0)
                                            for r in range(rep)], axis=1)
            s[g] = jnp.dot(keys(kt, g), q_aug[g], preferred_element_type=F32)
            if bias is not None:
                s[g] = s[g] + bias

        def softmax(g):
            m_i = carry[g][0]
            m_new = jnp.maximum(m_i, jnp.max(s[g], axis=0, keepdims=True))
            soft[g] = (m_new, jnp.exp2(m_i - m_new), jnp.exp2(s[g] - m_new).astype(BF16))

        def values(g):
            m_new, alpha, p = soft[g]
            v_aug = jnp.concatenate([vs_ref[kt, g * hd:(g + 1) * hd, :], ones_tk], axis=0)
            out[g] = (m_new, alpha * carry[g][1] + jnp.dot(v_aug, p, preferred_element_type=F32))

        return [(functools.partial(scores, g), functools.partial(softmax, g), functools.partial(values, g))
                for g in range(ng)]

    kt_d = t0 // tk
    kpos_d = kt_d * tk + lax.broadcasted_iota(jnp.int32, (tk, 1), 0)
    bias_d = wide(jnp.where(kpos_d <= tl, 0.0, NEG))
    ones_tk = ones_row(tk)
    empty = (jnp.full((1, rep * tq), NEG, F32), jnp.zeros((hd + SUM_ROWS, rep * tq), F32))
    init = [None] * ng
    tasks = []
    for g in range(ng):
        tasks.append((functools.partial(cmp_scores, g), functools.partial(cmp_softmax, g),
                      functools.partial(cmp_values, g)))
        tasks.append((functools.partial(win_scores, g), functools.partial(win_softmax, g),
                      functools.partial(win_values, g)))
    _run_skewed(tasks + sweep_tasks(kt_d, [empty] * ng, init, bias_d))

    def step(kt, carry):
        out = [None] * ng
        _run_skewed(sweep_tasks(kt, carry, out, None))
        return tuple(out)

    final = lax.fori_loop(0, kt_d, step, tuple(init))

    for g in range(ng):
        acc_s = final[g][1]
        o_sel = acc_s[0:hd] / acc_s[hd:hd + 1]
        heads = []
        for r in range(rep):
            h = g * rep + r
            cols = slice(r * tq, (r + 1) * tq)
            heads.append(sig[h:h + 1, :] * o_cmp[g][:, cols] + sig[nh + h:nh + h + 1, :] * o_sel[:, cols]
                         + sig[2 * nh + h:2 * nh + h + 1, :] * o_win[g][:, cols])
        for pair in range(rep // 2):
            o_ref[:, (g * rep + 2 * pair) * hd:(g * rep + 2 * pair + 2) * hd] = (
                jnp.concatenate([heads[2 * pair], heads[2 * pair + 1]], axis=0).T)


def nsa_prompt_t(q_t, kcvc, ka, vs_t, kw, vw_t, gates_t, nseq, seq_len, tq=256):
    m = ka.shape[0]
    nsub = kcvc.shape[0] // nseq
    tk, wt = vs_t.shape[-1], vw_t.shape[-1]
    nc = (seq_len - CMP_LEN) // CMP_STRIDE + 1
    n_sel = seq_len // SEL_BLK
    assert n_sel <= SEL_BLK and seq_len >= WINDOW + tq and tk % tq == 0 and tq % wt == 0 and WINDOW % wt == 0
    nq = seq_len // tq
    width = q_t.shape[1]
    cov_t = _cover(nsub, nc, n_sel).T
    key = jnp.arange(seq_len)
    et = jnp.concatenate([jnp.zeros((seq_len, HEAD_DIM), BF16),
                          (key[:, None] // SEL_BLK == jnp.arange(SEL_BLK)[None, :]).astype(BF16)], axis=1)
    cols = lambda a: pl.BlockSpec((None, a.shape[1], tq), lambda b, t: (b, 0, t))
    seq_rows = lambda a: pl.BlockSpec((seq_len, a.shape[1]), lambda b, t: (b, 0))
    whole = lambda a: pl.BlockSpec((None,) + a.shape[1:], lambda b, t: (b, 0, 0, 0))
    return pl.pallas_call(
        functools.partial(_nsa_prompt_lanes_body, tq=tq, nc=nc, n_sel=n_sel),
        grid=(nseq, nq),
        in_specs=[cols(q_t), pl.BlockSpec((nsub, ROW_W), lambda b, t: (b, 0)), seq_rows(ka), whole(vs_t),
                  seq_rows(kw), whole(vw_t), cols(gates_t),
                  pl.BlockSpec(cov_t.shape, lambda b, t: (0, 0)),
                  pl.BlockSpec(et.shape, lambda b, t: (0, 0))],
        out_specs=pl.BlockSpec((tq, width), lambda b, t: (b * nq + t, 0)),
        out_shape=jax.ShapeDtypeStruct((m, width), F32),
        compiler_params=_params("parallel", "parallel"),
        name="nsa_prompt",
    )(q_t, kcvc, ka, vs_t, kw, vw_t, gates_t, cov_t, et)


def _joint_softmax(s_a, ok_a, s_b, ok_b):
    s_a = jnp.where(ok_a, s_a, NEG)
    s_b = jnp.where(ok_b, s_b, NEG)
    mx = jnp.maximum(jnp.max(s_a, -1, keepdims=True), jnp.max(s_b, -1, keepdims=True))
    e_a = jnp.exp(s_a - mx)
    e_b = jnp.exp(s_b - mx)
    return e_a, e_b, jnp.sum(e_a, -1, keepdims=True) + jnp.sum(e_b, -1, keepdims=True)


def _nsa_sample_body(pt_ref, q_ref, kc_ref, *rest, n, n_pages, past_len, nc, n_sel, fill):
    page_refs = rest[:n_pages]
    ns_ref, nw_ref, win_ref, gt_ref, cov_ref, e_ref = rest[n_pages:n_pages + 6]
    o_ref, wo_ref = rest[-2:]
    if fill is not None:
        for slot in range(fill[1]):
            if slot != fill[0]:
                wo_ref[slot] = jnp.zeros(wo_ref.shape[1:], F32)
        wo_ref = wo_ref.at[fill[0]]
    np_, rep, hd, kvw, nkv = SAMPLE_PAD, NSA_REP, HEAD_DIM, NSA_KV * HEAD_DIM, NSA_KV
    tpos = past_len + lax.broadcasted_iota(jnp.int32, (np_, 1), 0)
    sig = jax.nn.sigmoid(gt_ref[...])
    q = q_ref[...]

    blocks = []
    for g in range(nkv):
        for r in range(rep):
            h = g * rep + r
            parts = [q[:, h * hd:(h + 1) * hd]]
            if g > 0:
                parts.insert(0, jnp.zeros((np_, g * hd), F32))
            if g < nkv - 1:
                parts.append(jnp.zeros((np_, (nkv - 1 - g) * hd), F32))
            blocks.append(jnp.concatenate(parts, axis=1))
    qbd = jnp.concatenate(blocks, axis=0).astype(BF16)
    nrow = nkv * rep * np_

    def grp(x):
        return x.reshape(nkv, rep, np_, x.shape[-1])

    def scores(keys):
        return grp(lax.dot_general(qbd, keys, _NT, preferred_element_type=F32))

    def scores_t(keys_t):
        return grp(jnp.dot(qbd, keys_t, preferred_element_type=F32))

    new_lane = lax.broadcasted_iota(jnp.int32, (np_, np_), 1)
    new_pos = past_len + new_lane

    page = page_refs[0].shape[-1]
    w_buf = win_ref.shape[-1]
    ns, nw = ns_ref[...], nw_ref[...]
    k_w = win_ref[0].reshape(kvw, w_buf)
    v_w = win_ref[1].reshape(kvw, w_buf)
    k_all = jnp.concatenate([pr[0].reshape(kvw, page) for pr in page_refs], axis=1).astype(BF16)
    v_all = jnp.concatenate([pr[1].reshape(kvw, page) for pr in page_refs], axis=1).astype(BF16)
    s_cmp = scores(kc_ref[:, 0:kvw])
    s_win, s_win_new = scores_t(k_w.astype(BF16)), scores(nw[:, 0:kvw].astype(BF16))
    s_sel, s_sel_new = scores_t(k_all), scores(ns[:, 0:kvw].astype(BF16))

    nsub = kc_ref.shape[0]
    mcol = lax.broadcasted_iota(jnp.int32, (np_, nsub), 1)
    valid = (mcol * CMP_STRIDE + CMP_LEN - 1 <= tpos) & (mcol < nc)
    p = _masked_softmax(s_cmp, valid[None, None]) * jnp.where(valid, 1.0, 0.0)[None, None]
    o_cmp = jnp.dot(p.reshape(nrow, nsub).astype(BF16), kc_ref[:, kvw:2 * kvw], preferred_element_type=F32)
    sels = _topk_masks_rows([_importance(jnp.sum(p[g], axis=0), cov_ref, tpos, n_sel) for g in range(nkv)])

    kpos = lax.broadcasted_iota(jnp.int32, (np_, past_len), 1)
    ok_c = jnp.stack([jnp.where(kpos <= tpos, jnp.dot(sels[g].astype(BF16), e_ref[0], preferred_element_type=F32), 0.0)
                      for g in range(nkv)]) > 0.5
    new_ok = (new_pos <= tpos) & (new_lane < n)
    ok_n = jnp.stack([jnp.where(new_ok, jnp.concatenate(
        [sels[g][:, (past_len + i) // SEL_BLK:(past_len + i) // SEL_BLK + 1] for i in range(np_)], axis=1), 0.0)
        for g in range(nkv)]) > 0.5
    e_c, e_n, l_s = _joint_softmax(s_sel, ok_c[:, None], s_sel_new, ok_n[:, None])
    o_sel = lax.dot_general(e_c.reshape(nrow, past_len).astype(BF16), v_all, _NT, preferred_element_type=F32)
    e_n = e_n.reshape(nrow, np_)
    for i in range(n):
        o_sel = o_sel + e_n[:, i:i + 1] * ns[i:i + 1, kvw:2 * kvw]
    o_sel = o_sel / l_s.reshape(nrow, 1)

    kpos_w = past_len - w_buf + lax.broadcasted_iota(jnp.int32, (np_, w_buf), 1)
    ok_w = (kpos_w <= tpos) & (kpos_w > tpos - WINDOW)
    ok_wn = (new_pos <= tpos) & (new_pos > tpos - WINDOW) & (new_lane < n)
    e_w, e_wn, l_w = _joint_softmax(s_win, ok_w[None, None], s_win_new, ok_wn[None, None])
    o_win = lax.dot_general(e_w.reshape(nrow, w_buf).astype(BF16), v_w.astype(BF16), _NT,
                            preferred_element_type=F32)
    e_wn = e_wn.reshape(nrow, np_)
    for i in range(n):
        o_win = o_win + e_wn[:, i:i + 1] * nw[i:i + 1, kvw:2 * kvw]
    o_win = o_win / l_w.reshape(nrow, 1)

    nh = nkv * rep
    for g in range(nkv):
        for r in range(rep):
            h = g * rep + r
            rows = slice(h * np_, (h + 1) * np_)
            cols = slice(g * hd, (g + 1) * hd)
            o_ref[:, h * hd:(h + 1) * hd] = (sig[:, h:h + 1] * o_cmp[rows, cols]
                                            + sig[:, nh + h:nh + h + 1] * o_sel[rows, cols]
                                            + sig[:, 2 * nh + h:2 * nh + h + 1] * o_win[rows, cols])

    body = w_buf - LANE
    for c, old in enumerate((k_w, v_w)):
        shifted = pltpu.roll(old, w_buf - n, 1)
        new_t = jnp.concatenate([nw[:, c * kvw:(c + 1) * kvw].T, jnp.zeros((kvw, LANE - np_), F32)], axis=1)
        lane = lax.broadcasted_iota(jnp.int32, (kvw, LANE), 1)
        wo_ref[c * kvw:(c + 1) * kvw, 0:body] = shifted[:, 0:body]
        wo_ref[c * kvw:(c + 1) * kvw, body:w_buf] = jnp.where(lane < LANE - n, shifted[:, body:w_buf],
                                                              pltpu.roll(new_t, LANE - n, 1))


def nsa_sample(q, kcvc, pool, layer, page_table, rows_s, rows_w, win, gates, n, prev=None):
    db, n_pages = page_table.shape
    page = pool.shape[-1]
    past_len = n_pages * page
    nsub = kcvc.shape[0] // db
    w_buf = win.shape[-1]
    assert w_buf == WINDOW and n <= SAMPLE_PAD and past_len % SEL_BLK == 0
    nc = (past_len + n - CMP_LEN) // CMP_STRIDE + 1
    n_sel = -(-(past_len + n) // SEL_BLK)
    assert n_sel <= SEL_BLK and nc <= nsub
    cov = _cover(nsub, nc, n_sel)
    exp = _expand(past_len, past_len)
    rows8 = lambda w: pl.BlockSpec((SAMPLE_PAD, w), lambda i, pt: (i, 0))
    pages = [pl.BlockSpec((None, None) + pool.shape[2:],
                          functools.partial(lambda i, pt, k: (layer, pt[i * n_pages + k], 0, 0, 0, 0), k=k))
             for k in range(n_pages)]
    grid_spec = pltpu.PrefetchScalarGridSpec(
        num_scalar_prefetch=1,
        grid=(db,),
        in_specs=[rows8(q.shape[1]), pl.BlockSpec((nsub, ROW_W), lambda i, pt: (i, 0))] + pages
        + [rows8(ROW_W), rows8(ROW_W),
           pl.BlockSpec((None, None) + win.shape[2:], lambda i, pt: (layer, i, 0, 0, 0, 0)),
           rows8(LANE),
           pl.BlockSpec(cov.shape, lambda i, pt: (0, 0)),
           pl.BlockSpec(exp.shape, lambda i, pt: (0, 0, 0))]
        + ([pl.BlockSpec(memory_space=pl.ANY)] if prev is not None else []),
        out_specs=[rows8(q.shape[1]),
                   pl.BlockSpec((None, None, ROW_W, w_buf), lambda i, pt: (layer, i, 0, 0)) if prev is not None
                   else pl.BlockSpec((win.shape[0], None, ROW_W, w_buf), lambda i, pt: (0, i, 0, 0))],
    )
    args = (page_table.reshape(-1), q, kcvc, *([pool] * n_pages), rows_s, rows_w, win, gates, cov, exp)
    return pl.pallas_call(
        functools.partial(_nsa_sample_body, n=n, n_pages=n_pages, past_len=past_len, nc=nc, n_sel=n_sel,
                          fill=None if prev is not None else (layer, win.shape[0])),
        grid_spec=grid_spec,
        out_shape=[jax.ShapeDtypeStruct(q.shape, F32),
                   jax.ShapeDtypeStruct((win.shape[0], db, ROW_W, w_buf), F32)],
        input_output_aliases={len(args): 1} if prev is not None else {},
        compiler_params=_params("parallel"),
        name="nsa_sample",
    )(*args, *([prev] if prev is not None else []))


def kernel(x_prompt, x_sample, state_ret, cache_cmp, cache_sel, state_win, state_ffn, page_table, norm_mix_pre, norm_mix_post, norm_ffn_pre, norm_ffn_post, e_w_in, e_w_out, e_sg_ln_g, e_sg_ln_b, e_sg_w, e_sg_b, o_w_in, o_w_out, o_pe_k, o_pe_v, o_phi_k, o_phi_v, f_w_in, f_conv_w, f_conv_b, f_w_out):
    b, s, d = x_prompt.shape
    db, n, _ = x_sample.shape
    depth = norm_mix_pre.shape[0]
    heads = state_ret.shape[2]
    groups = e_sg_w.shape[1]
    n_pages = page_table.shape[1]
    page = cache_cmp.shape[2]
    past_len = n_pages * page
    pad = SAMPLE_PAD
    assert n <= pad and n >= CONV_W - 1 and n < CMP_STRIDE and state_ret.shape[3] == LANE

    xp = x_prompt.reshape(b * s, d)
    xs = jnp.pad(x_sample, ((0, 0), (0, pad - n), (0, 0))).reshape(db * pad, d)
    pos_p = jnp.arange(s, dtype=jnp.int32)
    pos_s = jnp.tile(past_len + jnp.arange(pad, dtype=jnp.int32), db)
    ret_p, ret_s = _ret_rope_tabs(pos_p), _ret_rope_tabs(pos_s)
    nsa_p, nsa_s = _nsa_rope_tabs(pos_p), _nsa_rope_tabs(pos_s)
    even_modes = [ROPE_RET_Q] * heads + [ROPE_RET_K] * heads + [ROPE_NONE] * (4 * heads)
    even_outs = [(0, 6 * heads * LANE, 1.0, F32, None)]
    kscale = LANE ** -0.5
    odd_cols = o_w_in.shape[2]
    odd_pad = -(-odd_cols // LANE) * LANE
    rows_last = lambda a: jnp.transpose(a, (0, 1, 3, 4, 5, 2))
    rows_first = lambda a: jnp.transpose(a.reshape(a.shape[:2] + kv_shape + (a.shape[-1],)), (0, 1, 5, 2, 3, 4))
    pool_c, pool_s, win = rows_last(cache_cmp), rows_last(cache_sel), rows_last(state_win)
    kv_shape = cache_cmp.shape[3:]
    keep = min(WINDOW, s)

    out = {k: [] for k in ("ret_p", "sgv", "cmp_s", "sel_s", "win_p", "ffn_p", "ffn_s")}
    ret_s_all = cmp_all = sel_all = win_all = None
    n_odd = o_w_in.shape[0]
    ffn_w_in, ffn_w_out = f_w_in.astype(BF16), f_w_out.astype(BF16)
    ffn_cb = f_conv_b[:, None, :]
    for l in range(depth):
        i = l // 2
        if l % 2 == 0:
            w_in = e_w_in[i].astype(BF16)
            w_out = e_w_out[i].astype(BF16)
            (pp,) = proj_in(xp, norm_mix_pre[l], w_in, (ret_p[0], ret_p[1], ret_p[1]), even_modes, even_outs, kscale)
            (ps,) = proj_in(xs, norm_mix_pre[l], w_in, (ret_s[0], ret_s[1], ret_s[1]), even_modes, even_outs, kscale)
            cat_p, st_p = even_seq(pp, b, s, e_sg_ln_g[i], e_sg_ln_b[i], e_sg_w[i], e_sg_b[i], heads, groups)
            cat_s, ret_s_all, svn_s = even_seq_sample(ps, state_ret, i, n, e_sg_ln_g[i], e_sg_ln_b[i], e_sg_w[i],
                                                      e_sg_b[i], heads, groups, prev=ret_s_all)
            a_p, a_s = cat_p, cat_s
            out["ret_p"].append(st_p)
            out["sgv"].append(svn_s.reshape(db, pad, -1)[:, :n])
        else:
            w_in = jnp.pad(o_w_in[i], ((0, 0), (0, odd_pad - odd_cols))).astype(BF16)
            w_out = o_w_out[i].astype(BF16)
            wts, pes = _compress_weights(o_phi_k[i], o_phi_v[i], o_pe_k[i], o_pe_v[i])
            q_p, rc_p, cmp_all, sel_all, win_t, g_p, ka_p, vs_p, kw_p, vw_p = proj_in(
                xp, norm_mix_pre[l], w_in, nsa_p, _odd_modes(), _odd_outs(True), seq_len=s,
                layer=i, n_layers=n_odd, stacked={2: cmp_all, 3: sel_all})
            q_s, rc_s, rs_s, rw_s, g_s = proj_in(xs, norm_mix_pre[l], w_in, nsa_s, _odd_modes(), _odd_outs(False))
            kc_p = compress_prompt(rc_p, b, wts, pes)
            o_p = nsa_prompt_t(q_p, kc_p, ka_p, vs_p, kw_p, vw_p, g_p, b, s)
            kc_s = compress_pages(pool_c, i, page_table, wts, pes)
            a_s, win_all = nsa_sample(q_s, kc_s, pool_s, i, page_table, rs_s, rw_s, win, g_s, n, prev=win_all)
            a_p = o_p
            out["win_p"].append(win_t[:, :, s - keep:])
            out["cmp_s"].append(rc_s.reshape((db, pad) + kv_shape)[:, :n])
            out["sel_s"].append(rs_s.reshape((db, pad) + kv_shape)[:, :n])
        xp, st_p = mix_out_ffn(xp, a_p, w_out, norm_mix_post[l], s, norm_ffn_pre[l], ffn_w_in, f_conv_w, ffn_cb,
                               ffn_w_out, norm_ffn_post[l], l)
        xs, st_s = mix_out_ffn(xs, a_s, w_out, norm_mix_post[l], pad, norm_ffn_pre[l], ffn_w_in, f_conv_w, ffn_cb,
                               ffn_w_out, norm_ffn_post[l], l, prev=state_ffn, tm=256)
        out["ffn_p"].append(st_p.reshape(b, 8, -1)[:, 8 - (CONV_W - 1):])
        out["ffn_s"].append(st_s.reshape(db, pad, -1)[:, n - (CONV_W - 1):n])

    stack = lambda k: jnp.stack(out[k])
    return (xp.reshape(b, s, d), xs.reshape(db, pad, d)[:, :n], stack("ret_p"), ret_s_all, stack("sgv"),
            rows_first(cmp_all), stack("cmp_s"), rows_first(sel_all), stack("sel_s"), rows_first(stack("win_p")),
            rows_first(win_all), stack("ffn_p"), stack("ffn_s"))
```
